```python
import numpy as np
import jax, jax.numpy as jnp
from jax import lax

D_MODEL = 2048
BATCH = 2
SEQ = 16384
DEPTH = 2

GRID_W = 64
CTX_LEN = 256
N_MIXERS = 2
RMS_EPS = 1e-6
RWKV_HEAD = 64
RWKV_HEADS = D_MODEL // RWKV_HEAD
DECAY_LORA = max(32, int(round(1.8 * D_MODEL ** 0.5 / 32)) * 32)
AAA_LORA = max(32, int(round(1.8 * D_MODEL ** 0.5 / 32)) * 32)
GATE_LORA = max(32, int(round(0.6 * D_MODEL ** 0.8 / 32)) * 32)
GN_EPS = 64e-5
NA_HEAD = 128
NA_HEADS = D_MODEL // NA_HEAD
WIN_H = 8
WIN_W = 16
COL_BLOCK = 16
SLAB_W = COL_BLOCK + WIN_W
ROPE_THETA = 10000.0
NEG_INF = -1e30
D_FF = ((8 * D_MODEL + 3 * 256 - 1) // (3 * 256)) * 256

kernel_name = 'hybrid_rwkv7_natten_prefix_dit'


def _rmsnorm(x, g):
    xf = x.astype(jnp.float32)
    y = xf * lax.rsqrt(jnp.mean(xf * xf, axis=-1, keepdims=True) + RMS_EPS)
    return (y * g.astype(jnp.float32)).astype(x.dtype)


def _modulate(x, gain, shift, scale):
    return _rmsnorm(x, gain) * (1 + scale) + shift


def _ada(cvec, w, b):
    return jnp.split(jax.nn.silu(cvec) @ w + b, 6, axis=-1)


def _swiglu(h, w1, w3, w2):
    return (jax.nn.silu(h @ w1) * (h @ w3)) @ w2


def _heads(t):
    return t.reshape(*t.shape[:-1], RWKV_HEADS, RWKV_HEAD)


def _qshift(h, rows):
    B, T, D = h.shape
    g = h.reshape(B, rows, GRID_W, D)
    q = D // 4
    left = jnp.pad(g[:, :, :-1, :q], ((0, 0), (0, 0), (1, 0), (0, 0)))
    right = jnp.pad(g[:, :, 1:, q:2 * q], ((0, 0), (0, 0), (0, 1), (0, 0)))
    up = jnp.pad(g[:, :-1, :, 2 * q:3 * q], ((0, 0), (1, 0), (0, 0), (0, 0)))
    down = jnp.pad(g[:, 1:, :, 3 * q:], ((0, 0), (0, 1), (0, 0), (0, 0)))
    return jnp.concatenate([left, right, up, down], axis=-1).reshape(B, T, D)


def _bishift(h):
    half = h.shape[-1] // 2
    prev = jnp.pad(h[:, :-1, :half], ((0, 0), (1, 0), (0, 0)))
    nxt = jnp.pad(h[:, 1:, half:], ((0, 0), (0, 1), (0, 0)))
    return jnp.concatenate([prev, nxt], axis=-1)


def _rwkv_shared(h, hs, p):
    xx = hs - h
    mu = p['mu']
    r = (h + xx * mu[0]) @ p['w_r']
    k = (h + xx * mu[2]) @ p['w_k']
    v = (h + xx * mu[3]) @ p['w_v']
    g = jax.nn.sigmoid((h + xx * mu[5]) @ p['g1']) @ p['g2']
    kkf = _heads(k * p['k_k']).astype(jnp.float32)
    kk = (kkf / jnp.maximum(jnp.sqrt(jnp.sum(kkf * kkf, axis=-1, keepdims=True)), 1e-12)).astype(h.dtype)
    return _heads(r), k, _heads(v), kk, g


def _rwkv_direction(h, hs, k, kk, p, d):
    xx = hs - h
    xw = h + xx * p['mu'][1]
    xa = h + xx * p['mu'][4]
    lw = -jax.nn.softplus(-(p['w0'][d] + jnp.tanh(xw @ p['w1'][d]) @ p['w2'][d])) - 0.5
    decay = jnp.exp(-jnp.exp(lw.astype(jnp.float32)))
    a = jax.nn.sigmoid(p['a0'][d] + (xa @ p['a1'][d]) @ p['a2'][d])
    k_d = k * (1 + (a - 1) * p['k_a'])
    return _heads(decay), _heads(k_d), kk * _heads(a)


def _wkv_scan(state0, r, decay, k, v, kk, b, reverse):
    def step(S, inp):
        r_t, w_t, k_t, v_t, kk_t, b_t = inp
        sa = -jnp.einsum('bhvk,bhk->bhv', S, kk_t)
        S = S * w_t[:, :, None, :] + sa[..., None] * b_t[:, :, None, :] + v_t[..., None] * k_t[:, :, None, :]
        return S, jnp.einsum('bhvk,bhk->bhv', S, r_t)
    xs = tuple(jnp.swapaxes(t, 0, 1).astype(jnp.float32) for t in (r, decay, k, v, kk, b))
    S, y = lax.scan(step, state0, xs, reverse=reverse)
    return S, jnp.swapaxes(y, 0, 1)


def _rwkv_out(wkv, r, k_sum, v, g, p):
    B, T, D = g.shape
    mean = jnp.mean(wkv, axis=-1, keepdims=True)
    var = jnp.mean(jnp.square(wkv - mean), axis=-1, keepdims=True)
    y = ((wkv - mean) * lax.rsqrt(var + GN_EPS)).reshape(B, T, D).astype(g.dtype)
    y = y * p['ln_w'] + p['ln_b']
    bonus = jnp.sum(r * k_sum * p['r_k'], axis=-1, keepdims=True) * v
    y = y + bonus.reshape(B, T, D)
    return (y * g) @ p['w_o']


def _rwkv_mixer(h_lat, h_ctx, rows, p, ctx_out):
    B = h_lat.shape[0]
    hs_lat, hs_ctx = _qshift(h_lat, rows), _bishift(h_ctx)
    r_l, k_l, v_l, kk_l, g_l = _rwkv_shared(h_lat, hs_lat, p)
    r_c, k_c, v_c, kk_c, g_c = _rwkv_shared(h_ctx, hs_ctx, p)
    s0 = jnp.zeros((B, RWKV_HEADS, RWKV_HEAD, RWKV_HEAD), jnp.float32)
    wkv_l, ks_l, wkv_c, ks_c = [], [], [], []
    for d, reverse in enumerate((False, True)):
        dec, kd, b = _rwkv_direction(h_ctx, hs_ctx, k_c, kk_c, p, d)
        s_ctx, y = _wkv_scan(s0, r_c, dec, kd, v_c, kk_c, b, reverse)
        wkv_c.append(y)
        ks_c.append(kd)
        dec, kd, b = _rwkv_direction(h_lat, hs_lat, k_l, kk_l, p, d)
        _, y = _wkv_scan(s_ctx, r_l, dec, kd, v_l, kk_l, b, reverse)
        wkv_l.append(y)
        ks_l.append(kd)
    y_lat = _rwkv_out(wkv_l[0] + wkv_l[1], r_l, ks_l[0] + ks_l[1], v_l, g_l, p)
    y_ctx = _rwkv_out(wkv_c[0] + wkv_c[1], r_c, ks_c[0] + ks_c[1], v_c, g_c, p) if ctx_out else None
    return y_lat, y_ctx


def _rope_2d(x):
    B, T, H, E = x.shape
    t = jnp.arange(T)
    pos = jnp.stack([t // GRID_W, t % GRID_W]).astype(jnp.float32)
    half = E // 2
    freqs = ROPE_THETA ** (-jnp.arange(0, half, 2, dtype=jnp.float32) / half)
    ang = jnp.swapaxes(pos[:, :, None] * freqs, 0, 1)[None, :, None]
    cos, sin = jnp.cos(ang), jnp.sin(ang)
    xa = x.astype(jnp.float32).reshape(B, T, H, 2, half)
    x1, x2 = xa[..., :half // 2], xa[..., half // 2:]
    out = jnp.concatenate([x1 * cos - x2 * sin, x2 * cos + x1 * sin], axis=-1)
    return out.reshape(B, T, H, E).astype(x.dtype)


def _neighbourhood_attention(q, k, v, k_c, v_c, rpb, rows):
    B, T, H, E = q.shape
    win_h = min(WIN_H, rows)
    nb = GRID_W // COL_BLOCK
    col_start = np.clip(np.arange(nb) * COL_BLOCK - WIN_W // 2, 0, GRID_W - SLAB_W)
    col_idx = col_start[:, None] + np.arange(SLAB_W)
    qcol = np.arange(nb)[:, None] * COL_BLOCK + np.arange(COL_BLOCK)
    wstart = np.clip(qcol - WIN_W // 2, 0, GRID_W - WIN_W)
    kcol = col_idx[:, None, :]
    valid = (kcol >= wstart[..., None]) & (kcol < wstart[..., None] + WIN_W)
    rel_idx = np.clip(kcol - qcol[..., None], -(WIN_W - 1), WIN_W - 1) + WIN_W - 1
    mask = np.broadcast_to(valid[:, :, None, :], (nb, COL_BLOCK, win_h, SLAB_W)).reshape(nb, COL_BLOCK, win_h * SLAB_W)
    rpb_c = rpb[:, :, rel_idx]
    n_lat = win_h * SLAB_W
    qg = (q * E ** -0.5).reshape(B, rows, GRID_W, H, E)
    kg = k.reshape(B, rows, GRID_W, H, E)
    vg = v.reshape(B, rows, GRID_W, H, E)

    def row_block(r):
        r0 = jnp.clip(r - WIN_H // 2, 0, rows - win_h)
        q_r = lax.dynamic_index_in_dim(qg, r, axis=1, keepdims=False).reshape(B, nb, COL_BLOCK, H, E)
        k_r = lax.dynamic_slice_in_dim(kg, r0, win_h, axis=1)[:, :, col_idx]
        v_r = lax.dynamic_slice_in_dim(vg, r0, win_h, axis=1)[:, :, col_idx]
        k_r = jnp.transpose(k_r, (0, 2, 1, 3, 4, 5)).reshape(B, nb, n_lat, H, E)
        v_r = jnp.transpose(v_r, (0, 2, 1, 3, 4, 5)).reshape(B, nb, n_lat, H, E)
        bias = lax.dynamic_slice_in_dim(rpb_c, r0 - r + WIN_H - 1, win_h, axis=1)
        bias = jnp.transpose(bias, (0, 2, 3, 1, 4)).reshape(H, nb, COL_BLOCK, n_lat)
        s_lat = jnp.einsum('bnqhe,bnkhe->bhnqk', q_r, k_r).astype(jnp.float32) + bias.astype(jnp.float32)
        s_lat = jnp.where(mask, s_lat, NEG_INF)
        s_ctx = jnp.einsum('bnqhe,bche->bhnqc', q_r, k_c).astype(jnp.float32)
        pr = jax.nn.softmax(jnp.concatenate([s_lat, s_ctx], axis=-1), axis=-1).astype(v.dtype)
        o = (jnp.einsum('bhnqk,bnkhe->bnqhe', pr[..., :n_lat], v_r)
             + jnp.einsum('bhnqc,bche->bnqhe', pr[..., n_lat:], v_c))
        return o.reshape(B, GRID_W, H * E)

    out = lax.map(row_block, jnp.arange(rows))
    return jnp.transpose(out, (1, 0, 2, 3)).reshape(B, T, H * E)


def _dense_attention(q, k, v):
    B, C, H, E = q.shape
    s = jnp.einsum('bqhe,bkhe->bhqk', q * E ** -0.5, k).astype(jnp.float32)
    pr = jax.nn.softmax(s, axis=-1).astype(v.dtype)
    return jnp.einsum('bhqk,bkhe->bqhe', pr, v).reshape(B, C, H * E)


def _na_mixer(h_lat, h_ctx, rows, p, ctx_out):
    B, T, D = h_lat.shape
    C = h_ctx.shape[1]
    hs = lambda t, n: t.reshape(B, n, NA_HEADS, NA_HEAD)
    q, k, v = jnp.split(h_lat @ p['w_qkv'], 3, axis=-1)
    q = _rope_2d(_rmsnorm(hs(q, T), p['q_gain']))
    k = _rope_2d(_rmsnorm(hs(k, T), p['k_gain']))
    v = hs(v, T)
    if ctx_out:
        q_c, k_c, v_c = jnp.split(h_ctx @ p['w_qkv'], 3, axis=-1)
    else:
        k_c, v_c = jnp.split(h_ctx @ p['w_qkv'][:, D:], 2, axis=-1)
    k_c = _rmsnorm(hs(k_c, C), p['k_gain'])
    v_c = hs(v_c, C)
    y_lat = _neighbourhood_attention(q, k, v, k_c, v_c, p['rpb'], rows) @ p['w_o']
    y_ctx = None
    if ctx_out:
        q_c = _rmsnorm(hs(q_c, C), p['q_gain'])
        y_ctx = _dense_attention(q_c, k_c, v_c) @ p['w_o']
    return y_lat, y_ctx


def setup_inputs(seed: int = 0) -> dict:
    key = jax.random.key(seed)
    ks = iter(jax.random.split(key, 48))
    f32 = jnp.float32
    nrm = lambda shape, s: jax.random.normal(next(ks), shape, f32) * s
    D = D_MODEL
    n_rw = (DEPTH + N_MIXERS - 1) // N_MIXERS
    n_na = DEPTH // N_MIXERS
    return {
        'x': nrm((BATCH, SEQ, D), 1.0),
        'c': nrm((BATCH, D), 1.0),
        'ctx': nrm((BATCH, CTX_LEN, D), 1.0),
        'c_ctx': nrm((D,), 1.0),
        'ada_w': nrm((DEPTH, D, 6 * D), 0.5 * D ** -0.5),
        'ada_b': nrm((DEPTH, 6 * D), 0.02),
        'norm1': 1.0 + nrm((DEPTH, D), 0.02),
        'norm2': 1.0 + nrm((DEPTH, D), 0.02),
        'rwkv_mu': jax.random.uniform(next(ks), (n_rw, 6, D), f32),
        'rwkv_w_r': nrm((n_rw, D, D), D ** -0.5),
        'rwkv_w_k': nrm((n_rw, D, D), D ** -0.5),
        'rwkv_w_v': nrm((n_rw, D, D), D ** -0.5),
        'rwkv_w_o': nrm((n_rw, D, D), D ** -0.5),
        'rwkv_w0': jax.random.uniform(next(ks), (n_rw, 2, D), f32, minval=-6.0, maxval=-1.0),
        'rwkv_w1': nrm((n_rw, 2, D, DECAY_LORA), D ** -0.5),
        'rwkv_w2': nrm((n_rw, 2, DECAY_LORA, D), 0.5 * DECAY_LORA ** -0.5),
        'rwkv_a0': nrm((n_rw, 2, D), 0.5),
        'rwkv_a1': nrm((n_rw, 2, D, AAA_LORA), D ** -0.5),
        'rwkv_a2': nrm((n_rw, 2, AAA_LORA, D), 0.5 * AAA_LORA ** -0.5),
        'rwkv_g1': nrm((n_rw, D, GATE_LORA), D ** -0.5),
        'rwkv_g2': nrm((n_rw, GATE_LORA, D), GATE_LORA ** -0.5),
        'rwkv_k_k': 0.85 + nrm((n_rw, D), 0.1),
        'rwkv_k_a': 1.0 + nrm((n_rw, D), 0.1),
        'rwkv_r_k': nrm((n_rw, RWKV_HEADS, RWKV_HEAD), 0.1),
        'rwkv_ln_w': 1.0 + nrm((n_rw, D), 0.02),
        'rwkv_ln_b': nrm((n_rw, D), 0.02),
        'na_w_qkv': nrm((n_na, D, 3 * D), D ** -0.5),
        'na_w_o': nrm((n_na, D, D), D ** -0.5),
        'na_q_gain': 1.0 + nrm((n_na, NA_HEAD), 0.02),
        'na_k_gain': 1.0 + nrm((n_na, NA_HEAD), 0.02),
        'na_rpb': nrm((n_na, NA_HEADS, 2 * WIN_H - 1, 2 * WIN_W - 1), 0.1),
        'ffn_w1': nrm((DEPTH, D, D_FF), D ** -0.5),
        'ffn_w3': nrm((DEPTH, D, D_FF), D ** -0.5),
        'ffn_w2': nrm((DEPTH, D_FF, D), D_FF ** -0.5),
    }


def reference(x, c, ctx, c_ctx, ada_w, ada_b, norm1, norm2,
              rwkv_mu, rwkv_w_r, rwkv_w_k, rwkv_w_v, rwkv_w_o, rwkv_w0, rwkv_w1, rwkv_w2,
              rwkv_a0, rwkv_a1, rwkv_a2, rwkv_g1, rwkv_g2, rwkv_k_k, rwkv_k_a, rwkv_r_k,
              rwkv_ln_w, rwkv_ln_b, na_w_qkv, na_w_o, na_q_gain, na_k_gain, na_rpb,
              ffn_w1, ffn_w3, ffn_w2):
    rows = x.shape[1] // GRID_W
    for i in range(DEPTH):
        last = i == DEPTH - 1
        j = i // N_MIXERS
        m = [t[:, None, :] for t in _ada(c, ada_w[i], ada_b[i])]
        mc = _ada(c_ctx, ada_w[i], ada_b[i])
        h_lat = _modulate(x, norm1[i], m[0], m[1])
        h_ctx = _modulate(ctx, norm1[i], mc[0], mc[1])
        if i % N_MIXERS == 0:
            p = dict(mu=rwkv_mu[j], w_r=rwkv_w_r[j], w_k=rwkv_w_k[j], w_v=rwkv_w_v[j], w_o=rwkv_w_o[j],
                     w0=rwkv_w0[j], w1=rwkv_w1[j], w2=rwkv_w2[j], a0=rwkv_a0[j], a1=rwkv_a1[j], a2=rwkv_a2[j],
                     g1=rwkv_g1[j], g2=rwkv_g2[j], k_k=rwkv_k_k[j], k_a=rwkv_k_a[j], r_k=rwkv_r_k[j],
                     ln_w=rwkv_ln_w[j], ln_b=rwkv_ln_b[j])
            y_lat, y_ctx = _rwkv_mixer(h_lat, h_ctx, rows, p, not last)
        else:
            p = dict(w_qkv=na_w_qkv[j], w_o=na_w_o[j], q_gain=na_q_gain[j], k_gain=na_k_gain[j], rpb=na_rpb[j])
            y_lat, y_ctx = _na_mixer(h_lat, h_ctx, rows, p, not last)
        x = x + m[2] * y_lat
        x = x + m[5] * _swiglu(_modulate(x, norm2[i], m[3], m[4]), ffn_w1[i], ffn_w3[i], ffn_w2[i])
        if not last:
            ctx = ctx + mc[2] * y_ctx
            ctx = ctx + mc[5] * _swiglu(_modulate(ctx, norm2[i], mc[3], mc[4]), ffn_w1[i], ffn_w3[i], ffn_w2[i])
    return x
```

```python
import functools
import math

import numpy as np
import jax
import jax.numpy as jnp
from jax import lax
from jax.experimental import pallas as pl
from jax.experimental.pallas import tpu as pltpu

F32 = jnp.float32
BF16 = jnp.bfloat16

GRID_W = 64
RMS_EPS = 1e-6
RWKV_HEAD = 64
GN_EPS = 64e-5
NA_HEAD = 128
WIN_H = 8
WIN_W = 16
ROPE_THETA = 10000.0
NEG_INF = -1e30

LANE = 128
MXU_DIM = 256
VMEM_LIMIT = 56 * 1024 * 1024

CHUNK = 64
PAIR = 2 * RWKV_HEAD
SCAN_PAIRS = 4
LORA_BLK = 256
TN1 = 512


def _cparams(sem):
    return pltpu.CompilerParams(dimension_semantics=sem, vmem_limit_bytes=VMEM_LIMIT)


def _dot(a, b):
    return jnp.dot(a, b, preferred_element_type=F32)


def _split2(x):
    hi = x.astype(BF16)
    lo = (x - hi.astype(F32)).astype(BF16)
    return hi, lo


def _dot_hilo(a, b_bf16):
    hi, lo = _split2(a)
    return _dot(hi, b_bf16) + _dot(lo, b_bf16)


def _rms_scale(x):
    return lax.rsqrt(jnp.mean(x * x, axis=-1, keepdims=True) + RMS_EPS)


def _modulate(x, gain, shift, scale):
    return (x * _rms_scale(x) * gain) * (1.0 + scale) + shift


def _sigmoid(x):
    return 1.0 / (1.0 + jnp.exp(-x))


def _silu(x):
    return x * _sigmoid(x)


def _group_ones(n, group):
    idx = np.arange(n) // group
    return jnp.asarray((idx[:, None] == idx[None, :]).astype(np.float32), dtype=BF16)


def _ada_kernel(c_ref, w_ref, b_ref, o_ref):
    s = _silu(c_ref[...]).astype(BF16)
    o_ref[...] = _dot(s, w_ref[...].astype(BF16)) + b_ref[...]


def _ada(cvec, ada_w, ada_b):
    depth, D, N = ada_w.shape
    R = cvec.shape[0]
    tn = 1024
    return pl.pallas_call(
        _ada_kernel,
        grid=(depth, N // tn),
        in_specs=[
            pl.BlockSpec((R, D), lambda l, n: (0, 0)),
            pl.BlockSpec((None, D, tn), lambda l, n: (l, 0, n)),
            pl.BlockSpec((None, 1, tn), lambda l, n: (l, 0, n)),
        ],
        out_specs=pl.BlockSpec((None, R, tn), lambda l, n: (l, 0, n)),
        out_shape=jax.ShapeDtypeStruct((depth, R, N), F32),
        compiler_params=_cparams(("arbitrary", "arbitrary")),
        name="ada",
    )(cvec, ada_w, ada_b.reshape(depth, 1, N))


def _s1_kernel(xp_ref, x_ref, xn_ref, g_ref, sh_ref, sc_ref, mu_ref, kkw_ref, ones_ref, w_ref,
               o_ref, kk_ref, a_scr, *, latent, rows, tm, D, nb):
    i = pl.program_id(0)
    n = pl.program_id(1)

    @pl.when(n == 0)
    def _():
        g = g_ref[...]
        sh = sh_ref[...]
        sc = sc_ref[...]
        h = _modulate(x_ref[...], g, sh, sc)
        if latent:
            q = D // 4
            tok = lax.broadcasted_iota(jnp.int32, (tm, q), 0)
            col = tok & (GRID_W - 1)
            grow = (i * (tm // GRID_W) + tok // GRID_W) & (rows - 1)

            def halo(ref, lo):
                xh = ref[...]
                sl = slice(lo, lo + q)
                return (xh[:, sl] * _rms_scale(xh) * g[:, sl]) * (1.0 + sc[:, sl]) + sh[:, sl]

            left = jnp.where(col > 0, pltpu.roll(h[:, 0:q], 1, axis=0), 0.0)
            right = jnp.where(col < GRID_W - 1, pltpu.roll(h[:, q:2 * q], tm - 1, axis=0), 0.0)
            up = jnp.concatenate([halo(xp_ref, 2 * q), h[:tm - GRID_W, 2 * q:3 * q]], axis=0)
            up = jnp.where(grow > 0, up, 0.0)
            down = jnp.concatenate([h[GRID_W:, 3 * q:], halo(xn_ref, 3 * q)], axis=0)
            down = jnp.where(grow < rows - 1, down, 0.0)
            hs = jnp.concatenate([left, right, up, down], axis=1)
        else:
            half = D // 2
            tok = lax.broadcasted_iota(jnp.int32, (tm, half), 0)
            prev = jnp.where(tok > 0, pltpu.roll(h[:, :half], 1, axis=0), 0.0)
            nxt = jnp.where(tok < tm - 1, pltpu.roll(h[:, half:], tm - 1, axis=0), 0.0)
            hs = jnp.concatenate([prev, nxt], axis=1)
        xx = hs - h
        for j in range(6):
            a_scr[j] = (h + xx * mu_ref[j:j + 1, :]).astype(BF16)

    @pl.when(n < nb)
    def _():
        o_ref[...] = _dot(a_scr[0], w_ref[...])

    @pl.when((n >= nb) & (n < 2 * nb))
    def _():
        acc = _dot(a_scr[2], w_ref[...])
        o_ref[...] = acc
        kkf = acc * kkw_ref[...]
        sq = kkf * kkf
        ones = ones_ref[...]
        parts = []
        for s in range(TN1 // MXU_DIM):
            parts.append(_dot_hilo(sq[:, s * MXU_DIM:(s + 1) * MXU_DIM], ones))
        ss = jnp.concatenate(parts, axis=1)
        kk_ref[...] = kkf / jnp.maximum(jnp.sqrt(ss), 1e-12)

    @pl.when((n >= 2 * nb) & (n < 3 * nb))
    def _():
        o_ref[...] = _dot(a_scr[3], w_ref[...])

    @pl.when(n == 3 * nb)
    def _():
        o_ref[:, :LORA_BLK] = jnp.tanh(_dot(a_scr[1], w_ref[:, :LORA_BLK]))
        o_ref[:, LORA_BLK:] = _dot(a_scr[4], w_ref[:, LORA_BLK:])

    @pl.when(n == 3 * nb + 1)
    def _():
        o_ref[...] = _sigmoid(_dot(a_scr[5], w_ref[...]))


def _rwkv_stage1(x, mod, row_fn, gain, mu, k_k, wcat, *, latent, rows, tm):
    M, D = x.shape
    nb = D // TN1
    N1 = wcat.shape[1]
    nsteps = N1 // TN1
    hb = tm // GRID_W if latent else 1
    hrows = GRID_W if latent else 8
    nhb = M // hrows
    kcol = lambda n: jnp.clip(n - nb, 0, nb - 1)
    kern = functools.partial(_s1_kernel, latent=latent, rows=rows, tm=tm, D=D, nb=nb)
    return pl.pallas_call(
        kern,
        grid=(M // tm, nsteps),
        in_specs=[
            pl.BlockSpec((hrows, D), lambda i, n: (jnp.maximum(i * hb - 1, 0), 0)),
            pl.BlockSpec((tm, D), lambda i, n: (i, 0)),
            pl.BlockSpec((hrows, D), lambda i, n: (jnp.minimum(i * hb + hb, nhb - 1), 0)),
            pl.BlockSpec((1, D), lambda i, n: (0, 0)),
            pl.BlockSpec((None, 1, D), lambda i, n: (row_fn(i), 0, 0)),
            pl.BlockSpec((None, 1, D), lambda i, n: (row_fn(i), 0, 1)),
            pl.BlockSpec((6, D), lambda i, n: (0, 0)),
            pl.BlockSpec((1, TN1), lambda i, n: (0, kcol(n))),
            pl.BlockSpec((MXU_DIM, MXU_DIM), lambda i, n: (0, 0)),
            pl.BlockSpec((D, TN1), lambda i, n: (0, n)),
        ],
        out_specs=[
            pl.BlockSpec((tm, TN1), lambda i, n: (i, n)),
            pl.BlockSpec((tm, TN1), lambda i, n: (i, kcol(n))),
        ],
        out_shape=[jax.ShapeDtypeStruct((M, N1), F32), jax.ShapeDtypeStruct((M, D), F32)],
        scratch_shapes=[pltpu.VMEM((6, tm, D), BF16)],
        compiler_params=_cparams(("arbitrary", "arbitrary")),
        name="rwkv_stage1",
    )(x, x, x, gain, mod, mod, mu, k_k, _group_ones(MXU_DIM, RWKV_HEAD), wcat)


def _s2_kernel(lo1_ref, lo2_ref, w_ref, b_ref, o_ref):
    n = pl.program_id(1)
    w = w_ref[...]
    b = b_ref[...]

    @pl.when(n < 2)
    def _():
        z = _dot(lo1_ref[:, :LORA_BLK].astype(BF16), w) + b
        u = -z
        softplus = jnp.maximum(u, 0.0) + jnp.log(1.0 + jnp.exp(-jnp.abs(u)))
        o_ref[...] = -jnp.exp(-softplus - 0.5)

    @pl.when((n >= 2) & (n < 4))
    def _():
        o_ref[...] = _sigmoid(_dot(lo1_ref[:, LORA_BLK:].astype(BF16), w) + b)

    @pl.when(n == 4)
    def _():
        o_ref[...] = _dot(lo2_ref[:, :LORA_BLK].astype(BF16), w)


def _rwkv_stage2(s1, w2cat, bias2, D, tm):
    M = s1.shape[0]
    c0 = 3 * D // (2 * LORA_BLK)
    return pl.pallas_call(
        _s2_kernel,
        grid=(M // tm, 5),
        in_specs=[
            pl.BlockSpec((tm, 2 * LORA_BLK), lambda i, n: (i, c0)),
            pl.BlockSpec((tm, 2 * LORA_BLK), lambda i, n: (i, c0 + 1)),
            pl.BlockSpec((LORA_BLK, D), lambda i, n: (0, n)),
            pl.BlockSpec((1, D), lambda i, n: (0, n)),
        ],
        out_specs=pl.BlockSpec((tm, D), lambda i, n: (i, n)),
        out_shape=jax.ShapeDtypeStruct((M, 5 * D), F32),
        compiler_params=_cparams(("arbitrary", "arbitrary")),
        name="rwkv_stage2",
    )(s1, s1, w2cat, bias2)


_NT = (((1,), (1,)), ((), ()))
_TN = (((0,), (0,)), ((), ()))


def _mm(a, b, dims=None):
    a = a.astype(BF16)
    b = b.astype(BF16)
    if dims is None:
        return jnp.dot(a, b, preferred_element_type=F32)
    return lax.dot_general(a, b, dims, preferred_element_type=F32)


def _scan_pair(r, k, v, kk, a, ld, ka, H, sgn, masks):
    L = CHUNK
    m0, tri, strict, incl, bd = masks
    p0 = ld.astype(BF16)
    r1 = ld - p0.astype(F32)
    p1 = r1.astype(BF16)
    p2 = (r1 - p1.astype(F32)).astype(BF16)
    cum = _dot(tri, p0) + _dot(tri, p1) + _dot(tri, p2)
    Pt = jnp.exp(cum)
    Pprev = jnp.exp(cum - ld)
    invP = jnp.exp(-cum)
    PL = jnp.exp(jnp.sum(ld, axis=0, keepdims=True))
    kd = k * (1.0 + (a - 1.0) * ka)
    b = kk * a
    At = -kk * Pprev
    Bt = b * invP
    Kt = kd * invP
    Rt = r * Pt

    def sm(x):
        return jnp.concatenate([jnp.where(m0, x, 0.0), jnp.where(m0, 0.0, x)], axis=0)

    lhs = jnp.concatenate([sm(At), sm(Rt)], axis=0)
    rhs = jnp.concatenate([sm(Bt), sm(Kt)], axis=0)
    G = _mm(lhs, rhs, _NT)
    N = jnp.where(strict, G[:2 * L, :2 * L], 0.0)
    Aak = jnp.where(strict, G[:2 * L, 2 * L:], 0.0)
    Mrb = jnp.where(incl, G[2 * L:, :2 * L], 0.0)
    Mrk = jnp.where(incl, G[2 * L:, 2 * L:], 0.0)
    Vs = sm(v)
    AH = _mm(lhs, H)
    X = AH[:2 * L] + _mm(Aak, Vs)
    Np = N
    nsteps = int(math.log2(L))
    for s in range(nsteps):
        X = X + _mm(Np, X)
        if s < nsteps - 1:
            Np = _mm(Np, Np)
    Ys = AH[2 * L:] + _mm(jnp.concatenate([Mrb, Mrk], axis=1), jnp.concatenate([X, Vs], axis=0))
    y = Ys[:L] + Ys[L:]
    U = X[:L] + X[L:]
    upd = _mm(jnp.concatenate([Bt * PL, Kt * PL], axis=0), jnp.concatenate([U, v], axis=0), _TN)
    Hn = H * jnp.transpose(jnp.broadcast_to(PL, (PAIR, PAIR))) + jnp.where(bd, upd, 0.0)
    return y, Hn


def _scan_kernel(r_ref, k_ref, v_ref, kk_ref, ld_ref, a_ref, ka_ref, h0_ref, y_ref, hT_ref, h_scr, *, G):
    d = pl.program_id(0)
    c = pl.program_id(3)
    nc = pl.num_programs(3)
    L = CHUNK

    @pl.when(c == 0)
    def _():
        h_scr[...] = h0_ref[...]

    sgn = 1 - 2 * d
    lane = lax.broadcasted_iota(jnp.int32, (L, PAIR), 1)
    m0 = lane < RWKV_HEAD
    ti = lax.broadcasted_iota(jnp.int32, (L, L), 0)
    si = lax.broadcasted_iota(jnp.int32, (L, L), 1)
    tri = jnp.where((si - ti) * sgn <= 0, 1.0, 0.0).astype(BF16)
    i2 = lax.broadcasted_iota(jnp.int32, (2 * L, 2 * L), 0)
    j2 = lax.broadcasted_iota(jnp.int32, (2 * L, 2 * L), 1)
    same = (i2 // L) == (j2 // L)
    dif = ((j2 % L) - (i2 % L)) * sgn
    strict = jnp.where(same, dif, 1) < 0
    incl = jnp.where(same, dif, 1) <= 0
    r2 = lax.broadcasted_iota(jnp.int32, (PAIR, PAIR), 0)
    c2 = lax.broadcasted_iota(jnp.int32, (PAIR, PAIR), 1)
    bd = (r2 // RWKV_HEAD) == (c2 // RWKV_HEAD)
    masks = (m0, tri, strict, incl, bd)

    for g in range(G):
        sl = slice(g * PAIR, (g + 1) * PAIR)
        y, Hn = _scan_pair(r_ref[:, sl], k_ref[:, sl], v_ref[:, sl], kk_ref[:, sl], a_ref[:, sl],
                           ld_ref[:, sl], ka_ref[:, sl], h_scr[g], sgn, masks)
        y_ref[:, sl] = y
        h_scr[g] = Hn

    @pl.when(c == nc - 1)
    def _():
        hT_ref[...] = h_scr[...]


def _wkv_scan(s1, kk, s2, k_a, h0, B, T, D):
    G = SCAN_PAIRS
    W = G * PAIR
    L = CHUNK
    nc = T // L
    npg = D // W
    N1 = s1.shape[1]
    s1 = s1.reshape(B, T, N1)
    kk = kk.reshape(B, T, D)
    s2 = s2.reshape(B, T, 5 * D)
    chunk = lambda d, c: c + d * (nc - 1 - 2 * c)
    tok = lambda off: pl.BlockSpec((None, L, W), lambda d, b, p, c: (b, chunk(d, c), off + p))
    y, hT = pl.pallas_call(
        functools.partial(_scan_kernel, G=G),
        grid=(2, B, npg, nc),
        in_specs=[
            tok(0), tok(npg), tok(2 * npg),
            pl.BlockSpec((None, L, W), lambda d, b, p, c: (b, chunk(d, c), p)),
            pl.BlockSpec((None, L, W), lambda d, b, p, c: (b, chunk(d, c), d * npg + p)),
            pl.BlockSpec((None, L, W), lambda d, b, p, c: (b, chunk(d, c), (2 + d) * npg + p)),
            pl.BlockSpec((1, W), lambda d, b, p, c: (0, p)),
            pl.BlockSpec((None, None, G, PAIR, PAIR), lambda d, b, p, c: (d, b, p, 0, 0)),
        ],
        out_specs=[
            pl.BlockSpec((None, None, L, W), lambda d, b, p, c: (d, b, chunk(d, c), p)),
            pl.BlockSpec((None, None, G, PAIR, PAIR), lambda d, b, p, c: (d, b, p, 0, 0)),
        ],
        out_shape=[jax.ShapeDtypeStruct((2, B, T, D), F32),
                   jax.ShapeDtypeStruct((2, B, D // PAIR, PAIR, PAIR), F32)],
        scratch_shapes=[pltpu.VMEM((G, PAIR, PAIR), F32)],
        compiler_params=_cparams(("arbitrary", "arbitrary", "arbitrary", "arbitrary")),
        name="wkv_scan",
    )(s1, s1, s1, kk, s2, s2, k_a, h0)
    return y.reshape(2, B * T, D), hT


def _rwkv_out_kernel(y0_ref, y1_ref, r_ref, k_ref, v_ref, a0_ref, a1_ref, g_ref,
                     lnw_ref, lnb_ref, rk_ref, ka_ref, ones_ref, w_ref, x_ref, gate_ref,
                     o_ref, a_scr, *, D):
    n = pl.program_id(1)

    @pl.when(n == 0)
    def _():
        ones = ones_ref[...]
        inv = 1.0 / RWKV_HEAD

        def gsum(t):
            parts = [_dot_hilo(t[:, s * MXU_DIM:(s + 1) * MXU_DIM], ones) for s in range(D // MXU_DIM)]
            return jnp.concatenate(parts, axis=1)

        wkv = y0_ref[...] + y1_ref[...]
        cen = wkv - gsum(wkv) * inv
        var = gsum(cen * cen) * inv
        yn = cen * lax.rsqrt(var + GN_EPS) * lnw_ref[...] + lnb_ref[...]
        ks = k_ref[...] * (2.0 + (a0_ref[...] + a1_ref[...] - 2.0) * ka_ref[...])
        bonus = gsum(r_ref[...] * ks * rk_ref[...]) * v_ref[...]
        a_scr[...] = ((yn + bonus) * g_ref[...]).astype(BF16)

    o_ref[...] = x_ref[...] + gate_ref[...] * _dot(a_scr[...], w_ref[...])


def _rwkv_out(y, s1, s2, x, mod, row_fn, ln_w, ln_b, r_k, k_a, w_o, tm, tn):
    M, D = x.shape
    big = lambda col: pl.BlockSpec((tm, D), lambda i, n: (i, col))
    row = pl.BlockSpec((1, D), lambda i, n: (0, 0))
    return pl.pallas_call(
        functools.partial(_rwkv_out_kernel, D=D),
        grid=(M // tm, D // tn),
        in_specs=[
            pl.BlockSpec((None, tm, D), lambda i, n: (0, i, 0)),
            pl.BlockSpec((None, tm, D), lambda i, n: (1, i, 0)),
            big(0), big(1), big(2),
            big(2), big(3), big(4),
            row, row, row, row,
            pl.BlockSpec((MXU_DIM, MXU_DIM), lambda i, n: (0, 0)),
            pl.BlockSpec((D, tn), lambda i, n: (0, n)),
            pl.BlockSpec((tm, tn), lambda i, n: (i, n)),
            pl.BlockSpec((None, 1, tn), lambda i, n: (row_fn(i), 0, 2 * (D // tn) + n)),
        ],
        out_specs=pl.BlockSpec((tm, tn), lambda i, n: (i, n)),
        out_shape=jax.ShapeDtypeStruct((M, D), F32),
        scratch_shapes=[pltpu.VMEM((tm, D), BF16)],
        compiler_params=_cparams(("arbitrary", "arbitrary")),
        name="rwkv_out",
    )(y, y, s1, s1, s1, s2, s2, s2, ln_w, ln_b, r_k, k_a, _group_ones(MXU_DIM, RWKV_HEAD), w_o, x, mod)


def _ffn_kernel(x_ref, g_ref, sh_ref, sc_ref, gate_ref, w1_ref, w3_ref, w2_ref, o_ref, a_scr, acc_scr):
    f = pl.program_id(1)
    nf = pl.num_programs(1)

    @pl.when(f == 0)
    def _():
        a_scr[...] = _modulate(x_ref[...], g_ref[...], sh_ref[...], sc_ref[...]).astype(BF16)
        acc_scr[...] = jnp.zeros_like(acc_scr)

    a = a_scr[...]
    h1 = _dot(a, w1_ref[...])
    h3 = _dot(a, w3_ref[...])
    act = (_silu(h1) * h3).astype(BF16)
    acc_scr[...] += _dot(act, w2_ref[...])

    @pl.when(f == nf - 1)
    def _():
        o_ref[...] = x_ref[...] + gate_ref[...] * acc_scr[...]


def _ffn(x, mod, row_fn, gain, w1, w3, w2, tm, tf):
    M, D = x.shape
    Fdim = w1.shape[1]
    modspec = lambda ch: pl.BlockSpec((None, 1, D), lambda i, f: (row_fn(i), 0, ch))
    return pl.pallas_call(
        _ffn_kernel,
        grid=(M // tm, Fdim // tf),
        in_specs=[
            pl.BlockSpec((tm, D), lambda i, f: (i, 0)),
            pl.BlockSpec((1, D), lambda i, f: (0, 0)),
            modspec(3), modspec(4), modspec(5),
            pl.BlockSpec((D, tf), lambda i, f: (0, f)),
            pl.BlockSpec((D, tf), lambda i, f: (0, f)),
            pl.BlockSpec((tf, D), lambda i, f: (f, 0)),
        ],
        out_specs=pl.BlockSpec((tm, D), lambda i, f: (i, 0)),
        out_shape=jax.ShapeDtypeStruct((M, D), F32),
        scratch_shapes=[pltpu.VMEM((tm, D), BF16), pltpu.VMEM((tm, D), F32)],
        compiler_params=_cparams(("arbitrary", "arbitrary")),
        name="ffn",
    )(x, gain, mod, mod, mod, w1, w3, w2)


def _qkv_kernel(x_ref, g_ref, sh_ref, sc_ref, w_ref, hg_ref, cos_ref, sin_ref, o_ref, a_scr, *, nrope, nnorm, tn):
    n = pl.program_id(1)

    @pl.when(n == 0)
    def _():
        a_scr[...] = _modulate(x_ref[...], g_ref[...], sh_ref[...], sc_ref[...]).astype(BF16)

    acc = _dot(a_scr[...], w_ref[...])

    @pl.when(n < nnorm)
    def _():
        hg = hg_ref[...]
        lane = lax.broadcasted_iota(jnp.int32, (acc.shape[0], NA_HEAD), 1)
        first = (lane & (NA_HEAD // 2 - 1)) < NA_HEAD // 4
        for s in range(tn // NA_HEAD):
            t = acc[:, s * NA_HEAD:(s + 1) * NA_HEAD]
            t = t * _rms_scale(t) * hg
            if nrope:
                swapped = jnp.where(first, pltpu.roll(t, NA_HEAD - NA_HEAD // 4, axis=1),
                                    pltpu.roll(t, NA_HEAD // 4, axis=1))
                roped = t * cos_ref[...] + swapped * sin_ref[...]
                t = jnp.where(n < nrope, roped, t)
            o_ref[:, s * NA_HEAD:(s + 1) * NA_HEAD] = t.astype(o_ref.dtype)

    @pl.when(n >= nnorm)
    def _():
        o_ref[...] = acc.astype(o_ref.dtype)


def _na_qkv(x, mod, row_fn, gain, w, head_gain, cos, sin, *, nrope, nnorm, tm, tn):
    M, D = x.shape
    N = w.shape[1]
    tpb = cos.shape[0] // tm
    hrow = lambda n: jnp.minimum(n, head_gain.shape[0] - 1)
    return pl.pallas_call(
        functools.partial(_qkv_kernel, nrope=nrope, nnorm=nnorm, tn=tn),
        grid=(M // tm, N // tn),
        in_specs=[
            pl.BlockSpec((tm, D), lambda i, n: (i, 0)),
            pl.BlockSpec((1, D), lambda i, n: (0, 0)),
            pl.BlockSpec((None, 1, D), lambda i, n: (row_fn(i), 0, 0)),
            pl.BlockSpec((None, 1, D), lambda i, n: (row_fn(i), 0, 1)),
            pl.BlockSpec((D, tn), lambda i, n: (0, n)),
            pl.BlockSpec((None, 1, NA_HEAD), lambda i, n: (hrow(n), 0, 0)),
            pl.BlockSpec((tm, NA_HEAD), lambda i, n: (i % tpb, 0)),
            pl.BlockSpec((tm, NA_HEAD), lambda i, n: (i % tpb, 0)),
        ],
        out_specs=pl.BlockSpec((tm, tn), lambda i, n: (i, n)),
        out_shape=jax.ShapeDtypeStruct((M, N), BF16),
        scratch_shapes=[pltpu.VMEM((tm, D), BF16)],
        compiler_params=_cparams(("arbitrary", "arbitrary")),
        name="na_qkv",
    )(x, gain, mod, mod, w, head_gain, cos, sin)


def _na_kernel(q_ref, k_ref, v_ref, kc_ref, vc_ref, bias_ref, o_ref, *, rows, rb):
    j0 = pl.program_id(2) * rb
    win_h = min(WIN_H, rows)
    kc = kc_ref[...]
    vc = vc_ref[...]

    def body(j, carry):
        r = j0 + j
        r0 = jnp.clip(r - WIN_H // 2, 0, rows - win_h)
        q = q_ref[pl.ds(pl.multiple_of(j * GRID_W, GRID_W), GRID_W), :]
        ks = k_ref[pl.ds(pl.multiple_of(r0 * GRID_W, GRID_W), win_h * GRID_W), :]
        vs = v_ref[pl.ds(pl.multiple_of(r0 * GRID_W, GRID_W), win_h * GRID_W), :]
        s_lat = lax.dot_general(q, ks, _NT, preferred_element_type=F32) + bias_ref[r0 - r + WIN_H - 1]
        s_ctx = lax.dot_general(q, kc, _NT, preferred_element_type=F32)
        m = jnp.maximum(jnp.max(s_lat, axis=-1, keepdims=True), jnp.max(s_ctx, axis=-1, keepdims=True))
        p_lat = jnp.exp(s_lat - m)
        p_ctx = jnp.exp(s_ctx - m)
        den = jnp.sum(p_lat, axis=-1, keepdims=True) + jnp.sum(p_ctx, axis=-1, keepdims=True)
        o = _dot(p_lat.astype(BF16), vs) + _dot(p_ctx.astype(BF16), vc)
        o_ref[pl.ds(pl.multiple_of(j * GRID_W, GRID_W), GRID_W), :] = (o / den).astype(o_ref.dtype)
        return carry

    lax.fori_loop(0, rb, body, 0)


def _na_attention(qkv, kvc, bias, B, T, D, rb):
    H = D // NA_HEAD
    rows = T // GRID_W
    C = kvc.shape[1]
    nbias = bias.shape[1]
    return pl.pallas_call(
        functools.partial(_na_kernel, rows=rows, rb=rb),
        grid=(B, H, rows // rb),
        in_specs=[
            pl.BlockSpec((None, rb * GRID_W, NA_HEAD), lambda b, h, j: (b, j, h)),
            pl.BlockSpec((None, T, NA_HEAD), lambda b, h, j: (b, 0, H + h)),
            pl.BlockSpec((None, T, NA_HEAD), lambda b, h, j: (b, 0, 2 * H + h)),
            pl.BlockSpec((None, C, NA_HEAD), lambda b, h, j: (b, 0, h)),
            pl.BlockSpec((None, C, NA_HEAD), lambda b, h, j: (b, 0, H + h)),
            pl.BlockSpec((None, nbias, GRID_W, bias.shape[3]), lambda b, h, j: (h, 0, 0, 0)),
        ],
        out_specs=pl.BlockSpec((None, rb * GRID_W, NA_HEAD), lambda b, h, j: (b, j, h)),
        out_shape=jax.ShapeDtypeStruct((B, T, D), BF16),
        compiler_params=_cparams(("arbitrary", "arbitrary", "arbitrary")),
        name="na_attention",
    )(qkv, qkv, qkv, kvc, kvc, bias)


def _proj_res_kernel(a_ref, w_ref, x_ref, gate_ref, o_ref):
    o_ref[...] = x_ref[...] + gate_ref[...] * _dot(a_ref[...], w_ref[...])


def _proj_residual(a, w, x, mod, row_fn, chunk, tm, tn):
    M, K = a.shape
    N = w.shape[1]
    return pl.pallas_call(
        _proj_res_kernel,
        grid=(M // tm, N // tn),
        in_specs=[
            pl.BlockSpec((tm, K), lambda i, n: (i, 0)),
            pl.BlockSpec((K, tn), lambda i, n: (0, n)),
            pl.BlockSpec((tm, tn), lambda i, n: (i, n)),
            pl.BlockSpec((None, 1, tn), lambda i, n: (row_fn(i), 0, chunk * (N // tn) + n)),
        ],
        out_specs=pl.BlockSpec((tm, tn), lambda i, n: (i, n)),
        out_shape=jax.ShapeDtypeStruct((M, N), F32),
        compiler_params=_cparams(("arbitrary", "arbitrary")),
        name="proj_residual",
    )(a, w, x, mod)


def _rope_tables(T):
    t = np.arange(T)
    half = NA_HEAD // 2
    freqs = ROPE_THETA ** (-np.arange(0, half, 2, dtype=np.float64) / half)
    ang_r = (t // GRID_W).astype(np.float64)[:, None] * freqs
    ang_c = (t % GRID_W).astype(np.float64)[:, None] * freqs
    cos = np.concatenate([np.cos(ang_r), np.cos(ang_r), np.cos(ang_c), np.cos(ang_c)], axis=1)
    sin = np.concatenate([-np.sin(ang_r), np.sin(ang_r), -np.sin(ang_c), np.sin(ang_c)], axis=1)
    return jnp.asarray(cos, F32), jnp.asarray(sin, F32)


def _na_bias(rpb, rows):
    win_h = min(WIN_H, rows)
    qc = np.arange(GRID_W)[:, None]
    kc = np.arange(GRID_W)[None, :]
    wstart = np.clip(qc - WIN_W // 2, 0, GRID_W - WIN_W)
    valid = (kc >= wstart) & (kc < wstart + WIN_W)
    rel = np.clip(kc - qc, -(WIN_W - 1), WIN_W - 1) + WIN_W - 1
    g = rpb[:, :, rel]
    g = jnp.where(valid[None, None], g, NEG_INF)
    tabs = []
    for i0 in range(2 * WIN_H - win_h):
        blk = g[:, i0:i0 + win_h]
        tabs.append(jnp.transpose(blk, (0, 2, 1, 3)).reshape(g.shape[0], GRID_W, win_h * GRID_W))
    return jnp.stack(tabs, axis=1)


def _pad_cols(w, n):
    return jnp.pad(w, ((0, 0), (0, n - w.shape[1])))


def _pad_rows(w, n):
    return jnp.pad(w, ((0, n - w.shape[0]), (0, 0)))


def kernel(x, c, ctx, c_ctx, ada_w, ada_b, norm1, norm2, rwkv_mu, rwkv_w_r, rwkv_w_k, rwkv_w_v, rwkv_w_o, rwkv_w0, rwkv_w1, rwkv_w2, rwkv_a0, rwkv_a1, rwkv_a2, rwkv_g1, rwkv_g2, rwkv_k_k, rwkv_k_a, rwkv_r_k, rwkv_ln_w, rwkv_ln_b, na_w_qkv, na_w_o, na_q_gain, na_k_gain, na_rpb, ffn_w1, ffn_w3, ffn_w2):
    B, T, D = x.shape
    C = ctx.shape[1]
    rows = T // GRID_W
    depth = ada_w.shape[0]
    n_mix = 2
    TM = 512
    TMO = 256
    TNO = min(1024, D)
    TF = 512 if ffn_w1.shape[-1] % 512 == 0 else 256

    cvec = jnp.concatenate([c, c_ctx[None, :], jnp.zeros((8 - B - 1, D), F32)], axis=0)
    mods = _ada(cvec, ada_w, ada_b)
    lat_row = lambda i: i // (T // TM)
    ctx_row = lambda i: B

    xl = x.reshape(B * T, D)
    xc = ctx.reshape(B * C, D)

    for i in range(depth):
        last = i == depth - 1
        j = i // n_mix
        mod = mods[i].reshape(8, 1, 6 * D)
        g1 = norm1[i].reshape(1, D)
        g2 = norm2[i].reshape(1, D)
        w1 = ffn_w1[i].astype(BF16)
        w3 = ffn_w3[i].astype(BF16)
        w2 = ffn_w2[i].astype(BF16)
        if i % n_mix == 0:
            lw = rwkv_w1.shape[-1]
            la = rwkv_a1.shape[-1]
            lg = rwkv_g1.shape[-1]
            wcat = jnp.concatenate([
                rwkv_w_r[j], rwkv_w_k[j], rwkv_w_v[j],
                _pad_cols(jnp.concatenate([rwkv_w1[j, 0], rwkv_w1[j, 1]], axis=1), LORA_BLK),
                _pad_cols(jnp.concatenate([rwkv_a1[j, 0], rwkv_a1[j, 1]], axis=1), LORA_BLK),
                _pad_cols(rwkv_g1[j], 2 * LORA_BLK)], axis=1).astype(BF16)
            z = lambda n: jnp.zeros((n, D), F32)
            w2cat = jnp.concatenate([
                _pad_rows(rwkv_w2[j, 0], LORA_BLK),
                _pad_rows(jnp.concatenate([z(lw), rwkv_w2[j, 1]], axis=0), LORA_BLK),
                _pad_rows(rwkv_a2[j, 0], LORA_BLK),
                _pad_rows(jnp.concatenate([z(la), rwkv_a2[j, 1]], axis=0), LORA_BLK),
                _pad_rows(rwkv_g2[j], LORA_BLK)], axis=1).astype(BF16)
            bias2 = jnp.concatenate([rwkv_w0[j, 0], rwkv_w0[j, 1], rwkv_a0[j, 0], rwkv_a0[j, 1],
                                     jnp.zeros((D,), F32)]).reshape(1, 5 * D)
            k_k = rwkv_k_k[j].reshape(1, D)
            k_a = rwkv_k_a[j].reshape(1, D)
            r_k = rwkv_r_k[j].reshape(1, D)
            ln_w = rwkv_ln_w[j].reshape(1, D)
            ln_b = rwkv_ln_b[j].reshape(1, D)
            w_o = rwkv_w_o[j].astype(BF16)

            s1c, kkc = _rwkv_stage1(xc, mod, ctx_row, g1, rwkv_mu[j], k_k, wcat, latent=False, rows=rows, tm=C)
            s2c = _rwkv_stage2(s1c, w2cat, bias2, D, C)
            h0 = jnp.zeros((2, B, D // PAIR, PAIR, PAIR), F32)
            yc, hc = _wkv_scan(s1c, kkc, s2c, k_a, h0, B, C, D)
            s1l, kkl = _rwkv_stage1(xl, mod, lat_row, g1, rwkv_mu[j], k_k, wcat, latent=True, rows=rows, tm=TM)
            s2l = _rwkv_stage2(s1l, w2cat, bias2, D, TM)
            yl, _ = _wkv_scan(s1l, kkl, s2l, k_a, hc, B, T, D)
            xl = _rwkv_out(yl, s1l, s2l, xl, mod, lambda t: t // (T // TMO), ln_w, ln_b, r_k, k_a, w_o, TMO, TNO)
            if not last:
                xc = _rwkv_out(yc, s1c, s2c, xc, mod, ctx_row, ln_w, ln_b, r_k, k_a, w_o, min(TMO, C), TNO)
        else:
            H = D // NA_HEAD
            wqkv = na_w_qkv[j].astype(BF16)
            cos, sin = _rope_tables(T)
            scale = NA_HEAD ** -0.5
            hg = jnp.stack([jnp.broadcast_to(na_q_gain[j] * scale, (NA_HEAD,)), na_k_gain[j]]).reshape(2, 1, NA_HEAD)
            tn = 512
            nq = D // tn
            hg_blocks = jnp.concatenate([jnp.broadcast_to(hg[0], (nq, 1, NA_HEAD)),
                                         jnp.broadcast_to(hg[1], (nq, 1, NA_HEAD))], axis=0)
            qkv = _na_qkv(xl, mod, lat_row, g1, wqkv, hg_blocks, cos, sin, nrope=2 * nq, nnorm=2 * nq, tm=TM, tn=tn)
            kg_blocks = jnp.broadcast_to(hg[1], (nq, 1, NA_HEAD))
            kvc = _na_qkv(xc, mod, ctx_row, g1, wqkv[:, D:], kg_blocks, cos[:C], sin[:C],
                          nrope=0, nnorm=nq, tm=C, tn=tn)
            bias = _na_bias(na_rpb[j], rows)
            o = _na_attention(qkv.reshape(B, T, 3 * D), kvc.reshape(B, C, 2 * D), bias, B, T, D, 8)
            xl = _proj_residual(o.reshape(B * T, D), na_w_o[j].astype(BF16), xl, mod, lat_row, 2, TM, TNO)
            if not last:
                raise NotImplementedError("attention layer with context output")
        xl = _ffn(xl, mod, lat_row, g2, w1, w3, w2, TM, TF)
        if not last:
            xc = _ffn(xc, mod, ctx_row, g2, w1, w3, w2, C, TF)
    return xl.reshape(B, T, D)
```

```python
import functools
import math

import numpy as np
import jax
import jax.numpy as jnp
from jax import lax
from jax.experimental import pallas as pl
from jax.experimental.pallas import tpu as pltpu

F32 = jnp.float32
BF16 = jnp.bfloat16

GRID_W = 64
RMS_EPS = 1e-6
RWKV_HEAD = 64
GN_EPS = 64e-5
NA_HEAD = 128
WIN_H = 8
WIN_W = 16
ROPE_THETA = 10000.0
NEG_INF = -1e30

LANE = 128
MXU_DIM = 256
VMEM_LIMIT = 56 * 1024 * 1024

CHUNK = 64
PAIR = 2 * RWKV_HEAD
SCAN_PAIRS = 8
LORA_BLK = 256
TN1 = 512


def _cparams(sem):
    return pltpu.CompilerParams(dimension_semantics=sem, vmem_limit_bytes=VMEM_LIMIT)


def _dot(a, b):
    return jnp.dot(a, b, preferred_element_type=F32)


def _split2(x):
    hi = x.astype(BF16)
    lo = (x - hi.astype(F32)).astype(BF16)
    return hi, lo


def _dot_hilo(a, b_bf16):
    hi, lo = _split2(a)
    return _dot(hi, b_bf16) + _dot(lo, b_bf16)


def _rms_scale(x):
    return lax.rsqrt(jnp.mean(x * x, axis=-1, keepdims=True) + RMS_EPS)


def _modulate(x, gain, shift, scale):
    return (x * _rms_scale(x) * gain) * (1.0 + scale) + shift


def _sigmoid(x):
    return 1.0 / (1.0 + jnp.exp(-x))


def _silu(x):
    return x * _sigmoid(x)


def _group_ones(n, group):
    idx = np.arange(n) // group
    return jnp.asarray((idx[:, None] == idx[None, :]).astype(np.float32), dtype=BF16)


def _ada_kernel(c_ref, w_ref, b_ref, o_ref):
    s = _silu(c_ref[...]).astype(BF16)
    o_ref[...] = _dot(s, w_ref[...].astype(BF16)) + b_ref[...]


def _ada(cvec, ada_w, ada_b):
    depth, D, N = ada_w.shape
    R = cvec.shape[0]
    tn = 1024
    return pl.pallas_call(
        _ada_kernel,
        grid=(depth, N // tn),
        in_specs=[
            pl.BlockSpec((R, D), lambda l, n: (0, 0)),
            pl.BlockSpec((None, D, tn), lambda l, n: (l, 0, n)),
            pl.BlockSpec((None, 1, tn), lambda l, n: (l, 0, n)),
        ],
        out_specs=pl.BlockSpec((None, R, tn), lambda l, n: (l, 0, n)),
        out_shape=jax.ShapeDtypeStruct((depth, R, N), F32),
        compiler_params=_cparams(("arbitrary", "arbitrary")),
        name="ada",
    )(cvec, ada_w, ada_b.reshape(depth, 1, N))


def _s1_kernel(xp_ref, x_ref, xn_ref, g_ref, sh_ref, sc_ref, mu_ref, kkw_ref, ones_ref, w_ref,
               o_ref, kk_ref, a_scr, *, latent, rows, tm, D, nb):
    i = pl.program_id(0)
    n = pl.program_id(1)

    @pl.when(n == 0)
    def _():
        g = g_ref[...]
        sh = sh_ref[...]
        sc = sc_ref[...]
        h = _modulate(x_ref[...], g, sh, sc)
        if latent:
            q = D // 4
            tok = lax.broadcasted_iota(jnp.int32, (tm, q), 0)
            col = tok & (GRID_W - 1)
            grow = (i * (tm // GRID_W) + tok // GRID_W) & (rows - 1)

            def halo(ref, lo):
                xh = ref[...]
                sl = slice(lo, lo + q)
                return (xh[:, sl] * _rms_scale(xh) * g[:, sl]) * (1.0 + sc[:, sl]) + sh[:, sl]

            left = jnp.where(col > 0, pltpu.roll(h[:, 0:q], 1, axis=0), 0.0)
            right = jnp.where(col < GRID_W - 1, pltpu.roll(h[:, q:2 * q], tm - 1, axis=0), 0.0)
            up = jnp.concatenate([halo(xp_ref, 2 * q), h[:tm - GRID_W, 2 * q:3 * q]], axis=0)
            up = jnp.where(grow > 0, up, 0.0)
            down = jnp.concatenate([h[GRID_W:, 3 * q:], halo(xn_ref, 3 * q)], axis=0)
            down = jnp.where(grow < rows - 1, down, 0.0)
            hs = jnp.concatenate([left, right, up, down], axis=1)
        else:
            half = D // 2
            tok = lax.broadcasted_iota(jnp.int32, (tm, half), 0)
            prev = jnp.where(tok > 0, pltpu.roll(h[:, :half], 1, axis=0), 0.0)
            nxt = jnp.where(tok < tm - 1, pltpu.roll(h[:, half:], tm - 1, axis=0), 0.0)
            hs = jnp.concatenate([prev, nxt], axis=1)
        xx = hs - h
        for j in range(6):
            a_scr[j] = (h + xx * mu_ref[j:j + 1, :]).astype(BF16)

    @pl.when(n < nb)
    def _():
        o_ref[...] = _dot(a_scr[0], w_ref[...])

    @pl.when((n >= nb) & (n < 2 * nb))
    def _():
        acc = _dot(a_scr[2], w_ref[...])
        o_ref[...] = acc
        kkf = acc * kkw_ref[...]
        sq = kkf * kkf
        ones = ones_ref[...]
        parts = []
        for s in range(TN1 // MXU_DIM):
            parts.append(_dot_hilo(sq[:, s * MXU_DIM:(s + 1) * MXU_DIM], ones))
        ss = jnp.concatenate(parts, axis=1)
        kk_ref[...] = kkf / jnp.maximum(jnp.sqrt(ss), 1e-12)

    @pl.when((n >= 2 * nb) & (n < 3 * nb))
    def _():
        o_ref[...] = _dot(a_scr[3], w_ref[...])

    @pl.when(n == 3 * nb)
    def _():
        o_ref[:, :LORA_BLK] = jnp.tanh(_dot(a_scr[1], w_ref[:, :LORA_BLK]))
        o_ref[:, LORA_BLK:] = _dot(a_scr[4], w_ref[:, LORA_BLK:])

    @pl.when(n == 3 * nb + 1)
    def _():
        o_ref[...] = _sigmoid(_dot(a_scr[5], w_ref[...]))


def _rwkv_stage1(x, mod, row_fn, gain, mu, k_k, wcat, *, latent, rows, tm):
    M, D = x.shape
    nb = D // TN1
    N1 = wcat.shape[1]
    nsteps = N1 // TN1
    hb = tm // GRID_W if latent else 1
    hrows = GRID_W if latent else 8
    nhb = M // hrows
    kcol = lambda n: jnp.clip(n - nb, 0, nb - 1)
    kern = functools.partial(_s1_kernel, latent=latent, rows=rows, tm=tm, D=D, nb=nb)
    return pl.pallas_call(
        kern,
        grid=(M // tm, nsteps),
        in_specs=[
            pl.BlockSpec((hrows, D), lambda i, n: (jnp.maximum(i * hb - 1, 0), 0)),
            pl.BlockSpec((tm, D), lambda i, n: (i, 0)),
            pl.BlockSpec((hrows, D), lambda i, n: (jnp.minimum(i * hb + hb, nhb - 1), 0)),
            pl.BlockSpec((1, D), lambda i, n: (0, 0)),
            pl.BlockSpec((None, 1, D), lambda i, n: (row_fn(i), 0, 0)),
            pl.BlockSpec((None, 1, D), lambda i, n: (row_fn(i), 0, 1)),
            pl.BlockSpec((6, D), lambda i, n: (0, 0)),
            pl.BlockSpec((1, TN1), lambda i, n: (0, kcol(n))),
            pl.BlockSpec((MXU_DIM, MXU_DIM), lambda i, n: (0, 0)),
            pl.BlockSpec((D, TN1), lambda i, n: (0, n)),
        ],
        out_specs=[
            pl.BlockSpec((tm, TN1), lambda i, n: (i, n)),
            pl.BlockSpec((tm, TN1), lambda i, n: (i, kcol(n))),
        ],
        out_shape=[jax.ShapeDtypeStruct((M, N1), F32), jax.ShapeDtypeStruct((M, D), F32)],
        scratch_shapes=[pltpu.VMEM((6, tm, D), BF16)],
        compiler_params=_cparams(("arbitrary", "arbitrary")),
        name="rwkv_stage1",
    )(x, x, x, gain, mod, mod, mu, k_k, _group_ones(MXU_DIM, RWKV_HEAD), wcat)


def _s2_kernel(lo1_ref, lo2_ref, w_ref, b_ref, o_ref):
    n = pl.program_id(1)
    w = w_ref[...]
    b = b_ref[...]

    @pl.when(n < 2)
    def _():
        z = _dot(lo1_ref[:, :LORA_BLK].astype(BF16), w) + b
        u = -z
        softplus = jnp.maximum(u, 0.0) + jnp.log(1.0 + jnp.exp(-jnp.abs(u)))
        o_ref[...] = -jnp.exp(-softplus - 0.5)

    @pl.when((n >= 2) & (n < 4))
    def _():
        o_ref[...] = _sigmoid(_dot(lo1_ref[:, LORA_BLK:].astype(BF16), w) + b)

    @pl.when(n == 4)
    def _():
        o_ref[...] = _dot(lo2_ref[:, :LORA_BLK].astype(BF16), w)


def _rwkv_stage2(s1, w2cat, bias2, D, tm):
    M = s1.shape[0]
    c0 = 3 * D // (2 * LORA_BLK)
    return pl.pallas_call(
        _s2_kernel,
        grid=(M // tm, 5),
        in_specs=[
            pl.BlockSpec((tm, 2 * LORA_BLK), lambda i, n: (i, c0)),
            pl.BlockSpec((tm, 2 * LORA_BLK), lambda i, n: (i, c0 + 1)),
            pl.BlockSpec((LORA_BLK, D), lambda i, n: (0, n)),
            pl.BlockSpec((1, D), lambda i, n: (0, n)),
        ],
        out_specs=pl.BlockSpec((tm, D), lambda i, n: (i, n)),
        out_shape=jax.ShapeDtypeStruct((M, 5 * D), F32),
        compiler_params=_cparams(("arbitrary", "arbitrary")),
        name="rwkv_stage2",
    )(s1, s1, w2cat, bias2)


_NT = (((1,), (1,)), ((), ()))
_TN = (((0,), (0,)), ((), ()))


def _mm(a, b, dims=None):
    a = a.astype(BF16)
    b = b.astype(BF16)
    if dims is None:
        return jnp.dot(a, b, preferred_element_type=F32)
    return lax.dot_general(a, b, dims, preferred_element_type=F32)


def _scan_kernel(r_ref, k_ref, v_ref, kk_ref, ld_ref, a_ref, ka_ref, h0_ref, y_ref, hT_ref, h_scr, *, G):
    d = pl.program_id(0)
    c = pl.program_id(3)
    nc = pl.num_programs(3)
    L = CHUNK
    pairs = range(G)

    @pl.when(c == 0)
    def _():
        h_scr[...] = h0_ref[...]

    sgn = 1 - 2 * d
    lane = lax.broadcasted_iota(jnp.int32, (L, PAIR), 1)
    m0 = lane < RWKV_HEAD
    ti = lax.broadcasted_iota(jnp.int32, (L, L), 0)
    si = lax.broadcasted_iota(jnp.int32, (L, L), 1)
    tri = jnp.where((si - ti) * sgn <= 0, 1.0, 0.0).astype(BF16)
    i2 = lax.broadcasted_iota(jnp.int32, (2 * L, 2 * L), 0)
    j2 = lax.broadcasted_iota(jnp.int32, (2 * L, 2 * L), 1)
    dif = jnp.where((i2 // L) == (j2 // L), ((j2 % L) - (i2 % L)) * sgn, 1)
    strict = dif < 0
    incl = dif <= 0
    r2 = lax.broadcasted_iota(jnp.int32, (PAIR, PAIR), 0)
    c2 = lax.broadcasted_iota(jnp.int32, (PAIR, PAIR), 1)
    bd = (r2 // RWKV_HEAD) == (c2 // RWKV_HEAD)

    def sm(x):
        return jnp.concatenate([jnp.where(m0, x, 0.0), jnp.where(m0, 0.0, x)], axis=0)

    def sl(x, g):
        return x[:, g * PAIR:(g + 1) * PAIR]

    ld = ld_ref[...]
    p0 = ld.astype(BF16)
    r1 = ld - p0.astype(F32)
    p1 = r1.astype(BF16)
    p2 = (r1 - p1.astype(F32)).astype(BF16)
    cum = _dot(tri, p0) + _dot(tri, p1) + _dot(tri, p2)
    Pt = jnp.exp(cum)
    Pprev = jnp.exp(cum - ld)
    invP = jnp.exp(-cum)
    PL = jnp.exp(jnp.sum(ld, axis=0, keepdims=True))
    a = a_ref[...]
    kk = kk_ref[...]
    v = v_ref[...]
    At = -kk * Pprev
    Bt = (kk * a) * invP
    Kt = (k_ref[...] * (1.0 + (a - 1.0) * ka_ref[...])) * invP
    Rt = r_ref[...] * Pt
    Bh = Bt * PL
    Kh = Kt * PL

    As = [sm(sl(At, g)) for g in pairs]
    Rs = [sm(sl(Rt, g)) for g in pairs]
    Vs = [sm(sl(v, g)) for g in pairs]
    Gm = [_mm(jnp.concatenate([As[g], Rs[g]], axis=0),
              jnp.concatenate([sm(sl(Bt, g)), sm(sl(Kt, g))], axis=0), _NT) for g in pairs]
    Np = [jnp.where(strict, Gm[g][:2 * L, :2 * L], 0.0).astype(BF16) for g in pairs]
    Aak = [jnp.where(strict, Gm[g][:2 * L, 2 * L:], 0.0) for g in pairs]
    Mr = [jnp.concatenate([jnp.where(incl, Gm[g][2 * L:, :2 * L], 0.0),
                           jnp.where(incl, Gm[g][2 * L:, 2 * L:], 0.0)], axis=1).astype(BF16) for g in pairs]
    X = [jnp.concatenate([As[g], _mm(Aak[g], Vs[g])], axis=1) for g in pairs]
    nsteps = int(math.log2(L))
    for s in range(nsteps):
        X = [X[g] + _mm(Np[g], X[g]) for g in pairs]
        if s < nsteps - 1:
            Np = [_mm(Np[g], Np[g]).astype(BF16) for g in pairs]

    H = [h_scr[g] for g in pairs]
    AH = [_mm(jnp.concatenate([X[g][:, :PAIR], Rs[g]], axis=0), H[g]) for g in pairs]
    Us = [AH[g][:2 * L] + X[g][:, PAIR:] for g in pairs]
    Ys = [AH[g][2 * L:] + _mm(Mr[g], jnp.concatenate([Us[g], Vs[g]], axis=0)) for g in pairs]
    upd = [_mm(jnp.concatenate([sl(Bh, g), sl(Kh, g)], axis=0),
               jnp.concatenate([Us[g][:L] + Us[g][L:], sl(v, g)], axis=0), _TN) for g in pairs]
    for g in pairs:
        y_ref[:, g * PAIR:(g + 1) * PAIR] = Ys[g][:L] + Ys[g][L:]
        plc = jnp.transpose(jnp.broadcast_to(sl(PL, g), (PAIR, PAIR)))
        h_scr[g] = H[g] * plc + jnp.where(bd, upd[g], 0.0)

    @pl.when(c == nc - 1)
    def _():
        hT_ref[...] = h_scr[...]


def _wkv_scan(s1, kk, s2, k_a, h0, B, T, D):
    G = min(SCAN_PAIRS, D // PAIR)
    W = G * PAIR
    L = CHUNK
    nc = T // L
    npg = D // W
    N1 = s1.shape[1]
    s1 = s1.reshape(B, T, N1)
    kk = kk.reshape(B, T, D)
    s2 = s2.reshape(B, T, 5 * D)
    chunk = lambda d, c: c + d * (nc - 1 - 2 * c)
    tok = lambda off: pl.BlockSpec((None, L, W), lambda d, b, p, c: (b, chunk(d, c), off + p))
    y, hT = pl.pallas_call(
        functools.partial(_scan_kernel, G=G),
        grid=(2, B, npg, nc),
        in_specs=[
            tok(0), tok(npg), tok(2 * npg),
            pl.BlockSpec((None, L, W), lambda d, b, p, c: (b, chunk(d, c), p)),
            pl.BlockSpec((None, L, W), lambda d, b, p, c: (b, chunk(d, c), d * npg + p)),
            pl.BlockSpec((None, L, W), lambda d, b, p, c: (b, chunk(d, c), (2 + d) * npg + p)),
            pl.BlockSpec((1, W), lambda d, b, p, c: (0, p)),
            pl.BlockSpec((None, None, G, PAIR, PAIR), lambda d, b, p, c: (d, b, p, 0, 0)),
        ],
        out_specs=[
            pl.BlockSpec((None, None, L, W), lambda d, b, p, c: (d, b, chunk(d, c), p)),
            pl.BlockSpec((None, None, G, PAIR, PAIR), lambda d, b, p, c: (d, b, p, 0, 0)),
        ],
        out_shape=[jax.ShapeDtypeStruct((2, B, T, D), F32),
                   jax.ShapeDtypeStruct((2, B, D // PAIR, PAIR, PAIR), F32)],
        scratch_shapes=[pltpu.VMEM((G, PAIR, PAIR), F32)],
        compiler_params=_cparams(("arbitrary", "arbitrary", "arbitrary", "arbitrary")),
        name="wkv_scan",
    )(s1, s1, s1, kk, s2, s2, k_a, h0)
    return y.reshape(2, B * T, D), hT


def _rwkv_out_kernel(y0_ref, y1_ref, r_ref, k_ref, v_ref, a0_ref, a1_ref, g_ref,
                     lnw_ref, lnb_ref, rk_ref, ka_ref, ones_ref, w_ref, x_ref, gate_ref,
                     o_ref, a_scr, *, D):
    n = pl.program_id(1)

    @pl.when(n == 0)
    def _():
        ones = ones_ref[...]
        inv = 1.0 / RWKV_HEAD

        def gsum(t):
            parts = [_dot_hilo(t[:, s * MXU_DIM:(s + 1) * MXU_DIM], ones) for s in range(D // MXU_DIM)]
            return jnp.concatenate(parts, axis=1)

        wkv = y0_ref[...] + y1_ref[...]
        cen = wkv - gsum(wkv) * inv
        var = gsum(cen * cen) * inv
        yn = cen * lax.rsqrt(var + GN_EPS) * lnw_ref[...] + lnb_ref[...]
        ks = k_ref[...] * (2.0 + (a0_ref[...] + a1_ref[...] - 2.0) * ka_ref[...])
        bonus = gsum(r_ref[...] * ks * rk_ref[...]) * v_ref[...]
        a_scr[...] = ((yn + bonus) * g_ref[...]).astype(BF16)

    o_ref[...] = x_ref[...] + gate_ref[...] * _dot(a_scr[...], w_ref[...])


def _rwkv_out(y, s1, s2, x, mod, row_fn, ln_w, ln_b, r_k, k_a, w_o, tm, tn):
    M, D = x.shape
    big = lambda col: pl.BlockSpec((tm, D), lambda i, n: (i, col))
    row = pl.BlockSpec((1, D), lambda i, n: (0, 0))
    return pl.pallas_call(
        functools.partial(_rwkv_out_kernel, D=D),
        grid=(M // tm, D // tn),
        in_specs=[
            pl.BlockSpec((None, tm, D), lambda i, n: (0, i, 0)),
            pl.BlockSpec((None, tm, D), lambda i, n: (1, i, 0)),
            big(0), big(1), big(2),
            big(2), big(3), big(4),
            row, row, row, row,
            pl.BlockSpec((MXU_DIM, MXU_DIM), lambda i, n: (0, 0)),
            pl.BlockSpec((D, tn), lambda i, n: (0, n)),
            pl.BlockSpec((tm, tn), lambda i, n: (i, n)),
            pl.BlockSpec((None, 1, tn), lambda i, n: (row_fn(i), 0, 2 * (D // tn) + n)),
        ],
        out_specs=pl.BlockSpec((tm, tn), lambda i, n: (i, n)),
        out_shape=jax.ShapeDtypeStruct((M, D), F32),
        scratch_shapes=[pltpu.VMEM((tm, D), BF16)],
        compiler_params=_cparams(("arbitrary", "arbitrary")),
        name="rwkv_out",
    )(y, y, s1, s1, s1, s2, s2, s2, ln_w, ln_b, r_k, k_a, _group_ones(MXU_DIM, RWKV_HEAD), w_o, x, mod)


def _ffn_kernel(x_ref, g_ref, sh_ref, sc_ref, gate_ref, w1_ref, w3_ref, w2_ref, o_ref, a_scr, acc_scr):
    f = pl.program_id(1)
    nf = pl.num_programs(1)

    @pl.when(f == 0)
    def _():
        a_scr[...] = _modulate(x_ref[...], g_ref[...], sh_ref[...], sc_ref[...]).astype(BF16)
        acc_scr[...] = jnp.zeros_like(acc_scr)

    a = a_scr[...]
    h1 = _dot(a, w1_ref[...])
    h3 = _dot(a, w3_ref[...])
    act = (_silu(h1) * h3).astype(BF16)
    acc_scr[...] += _dot(act, w2_ref[...])

    @pl.when(f == nf - 1)
    def _():
        o_ref[...] = x_ref[...] + gate_ref[...] * acc_scr[...]


def _ffn(x, mod, row_fn, gain, w1, w3, w2, tm, tf):
    M, D = x.shape
    Fdim = w1.shape[1]
    modspec = lambda ch: pl.BlockSpec((None, 1, D), lambda i, f: (row_fn(i), 0, ch))
    return pl.pallas_call(
        _ffn_kernel,
        grid=(M // tm, Fdim // tf),
        in_specs=[
            pl.BlockSpec((tm, D), lambda i, f: (i, 0)),
            pl.BlockSpec((1, D), lambda i, f: (0, 0)),
            modspec(3), modspec(4), modspec(5),
            pl.BlockSpec((D, tf), lambda i, f: (0, f)),
            pl.BlockSpec((D, tf), lambda i, f: (0, f)),
            pl.BlockSpec((tf, D), lambda i, f: (f, 0)),
        ],
        out_specs=pl.BlockSpec((tm, D), lambda i, f: (i, 0)),
        out_shape=jax.ShapeDtypeStruct((M, D), F32),
        scratch_shapes=[pltpu.VMEM((tm, D), BF16), pltpu.VMEM((tm, D), F32)],
        compiler_params=_cparams(("arbitrary", "arbitrary")),
        name="ffn",
    )(x, gain, mod, mod, mod, w1, w3, w2)


def _qkv_kernel(x_ref, g_ref, sh_ref, sc_ref, w_ref, hg_ref, cos_ref, sin_ref, o_ref, a_scr, *, nrope, nnorm, tn):
    n = pl.program_id(1)

    @pl.when(n == 0)
    def _():
        a_scr[...] = _modulate(x_ref[...], g_ref[...], sh_ref[...], sc_ref[...]).astype(BF16)

    acc = _dot(a_scr[...], w_ref[...])

    @pl.when(n < nnorm)
    def _():
        hg = hg_ref[...]
        lane = lax.broadcasted_iota(jnp.int32, (acc.shape[0], NA_HEAD), 1)
        first = (lane & (NA_HEAD // 2 - 1)) < NA_HEAD // 4
        for s in range(tn // NA_HEAD):
            t = acc[:, s * NA_HEAD:(s + 1) * NA_HEAD]
            t = t * _rms_scale(t) * hg
            if nrope:
                swapped = jnp.where(first, pltpu.roll(t, NA_HEAD - NA_HEAD // 4, axis=1),
                                    pltpu.roll(t, NA_HEAD // 4, axis=1))
                roped = t * cos_ref[...] + swapped * sin_ref[...]
                t = jnp.where(n < nrope, roped, t)
            o_ref[:, s * NA_HEAD:(s + 1) * NA_HEAD] = t.astype(o_ref.dtype)

    @pl.when(n >= nnorm)
    def _():
        o_ref[...] = acc.astype(o_ref.dtype)


def _na_qkv(x, mod, row_fn, gain, w, head_gain, cos, sin, *, nrope, nnorm, tm, tn):
    M, D = x.shape
    N = w.shape[1]
    tpb = cos.shape[0] // tm
    hrow = lambda n: jnp.minimum(n, head_gain.shape[0] - 1)
    return pl.pallas_call(
        functools.partial(_qkv_kernel, nrope=nrope, nnorm=nnorm, tn=tn),
        grid=(M // tm, N // tn),
        in_specs=[
            pl.BlockSpec((tm, D), lambda i, n: (i, 0)),
            pl.BlockSpec((1, D), lambda i, n: (0, 0)),
            pl.BlockSpec((None, 1, D), lambda i, n: (row_fn(i), 0, 0)),
            pl.BlockSpec((None, 1, D), lambda i, n: (row_fn(i), 0, 1)),
            pl.BlockSpec((D, tn), lambda i, n: (0, n)),
            pl.BlockSpec((None, 1, NA_HEAD), lambda i, n: (hrow(n), 0, 0)),
            pl.BlockSpec((tm, NA_HEAD), lambda i, n: (i % tpb, 0)),
            pl.BlockSpec((tm, NA_HEAD), lambda i, n: (i % tpb, 0)),
        ],
        out_specs=pl.BlockSpec((tm, tn), lambda i, n: (i, n)),
        out_shape=jax.ShapeDtypeStruct((M, N), BF16),
        scratch_shapes=[pltpu.VMEM((tm, D), BF16)],
        compiler_params=_cparams(("arbitrary", "arbitrary")),
        name="na_qkv",
    )(x, gain, mod, mod, w, head_gain, cos, sin)


def _na_kernel(q_ref, k_ref, v_ref, kc_ref, vc_ref, bias_ref, o_ref, *, rows, rb):
    j0 = pl.program_id(2) * rb
    win_h = min(WIN_H, rows)
    kc = kc_ref[...]
    vc = vc_ref[...]

    def body(j, carry):
        r = j0 + j
        r0 = jnp.clip(r - WIN_H // 2, 0, rows - win_h)
        q = q_ref[pl.ds(pl.multiple_of(j * GRID_W, GRID_W), GRID_W), :]
        ks = k_ref[pl.ds(pl.multiple_of(r0 * GRID_W, GRID_W), win_h * GRID_W), :]
        vs = v_ref[pl.ds(pl.multiple_of(r0 * GRID_W, GRID_W), win_h * GRID_W), :]
        s_lat = lax.dot_general(q, ks, _NT, preferred_element_type=F32) + bias_ref[r0 - r + WIN_H - 1]
        s_ctx = lax.dot_general(q, kc, _NT, preferred_element_type=F32)
        m = jnp.maximum(jnp.max(s_lat, axis=-1, keepdims=True), jnp.max(s_ctx, axis=-1, keepdims=True))
        p_lat = jnp.exp(s_lat - m)
        p_ctx = jnp.exp(s_ctx - m)
        den = jnp.sum(p_lat, axis=-1, keepdims=True) + jnp.sum(p_ctx, axis=-1, keepdims=True)
        o = _dot(p_lat.astype(BF16), vs) + _dot(p_ctx.astype(BF16), vc)
        o_ref[pl.ds(pl.multiple_of(j * GRID_W, GRID_W), GRID_W), :] = (o / den).astype(o_ref.dtype)
        return carry

    lax.fori_loop(0, rb, body, 0)


def _na_attention(qkv, kvc, bias, B, T, D, rb):
    H = D // NA_HEAD
    rows = T // GRID_W
    C = kvc.shape[1]
    nbias = bias.shape[1]
    return pl.pallas_call(
        functools.partial(_na_kernel, rows=rows, rb=rb),
        grid=(B, H, rows // rb),
        in_specs=[
            pl.BlockSpec((None, rb * GRID_W, NA_HEAD), lambda b, h, j: (b, j, h)),
            pl.BlockSpec((None, T, NA_HEAD), lambda b, h, j: (b, 0, H + h)),
            pl.BlockSpec((None, T, NA_HEAD), lambda b, h, j: (b, 0, 2 * H + h)),
            pl.BlockSpec((None, C, NA_HEAD), lambda b, h, j: (b, 0, h)),
            pl.BlockSpec((None, C, NA_HEAD), lambda b, h, j: (b, 0, H + h)),
            pl.BlockSpec((None, nbias, GRID_W, bias.shape[3]), lambda b, h, j: (h, 0, 0, 0)),
        ],
        out_specs=pl.BlockSpec((None, rb * GRID_W, NA_HEAD), lambda b, h, j: (b, j, h)),
        out_shape=jax.ShapeDtypeStruct((B, T, D), BF16),
        compiler_params=_cparams(("arbitrary", "arbitrary", "arbitrary")),
        name="na_attention",
    )(qkv, qkv, qkv, kvc, kvc, bias)


def _proj_res_kernel(a_ref, w_ref, x_ref, gate_ref, o_ref):
    o_ref[...] = x_ref[...] + gate_ref[...] * _dot(a_ref[...], w_ref[...])


def _proj_residual(a, w, x, mod, row_fn, chunk, tm, tn):
    M, K = a.shape
    N = w.shape[1]
    return pl.pallas_call(
        _proj_res_kernel,
        grid=(M // tm, N // tn),
        in_specs=[
            pl.BlockSpec((tm, K), lambda i, n: (i, 0)),
            pl.BlockSpec((K, tn), lambda i, n: (0, n)),
            pl.BlockSpec((tm, tn), lambda i, n: (i, n)),
            pl.BlockSpec((None, 1, tn), lambda i, n: (row_fn(i), 0, chunk * (N // tn) + n)),
        ],
        out_specs=pl.BlockSpec((tm, tn), lambda i, n: (i, n)),
        out_shape=jax.ShapeDtypeStruct((M, N), F32),
        compiler_params=_cparams(("arbitrary", "arbitrary")),
        name="proj_residual",
    )(a, w, x, mod)


def _rope_tables(T):
    t = np.arange(T)
    half = NA_HEAD // 2
    freqs = ROPE_THETA ** (-np.arange(0, half, 2, dtype=np.float64) / half)
    ang_r = (t // GRID_W).astype(np.float64)[:, None] * freqs
    ang_c = (t % GRID_W).astype(np.float64)[:, None] * freqs
    cos = np.concatenate([np.cos(ang_r), np.cos(ang_r), np.cos(ang_c), np.cos(ang_c)], axis=1)
    sin = np.concatenate([-np.sin(ang_r), np.sin(ang_r), -np.sin(ang_c), np.sin(ang_c)], axis=1)
    return jnp.asarray(cos, F32), jnp.asarray(sin, F32)


def _na_bias(rpb, rows):
    win_h = min(WIN_H, rows)
    qc = np.arange(GRID_W)[:, None]
    kc = np.arange(GRID_W)[None, :]
    wstart = np.clip(qc - WIN_W // 2, 0, GRID_W - WIN_W)
    valid = (kc >= wstart) & (kc < wstart + WIN_W)
    rel = np.clip(kc - qc, -(WIN_W - 1), WIN_W - 1) + WIN_W - 1
    g = rpb[:, :, rel]
    g = jnp.where(valid[None, None], g, NEG_INF)
    tabs = []
    for i0 in range(2 * WIN_H - win_h):
        blk = g[:, i0:i0 + win_h]
        tabs.append(jnp.transpose(blk, (0, 2, 1, 3)).reshape(g.shape[0], GRID_W, win_h * GRID_W))
    return jnp.stack(tabs, axis=1)


def _pad_cols(w, n):
    return jnp.pad(w, ((0, 0), (0, n - w.shape[1])))


def _pad_rows(w, n):
    return jnp.pad(w, ((0, n - w.shape[0]), (0, 0)))


def kernel(x, c, ctx, c_ctx, ada_w, ada_b, norm1, norm2, rwkv_mu, rwkv_w_r, rwkv_w_k, rwkv_w_v, rwkv_w_o, rwkv_w0, rwkv_w1, rwkv_w2, rwkv_a0, rwkv_a1, rwkv_a2, rwkv_g1, rwkv_g2, rwkv_k_k, rwkv_k_a, rwkv_r_k, rwkv_ln_w, rwkv_ln_b, na_w_qkv, na_w_o, na_q_gain, na_k_gain, na_rpb, ffn_w1, ffn_w3, ffn_w2):
    B, T, D = x.shape
    C = ctx.shape[1]
    rows = T // GRID_W
    depth = ada_w.shape[0]
    n_mix = 2
    TM = 512
    TMO = 256
    TNO = min(1024, D)
    TF = 512 if ffn_w1.shape[-1] % 512 == 0 else 256

    cvec = jnp.concatenate([c, c_ctx[None, :], jnp.zeros((8 - B - 1, D), F32)], axis=0)
    mods = _ada(cvec, ada_w, ada_b)
    lat_row = lambda i: i // (T // TM)
    ctx_row = lambda i: B

    xl = x.reshape(B * T, D)
    xc = ctx.reshape(B * C, D)

    for i in range(depth):
        last = i == depth - 1
        j = i // n_mix
        mod = mods[i].reshape(8, 1, 6 * D)
        g1 = norm1[i].reshape(1, D)
        g2 = norm2[i].reshape(1, D)
        w1 = ffn_w1[i].astype(BF16)
        w3 = ffn_w3[i].astype(BF16)
        w2 = ffn_w2[i].astype(BF16)
        if i % n_mix == 0:
            lw = rwkv_w1.shape[-1]
            la = rwkv_a1.shape[-1]
            lg = rwkv_g1.shape[-1]
            wcat = jnp.concatenate([
                rwkv_w_r[j], rwkv_w_k[j], rwkv_w_v[j],
                _pad_cols(jnp.concatenate([rwkv_w1[j, 0], rwkv_w1[j, 1]], axis=1), LORA_BLK),
                _pad_cols(jnp.concatenate([rwkv_a1[j, 0], rwkv_a1[j, 1]], axis=1), LORA_BLK),
                _pad_cols(rwkv_g1[j], 2 * LORA_BLK)], axis=1).astype(BF16)
            z = lambda n: jnp.zeros((n, D), F32)
            w2cat = jnp.concatenate([
                _pad_rows(rwkv_w2[j, 0], LORA_BLK),
                _pad_rows(jnp.concatenate([z(lw), rwkv_w2[j, 1]], axis=0), LORA_BLK),
                _pad_rows(rwkv_a2[j, 0], LORA_BLK),
                _pad_rows(jnp.concatenate([z(la), rwkv_a2[j, 1]], axis=0), LORA_BLK),
                _pad_rows(rwkv_g2[j], LORA_BLK)], axis=1).astype(BF16)
            bias2 = jnp.concatenate([rwkv_w0[j, 0], rwkv_w0[j, 1], rwkv_a0[j, 0], rwkv_a0[j, 1],
                                     jnp.zeros((D,), F32)]).reshape(1, 5 * D)
            k_k = rwkv_k_k[j].reshape(1, D)
            k_a = rwkv_k_a[j].reshape(1, D)
            r_k = rwkv_r_k[j].reshape(1, D)
            ln_w = rwkv_ln_w[j].reshape(1, D)
            ln_b = rwkv_ln_b[j].reshape(1, D)
            w_o = rwkv_w_o[j].astype(BF16)

            s1c, kkc = _rwkv_stage1(xc, mod, ctx_row, g1, rwkv_mu[j], k_k, wcat, latent=False, rows=rows, tm=C)
            s2c = _rwkv_stage2(s1c, w2cat, bias2, D, C)
            h0 = jnp.zeros((2, B, D // PAIR, PAIR, PAIR), F32)
            yc, hc = _wkv_scan(s1c, kkc, s2c, k_a, h0, B, C, D)
            s1l, kkl = _rwkv_stage1(xl, mod, lat_row, g1, rwkv_mu[j], k_k, wcat, latent=True, rows=rows, tm=TM)
            s2l = _rwkv_stage2(s1l, w2cat, bias2, D, TM)
            yl, _ = _wkv_scan(s1l, kkl, s2l, k_a, hc, B, T, D)
            xl = _rwkv_out(yl, s1l, s2l, xl, mod, lambda t: t // (T // TMO), ln_w, ln_b, r_k, k_a, w_o, TMO, TNO)
            if not last:
                xc = _rwkv_out(yc, s1c, s2c, xc, mod, ctx_row, ln_w, ln_b, r_k, k_a, w_o, min(TMO, C), TNO)
        else:
            H = D // NA_HEAD
            wqkv = na_w_qkv[j].astype(BF16)
            cos, sin = _rope_tables(T)
            scale = NA_HEAD ** -0.5
            hg = jnp.stack([jnp.broadcast_to(na_q_gain[j] * scale, (NA_HEAD,)), na_k_gain[j]]).reshape(2, 1, NA_HEAD)
            tn = 512
            nq = D // tn
            hg_blocks = jnp.concatenate([jnp.broadcast_to(hg[0], (nq, 1, NA_HEAD)),
                                         jnp.broadcast_to(hg[1], (nq, 1, NA_HEAD))], axis=0)
            qkv = _na_qkv(xl, mod, lat_row, g1, wqkv, hg_blocks, cos, sin, nrope=2 * nq, nnorm=2 * nq, tm=TM, tn=tn)
            kg_blocks = jnp.broadcast_to(hg[1], (nq, 1, NA_HEAD))
            kvc = _na_qkv(xc, mod, ctx_row, g1, wqkv[:, D:], kg_blocks, cos[:C], sin[:C],
                          nrope=0, nnorm=nq, tm=C, tn=tn)
            bias = _na_bias(na_rpb[j], rows)
            o = _na_attention(qkv.reshape(B, T, 3 * D), kvc.reshape(B, C, 2 * D), bias, B, T, D, 8)
            xl = _proj_residual(o.reshape(B * T, D), na_w_o[j].astype(BF16), xl, mod, lat_row, 2, TM, TNO)
            if not last:
                raise NotImplementedError("attention layer with context output")
        xl = _ffn(xl, mod, lat_row, g2, w1, w3, w2, TM, TF)
        if not last:
            xc = _ffn(xc, mod, ctx_row, g2, w1, w3, w2, C, TF)
    return xl.reshape(B, T, D)
```

```python
import functools
import math

import numpy as np
import jax
import jax.numpy as jnp
from jax import lax
from jax.experimental import pallas as pl
from jax.experimental.pallas import tpu as pltpu

F32 = jnp.float32
BF16 = jnp.bfloat16

GRID_W = 64
RMS_EPS = 1e-6
RWKV_HEAD = 64
GN_EPS = 64e-5
NA_HEAD = 128
WIN_H = 8
WIN_W = 16
ROPE_THETA = 10000.0
NEG_INF = -1e30

LANE = 128
MXU_DIM = 256
VMEM_LIMIT = 56 * 1024 * 1024

CHUNK = 64
PAIR = 2 * RWKV_HEAD
SCAN_PAIRS = 8
LORA_BLK = 256
TN1 = 512


def _cparams(sem):
    return pltpu.CompilerParams(dimension_semantics=sem, vmem_limit_bytes=VMEM_LIMIT)


def _dot(a, b):
    return jnp.dot(a, b, preferred_element_type=F32)


def _split2(x):
    hi = x.astype(BF16)
    lo = (x - hi.astype(F32)).astype(BF16)
    return hi, lo


def _dot_hilo(a, b_bf16):
    hi, lo = _split2(a)
    return _dot(hi, b_bf16) + _dot(lo, b_bf16)


def _rms_scale(x):
    return lax.rsqrt(jnp.mean(x * x, axis=-1, keepdims=True) + RMS_EPS)


def _modulate(x, gain, shift, scale):
    return (x * _rms_scale(x) * gain) * (1.0 + scale) + shift


def _sigmoid(x):
    return 1.0 / (1.0 + jnp.exp(-x))


def _silu(x):
    return x * _sigmoid(x)


def _group_ones(n, group):
    idx = np.arange(n) // group
    return jnp.asarray((idx[:, None] == idx[None, :]).astype(np.float32), dtype=BF16)


def _ada_kernel(c_ref, w_ref, b_ref, o_ref):
    s = _silu(c_ref[...]).astype(BF16)
    o_ref[...] = _dot(s, w_ref[...].astype(BF16)) + b_ref[...]


def _ada(cvec, ada_w, ada_b):
    depth, D, N = ada_w.shape
    R = cvec.shape[0]
    tn = 1024
    return pl.pallas_call(
        _ada_kernel,
        grid=(depth, N // tn),
        in_specs=[
            pl.BlockSpec((R, D), lambda l, n: (0, 0)),
            pl.BlockSpec((None, D, tn), lambda l, n: (l, 0, n)),
            pl.BlockSpec((None, 1, tn), lambda l, n: (l, 0, n)),
        ],
        out_specs=pl.BlockSpec((None, R, tn), lambda l, n: (l, 0, n)),
        out_shape=jax.ShapeDtypeStruct((depth, R, N), F32),
        compiler_params=_cparams(("arbitrary", "arbitrary")),
        name="ada",
    )(cvec, ada_w, ada_b.reshape(depth, 1, N))


def _s1_kernel(xp_ref, x_ref, xn_ref, g_ref, sh_ref, sc_ref, mu_ref, kkw_ref, ones_ref, w_ref,
               o_ref, kk_ref, a_scr, *, latent, rows, tm, D, nb):
    i = pl.program_id(0)
    n = pl.program_id(1)

    @pl.when(n == 0)
    def _():
        g = g_ref[...]
        sh = sh_ref[...]
        sc = sc_ref[...]
        h = _modulate(x_ref[...], g, sh, sc)
        if latent:
            q = D // 4
            tok = lax.broadcasted_iota(jnp.int32, (tm, q), 0)
            col = tok & (GRID_W - 1)
            grow = (i * (tm // GRID_W) + tok // GRID_W) & (rows - 1)

            def halo(ref, lo):
                xh = ref[...]
                sl = slice(lo, lo + q)
                return (xh[:, sl] * _rms_scale(xh) * g[:, sl]) * (1.0 + sc[:, sl]) + sh[:, sl]

            left = jnp.where(col > 0, pltpu.roll(h[:, 0:q], 1, axis=0), 0.0)
            right = jnp.where(col < GRID_W - 1, pltpu.roll(h[:, q:2 * q], tm - 1, axis=0), 0.0)
            up = jnp.concatenate([halo(xp_ref, 2 * q), h[:tm - GRID_W, 2 * q:3 * q]], axis=0)
            up = jnp.where(grow > 0, up, 0.0)
            down = jnp.concatenate([h[GRID_W:, 3 * q:], halo(xn_ref, 3 * q)], axis=0)
            down = jnp.where(grow < rows - 1, down, 0.0)
            hs = jnp.concatenate([left, right, up, down], axis=1)
        else:
            half = D // 2
            tok = lax.broadcasted_iota(jnp.int32, (tm, half), 0)
            prev = jnp.where(tok > 0, pltpu.roll(h[:, :half], 1, axis=0), 0.0)
            nxt = jnp.where(tok < tm - 1, pltpu.roll(h[:, half:], tm - 1, axis=0), 0.0)
            hs = jnp.concatenate([prev, nxt], axis=1)
        xx = hs - h
        for j in range(6):
            a_scr[j] = (h + xx * mu_ref[j:j + 1, :]).astype(BF16)

    @pl.when(n < nb)
    def _():
        o_ref[...] = _dot(a_scr[0], w_ref[...])

    @pl.when((n >= nb) & (n < 2 * nb))
    def _():
        acc = _dot(a_scr[2], w_ref[...])
        o_ref[...] = acc
        kkf = acc * kkw_ref[...]
        sq = kkf * kkf
        ones = ones_ref[...]
        parts = []
        for s in range(TN1 // MXU_DIM):
            parts.append(_dot_hilo(sq[:, s * MXU_DIM:(s + 1) * MXU_DIM], ones))
        ss = jnp.concatenate(parts, axis=1)
        kk_ref[...] = kkf / jnp.maximum(jnp.sqrt(ss), 1e-12)

    @pl.when((n >= 2 * nb) & (n < 3 * nb))
    def _():
        o_ref[...] = _dot(a_scr[3], w_ref[...])

    @pl.when(n == 3 * nb)
    def _():
        o_ref[:, :LORA_BLK] = jnp.tanh(_dot(a_scr[1], w_ref[:, :LORA_BLK]))
        o_ref[:, LORA_BLK:] = _dot(a_scr[4], w_ref[:, LORA_BLK:])

    @pl.when(n == 3 * nb + 1)
    def _():
        o_ref[...] = _sigmoid(_dot(a_scr[5], w_ref[...]))


def _rwkv_stage1(x, mod, row_fn, gain, mu, k_k, wcat, *, latent, rows, tm):
    M, D = x.shape
    nb = D // TN1
    nsteps = wcat.shape[0]
    N1 = nsteps * TN1
    hb = tm // GRID_W if latent else 1
    hrows = GRID_W if latent else 8
    nhb = M // hrows
    kcol = lambda n: jnp.clip(n - nb, 0, nb - 1)
    kern = functools.partial(_s1_kernel, latent=latent, rows=rows, tm=tm, D=D, nb=nb)
    return pl.pallas_call(
        kern,
        grid=(M // tm, nsteps),
        in_specs=[
            pl.BlockSpec((hrows, D), lambda i, n: (jnp.maximum(i * hb - 1, 0), 0)),
            pl.BlockSpec((tm, D), lambda i, n: (i, 0)),
            pl.BlockSpec((hrows, D), lambda i, n: (jnp.minimum(i * hb + hb, nhb - 1), 0)),
            pl.BlockSpec((1, D), lambda i, n: (0, 0)),
            pl.BlockSpec((None, 1, D), lambda i, n: (row_fn(i), 0, 0)),
            pl.BlockSpec((None, 1, D), lambda i, n: (row_fn(i), 0, 1)),
            pl.BlockSpec((6, D), lambda i, n: (0, 0)),
            pl.BlockSpec((1, TN1), lambda i, n: (0, kcol(n))),
            pl.BlockSpec((MXU_DIM, MXU_DIM), lambda i, n: (0, 0)),
            pl.BlockSpec((None, D, TN1), lambda i, n: (n, 0, 0)),
        ],
        out_specs=[
            pl.BlockSpec((tm, TN1), lambda i, n: (i, n)),
            pl.BlockSpec((tm, TN1), lambda i, n: (i, kcol(n))),
        ],
        out_shape=[jax.ShapeDtypeStruct((M, N1), F32), jax.ShapeDtypeStruct((M, D), F32)],
        scratch_shapes=[pltpu.VMEM((6, tm, D), BF16)],
        compiler_params=_cparams(("arbitrary", "arbitrary")),
        name="rwkv_stage1",
    )(x, x, x, gain, mod, mod, mu, k_k, _group_ones(MXU_DIM, RWKV_HEAD), wcat)


def _s2_kernel(lo1_ref, lo2_ref, w_ref, b_ref, o_ref):
    n = pl.program_id(1)
    w = w_ref[...]
    b = b_ref[...]

    @pl.when(n < 2)
    def _():
        z = _dot(lo1_ref[:, :LORA_BLK].astype(BF16), w) + b
        u = -z
        softplus = jnp.maximum(u, 0.0) + jnp.log(1.0 + jnp.exp(-jnp.abs(u)))
        o_ref[...] = -jnp.exp(-softplus - 0.5)

    @pl.when((n >= 2) & (n < 4))
    def _():
        o_ref[...] = _sigmoid(_dot(lo1_ref[:, LORA_BLK:].astype(BF16), w) + b)

    @pl.when(n == 4)
    def _():
        o_ref[...] = _dot(lo2_ref[:, :LORA_BLK].astype(BF16), w)


def _rwkv_stage2(s1, w2cat, bias2, D, tm):
    M = s1.shape[0]
    c0 = 3 * D // (2 * LORA_BLK)
    return pl.pallas_call(
        _s2_kernel,
        grid=(M // tm, 5),
        in_specs=[
            pl.BlockSpec((tm, 2 * LORA_BLK), lambda i, n: (i, c0)),
            pl.BlockSpec((tm, 2 * LORA_BLK), lambda i, n: (i, c0 + 1)),
            pl.BlockSpec((LORA_BLK, D), lambda i, n: (0, n)),
            pl.BlockSpec((1, D), lambda i, n: (0, n)),
        ],
        out_specs=pl.BlockSpec((tm, D), lambda i, n: (i, n)),
        out_shape=jax.ShapeDtypeStruct((M, 5 * D), F32),
        compiler_params=_cparams(("arbitrary", "arbitrary")),
        name="rwkv_stage2",
    )(s1, s1, w2cat, bias2)


_NT = (((1,), (1,)), ((), ()))
_TN = (((0,), (0,)), ((), ()))


def _mm(a, b, dims=None):
    a = a.astype(BF16)
    b = b.astype(BF16)
    if dims is None:
        return jnp.dot(a, b, preferred_element_type=F32)
    return lax.dot_general(a, b, dims, preferred_element_type=F32)


def _scan_kernel(r_ref, k_ref, v_ref, kk_ref, ld_ref, a_ref, ka_ref, h0_ref, y_ref, hT_ref, h_scr, *, G):
    d = pl.program_id(0)
    c = pl.program_id(3)
    nc = pl.num_programs(3)
    L = CHUNK
    pairs = range(G)

    @pl.when(c == 0)
    def _():
        h_scr[...] = h0_ref[...]

    sgn = 1 - 2 * d
    lane = lax.broadcasted_iota(jnp.int32, (L, PAIR), 1)
    m0 = lane < RWKV_HEAD
    ti = lax.broadcasted_iota(jnp.int32, (L, L), 0)
    si = lax.broadcasted_iota(jnp.int32, (L, L), 1)
    tri = jnp.where((si - ti) * sgn <= 0, 1.0, 0.0).astype(BF16)
    i2 = lax.broadcasted_iota(jnp.int32, (2 * L, 2 * L), 0)
    j2 = lax.broadcasted_iota(jnp.int32, (2 * L, 2 * L), 1)
    dif = jnp.where((i2 // L) == (j2 // L), ((j2 % L) - (i2 % L)) * sgn, 1)
    strict = dif < 0
    incl = dif <= 0
    r2 = lax.broadcasted_iota(jnp.int32, (PAIR, PAIR), 0)
    c2 = lax.broadcasted_iota(jnp.int32, (PAIR, PAIR), 1)
    bd = (r2 // RWKV_HEAD) == (c2 // RWKV_HEAD)

    def sm(x):
        return jnp.concatenate([jnp.where(m0, x, 0.0), jnp.where(m0, 0.0, x)], axis=0)

    def sl(x, g):
        return x[:, g * PAIR:(g + 1) * PAIR]

    ld = ld_ref[...]
    p0 = ld.astype(BF16)
    r1 = ld - p0.astype(F32)
    p1 = r1.astype(BF16)
    p2 = (r1 - p1.astype(F32)).astype(BF16)
    cum = _dot(tri, p0) + _dot(tri, p1) + _dot(tri, p2)
    Pt = jnp.exp(cum)
    Pprev = jnp.exp(cum - ld)
    invP = jnp.exp(-cum)
    PL = jnp.exp(jnp.sum(ld, axis=0, keepdims=True))
    a = a_ref[...]
    kk = kk_ref[...]
    v = v_ref[...]
    At = -kk * Pprev
    Bt = (kk * a) * invP
    Kt = (k_ref[...] * (1.0 + (a - 1.0) * ka_ref[...])) * invP
    Rt = r_ref[...] * Pt
    Bh = Bt * PL
    Kh = Kt * PL

    As = [sm(sl(At, g)) for g in pairs]
    Rs = [sm(sl(Rt, g)) for g in pairs]
    Vs = [sm(sl(v, g)) for g in pairs]
    Gm = [_mm(jnp.concatenate([As[g], Rs[g]], axis=0),
              jnp.concatenate([sm(sl(Bt, g)), sm(sl(Kt, g))], axis=0), _NT) for g in pairs]
    Np = [jnp.where(strict, Gm[g][:2 * L, :2 * L], 0.0).astype(BF16) for g in pairs]
    Aak = [jnp.where(strict, Gm[g][:2 * L, 2 * L:], 0.0) for g in pairs]
    Mr = [jnp.concatenate([jnp.where(incl, Gm[g][2 * L:, :2 * L], 0.0),
                           jnp.where(incl, Gm[g][2 * L:, 2 * L:], 0.0)], axis=1).astype(BF16) for g in pairs]
    X = [jnp.concatenate([As[g], _mm(Aak[g], Vs[g])], axis=1) for g in pairs]
    nsteps = int(math.log2(L))
    for s in range(nsteps):
        X = [X[g] + _mm(Np[g], X[g]) for g in pairs]
        if s < nsteps - 1:
            Np = [_mm(Np[g], Np[g]).astype(BF16) for g in pairs]

    H = [h_scr[g] for g in pairs]
    AH = [_mm(jnp.concatenate([X[g][:, :PAIR], Rs[g]], axis=0), H[g]) for g in pairs]
    Us = [AH[g][:2 * L] + X[g][:, PAIR:] for g in pairs]
    Ys = [AH[g][2 * L:] + _mm(Mr[g], jnp.concatenate([Us[g], Vs[g]], axis=0)) for g in pairs]
    upd = [_mm(jnp.concatenate([sl(Bh, g), sl(Kh, g)], axis=0),
               jnp.concatenate([Us[g][:L] + Us[g][L:], sl(v, g)], axis=0), _TN) for g in pairs]
    for g in pairs:
        y_ref[:, g * PAIR:(g + 1) * PAIR] = Ys[g][:L] + Ys[g][L:]
        plc = jnp.transpose(jnp.broadcast_to(sl(PL, g), (PAIR, PAIR)))
        h_scr[g] = H[g] * plc + jnp.where(bd, upd[g], 0.0)

    @pl.when(c == nc - 1)
    def _():
        hT_ref[...] = h_scr[...]


def _wkv_scan(s1, kk, s2, k_a, h0, B, T, D):
    G = min(SCAN_PAIRS, D // PAIR)
    W = G * PAIR
    L = CHUNK
    nc = T // L
    npg = D // W
    N1 = s1.shape[1]
    s1 = s1.reshape(B, T, N1)
    kk = kk.reshape(B, T, D)
    s2 = s2.reshape(B, T, 5 * D)
    chunk = lambda d, c: c + d * (nc - 1 - 2 * c)
    tok = lambda off: pl.BlockSpec((None, L, W), lambda d, b, p, c: (b, chunk(d, c), off + p))
    y, hT = pl.pallas_call(
        functools.partial(_scan_kernel, G=G),
        grid=(2, B, npg, nc),
        in_specs=[
            tok(0), tok(npg), tok(2 * npg),
            pl.BlockSpec((None, L, W), lambda d, b, p, c: (b, chunk(d, c), p)),
            pl.BlockSpec((None, L, W), lambda d, b, p, c: (b, chunk(d, c), d * npg + p)),
            pl.BlockSpec((None, L, W), lambda d, b, p, c: (b, chunk(d, c), (2 + d) * npg + p)),
            pl.BlockSpec((1, W), lambda d, b, p, c: (0, p)),
            pl.BlockSpec((None, None, G, PAIR, PAIR), lambda d, b, p, c: (d, b, p, 0, 0)),
        ],
        out_specs=[
            pl.BlockSpec((None, None, L, W), lambda d, b, p, c: (d, b, chunk(d, c), p)),
            pl.BlockSpec((None, None, G, PAIR, PAIR), lambda d, b, p, c: (d, b, p, 0, 0)),
        ],
        out_shape=[jax.ShapeDtypeStruct((2, B, T, D), F32),
                   jax.ShapeDtypeStruct((2, B, D // PAIR, PAIR, PAIR), F32)],
        scratch_shapes=[pltpu.VMEM((G, PAIR, PAIR), F32)],
        compiler_params=_cparams(("arbitrary", "arbitrary", "arbitrary", "arbitrary")),
        name="wkv_scan",
    )(s1, s1, s1, kk, s2, s2, k_a, h0)
    return y.reshape(2, B * T, D), hT


def _rwkv_out_kernel(y0_ref, y1_ref, r_ref, k_ref, v_ref, a0_ref, a1_ref, g_ref,
                     lnw_ref, lnb_ref, rk_ref, ka_ref, ones_ref, w_ref, x_ref, gate_ref,
                     o_ref, a_scr, *, D):
    n = pl.program_id(1)

    @pl.when(n == 0)
    def _():
        ones = ones_ref[...]
        inv = 1.0 / RWKV_HEAD

        def gsum(t):
            parts = [_dot_hilo(t[:, s * MXU_DIM:(s + 1) * MXU_DIM], ones) for s in range(D // MXU_DIM)]
            return jnp.concatenate(parts, axis=1)

        wkv = y0_ref[...] + y1_ref[...]
        cen = wkv - gsum(wkv) * inv
        var = gsum(cen * cen) * inv
        yn = cen * lax.rsqrt(var + GN_EPS) * lnw_ref[...] + lnb_ref[...]
        ks = k_ref[...] * (2.0 + (a0_ref[...] + a1_ref[...] - 2.0) * ka_ref[...])
        bonus = gsum(r_ref[...] * ks * rk_ref[...]) * v_ref[...]
        a_scr[...] = ((yn + bonus) * g_ref[...]).astype(BF16)

    o_ref[...] = x_ref[...] + gate_ref[...] * _dot(a_scr[...], w_ref[...])


def _rwkv_out(y, s1, s2, x, mod, row_fn, ln_w, ln_b, r_k, k_a, w_o, tm):
    M, D = x.shape
    tn = w_o.shape[2]
    big = lambda col: pl.BlockSpec((tm, D), lambda i, n: (i, col))
    row = pl.BlockSpec((1, D), lambda i, n: (0, 0))
    return pl.pallas_call(
        functools.partial(_rwkv_out_kernel, D=D),
        grid=(M // tm, D // tn),
        in_specs=[
            pl.BlockSpec((None, tm, D), lambda i, n: (0, i, 0)),
            pl.BlockSpec((None, tm, D), lambda i, n: (1, i, 0)),
            big(0), big(1), big(2),
            big(2), big(3), big(4),
            row, row, row, row,
            pl.BlockSpec((MXU_DIM, MXU_DIM), lambda i, n: (0, 0)),
            pl.BlockSpec((None, D, tn), lambda i, n: (n, 0, 0)),
            pl.BlockSpec((tm, tn), lambda i, n: (i, n)),
            pl.BlockSpec((None, 1, tn), lambda i, n: (row_fn(i), 0, 2 * (D // tn) + n)),
        ],
        out_specs=pl.BlockSpec((tm, tn), lambda i, n: (i, n)),
        out_shape=jax.ShapeDtypeStruct((M, D), F32),
        scratch_shapes=[pltpu.VMEM((tm, D), BF16)],
        compiler_params=_cparams(("arbitrary", "arbitrary")),
        name="rwkv_out",
    )(y, y, s1, s1, s1, s2, s2, s2, ln_w, ln_b, r_k, k_a, _group_ones(MXU_DIM, RWKV_HEAD), w_o, x, mod)


def _ffn_kernel(x_ref, g_ref, sh_ref, sc_ref, gate_ref, w1_ref, w3_ref, w2_ref, o_ref, a_scr, *, tf):
    f = pl.program_id(1)
    nf = pl.num_programs(1)

    @pl.when(f == 0)
    def _():
        a_scr[...] = _modulate(x_ref[...], g_ref[...], sh_ref[...], sc_ref[...]).astype(BF16)
        o_ref[...] = jnp.zeros_like(o_ref)

    a = a_scr[...]
    acts = []
    for s in range(tf // MXU_DIM):
        cs = slice(s * MXU_DIM, (s + 1) * MXU_DIM)
        h1 = _dot(a, w1_ref[:, cs])
        h3 = _dot(a, w3_ref[:, cs])
        acts.append((_silu(h1) * h3).astype(BF16))
    o_ref[...] += _dot(jnp.concatenate(acts, axis=1), w2_ref[...])

    @pl.when(f == nf - 1)
    def _():
        o_ref[...] = x_ref[...] + gate_ref[...] * o_ref[...]


def _ffn(x, mod, row_fn, gain, w1b, w3b, w2, tm):
    M, D = x.shape
    nf, _, tf = w1b.shape
    modspec = lambda ch: pl.BlockSpec((None, 1, D), lambda i, f: (row_fn(i), 0, ch))
    return pl.pallas_call(
        functools.partial(_ffn_kernel, tf=tf),
        grid=(M // tm, nf),
        in_specs=[
            pl.BlockSpec((tm, D), lambda i, f: (i, 0), pipeline_mode=pl.Buffered(1)),
            pl.BlockSpec((1, D), lambda i, f: (0, 0)),
            modspec(3), modspec(4), modspec(5),
            pl.BlockSpec((None, D, tf), lambda i, f: (f, 0, 0)),
            pl.BlockSpec((None, D, tf), lambda i, f: (f, 0, 0)),
            pl.BlockSpec((tf, D), lambda i, f: (f, 0)),
        ],
        out_specs=pl.BlockSpec((tm, D), lambda i, f: (i, 0)),
        out_shape=jax.ShapeDtypeStruct((M, D), F32),
        scratch_shapes=[pltpu.VMEM((tm, D), BF16)],
        compiler_params=_cparams(("arbitrary", "arbitrary")),
        name="ffn",
    )(x, gain, mod, mod, mod, w1b, w3b, w2)


def _block_cols(w, tn):
    K, N = w.shape
    return jnp.transpose(w.reshape(K, N // tn, tn), (1, 0, 2))


def _qkv_kernel(x_ref, g_ref, sh_ref, sc_ref, w_ref, hg_ref, cos_ref, sin_ref, o_ref, a_scr, *, nrope, nnorm, tn):
    n = pl.program_id(1)

    @pl.when(n == 0)
    def _():
        a_scr[...] = _modulate(x_ref[...], g_ref[...], sh_ref[...], sc_ref[...]).astype(BF16)

    acc = _dot(a_scr[...], w_ref[...])

    @pl.when(n < nnorm)
    def _():
        hg = hg_ref[...]
        lane = lax.broadcasted_iota(jnp.int32, (acc.shape[0], NA_HEAD), 1)
        first = (lane & (NA_HEAD // 2 - 1)) < NA_HEAD // 4
        for s in range(tn // NA_HEAD):
            t = acc[:, s * NA_HEAD:(s + 1) * NA_HEAD]
            t = t * _rms_scale(t) * hg
            if nrope:
                swapped = jnp.where(first, pltpu.roll(t, NA_HEAD - NA_HEAD // 4, axis=1),
                                    pltpu.roll(t, NA_HEAD // 4, axis=1))
                roped = t * cos_ref[...] + swapped * sin_ref[...]
                t = jnp.where(n < nrope, roped, t)
            o_ref[:, s * NA_HEAD:(s + 1) * NA_HEAD] = t.astype(o_ref.dtype)

    @pl.when(n >= nnorm)
    def _():
        o_ref[...] = acc.astype(o_ref.dtype)


def _na_qkv(x, mod, row_fn, gain, w, head_gain, cos, sin, *, nrope, nnorm, tm):
    M, D = x.shape
    nblk, _, tn = w.shape
    N = nblk * tn
    tpb = cos.shape[0] // tm
    hrow = lambda n: jnp.minimum(n, head_gain.shape[0] - 1)
    return pl.pallas_call(
        functools.partial(_qkv_kernel, nrope=nrope, nnorm=nnorm, tn=tn),
        grid=(M // tm, N // tn),
        in_specs=[
            pl.BlockSpec((tm, D), lambda i, n: (i, 0)),
            pl.BlockSpec((1, D), lambda i, n: (0, 0)),
            pl.BlockSpec((None, 1, D), lambda i, n: (row_fn(i), 0, 0)),
            pl.BlockSpec((None, 1, D), lambda i, n: (row_fn(i), 0, 1)),
            pl.BlockSpec((None, D, tn), lambda i, n: (n, 0, 0)),
            pl.BlockSpec((None, 1, NA_HEAD), lambda i, n: (hrow(n), 0, 0)),
            pl.BlockSpec((tm, NA_HEAD), lambda i, n: (i % tpb, 0)),
            pl.BlockSpec((tm, NA_HEAD), lambda i, n: (i % tpb, 0)),
        ],
        out_specs=pl.BlockSpec((tm, tn), lambda i, n: (i, n)),
        out_shape=jax.ShapeDtypeStruct((M, N), BF16),
        scratch_shapes=[pltpu.VMEM((tm, D), BF16)],
        compiler_params=_cparams(("arbitrary", "arbitrary")),
        name="na_qkv",
    )(x, gain, mod, mod, w, head_gain, cos, sin)


def _na_kernel(q_ref, k_ref, v_ref, kc_ref, vc_ref, bias_ref, o_ref, *, rows, rb):
    j0 = pl.program_id(2) * rb
    win_h = min(WIN_H, rows)
    nk = win_h * GRID_W
    qrows = range(rb)
    rs = lambda t, j: t[j * GRID_W:(j + 1) * GRID_W]

    qa = q_ref[...]
    s_ctx = lax.dot_general(qa, kc_ref[...], _NT, preferred_element_type=F32)
    r0 = [jnp.clip(j0 + j - WIN_H // 2, 0, rows - win_h) for j in qrows]
    off = [pl.multiple_of(r0[j] * GRID_W, GRID_W) for j in qrows]
    s_lat = [lax.dot_general(rs(qa, j), k_ref[pl.ds(off[j], nk), :], _NT, preferred_element_type=F32)
             + bias_ref[r0[j] - (j0 + j) + WIN_H - 1] for j in qrows]
    m = [jnp.maximum(jnp.max(s_lat[j], axis=-1, keepdims=True),
                     jnp.max(rs(s_ctx, j), axis=-1, keepdims=True)) for j in qrows]
    p_lat = [jnp.exp(s_lat[j] - m[j]) for j in qrows]
    p_ctx = jnp.exp(s_ctx - jnp.concatenate(m, axis=0))
    den = [jnp.sum(p_lat[j], axis=-1, keepdims=True) + jnp.sum(rs(p_ctx, j), axis=-1, keepdims=True)
           for j in qrows]
    o_ctx = _dot(p_ctx.astype(BF16), vc_ref[...])
    for j in qrows:
        o = _dot(p_lat[j].astype(BF16), v_ref[pl.ds(off[j], nk), :]) + rs(o_ctx, j)
        o_ref[j * GRID_W:(j + 1) * GRID_W, :] = (o / den[j]).astype(o_ref.dtype)


def _na_attention(qkv, kvc, bias, B, T, D, rb):
    H = D // NA_HEAD
    rows = T // GRID_W
    C = kvc.shape[1]
    nbias = bias.shape[1]
    return pl.pallas_call(
        functools.partial(_na_kernel, rows=rows, rb=rb),
        grid=(B, H, rows // rb),
        in_specs=[
            pl.BlockSpec((None, rb * GRID_W, NA_HEAD), lambda b, h, j: (b, j, h)),
            pl.BlockSpec((None, T, NA_HEAD), lambda b, h, j: (b, 0, H + h)),
            pl.BlockSpec((None, T, NA_HEAD), lambda b, h, j: (b, 0, 2 * H + h)),
            pl.BlockSpec((None, C, NA_HEAD), lambda b, h, j: (b, 0, h)),
            pl.BlockSpec((None, C, NA_HEAD), lambda b, h, j: (b, 0, H + h)),
            pl.BlockSpec((None, nbias, GRID_W, bias.shape[3]), lambda b, h, j: (h, 0, 0, 0)),
        ],
        out_specs=pl.BlockSpec((None, rb * GRID_W, NA_HEAD), lambda b, h, j: (b, j, h)),
        out_shape=jax.ShapeDtypeStruct((B, T, D), BF16),
        compiler_params=_cparams(("arbitrary", "arbitrary", "arbitrary")),
        name="na_attention",
    )(qkv, qkv, qkv, kvc, kvc, bias)


def _proj_res_kernel(a_ref, w_ref, x_ref, gate_ref, o_ref):
    o_ref[...] = x_ref[...] + gate_ref[...] * _dot(a_ref[...], w_ref[...])


def _proj_residual(a, w, x, mod, row_fn, chunk, tm):
    M, K = a.shape
    nblk, _, tn = w.shape
    N = nblk * tn
    return pl.pallas_call(
        _proj_res_kernel,
        grid=(M // tm, N // tn),
        in_specs=[
            pl.BlockSpec((tm, K), lambda i, n: (i, 0)),
            pl.BlockSpec((None, K, tn), lambda i, n: (n, 0, 0)),
            pl.BlockSpec((tm, tn), lambda i, n: (i, n)),
            pl.BlockSpec((None, 1, tn), lambda i, n: (row_fn(i), 0, chunk * (N // tn) + n)),
        ],
        out_specs=pl.BlockSpec((tm, tn), lambda i, n: (i, n)),
        out_shape=jax.ShapeDtypeStruct((M, N), F32),
        compiler_params=_cparams(("arbitrary", "arbitrary")),
        name="proj_residual",
    )(a, w, x, mod)


def _rope_tables(T):
    t = np.arange(T)
    half = NA_HEAD // 2
    freqs = ROPE_THETA ** (-np.arange(0, half, 2, dtype=np.float64) / half)
    ang_r = (t // GRID_W).astype(np.float64)[:, None] * freqs
    ang_c = (t % GRID_W).astype(np.float64)[:, None] * freqs
    cos = np.concatenate([np.cos(ang_r), np.cos(ang_r), np.cos(ang_c), np.cos(ang_c)], axis=1)
    sin = np.concatenate([-np.sin(ang_r), np.sin(ang_r), -np.sin(ang_c), np.sin(ang_c)], axis=1)
    return jnp.asarray(cos, F32), jnp.asarray(sin, F32)


def _na_bias(rpb, rows):
    win_h = min(WIN_H, rows)
    qc = np.arange(GRID_W)[:, None]
    kc = np.arange(GRID_W)[None, :]
    wstart = np.clip(qc - WIN_W // 2, 0, GRID_W - WIN_W)
    valid = (kc >= wstart) & (kc < wstart + WIN_W)
    rel = np.clip(kc - qc, -(WIN_W - 1), WIN_W - 1) + WIN_W - 1
    g = rpb[:, :, rel]
    g = jnp.where(valid[None, None], g, NEG_INF)
    tabs = []
    for i0 in range(2 * WIN_H - win_h):
        blk = g[:, i0:i0 + win_h]
        tabs.append(jnp.transpose(blk, (0, 2, 1, 3)).reshape(g.shape[0], GRID_W, win_h * GRID_W))
    return jnp.stack(tabs, axis=1)


def _pad_cols(w, n):
    return jnp.pad(w, ((0, 0), (0, n - w.shape[1])))


def _pad_rows(w, n):
    return jnp.pad(w, ((0, n - w.shape[0]), (0, 0)))


def kernel(x, c, ctx, c_ctx, ada_w, ada_b, norm1, norm2, rwkv_mu, rwkv_w_r, rwkv_w_k, rwkv_w_v, rwkv_w_o, rwkv_w0, rwkv_w1, rwkv_w2, rwkv_a0, rwkv_a1, rwkv_a2, rwkv_g1, rwkv_g2, rwkv_k_k, rwkv_k_a, rwkv_r_k, rwkv_ln_w, rwkv_ln_b, na_w_qkv, na_w_o, na_q_gain, na_k_gain, na_rpb, ffn_w1, ffn_w3, ffn_w2):
    B, T, D = x.shape
    C = ctx.shape[1]
    rows = T // GRID_W
    depth = ada_w.shape[0]
    n_mix = 2
    TM = 512
    TMO = 256
    TNO = min(1024, D)
    TF = 512 if ffn_w1.shape[-1] % 512 == 0 else 256
    TMF = min(1024, T)

    cvec = jnp.concatenate([c, c_ctx[None, :], jnp.zeros((8 - B - 1, D), F32)], axis=0)
    mods = _ada(cvec, ada_w, ada_b)
    lat_row = lambda tm: (lambda i: i // (T // tm))
    ctx_row = lambda i: B

    xl = x.reshape(B * T, D)
    xc = ctx.reshape(B * C, D)

    for i in range(depth):
        last = i == depth - 1
        j = i // n_mix
        mod = mods[i].reshape(8, 1, 6 * D)
        g1 = norm1[i].reshape(1, D)
        g2 = norm2[i].reshape(1, D)
        w1 = _block_cols(ffn_w1[i].astype(BF16), TF)
        w3 = _block_cols(ffn_w3[i].astype(BF16), TF)
        w2 = ffn_w2[i].astype(BF16)
        if i % n_mix == 0:
            lw = rwkv_w1.shape[-1]
            la = rwkv_a1.shape[-1]
            lg = rwkv_g1.shape[-1]
            wcat = jnp.concatenate([
                rwkv_w_r[j], rwkv_w_k[j], rwkv_w_v[j],
                _pad_cols(jnp.concatenate([rwkv_w1[j, 0], rwkv_w1[j, 1]], axis=1), LORA_BLK),
                _pad_cols(jnp.concatenate([rwkv_a1[j, 0], rwkv_a1[j, 1]], axis=1), LORA_BLK),
                _pad_cols(rwkv_g1[j], 2 * LORA_BLK)], axis=1).astype(BF16)
            wcat = _block_cols(wcat, TN1)
            z = lambda n: jnp.zeros((n, D), F32)
            w2cat = jnp.concatenate([
                _pad_rows(rwkv_w2[j, 0], LORA_BLK),
                _pad_rows(jnp.concatenate([z(lw), rwkv_w2[j, 1]], axis=0), LORA_BLK),
                _pad_rows(rwkv_a2[j, 0], LORA_BLK),
                _pad_rows(jnp.concatenate([z(la), rwkv_a2[j, 1]], axis=0), LORA_BLK),
                _pad_rows(rwkv_g2[j], LORA_BLK)], axis=1).astype(BF16)
            bias2 = jnp.concatenate([rwkv_w0[j, 0], rwkv_w0[j, 1], rwkv_a0[j, 0], rwkv_a0[j, 1],
                                     jnp.zeros((D,), F32)]).reshape(1, 5 * D)
            k_k = rwkv_k_k[j].reshape(1, D)
            k_a = rwkv_k_a[j].reshape(1, D)
            r_k = rwkv_r_k[j].reshape(1, D)
            ln_w = rwkv_ln_w[j].reshape(1, D)
            ln_b = rwkv_ln_b[j].reshape(1, D)
            w_o = _block_cols(rwkv_w_o[j].astype(BF16), TNO)

            s1c, kkc = _rwkv_stage1(xc, mod, ctx_row, g1, rwkv_mu[j], k_k, wcat, latent=False, rows=rows, tm=C)
            s2c = _rwkv_stage2(s1c, w2cat, bias2, D, C)
            h0 = jnp.zeros((2, B, D // PAIR, PAIR, PAIR), F32)
            yc, hc = _wkv_scan(s1c, kkc, s2c, k_a, h0, B, C, D)
            s1l, kkl = _rwkv_stage1(xl, mod, lat_row(TM), g1, rwkv_mu[j], k_k, wcat, latent=True, rows=rows, tm=TM)
            s2l = _rwkv_stage2(s1l, w2cat, bias2, D, TM)
            yl, _ = _wkv_scan(s1l, kkl, s2l, k_a, hc, B, T, D)
            xl = _rwkv_out(yl, s1l, s2l, xl, mod, lat_row(TMO), ln_w, ln_b, r_k, k_a, w_o, TMO)
            if not last:
                xc = _rwkv_out(yc, s1c, s2c, xc, mod, ctx_row, ln_w, ln_b, r_k, k_a, w_o, min(TMO, C))
        else:
            H = D // NA_HEAD
            cos, sin = _rope_tables(T)
            scale = NA_HEAD ** -0.5
            hg = jnp.stack([jnp.broadcast_to(na_q_gain[j] * scale, (NA_HEAD,)), na_k_gain[j]]).reshape(2, 1, NA_HEAD)
            tn = 512
            nq = D // tn
            wqkv = _block_cols(na_w_qkv[j].astype(BF16), tn)
            hg_blocks = jnp.concatenate([jnp.broadcast_to(hg[0], (nq, 1, NA_HEAD)),
                                         jnp.broadcast_to(hg[1], (nq, 1, NA_HEAD))], axis=0)
            qkv = _na_qkv(xl, mod, lat_row(TM), g1, wqkv, hg_blocks, cos, sin, nrope=2 * nq, nnorm=2 * nq, tm=TM)
            kg_blocks = jnp.broadcast_to(hg[1], (nq, 1, NA_HEAD))
            kvc = _na_qkv(xc, mod, ctx_row, g1, wqkv[nq:], kg_blocks, cos[:C], sin[:C], nrope=0, nnorm=nq, tm=C)
            bias = _na_bias(na_rpb[j], rows)
            o = _na_attention(qkv.reshape(B, T, 3 * D), kvc.reshape(B, C, 2 * D), bias, B, T, D, 8)
            xl = _proj_residual(o.reshape(B * T, D), _block_cols(na_w_o[j].astype(BF16), TNO), xl, mod,
                                lat_row(TM), 2, TM)
            if not last:
                raise NotImplementedError("attention layer with context output")
        xl = _ffn(xl, mod, lat_row(TMF), g2, w1, w3, w2, TMF)
        if not last:
            xc = _ffn(xc, mod, ctx_row, g2, w1, w3, w2, C)
    return xl.reshape(B, T, D)
```

```python
import functools
import math

import numpy as np
import jax
import jax.numpy as jnp
from jax import lax
from jax.experimental import pallas as pl
from jax.experimental.pallas import tpu as pltpu

F32 = jnp.float32
BF16 = jnp.bfloat16

GRID_W = 64
RMS_EPS = 1e-6
RWKV_HEAD = 64
GN_EPS = 64e-5
NA_HEAD = 128
WIN_H = 8
WIN_W = 16
ROPE_THETA = 10000.0
NEG_INF = -1e30

LANE = 128
MXU_DIM = 256
VMEM_LIMIT = 56 * 1024 * 1024

CHUNK = 64
PAIR = 2 * RWKV_HEAD
SCAN_PAIRS = 8
LORA_BLK = 256
TN1 = 512


def _cparams(sem):
    return pltpu.CompilerParams(dimension_semantics=sem, vmem_limit_bytes=VMEM_LIMIT)


def _dot(a, b):
    return jnp.dot(a, b, preferred_element_type=F32)


def _split2(x):
    hi = x.astype(BF16)
    lo = (x - hi.astype(F32)).astype(BF16)
    return hi, lo


def _dot_hilo(a, b_bf16):
    hi, lo = _split2(a)
    return _dot(hi, b_bf16) + _dot(lo, b_bf16)


def _rms_scale(x):
    return lax.rsqrt(jnp.mean(x * x, axis=-1, keepdims=True) + RMS_EPS)


def _modulate(x, gain, shift, scale):
    return (x * _rms_scale(x) * gain) * (1.0 + scale) + shift


def _sigmoid(x):
    return 1.0 / (1.0 + jnp.exp(-x))


def _silu(x):
    return x * _sigmoid(x)


def _group_ones(n, group):
    idx = np.arange(n) // group
    return jnp.asarray((idx[:, None] == idx[None, :]).astype(np.float32), dtype=BF16)


def _ada_kernel(c_ref, w_ref, b_ref, o_ref):
    s = _silu(c_ref[...]).astype(BF16)
    o_ref[...] = _dot(s, w_ref[...].astype(BF16)) + b_ref[...]


def _ada(cvec, ada_w, ada_b):
    depth, D, N = ada_w.shape
    R = cvec.shape[0]
    tn = 1024
    return pl.pallas_call(
        _ada_kernel,
        grid=(depth, N // tn),
        in_specs=[
            pl.BlockSpec((R, D), lambda l, n: (0, 0)),
            pl.BlockSpec((None, D, tn), lambda l, n: (l, 0, n)),
            pl.BlockSpec((None, 1, tn), lambda l, n: (l, 0, n)),
        ],
        out_specs=pl.BlockSpec((None, R, tn), lambda l, n: (l, 0, n)),
        out_shape=jax.ShapeDtypeStruct((depth, R, N), F32),
        compiler_params=_cparams(("arbitrary", "arbitrary")),
        name="ada",
    )(cvec, ada_w, ada_b.reshape(depth, 1, N))


def _s1_kernel(xp_ref, x_ref, xn_ref, g_ref, sh_ref, sc_ref, mu_ref, kkw_ref, ones_ref, w_ref,
               o_ref, kk_ref, a_scr, *, latent, rows, tm, D, nb):
    i = pl.program_id(0)
    n = pl.program_id(1)

    @pl.when(n == 0)
    def _():
        g = g_ref[...]
        sh = sh_ref[...]
        sc = sc_ref[...]
        h = _modulate(x_ref[...], g, sh, sc)
        if latent:
            q = D // 4
            tok = lax.broadcasted_iota(jnp.int32, (tm, q), 0)
            col = tok & (GRID_W - 1)
            grow = (i * (tm // GRID_W) + tok // GRID_W) & (rows - 1)

            def halo(ref, lo):
                xh = ref[...]
                sl = slice(lo, lo + q)
                return (xh[:, sl] * _rms_scale(xh) * g[:, sl]) * (1.0 + sc[:, sl]) + sh[:, sl]

            left = jnp.where(col > 0, pltpu.roll(h[:, 0:q], 1, axis=0), 0.0)
            right = jnp.where(col < GRID_W - 1, pltpu.roll(h[:, q:2 * q], tm - 1, axis=0), 0.0)
            up = jnp.concatenate([halo(xp_ref, 2 * q), h[:tm - GRID_W, 2 * q:3 * q]], axis=0)
            up = jnp.where(grow > 0, up, 0.0)
            down = jnp.concatenate([h[GRID_W:, 3 * q:], halo(xn_ref, 3 * q)], axis=0)
            down = jnp.where(grow < rows - 1, down, 0.0)
            hs = jnp.concatenate([left, right, up, down], axis=1)
        else:
            half = D // 2
            tok = lax.broadcasted_iota(jnp.int32, (tm, half), 0)
            prev = jnp.where(tok > 0, pltpu.roll(h[:, :half], 1, axis=0), 0.0)
            nxt = jnp.where(tok < tm - 1, pltpu.roll(h[:, half:], tm - 1, axis=0), 0.0)
            hs = jnp.concatenate([prev, nxt], axis=1)
        xx = hs - h
        for j in range(6):
            a_scr[j] = (h + xx * mu_ref[j:j + 1, :]).astype(BF16)

    @pl.when(n < nb)
    def _():
        o_ref[...] = _dot(a_scr[0], w_ref[...]).astype(o_ref.dtype)

    @pl.when((n >= nb) & (n < 2 * nb))
    def _():
        acc = _dot(a_scr[2], w_ref[...])
        o_ref[...] = acc.astype(o_ref.dtype)
        kkf = acc * kkw_ref[...]
        sq = kkf * kkf
        ones = ones_ref[...]
        parts = []
        for s in range(TN1 // MXU_DIM):
            parts.append(_dot_hilo(sq[:, s * MXU_DIM:(s + 1) * MXU_DIM], ones))
        ss = jnp.concatenate(parts, axis=1)
        kk_ref[...] = (kkf / jnp.maximum(jnp.sqrt(ss), 1e-12)).astype(kk_ref.dtype)

    @pl.when((n >= 2 * nb) & (n < 3 * nb))
    def _():
        o_ref[...] = _dot(a_scr[3], w_ref[...]).astype(o_ref.dtype)

    @pl.when(n == 3 * nb)
    def _():
        o_ref[:, :LORA_BLK] = jnp.tanh(_dot(a_scr[1], w_ref[:, :LORA_BLK])).astype(o_ref.dtype)
        o_ref[:, LORA_BLK:] = _dot(a_scr[4], w_ref[:, LORA_BLK:]).astype(o_ref.dtype)

    @pl.when(n == 3 * nb + 1)
    def _():
        o_ref[...] = _sigmoid(_dot(a_scr[5], w_ref[...])).astype(o_ref.dtype)


def _rwkv_stage1(x, mod, row_fn, gain, mu, k_k, wcat, *, latent, rows, tm):
    M, D = x.shape
    nb = D // TN1
    nsteps = wcat.shape[0]
    N1 = nsteps * TN1
    hb = tm // GRID_W if latent else 1
    hrows = GRID_W if latent else 8
    nhb = M // hrows
    kcol = lambda n: jnp.clip(n - nb, 0, nb - 1)
    kern = functools.partial(_s1_kernel, latent=latent, rows=rows, tm=tm, D=D, nb=nb)
    return pl.pallas_call(
        kern,
        grid=(M // tm, nsteps),
        in_specs=[
            pl.BlockSpec((hrows, D), lambda i, n: (jnp.maximum(i * hb - 1, 0), 0)),
            pl.BlockSpec((tm, D), lambda i, n: (i, 0)),
            pl.BlockSpec((hrows, D), lambda i, n: (jnp.minimum(i * hb + hb, nhb - 1), 0)),
            pl.BlockSpec((1, D), lambda i, n: (0, 0)),
            pl.BlockSpec((None, 1, D), lambda i, n: (row_fn(i), 0, 0)),
            pl.BlockSpec((None, 1, D), lambda i, n: (row_fn(i), 0, 1)),
            pl.BlockSpec((6, D), lambda i, n: (0, 0)),
            pl.BlockSpec((1, TN1), lambda i, n: (0, kcol(n))),
            pl.BlockSpec((MXU_DIM, MXU_DIM), lambda i, n: (0, 0)),
            pl.BlockSpec((None, D, TN1), lambda i, n: (n, 0, 0)),
        ],
        out_specs=[
            pl.BlockSpec((tm, TN1), lambda i, n: (i, n)),
            pl.BlockSpec((tm, TN1), lambda i, n: (i, kcol(n))),
        ],
        out_shape=[jax.ShapeDtypeStruct((M, N1), BF16), jax.ShapeDtypeStruct((M, D), BF16)],
        scratch_shapes=[pltpu.VMEM((6, tm, D), BF16)],
        compiler_params=_cparams(("arbitrary", "arbitrary")),
        name="rwkv_stage1",
    )(x, x, x, gain, mod, mod, mu, k_k, _group_ones(MXU_DIM, RWKV_HEAD), wcat)


_NT = (((1,), (1,)), ((), ()))
_TN = (((0,), (0,)), ((), ()))


def _mm(a, b, dims=None):
    a = a.astype(BF16)
    b = b.astype(BF16)
    if dims is None:
        return jnp.dot(a, b, preferred_element_type=F32)
    return lax.dot_general(a, b, dims, preferred_element_type=F32)


def _scan_kernel(r_ref, k_ref, v_ref, kk_ref, lo_ref, w2_ref, a2_ref, w0_ref, a0_ref, ka_ref, h0_ref,
                 y_ref, a_ref, hT_ref, h_scr, *, G):
    d = pl.program_id(0)
    c = pl.program_id(3)
    nc = pl.num_programs(3)
    L = CHUNK
    pairs = range(G)

    @pl.when(c == 0)
    def _():
        h_scr[...] = h0_ref[...]

    sgn = 1 - 2 * d
    lane = lax.broadcasted_iota(jnp.int32, (L, PAIR), 1)
    m0 = lane < RWKV_HEAD
    ti = lax.broadcasted_iota(jnp.int32, (L, L), 0)
    si = lax.broadcasted_iota(jnp.int32, (L, L), 1)
    tri = jnp.where((si - ti) * sgn <= 0, 1.0, 0.0).astype(BF16)
    i2 = lax.broadcasted_iota(jnp.int32, (2 * L, 2 * L), 0)
    j2 = lax.broadcasted_iota(jnp.int32, (2 * L, 2 * L), 1)
    dif = jnp.where((i2 // L) == (j2 // L), ((j2 % L) - (i2 % L)) * sgn, 1)
    strict = dif < 0
    incl = dif <= 0
    r2 = lax.broadcasted_iota(jnp.int32, (PAIR, PAIR), 0)
    c2 = lax.broadcasted_iota(jnp.int32, (PAIR, PAIR), 1)
    bd = (r2 // RWKV_HEAD) == (c2 // RWKV_HEAD)

    def sm(x):
        return jnp.concatenate([jnp.where(m0, x, 0.0), jnp.where(m0, 0.0, x)], axis=0)

    def sl(x, g):
        return x[:, g * PAIR:(g + 1) * PAIR]

    u = -(_dot(lo_ref[:, :LORA_BLK], w2_ref[...]) + w0_ref[...])
    ld = -jnp.exp(-(jnp.maximum(u, 0.0) + jnp.log(1.0 + jnp.exp(-jnp.abs(u)))) - 0.5)
    a = _sigmoid(_dot(lo_ref[:, LORA_BLK:], a2_ref[...]) + a0_ref[...])
    a_ref[...] = a.astype(a_ref.dtype)
    p0 = ld.astype(BF16)
    r1 = ld - p0.astype(F32)
    p1 = r1.astype(BF16)
    p2 = (r1 - p1.astype(F32)).astype(BF16)
    cum = _dot(tri, p0) + _dot(tri, p1) + _dot(tri, p2)
    Pt = jnp.exp(cum)
    Pprev = jnp.exp(cum - ld)
    invP = jnp.exp(-cum)
    PL = jnp.exp(jnp.sum(ld, axis=0, keepdims=True))
    kk = kk_ref[...].astype(F32)
    v = v_ref[...].astype(F32)
    At = -kk * Pprev
    Bt = (kk * a) * invP
    Kt = (k_ref[...].astype(F32) * (1.0 + (a - 1.0) * ka_ref[...])) * invP
    Rt = r_ref[...].astype(F32) * Pt
    Bh = Bt * PL
    Kh = Kt * PL

    As = [sm(sl(At, g)) for g in pairs]
    Rs = [sm(sl(Rt, g)) for g in pairs]
    Vs = [sm(sl(v, g)) for g in pairs]
    Gm = [_mm(jnp.concatenate([As[g], Rs[g]], axis=0),
              jnp.concatenate([sm(sl(Bt, g)), sm(sl(Kt, g))], axis=0), _NT) for g in pairs]
    Np = [jnp.where(strict, Gm[g][:2 * L, :2 * L], 0.0).astype(BF16) for g in pairs]
    Aak = [jnp.where(strict, Gm[g][:2 * L, 2 * L:], 0.0) for g in pairs]
    Mr = [jnp.concatenate([jnp.where(incl, Gm[g][2 * L:, :2 * L], 0.0),
                           jnp.where(incl, Gm[g][2 * L:, 2 * L:], 0.0)], axis=1).astype(BF16) for g in pairs]
    X = [jnp.concatenate([As[g], _mm(Aak[g], Vs[g])], axis=1) for g in pairs]
    nsteps = int(math.log2(L))
    for s in range(nsteps):
        X = [X[g] + _mm(Np[g], X[g]) for g in pairs]
        if s < nsteps - 1:
            Np = [_mm(Np[g], Np[g]).astype(BF16) for g in pairs]

    H = [h_scr[g] for g in pairs]
    AH = [_mm(jnp.concatenate([X[g][:, :PAIR], Rs[g]], axis=0), H[g]) for g in pairs]
    Us = [AH[g][:2 * L] + X[g][:, PAIR:] for g in pairs]
    Ys = [AH[g][2 * L:] + _mm(Mr[g], jnp.concatenate([Us[g], Vs[g]], axis=0)) for g in pairs]
    upd = [_mm(jnp.concatenate([sl(Bh, g), sl(Kh, g)], axis=0),
               jnp.concatenate([Us[g][:L] + Us[g][L:], sl(v, g)], axis=0), _TN) for g in pairs]
    for g in pairs:
        y_ref[:, g * PAIR:(g + 1) * PAIR] = (Ys[g][:L] + Ys[g][L:]).astype(y_ref.dtype)
        plc = jnp.transpose(jnp.broadcast_to(sl(PL, g), (PAIR, PAIR)))
        h_scr[g] = H[g] * plc + jnp.where(bd, upd[g], 0.0)

    @pl.when(c == nc - 1)
    def _():
        hT_ref[...] = h_scr[...]


def _wkv_scan(s1, kk, w2d, a2d, w0, a0, k_a, h0, B, T, D):
    G = min(SCAN_PAIRS, D // PAIR)
    W = G * PAIR
    L = CHUNK
    nc = T // L
    npg = D // W
    N1 = s1.shape[1]
    s1 = s1.reshape(B, T, N1)
    kk = kk.reshape(B, T, D)
    chunk = lambda d, c: c + d * (nc - 1 - 2 * c)
    tok = lambda off: pl.BlockSpec((None, L, W), lambda d, b, p, c: (b, chunk(d, c), off + p))
    dirw = lambda rows: pl.BlockSpec((None, rows, W), lambda d, b, p, c: (d, 0, p))
    state = pl.BlockSpec((None, None, G, PAIR, PAIR), lambda d, b, p, c: (d, b, p, 0, 0))
    out_tok = pl.BlockSpec((None, None, L, W), lambda d, b, p, c: (d, b, chunk(d, c), p))
    y, a, hT = pl.pallas_call(
        functools.partial(_scan_kernel, G=G),
        grid=(2, B, npg, nc),
        in_specs=[
            tok(0), tok(npg), tok(2 * npg),
            pl.BlockSpec((None, L, W), lambda d, b, p, c: (b, chunk(d, c), p)),
            pl.BlockSpec((None, L, 2 * LORA_BLK), lambda d, b, p, c: (b, chunk(d, c), 3 * D // (2 * LORA_BLK))),
            dirw(LORA_BLK), dirw(LORA_BLK), dirw(1), dirw(1),
            pl.BlockSpec((1, W), lambda d, b, p, c: (0, p)),
            state,
        ],
        out_specs=[out_tok, out_tok, state],
        out_shape=[jax.ShapeDtypeStruct((2, B, T, D), BF16),
                   jax.ShapeDtypeStruct((2, B, T, D), BF16),
                   jax.ShapeDtypeStruct((2, B, D // PAIR, PAIR, PAIR), F32)],
        scratch_shapes=[pltpu.VMEM((G, PAIR, PAIR), F32)],
        compiler_params=_cparams(("arbitrary", "arbitrary", "arbitrary", "arbitrary")),
        name="wkv_scan",
    )(s1, s1, s1, kk, s1, w2d, a2d, w0, a0, k_a, h0)
    return y.reshape(2, B * T, D), a.reshape(2, B * T, D), hT


def _rwkv_out_kernel(y0_ref, y1_ref, r_ref, k_ref, v_ref, a0_ref, a1_ref, sg_ref, g2_ref,
                     lnw_ref, lnb_ref, rk_ref, ka_ref, ones_ref, w_ref, x_ref, gate_ref,
                     o_ref, a_scr, *, D):
    n = pl.program_id(1)

    @pl.when(n == 0)
    def _():
        ones = ones_ref[...]
        inv = 1.0 / RWKV_HEAD
        f = lambda ref: ref[...].astype(F32)

        def gsum(t):
            parts = [_dot_hilo(t[:, s * MXU_DIM:(s + 1) * MXU_DIM], ones) for s in range(D // MXU_DIM)]
            return jnp.concatenate(parts, axis=1)

        wkv = f(y0_ref) + f(y1_ref)
        cen = wkv - gsum(wkv) * inv
        var = gsum(cen * cen) * inv
        yn = cen * lax.rsqrt(var + GN_EPS) * lnw_ref[...] + lnb_ref[...]
        ks = f(k_ref) * (2.0 + (f(a0_ref) + f(a1_ref) - 2.0) * ka_ref[...])
        bonus = gsum(f(r_ref) * ks * rk_ref[...]) * f(v_ref)
        gate = _dot(sg_ref[:, :LORA_BLK], g2_ref[...])
        a_scr[...] = ((yn + bonus) * gate).astype(BF16)

    o_ref[...] = x_ref[...] + gate_ref[...] * _dot(a_scr[...], w_ref[...])


def _rwkv_out(y, a, s1, g2, x, mod, row_fn, ln_w, ln_b, r_k, k_a, w_o, tm):
    M, D = x.shape
    tn = w_o.shape[2]
    big = lambda col: pl.BlockSpec((tm, D), lambda i, n: (i, col))
    dirblk = lambda d: pl.BlockSpec((None, tm, D), lambda i, n: (d, i, 0))
    row = pl.BlockSpec((1, D), lambda i, n: (0, 0))
    return pl.pallas_call(
        functools.partial(_rwkv_out_kernel, D=D),
        grid=(M // tm, D // tn),
        in_specs=[
            dirblk(0), dirblk(1),
            big(0), big(1), big(2),
            dirblk(0), dirblk(1),
            pl.BlockSpec((tm, 2 * LORA_BLK), lambda i, n: (i, 3 * D // (2 * LORA_BLK) + 1)),
            pl.BlockSpec((LORA_BLK, D), lambda i, n: (0, 0)),
            row, row, row, row,
            pl.BlockSpec((MXU_DIM, MXU_DIM), lambda i, n: (0, 0)),
            pl.BlockSpec((None, D, tn), lambda i, n: (n, 0, 0)),
            pl.BlockSpec((tm, tn), lambda i, n: (i, n)),
            pl.BlockSpec((None, 1, tn), lambda i, n: (row_fn(i), 0, 2 * (D // tn) + n)),
        ],
        out_specs=pl.BlockSpec((tm, tn), lambda i, n: (i, n)),
        out_shape=jax.ShapeDtypeStruct((M, D), F32),
        scratch_shapes=[pltpu.VMEM((tm, D), BF16)],
        compiler_params=_cparams(("arbitrary", "arbitrary")),
        name="rwkv_out",
    )(y, y, s1, s1, s1, a, a, s1, g2, ln_w, ln_b, r_k, k_a, _group_ones(MXU_DIM, RWKV_HEAD), w_o, x, mod)


def _ffn_kernel(x_ref, g_ref, sh_ref, sc_ref, gate_ref, w1_ref, w3_ref, w2_ref, o_ref, a_scr, *, tf):
    f = pl.program_id(1)
    nf = pl.num_programs(1)

    @pl.when(f == 0)
    def _():
        a_scr[...] = _modulate(x_ref[...], g_ref[...], sh_ref[...], sc_ref[...]).astype(BF16)
        o_ref[...] = jnp.zeros_like(o_ref)

    a = a_scr[...]
    acts = []
    for s in range(tf // MXU_DIM):
        cs = slice(s * MXU_DIM, (s + 1) * MXU_DIM)
        h1 = _dot(a, w1_ref[:, cs])
        h3 = _dot(a, w3_ref[:, cs])
        acts.append((_silu(h1) * h3).astype(BF16))
    o_ref[...] += _dot(jnp.concatenate(acts, axis=1), w2_ref[...])

    @pl.when(f == nf - 1)
    def _():
        o_ref[...] = x_ref[...] + gate_ref[...] * o_ref[...]


def _ffn(x, mod, row_fn, gain, w1b, w3b, w2, tm):
    M, D = x.shape
    nf, _, tf = w1b.shape
    modspec = lambda ch: pl.BlockSpec((None, 1, D), lambda i, f: (row_fn(i), 0, ch))
    return pl.pallas_call(
        functools.partial(_ffn_kernel, tf=tf),
        grid=(M // tm, nf),
        in_specs=[
            pl.BlockSpec((tm, D), lambda i, f: (i, 0), pipeline_mode=pl.Buffered(1)),
            pl.BlockSpec((1, D), lambda i, f: (0, 0)),
            modspec(3), modspec(4), modspec(5),
            pl.BlockSpec((None, D, tf), lambda i, f: (f, 0, 0)),
            pl.BlockSpec((None, D, tf), lambda i, f: (f, 0, 0)),
            pl.BlockSpec((tf, D), lambda i, f: (f, 0)),
        ],
        out_specs=pl.BlockSpec((tm, D), lambda i, f: (i, 0)),
        out_shape=jax.ShapeDtypeStruct((M, D), F32),
        scratch_shapes=[pltpu.VMEM((tm, D), BF16)],
        compiler_params=_cparams(("arbitrary", "arbitrary")),
        name="ffn",
    )(x, gain, mod, mod, mod, w1b, w3b, w2)


def _block_cols(w, tn):
    K, N = w.shape
    return jnp.transpose(w.reshape(K, N // tn, tn), (1, 0, 2))


def _qkv_kernel(x_ref, g_ref, sh_ref, sc_ref, w_ref, hg_ref, cos_ref, sin_ref, o_ref, a_scr, *, nrope, nnorm, tn):
    n = pl.program_id(1)

    @pl.when(n == 0)
    def _():
        a_scr[...] = _modulate(x_ref[...], g_ref[...], sh_ref[...], sc_ref[...]).astype(BF16)

    acc = _dot(a_scr[...], w_ref[...])

    @pl.when(n < nnorm)
    def _():
        hg = hg_ref[...]
        lane = lax.broadcasted_iota(jnp.int32, (acc.shape[0], NA_HEAD), 1)
        first = (lane & (NA_HEAD // 2 - 1)) < NA_HEAD // 4
        for s in range(tn // NA_HEAD):
            t = acc[:, s * NA_HEAD:(s + 1) * NA_HEAD]
            t = t * _rms_scale(t) * hg
            if nrope:
                swapped = jnp.where(first, pltpu.roll(t, NA_HEAD - NA_HEAD // 4, axis=1),
                                    pltpu.roll(t, NA_HEAD // 4, axis=1))
                roped = t * cos_ref[...] + swapped * sin_ref[...]
                t = jnp.where(n < nrope, roped, t)
            o_ref[:, s * NA_HEAD:(s + 1) * NA_HEAD] = t.astype(o_ref.dtype)

    @pl.when(n >= nnorm)
    def _():
        o_ref[...] = acc.astype(o_ref.dtype)


def _na_qkv(x, mod, row_fn, gain, w, head_gain, cos, sin, *, nrope, nnorm, tm):
    M, D = x.shape
    nblk, _, tn = w.shape
    N = nblk * tn
    tpb = cos.shape[0] // tm
    hrow = lambda n: jnp.minimum(n, head_gain.shape[0] - 1)
    return pl.pallas_call(
        functools.partial(_qkv_kernel, nrope=nrope, nnorm=nnorm, tn=tn),
        grid=(M // tm, N // tn),
        in_specs=[
            pl.BlockSpec((tm, D), lambda i, n: (i, 0)),
            pl.BlockSpec((1, D), lambda i, n: (0, 0)),
            pl.BlockSpec((None, 1, D), lambda i, n: (row_fn(i), 0, 0)),
            pl.BlockSpec((None, 1, D), lambda i, n: (row_fn(i), 0, 1)),
            pl.BlockSpec((None, D, tn), lambda i, n: (n, 0, 0)),
            pl.BlockSpec((None, 1, NA_HEAD), lambda i, n: (hrow(n), 0, 0)),
            pl.BlockSpec((tm, NA_HEAD), lambda i, n: (i % tpb, 0)),
            pl.BlockSpec((tm, NA_HEAD), lambda i, n: (i % tpb, 0)),
        ],
        out_specs=pl.BlockSpec((tm, tn), lambda i, n: (i, n)),
        out_shape=jax.ShapeDtypeStruct((M, N), BF16),
        scratch_shapes=[pltpu.VMEM((tm, D), BF16)],
        compiler_params=_cparams(("arbitrary", "arbitrary")),
        name="na_qkv",
    )(x, gain, mod, mod, w, head_gain, cos, sin)


def _na_kernel(q_ref, k_ref, v_ref, kc_ref, vc_ref, bias_ref, o_ref, *, rows, rb):
    j0 = pl.program_id(2) * rb
    win_h = min(WIN_H, rows)
    nk = win_h * GRID_W
    qrows = range(rb)
    rs = lambda t, j: t[j * GRID_W:(j + 1) * GRID_W]

    qa = q_ref[...]
    s_ctx = lax.dot_general(qa, kc_ref[...], _NT, preferred_element_type=F32)
    r0 = [jnp.clip(j0 + j - WIN_H // 2, 0, rows - win_h) for j in qrows]
    off = [pl.multiple_of(r0[j] * GRID_W, GRID_W) for j in qrows]
    s_lat = [lax.dot_general(rs(qa, j), k_ref[pl.ds(off[j], nk), :], _NT, preferred_element_type=F32)
             + bias_ref[r0[j] - (j0 + j) + WIN_H - 1] for j in qrows]
    m = [jnp.maximum(jnp.max(s_lat[j], axis=-1, keepdims=True),
                     jnp.max(rs(s_ctx, j), axis=-1, keepdims=True)) for j in qrows]
    p_lat = [jnp.exp(s_lat[j] - m[j]) for j in qrows]
    p_ctx = jnp.exp(s_ctx - jnp.concatenate(m, axis=0))
    den = [jnp.sum(p_lat[j], axis=-1, keepdims=True) + jnp.sum(rs(p_ctx, j), axis=-1, keepdims=True)
           for j in qrows]
    o_ctx = _dot(p_ctx.astype(BF16), vc_ref[...])
    for j in qrows:
        o = _dot(p_lat[j].astype(BF16), v_ref[pl.ds(off[j], nk), :]) + rs(o_ctx, j)
        o_ref[j * GRID_W:(j + 1) * GRID_W, :] = (o / den[j]).astype(o_ref.dtype)


def _na_attention(qkv, kvc, bias, B, T, D, rb):
    H = D // NA_HEAD
    rows = T // GRID_W
    C = kvc.shape[1]
    nbias = bias.shape[1]
    return pl.pallas_call(
        functools.partial(_na_kernel, rows=rows, rb=rb),
        grid=(B, H, rows // rb),
        in_specs=[
            pl.BlockSpec((None, rb * GRID_W, NA_HEAD), lambda b, h, j: (b, j, h)),
            pl.BlockSpec((None, T, NA_HEAD), lambda b, h, j: (b, 0, H + h)),
            pl.BlockSpec((None, T, NA_HEAD), lambda b, h, j: (b, 0, 2 * H + h)),
            pl.BlockSpec((None, C, NA_HEAD), lambda b, h, j: (b, 0, h)),
            pl.BlockSpec((None, C, NA_HEAD), lambda b, h, j: (b, 0, H + h)),
            pl.BlockSpec((None, nbias, GRID_W, bias.shape[3]), lambda b, h, j: (h, 0, 0, 0)),
        ],
        out_specs=pl.BlockSpec((None, rb * GRID_W, NA_HEAD), lambda b, h, j: (b, j, h)),
        out_shape=jax.ShapeDtypeStruct((B, T, D), BF16),
        compiler_params=_cparams(("arbitrary", "arbitrary", "arbitrary")),
        name="na_attention",
    )(qkv, qkv, qkv, kvc, kvc, bias)


def _proj_res_kernel(a_ref, w_ref, x_ref, gate_ref, o_ref):
    o_ref[...] = x_ref[...] + gate_ref[...] * _dot(a_ref[...], w_ref[...])


def _proj_residual(a, w, x, mod, row_fn, chunk, tm):
    M, K = a.shape
    nblk, _, tn = w.shape
    N = nblk * tn
    return pl.pallas_call(
        _proj_res_kernel,
        grid=(M // tm, N // tn),
        in_specs=[
            pl.BlockSpec((tm, K), lambda i, n: (i, 0)),
            pl.BlockSpec((None, K, tn), lambda i, n: (n, 0, 0)),
            pl.BlockSpec((tm, tn), lambda i, n: (i, n)),
            pl.BlockSpec((None, 1, tn), lambda i, n: (row_fn(i), 0, chunk * (N // tn) + n)),
        ],
        out_specs=pl.BlockSpec((tm, tn), lambda i, n: (i, n)),
        out_shape=jax.ShapeDtypeStruct((M, N), F32),
        compiler_params=_cparams(("arbitrary", "arbitrary")),
        name="proj_residual",
    )(a, w, x, mod)


def _rope_tables(T):
    t = np.arange(T)
    half = NA_HEAD // 2
    freqs = ROPE_THETA ** (-np.arange(0, half, 2, dtype=np.float64) / half)
    ang_r = (t // GRID_W).astype(np.float64)[:, None] * freqs
    ang_c = (t % GRID_W).astype(np.float64)[:, None] * freqs
    cos = np.concatenate([np.cos(ang_r), np.cos(ang_r), np.cos(ang_c), np.cos(ang_c)], axis=1)
    sin = np.concatenate([-np.sin(ang_r), np.sin(ang_r), -np.sin(ang_c), np.sin(ang_c)], axis=1)
    return jnp.asarray(cos, F32), jnp.asarray(sin, F32)


def _na_bias(rpb, rows):
    win_h = min(WIN_H, rows)
    qc = np.arange(GRID_W)[:, None]
    kc = np.arange(GRID_W)[None, :]
    wstart = np.clip(qc - WIN_W // 2, 0, GRID_W - WIN_W)
    valid = (kc >= wstart) & (kc < wstart + WIN_W)
    rel = np.clip(kc - qc, -(WIN_W - 1), WIN_W - 1) + WIN_W - 1
    g = rpb[:, :, rel]
    g = jnp.where(valid[None, None], g, NEG_INF)
    tabs = []
    for i0 in range(2 * WIN_H - win_h):
        blk = g[:, i0:i0 + win_h]
        tabs.append(jnp.transpose(blk, (0, 2, 1, 3)).reshape(g.shape[0], GRID_W, win_h * GRID_W))
    return jnp.stack(tabs, axis=1)


def _pad_cols(w, n):
    return jnp.pad(w, ((0, 0), (0, n - w.shape[1])))


def _pad_rows(w, n):
    return jnp.pad(w, ((0, n - w.shape[0]), (0, 0)))


def kernel(x, c, ctx, c_ctx, ada_w, ada_b, norm1, norm2, rwkv_mu, rwkv_w_r, rwkv_w_k, rwkv_w_v, rwkv_w_o, rwkv_w0, rwkv_w1, rwkv_w2, rwkv_a0, rwkv_a1, rwkv_a2, rwkv_g1, rwkv_g2, rwkv_k_k, rwkv_k_a, rwkv_r_k, rwkv_ln_w, rwkv_ln_b, na_w_qkv, na_w_o, na_q_gain, na_k_gain, na_rpb, ffn_w1, ffn_w3, ffn_w2):
    B, T, D = x.shape
    C = ctx.shape[1]
    rows = T // GRID_W
    depth = ada_w.shape[0]
    n_mix = 2
    TM = 512
    TMO = 256
    TNO = min(1024, D)
    TF = 512 if ffn_w1.shape[-1] % 512 == 0 else 256
    TMF = min(1024, T)

    cvec = jnp.concatenate([c, c_ctx[None, :], jnp.zeros((8 - B - 1, D), F32)], axis=0)
    mods = _ada(cvec, ada_w, ada_b)
    lat_row = lambda tm: (lambda i: i // (T // tm))
    ctx_row = lambda i: B

    xl = x.reshape(B * T, D)
    xc = ctx.reshape(B * C, D)

    for i in range(depth):
        last = i == depth - 1
        j = i // n_mix
        mod = mods[i].reshape(8, 1, 6 * D)
        g1 = norm1[i].reshape(1, D)
        g2 = norm2[i].reshape(1, D)
        w1 = _block_cols(ffn_w1[i].astype(BF16), TF)
        w3 = _block_cols(ffn_w3[i].astype(BF16), TF)
        w2 = ffn_w2[i].astype(BF16)
        if i % n_mix == 0:
            lw = rwkv_w1.shape[-1]
            la = rwkv_a1.shape[-1]
            lg = rwkv_g1.shape[-1]
            wcat = jnp.concatenate([
                rwkv_w_r[j], rwkv_w_k[j], rwkv_w_v[j],
                _pad_cols(jnp.concatenate([rwkv_w1[j, 0], rwkv_w1[j, 1]], axis=1), LORA_BLK),
                _pad_cols(jnp.concatenate([rwkv_a1[j, 0], rwkv_a1[j, 1]], axis=1), LORA_BLK),
                _pad_cols(rwkv_g1[j], 2 * LORA_BLK)], axis=1).astype(BF16)
            wcat = _block_cols(wcat, TN1)
            z = lambda n: jnp.zeros((n, D), F32)
            w2d = jnp.stack([_pad_rows(rwkv_w2[j, 0], LORA_BLK),
                             _pad_rows(jnp.concatenate([z(lw), rwkv_w2[j, 1]], axis=0), LORA_BLK)]).astype(BF16)
            a2d = jnp.stack([_pad_rows(rwkv_a2[j, 0], LORA_BLK),
                             _pad_rows(jnp.concatenate([z(la), rwkv_a2[j, 1]], axis=0), LORA_BLK)]).astype(BF16)
            wg2 = _pad_rows(rwkv_g2[j], LORA_BLK).astype(BF16)
            w0 = rwkv_w0[j].reshape(2, 1, D)
            a0 = rwkv_a0[j].reshape(2, 1, D)
            k_k = rwkv_k_k[j].reshape(1, D)
            k_a = rwkv_k_a[j].reshape(1, D)
            r_k = rwkv_r_k[j].reshape(1, D)
            ln_w = rwkv_ln_w[j].reshape(1, D)
            ln_b = rwkv_ln_b[j].reshape(1, D)
            w_o = _block_cols(rwkv_w_o[j].astype(BF16), TNO)

            s1c, kkc = _rwkv_stage1(xc, mod, ctx_row, g1, rwkv_mu[j], k_k, wcat, latent=False, rows=rows, tm=C)
            h0 = jnp.zeros((2, B, D // PAIR, PAIR, PAIR), F32)
            yc, ac, hc = _wkv_scan(s1c, kkc, w2d, a2d, w0, a0, k_a, h0, B, C, D)
            s1l, kkl = _rwkv_stage1(xl, mod, lat_row(TM), g1, rwkv_mu[j], k_k, wcat, latent=True, rows=rows, tm=TM)
            yl, al, _ = _wkv_scan(s1l, kkl, w2d, a2d, w0, a0, k_a, hc, B, T, D)
            xl = _rwkv_out(yl, al, s1l, wg2, xl, mod, lat_row(TMO), ln_w, ln_b, r_k, k_a, w_o, TMO)
            if not last:
                xc = _rwkv_out(yc, ac, s1c, wg2, xc, mod, ctx_row, ln_w, ln_b, r_k, k_a, w_o, min(TMO, C))
        else:
            H = D // NA_HEAD
            cos, sin = _rope_tables(T)
            scale = NA_HEAD ** -0.5
            hg = jnp.stack([jnp.broadcast_to(na_q_gain[j] * scale, (NA_HEAD,)), na_k_gain[j]]).reshape(2, 1, NA_HEAD)
            tn = 512
            nq = D // tn
            wqkv = _block_cols(na_w_qkv[j].astype(BF16), tn)
            hg_blocks = jnp.concatenate([jnp.broadcast_to(hg[0], (nq, 1, NA_HEAD)),
                                         jnp.broadcast_to(hg[1], (nq, 1, NA_HEAD))], axis=0)
            qkv = _na_qkv(xl, mod, lat_row(TM), g1, wqkv, hg_blocks, cos, sin, nrope=2 * nq, nnorm=2 * nq, tm=TM)
            kg_blocks = jnp.broadcast_to(hg[1], (nq, 1, NA_HEAD))
            kvc = _na_qkv(xc, mod, ctx_row, g1, wqkv[nq:], kg_blocks, cos[:C], sin[:C], nrope=0, nnorm=nq, tm=C)
            bias = _na_bias(na_rpb[j], rows)
            o = _na_attention(qkv.reshape(B, T, 3 * D), kvc.reshape(B, C, 2 * D), bias, B, T, D, 8)
            xl = _proj_residual(o.reshape(B * T, D), _block_cols(na_w_o[j].astype(BF16), TNO), xl, mod,
                                lat_row(TM), 2, TM)
            if not last:
                raise NotImplementedError("attention layer with context output")
        xl = _ffn(xl, mod, lat_row(TMF), g2, w1, w3, w2, TMF)
        if not last:
            xc = _ffn(xc, mod, ctx_row, g2, w1, w3, w2, C)
    return xl.reshape(B, T, D)
```

```python
import functools
import math

import numpy as np
import jax
import jax.numpy as jnp
from jax import lax
from jax.experimental import pallas as pl
from jax.experimental.pallas import tpu as pltpu

F32 = jnp.float32
BF16 = jnp.bfloat16

GRID_W = 64
RMS_EPS = 1e-6
RWKV_HEAD = 64
GN_EPS = 64e-5
NA_HEAD = 128
WIN_H = 8
WIN_W = 16
ROPE_THETA = 10000.0
NEG_INF = -1e30

LANE = 128
MXU_DIM = 256
VMEM_LIMIT = 56 * 1024 * 1024

CHUNK = 64
PAIR = 2 * RWKV_HEAD
SCAN_PAIRS = 8
SCAN_CHUNKS = 4
LORA_BLK = 256
TN1 = 512


def _cparams(sem):
    return pltpu.CompilerParams(dimension_semantics=sem, vmem_limit_bytes=VMEM_LIMIT)


def _dot(a, b):
    return jnp.dot(a, b, preferred_element_type=F32)


def _split2(x):
    hi = x.astype(BF16)
    lo = (x - hi.astype(F32)).astype(BF16)
    return hi, lo


def _dot_hilo(a, b_bf16):
    hi, lo = _split2(a)
    return _dot(hi, b_bf16) + _dot(lo, b_bf16)


def _rms_scale(x):
    return lax.rsqrt(jnp.mean(x * x, axis=-1, keepdims=True) + RMS_EPS)


def _modulate(x, gain, shift, scale):
    return (x * _rms_scale(x) * gain) * (1.0 + scale) + shift


def _sigmoid(x):
    return 1.0 / (1.0 + jnp.exp(-x))


def _silu(x):
    return x * _sigmoid(x)


def _group_ones(n, group):
    idx = np.arange(n) // group
    return jnp.asarray((idx[:, None] == idx[None, :]).astype(np.float32), dtype=BF16)


def _ada_kernel(c_ref, w_ref, b_ref, o_ref):
    s = _silu(c_ref[...]).astype(BF16)
    o_ref[...] = _dot(s, w_ref[...].astype(BF16)) + b_ref[...]


def _ada(cvec, ada_w, ada_b):
    depth, D, N = ada_w.shape
    R = cvec.shape[0]
    tn = 1024
    return pl.pallas_call(
        _ada_kernel,
        grid=(depth, N // tn),
        in_specs=[
            pl.BlockSpec((R, D), lambda l, n: (0, 0)),
            pl.BlockSpec((None, D, tn), lambda l, n: (l, 0, n)),
            pl.BlockSpec((None, 1, tn), lambda l, n: (l, 0, n)),
        ],
        out_specs=pl.BlockSpec((None, R, tn), lambda l, n: (l, 0, n)),
        out_shape=jax.ShapeDtypeStruct((depth, R, N), F32),
        compiler_params=_cparams(("arbitrary", "arbitrary")),
        name="ada",
    )(cvec, ada_w, ada_b.reshape(depth, 1, N))


def _s1_kernel(xp_ref, x_ref, xn_ref, g_ref, sh_ref, sc_ref, mu_ref, kkw_ref, ones_ref, w_ref,
               o_ref, kk_ref, a_scr, *, latent, rows, tm, D, nb):
    i = pl.program_id(0)
    n = pl.program_id(1)

    @pl.when(n == 0)
    def _():
        g = g_ref[...]
        sh = sh_ref[...]
        sc = sc_ref[...]
        h = _modulate(x_ref[...], g, sh, sc)
        if latent:
            q = D // 4
            tok = lax.broadcasted_iota(jnp.int32, (tm, q), 0)
            col = tok & (GRID_W - 1)
            grow = (i * (tm // GRID_W) + tok // GRID_W) & (rows - 1)

            def halo(ref, lo):
                xh = ref[...]
                sl = slice(lo, lo + q)
                return (xh[:, sl] * _rms_scale(xh) * g[:, sl]) * (1.0 + sc[:, sl]) + sh[:, sl]

            left = jnp.where(col > 0, pltpu.roll(h[:, 0:q], 1, axis=0), 0.0)
            right = jnp.where(col < GRID_W - 1, pltpu.roll(h[:, q:2 * q], tm - 1, axis=0), 0.0)
            up = jnp.concatenate([halo(xp_ref, 2 * q), h[:tm - GRID_W, 2 * q:3 * q]], axis=0)
            up = jnp.where(grow > 0, up, 0.0)
            down = jnp.concatenate([h[GRID_W:, 3 * q:], halo(xn_ref, 3 * q)], axis=0)
            down = jnp.where(grow < rows - 1, down, 0.0)
            hs = jnp.concatenate([left, right, up, down], axis=1)
        else:
            half = D // 2
            tok = lax.broadcasted_iota(jnp.int32, (tm, half), 0)
            prev = jnp.where(tok > 0, pltpu.roll(h[:, :half], 1, axis=0), 0.0)
            nxt = jnp.where(tok < tm - 1, pltpu.roll(h[:, half:], tm - 1, axis=0), 0.0)
            hs = jnp.concatenate([prev, nxt], axis=1)
        xx = hs - h
        for j in range(6):
            a_scr[j] = (h + xx * mu_ref[j:j + 1, :]).astype(BF16)

    @pl.when(n < nb)
    def _():
        o_ref[...] = _dot(a_scr[0], w_ref[...]).astype(o_ref.dtype)

    @pl.when((n >= nb) & (n < 2 * nb))
    def _():
        acc = _dot(a_scr[2], w_ref[...])
        o_ref[...] = acc.astype(o_ref.dtype)
        kkf = acc * kkw_ref[...]
        sq = kkf * kkf
        ones = ones_ref[...]
        parts = []
        for s in range(TN1 // MXU_DIM):
            parts.append(_dot_hilo(sq[:, s * MXU_DIM:(s + 1) * MXU_DIM], ones))
        ss = jnp.concatenate(parts, axis=1)
        kk_ref[...] = (kkf / jnp.maximum(jnp.sqrt(ss), 1e-12)).astype(kk_ref.dtype)

    @pl.when((n >= 2 * nb) & (n < 3 * nb))
    def _():
        o_ref[...] = _dot(a_scr[3], w_ref[...]).astype(o_ref.dtype)

    @pl.when(n == 3 * nb)
    def _():
        o_ref[:, :LORA_BLK] = jnp.tanh(_dot(a_scr[1], w_ref[:, :LORA_BLK])).astype(o_ref.dtype)
        o_ref[:, LORA_BLK:] = _dot(a_scr[4], w_ref[:, LORA_BLK:]).astype(o_ref.dtype)

    @pl.when(n == 3 * nb + 1)
    def _():
        o_ref[...] = _sigmoid(_dot(a_scr[5], w_ref[...])).astype(o_ref.dtype)


def _rwkv_stage1(x, mod, row_fn, gain, mu, k_k, wcat, *, latent, rows, tm):
    M, D = x.shape
    nb = D // TN1
    nsteps = wcat.shape[0]
    N1 = nsteps * TN1
    hb = tm // GRID_W if latent else 1
    hrows = GRID_W if latent else 8
    nhb = M // hrows
    kcol = lambda n: jnp.clip(n - nb, 0, nb - 1)
    kern = functools.partial(_s1_kernel, latent=latent, rows=rows, tm=tm, D=D, nb=nb)
    return pl.pallas_call(
        kern,
        grid=(M // tm, nsteps),
        in_specs=[
            pl.BlockSpec((hrows, D), lambda i, n: (jnp.maximum(i * hb - 1, 0), 0)),
            pl.BlockSpec((tm, D), lambda i, n: (i, 0)),
            pl.BlockSpec((hrows, D), lambda i, n: (jnp.minimum(i * hb + hb, nhb - 1), 0)),
            pl.BlockSpec((1, D), lambda i, n: (0, 0)),
            pl.BlockSpec((None, 1, D), lambda i, n: (row_fn(i), 0, 0)),
            pl.BlockSpec((None, 1, D), lambda i, n: (row_fn(i), 0, 1)),
            pl.BlockSpec((6, D), lambda i, n: (0, 0)),
            pl.BlockSpec((1, TN1), lambda i, n: (0, kcol(n))),
            pl.BlockSpec((MXU_DIM, MXU_DIM), lambda i, n: (0, 0)),
            pl.BlockSpec((None, D, TN1), lambda i, n: (n, 0, 0)),
        ],
        out_specs=[
            pl.BlockSpec((tm, TN1), lambda i, n: (i, n)),
            pl.BlockSpec((tm, TN1), lambda i, n: (i, kcol(n))),
        ],
        out_shape=[jax.ShapeDtypeStruct((M, N1), BF16), jax.ShapeDtypeStruct((M, D), BF16)],
        scratch_shapes=[pltpu.VMEM((6, tm, D), BF16)],
        compiler_params=_cparams(("arbitrary", "arbitrary")),
        name="rwkv_stage1",
    )(x, x, x, gain, mod, mod, mu, k_k, _group_ones(MXU_DIM, RWKV_HEAD), wcat)


_NT = (((1,), (1,)), ((), ()))
_TN = (((0,), (0,)), ((), ()))


def _mm(a, b, dims=None):
    a = a.astype(BF16)
    b = b.astype(BF16)
    if dims is None:
        return jnp.dot(a, b, preferred_element_type=F32)
    return lax.dot_general(a, b, dims, preferred_element_type=F32)


_DONE = object()
_FRONT_LEAD = 1


def _scan_kernel(r_ref, k_ref, v_ref, kk_ref, lo_ref, w2_ref, a2_ref, w0_ref, a0_ref, ka_ref, h0_ref,
                 y_ref, a_ref, hT_ref, h_scr, *, G, nsub):
    d = pl.program_id(0)
    c = pl.program_id(3)
    nc = pl.num_programs(3)
    L = CHUNK
    pairs = range(G)

    @pl.when(c == 0)
    def _():
        h_scr[...] = h0_ref[...]

    sgn = 1 - 2 * d
    lane = lax.broadcasted_iota(jnp.int32, (L, PAIR), 1)
    m0 = lane < RWKV_HEAD
    ti = lax.broadcasted_iota(jnp.int32, (L, L), 0)
    si = lax.broadcasted_iota(jnp.int32, (L, L), 1)
    tri = jnp.where((si - ti) * sgn <= 0, 1.0, 0.0).astype(BF16)
    i2 = lax.broadcasted_iota(jnp.int32, (2 * L, 2 * L), 0)
    j2 = lax.broadcasted_iota(jnp.int32, (2 * L, 2 * L), 1)
    dif = jnp.where((i2 // L) == (j2 // L), ((j2 % L) - (i2 % L)) * sgn, 1)
    strict = dif < 0
    incl = dif <= 0
    r2 = lax.broadcasted_iota(jnp.int32, (PAIR, PAIR), 0)
    c2 = lax.broadcasted_iota(jnp.int32, (PAIR, PAIR), 1)
    bd = (r2 // RWKV_HEAD) == (c2 // RWKV_HEAD)

    def sm(x):
        return jnp.concatenate([jnp.where(m0, x, 0.0), jnp.where(m0, 0.0, x)], axis=0)

    def sl(x, g):
        return x[:, g * PAIR:(g + 1) * PAIR]

    def rows(q):
        return pl.ds(pl.multiple_of((q + d * (nsub - 1 - 2 * q)) * L, L), L)

    res = [None] * nsub

    def front(q):
        rw = rows(q)
        u = -(_dot(lo_ref[rw, :LORA_BLK], w2_ref[...]) + w0_ref[...])
        za = _dot(lo_ref[rw, LORA_BLK:], a2_ref[...]) + a0_ref[...]
        yield
        ld = -jnp.exp(-(jnp.maximum(u, 0.0) + jnp.log(1.0 + jnp.exp(-jnp.abs(u)))) - 0.5)
        a = _sigmoid(za)
        a_ref[rw, :] = a.astype(a_ref.dtype)
        p0 = ld.astype(BF16)
        r1 = ld - p0.astype(F32)
        p1 = r1.astype(BF16)
        p2 = (r1 - p1.astype(F32)).astype(BF16)
        yield
        cum = _dot(tri, p0) + _dot(tri, p1) + _dot(tri, p2)
        PL = jnp.exp(jnp.sum(ld, axis=0, keepdims=True))
        yield
        invP = jnp.exp(-cum)
        kk = kk_ref[rw, :].astype(F32)
        v = v_ref[rw, :].astype(F32)
        At = -kk * jnp.exp(cum - ld)
        Bt = (kk * a) * invP
        Kt = (k_ref[rw, :].astype(F32) * (1.0 + (a - 1.0) * ka_ref[...])) * invP
        Rt = r_ref[rw, :].astype(F32) * jnp.exp(cum)
        Bh = Bt * PL
        Kh = Kt * PL
        yield
        As = [sm(sl(At, g)) for g in pairs]
        Rs = [sm(sl(Rt, g)) for g in pairs]
        Vs = [sm(sl(v, g)) for g in pairs]
        Gm = [_mm(jnp.concatenate([As[g], Rs[g]], axis=0),
                  jnp.concatenate([sm(sl(Bt, g)), sm(sl(Kt, g))], axis=0), _NT) for g in pairs]
        yield
        Np = [jnp.where(strict, Gm[g][:2 * L, :2 * L], 0.0).astype(BF16) for g in pairs]
        Aak = [jnp.where(strict, Gm[g][:2 * L, 2 * L:], 0.0) for g in pairs]
        Mr = [jnp.concatenate([jnp.where(incl, Gm[g][2 * L:, :2 * L], 0.0),
                               jnp.where(incl, Gm[g][2 * L:, 2 * L:], 0.0)], axis=1).astype(BF16) for g in pairs]
        X = [jnp.concatenate([As[g], _mm(Aak[g], Vs[g])], axis=1) for g in pairs]
        yield
        nsteps = int(math.log2(L))
        for s in range(nsteps):
            X = [X[g] + _mm(Np[g], X[g]) for g in pairs]
            if s < nsteps - 1:
                Np = [_mm(Np[g], Np[g]).astype(BF16) for g in pairs]
            yield
        plc = [jnp.transpose(jnp.broadcast_to(sl(PL, g), (PAIR, PAIR))) for g in pairs]
        res[q] = (X, Rs, Vs, Mr, Bh, Kh, v, plc)
        yield

    state = [[h_scr[g] for g in pairs]]

    def back(q):
        X, Rs, Vs, Mr, Bh, Kh, v, plc = res[q]
        H = state[0]
        AH = [_mm(jnp.concatenate([X[g][:, :PAIR], Rs[g]], axis=0), H[g]) for g in pairs]
        yield
        Us = [AH[g][:2 * L] + X[g][:, PAIR:] for g in pairs]
        Ys = [AH[g][2 * L:] + _mm(Mr[g], jnp.concatenate([Us[g], Vs[g]], axis=0)) for g in pairs]
        upd = [_mm(jnp.concatenate([sl(Bh, g), sl(Kh, g)], axis=0),
                   jnp.concatenate([Us[g][:L] + Us[g][L:], sl(v, g)], axis=0), _TN) for g in pairs]
        yield
        rw = rows(q)
        for g in pairs:
            y_ref[rw, g * PAIR:(g + 1) * PAIR] = (Ys[g][:L] + Ys[g][L:]).astype(y_ref.dtype)
        state[0] = [H[g] * plc[g] + jnp.where(bd, upd[g], 0.0) for g in pairs]
        yield

    def run(primary, others):
        while next(primary, _DONE) is not _DONE:
            for o in others:
                next(o, _DONE)

    fronts = [front(q) for q in range(nsub)]
    for _ in range(_FRONT_LEAD):
        next(fronts[0])
    for q in range(nsub):
        nxt = fronts[q + 1:q + 2]
        run(fronts[q], nxt)
        run(back(q), nxt)
    for g in pairs:
        h_scr[g] = state[0][g]

    @pl.when(c == nc - 1)
    def _():
        hT_ref[...] = h_scr[...]


def _wkv_scan(s1, kk, w2d, a2d, w0, a0, k_a, h0, B, T, D):
    G = min(SCAN_PAIRS, D // PAIR)
    W = G * PAIR
    nsub = SCAN_CHUNKS
    L = nsub * CHUNK
    nc = T // L
    npg = D // W
    N1 = s1.shape[1]
    s1 = s1.reshape(B, T, N1)
    kk = kk.reshape(B, T, D)
    chunk = lambda d, c: c + d * (nc - 1 - 2 * c)
    tok = lambda off: pl.BlockSpec((None, L, W), lambda d, b, p, c: (b, chunk(d, c), off + p))
    dirw = lambda rows: pl.BlockSpec((None, rows, W), lambda d, b, p, c: (d, 0, p))
    state = pl.BlockSpec((None, None, G, PAIR, PAIR), lambda d, b, p, c: (d, b, p, 0, 0))
    out_tok = pl.BlockSpec((None, None, L, W), lambda d, b, p, c: (d, b, chunk(d, c), p))
    y, a, hT = pl.pallas_call(
        functools.partial(_scan_kernel, G=G, nsub=nsub),
        grid=(2, B, npg, nc),
        in_specs=[
            tok(0), tok(npg), tok(2 * npg),
            pl.BlockSpec((None, L, W), lambda d, b, p, c: (b, chunk(d, c), p)),
            pl.BlockSpec((None, L, 2 * LORA_BLK), lambda d, b, p, c: (b, chunk(d, c), 3 * D // (2 * LORA_BLK))),
            dirw(LORA_BLK), dirw(LORA_BLK), dirw(1), dirw(1),
            pl.BlockSpec((1, W), lambda d, b, p, c: (0, p)),
            state,
        ],
        out_specs=[out_tok, out_tok, state],
        out_shape=[jax.ShapeDtypeStruct((2, B, T, D), BF16),
                   jax.ShapeDtypeStruct((2, B, T, D), BF16),
                   jax.ShapeDtypeStruct((2, B, D // PAIR, PAIR, PAIR), F32)],
        scratch_shapes=[pltpu.VMEM((G, PAIR, PAIR), F32)],
        compiler_params=_cparams(("arbitrary", "arbitrary", "arbitrary", "arbitrary")),
        name="wkv_scan",
    )(s1, s1, s1, kk, s1, w2d, a2d, w0, a0, k_a, h0)
    return y.reshape(2, B * T, D), a.reshape(2, B * T, D), hT


def _rwkv_out_kernel(y0_ref, y1_ref, r_ref, k_ref, v_ref, a0_ref, a1_ref, sg_ref, g2_ref,
                     lnw_ref, lnb_ref, rk_ref, ka_ref, ones_ref, w_ref, x_ref, gate_ref,
                     o_ref, a_scr, *, D):
    n = pl.program_id(1)

    @pl.when(n == 0)
    def _():
        ones = ones_ref[...]
        inv = 1.0 / RWKV_HEAD
        f = lambda ref: ref[...].astype(F32)

        def gsum(t):
            parts = [_dot_hilo(t[:, s * MXU_DIM:(s + 1) * MXU_DIM], ones) for s in range(D // MXU_DIM)]
            return jnp.concatenate(parts, axis=1)

        wkv = f(y0_ref) + f(y1_ref)
        cen = wkv - gsum(wkv) * inv
        var = gsum(cen * cen) * inv
        yn = cen * lax.rsqrt(var + GN_EPS) * lnw_ref[...] + lnb_ref[...]
        ks = f(k_ref) * (2.0 + (f(a0_ref) + f(a1_ref) - 2.0) * ka_ref[...])
        bonus = gsum(f(r_ref) * ks * rk_ref[...]) * f(v_ref)
        gate = _dot(sg_ref[:, :LORA_BLK], g2_ref[...])
        a_scr[...] = ((yn + bonus) * gate).astype(BF16)

    o_ref[...] = x_ref[...] + gate_ref[...] * _dot(a_scr[...], w_ref[...])


def _rwkv_out(y, a, s1, g2, x, mod, row_fn, ln_w, ln_b, r_k, k_a, w_o, tm):
    M, D = x.shape
    tn = w_o.shape[2]
    big = lambda col: pl.BlockSpec((tm, D), lambda i, n: (i, col))
    dirblk = lambda d: pl.BlockSpec((None, tm, D), lambda i, n: (d, i, 0))
    row = pl.BlockSpec((1, D), lambda i, n: (0, 0))
    return pl.pallas_call(
        functools.partial(_rwkv_out_kernel, D=D),
        grid=(M // tm, D // tn),
        in_specs=[
            dirblk(0), dirblk(1),
            big(0), big(1), big(2),
            dirblk(0), dirblk(1),
            pl.BlockSpec((tm, 2 * LORA_BLK), lambda i, n: (i, 3 * D // (2 * LORA_BLK) + 1)),
            pl.BlockSpec((LORA_BLK, D), lambda i, n: (0, 0)),
            row, row, row, row,
            pl.BlockSpec((MXU_DIM, MXU_DIM), lambda i, n: (0, 0)),
            pl.BlockSpec((None, D, tn), lambda i, n: (n, 0, 0)),
            pl.BlockSpec((tm, tn), lambda i, n: (i, n)),
            pl.BlockSpec((None, 1, tn), lambda i, n: (row_fn(i), 0, 2 * (D // tn) + n)),
        ],
        out_specs=pl.BlockSpec((tm, tn), lambda i, n: (i, n)),
        out_shape=jax.ShapeDtypeStruct((M, D), F32),
        scratch_shapes=[pltpu.VMEM((tm, D), BF16)],
        compiler_params=_cparams(("arbitrary", "arbitrary")),
        name="rwkv_out",
    )(y, y, s1, s1, s1, a, a, s1, g2, ln_w, ln_b, r_k, k_a, _group_ones(MXU_DIM, RWKV_HEAD), w_o, x, mod)


def _ffn_kernel(x_ref, g_ref, sh_ref, sc_ref, gate_ref, w1_ref, w3_ref, w2_ref, o_ref, a_scr, *, tf):
    f = pl.program_id(1)
    nf = pl.num_programs(1)

    @pl.when(f == 0)
    def _():
        a_scr[...] = _modulate(x_ref[...], g_ref[...], sh_ref[...], sc_ref[...]).astype(BF16)
        o_ref[...] = jnp.zeros_like(o_ref)

    a = a_scr[...]
    acts = []
    for s in range(tf // MXU_DIM):
        cs = slice(s * MXU_DIM, (s + 1) * MXU_DIM)
        h1 = _dot(a, w1_ref[:, cs])
        h3 = _dot(a, w3_ref[:, cs])
        acts.append((_silu(h1) * h3).astype(BF16))
    o_ref[...] += _dot(jnp.concatenate(acts, axis=1), w2_ref[...])

    @pl.when(f == nf - 1)
    def _():
        o_ref[...] = x_ref[...] + gate_ref[...] * o_ref[...]


def _ffn(x, mod, row_fn, gain, w1b, w3b, w2, tm):
    M, D = x.shape
    nf, _, tf = w1b.shape
    modspec = lambda ch: pl.BlockSpec((None, 1, D), lambda i, f: (row_fn(i), 0, ch))
    return pl.pallas_call(
        functools.partial(_ffn_kernel, tf=tf),
        grid=(M // tm, nf),
        in_specs=[
            pl.BlockSpec((tm, D), lambda i, f: (i, 0), pipeline_mode=pl.Buffered(1)),
            pl.BlockSpec((1, D), lambda i, f: (0, 0)),
            modspec(3), modspec(4), modspec(5),
            pl.BlockSpec((None, D, tf), lambda i, f: (f, 0, 0)),
            pl.BlockSpec((None, D, tf), lambda i, f: (f, 0, 0)),
            pl.BlockSpec((tf, D), lambda i, f: (f, 0)),
        ],
        out_specs=pl.BlockSpec((tm, D), lambda i, f: (i, 0)),
        out_shape=jax.ShapeDtypeStruct((M, D), F32),
        scratch_shapes=[pltpu.VMEM((tm, D), BF16)],
        compiler_params=_cparams(("arbitrary", "arbitrary")),
        name="ffn",
    )(x, gain, mod, mod, mod, w1b, w3b, w2)


def _block_cols(w, tn):
    K, N = w.shape
    return jnp.transpose(w.reshape(K, N // tn, tn), (1, 0, 2))


def _qkv_kernel(x_ref, g_ref, sh_ref, sc_ref, w_ref, hg_ref, cos_ref, sin_ref, o_ref, a_scr, *, nrope, nnorm, tn):
    n = pl.program_id(1)

    @pl.when(n == 0)
    def _():
        a_scr[...] = _modulate(x_ref[...], g_ref[...], sh_ref[...], sc_ref[...]).astype(BF16)

    acc = _dot(a_scr[...], w_ref[...])

    @pl.when(n < nnorm)
    def _():
        hg = hg_ref[...]
        lane = lax.broadcasted_iota(jnp.int32, (acc.shape[0], NA_HEAD), 1)
        first = (lane & (NA_HEAD // 2 - 1)) < NA_HEAD // 4
        for s in range(tn // NA_HEAD):
            t = acc[:, s * NA_HEAD:(s + 1) * NA_HEAD]
            t = t * _rms_scale(t) * hg
            if nrope:
                swapped = jnp.where(first, pltpu.roll(t, NA_HEAD - NA_HEAD // 4, axis=1),
                                    pltpu.roll(t, NA_HEAD // 4, axis=1))
                roped = t * cos_ref[...] + swapped * sin_ref[...]
                t = jnp.where(n < nrope, roped, t)
            o_ref[:, s * NA_HEAD:(s + 1) * NA_HEAD] = t.astype(o_ref.dtype)

    @pl.when(n >= nnorm)
    def _():
        o_ref[...] = acc.astype(o_ref.dtype)


def _na_qkv(x, mod, row_fn, gain, w, head_gain, cos, sin, *, nrope, nnorm, tm):
    M, D = x.shape
    nblk, _, tn = w.shape
    N = nblk * tn
    tpb = cos.shape[0] // tm
    hrow = lambda n: jnp.minimum(n, head_gain.shape[0] - 1)
    return pl.pallas_call(
        functools.partial(_qkv_kernel, nrope=nrope, nnorm=nnorm, tn=tn),
        grid=(M // tm, N // tn),
        in_specs=[
            pl.BlockSpec((tm, D), lambda i, n: (i, 0)),
            pl.BlockSpec((1, D), lambda i, n: (0, 0)),
            pl.BlockSpec((None, 1, D), lambda i, n: (row_fn(i), 0, 0)),
            pl.BlockSpec((None, 1, D), lambda i, n: (row_fn(i), 0, 1)),
            pl.BlockSpec((None, D, tn), lambda i, n: (n, 0, 0)),
            pl.BlockSpec((None, 1, NA_HEAD), lambda i, n: (hrow(n), 0, 0)),
            pl.BlockSpec((tm, NA_HEAD), lambda i, n: (i % tpb, 0)),
            pl.BlockSpec((tm, NA_HEAD), lambda i, n: (i % tpb, 0)),
        ],
        out_specs=pl.BlockSpec((tm, tn), lambda i, n: (i, n)),
        out_shape=jax.ShapeDtypeStruct((M, N), BF16),
        scratch_shapes=[pltpu.VMEM((tm, D), BF16)],
        compiler_params=_cparams(("arbitrary", "arbitrary")),
        name="na_qkv",
    )(x, gain, mod, mod, w, head_gain, cos, sin)


def _na_kernel(q_ref, k_ref, v_ref, kc_ref, vc_ref, bias_ref, o_ref, *, rows, rb):
    j0 = pl.program_id(2) * rb
    win_h = min(WIN_H, rows)
    nk = win_h * GRID_W
    qrows = range(rb)
    rs = lambda t, j: t[j * GRID_W:(j + 1) * GRID_W]

    qa = q_ref[...]
    s_ctx = lax.dot_general(qa, kc_ref[...], _NT, preferred_element_type=F32)
    r0 = [jnp.clip(j0 + j - WIN_H // 2, 0, rows - win_h) for j in qrows]
    off = [pl.multiple_of(r0[j] * GRID_W, GRID_W) for j in qrows]
    s_lat = [lax.dot_general(rs(qa, j), k_ref[pl.ds(off[j], nk), :], _NT, preferred_element_type=F32)
             + bias_ref[r0[j] - (j0 + j) + WIN_H - 1] for j in qrows]
    m = [jnp.maximum(jnp.max(s_lat[j], axis=-1, keepdims=True),
                     jnp.max(rs(s_ctx, j), axis=-1, keepdims=True)) for j in qrows]
    p_lat = [jnp.exp(s_lat[j] - m[j]) for j in qrows]
    p_ctx = jnp.exp(s_ctx - jnp.concatenate(m, axis=0))
    den = [jnp.sum(p_lat[j], axis=-1, keepdims=True) + jnp.sum(rs(p_ctx, j), axis=-1, keepdims=True)
           for j in qrows]
    o_ctx = _dot(p_ctx.astype(BF16), vc_ref[...])
    for j in qrows:
        o = _dot(p_lat[j].astype(BF16), v_ref[pl.ds(off[j], nk), :]) + rs(o_ctx, j)
        o_ref[j * GRID_W:(j + 1) * GRID_W, :] = (o / den[j]).astype(o_ref.dtype)


def _na_attention(qkv, kvc, bias, B, T, D, rb):
    H = D // NA_HEAD
    rows = T // GRID_W
    C = kvc.shape[1]
    nbias = bias.shape[1]
    return pl.pallas_call(
        functools.partial(_na_kernel, rows=rows, rb=rb),
        grid=(B, H, rows // rb),
        in_specs=[
            pl.BlockSpec((None, rb * GRID_W, NA_HEAD), lambda b, h, j: (b, j, h)),
            pl.BlockSpec((None, T, NA_HEAD), lambda b, h, j: (b, 0, H + h)),
            pl.BlockSpec((None, T, NA_HEAD), lambda b, h, j: (b, 0, 2 * H + h)),
            pl.BlockSpec((None, C, NA_HEAD), lambda b, h, j: (b, 0, h)),
            pl.BlockSpec((None, C, NA_HEAD), lambda b, h, j: (b, 0, H + h)),
            pl.BlockSpec((None, nbias, GRID_W, bias.shape[3]), lambda b, h, j: (h, 0, 0, 0)),
        ],
        out_specs=pl.BlockSpec((None, rb * GRID_W, NA_HEAD), lambda b, h, j: (b, j, h)),
        out_shape=jax.ShapeDtypeStruct((B, T, D), BF16),
        compiler_params=_cparams(("arbitrary", "arbitrary", "arbitrary")),
        name="na_attention",
    )(qkv, qkv, qkv, kvc, kvc, bias)


def _proj_res_kernel(a_ref, w_ref, x_ref, gate_ref, o_ref):
    o_ref[...] = x_ref[...] + gate_ref[...] * _dot(a_ref[...], w_ref[...])


def _proj_residual(a, w, x, mod, row_fn, chunk, tm):
    M, K = a.shape
    nblk, _, tn = w.shape
    N = nblk * tn
    return pl.pallas_call(
        _proj_res_kernel,
        grid=(M // tm, N // tn),
        in_specs=[
            pl.BlockSpec((tm, K), lambda i, n: (i, 0)),
            pl.BlockSpec((None, K, tn), lambda i, n: (n, 0, 0)),
            pl.BlockSpec((tm, tn), lambda i, n: (i, n)),
            pl.BlockSpec((None, 1, tn), lambda i, n: (row_fn(i), 0, chunk * (N // tn) + n)),
        ],
        out_specs=pl.BlockSpec((tm, tn), lambda i, n: (i, n)),
        out_shape=jax.ShapeDtypeStruct((M, N), F32),
        compiler_params=_cparams(("arbitrary", "arbitrary")),
        name="proj_residual",
    )(a, w, x, mod)


def _rope_tables(T):
    t = np.arange(T)
    half = NA_HEAD // 2
    freqs = ROPE_THETA ** (-np.arange(0, half, 2, dtype=np.float64) / half)
    ang_r = (t // GRID_W).astype(np.float64)[:, None] * freqs
    ang_c = (t % GRID_W).astype(np.float64)[:, None] * freqs
    cos = np.concatenate([np.cos(ang_r), np.cos(ang_r), np.cos(ang_c), np.cos(ang_c)], axis=1)
    sin = np.concatenate([-np.sin(ang_r), np.sin(ang_r), -np.sin(ang_c), np.sin(ang_c)], axis=1)
    return jnp.asarray(cos, F32), jnp.asarray(sin, F32)


def _na_bias(rpb, rows):
    win_h = min(WIN_H, rows)
    qc = np.arange(GRID_W)[:, None]
    kc = np.arange(GRID_W)[None, :]
    wstart = np.clip(qc - WIN_W // 2, 0, GRID_W - WIN_W)
    valid = (kc >= wstart) & (kc < wstart + WIN_W)
    rel = np.clip(kc - qc, -(WIN_W - 1), WIN_W - 1) + WIN_W - 1
    g = rpb[:, :, rel]
    g = jnp.where(valid[None, None], g, NEG_INF)
    tabs = []
    for i0 in range(2 * WIN_H - win_h):
        blk = g[:, i0:i0 + win_h]
        tabs.append(jnp.transpose(blk, (0, 2, 1, 3)).reshape(g.shape[0], GRID_W, win_h * GRID_W))
    return jnp.stack(tabs, axis=1)


def _pad_cols(w, n):
    return jnp.pad(w, ((0, 0), (0, n - w.shape[1])))


def _pad_rows(w, n):
    return jnp.pad(w, ((0, n - w.shape[0]), (0, 0)))


def kernel(x, c, ctx, c_ctx, ada_w, ada_b, norm1, norm2, rwkv_mu, rwkv_w_r, rwkv_w_k, rwkv_w_v, rwkv_w_o, rwkv_w0, rwkv_w1, rwkv_w2, rwkv_a0, rwkv_a1, rwkv_a2, rwkv_g1, rwkv_g2, rwkv_k_k, rwkv_k_a, rwkv_r_k, rwkv_ln_w, rwkv_ln_b, na_w_qkv, na_w_o, na_q_gain, na_k_gain, na_rpb, ffn_w1, ffn_w3, ffn_w2):
    B, T, D = x.shape
    C = ctx.shape[1]
    rows = T // GRID_W
    depth = ada_w.shape[0]
    n_mix = 2
    TM = 512
    TMO = 256
    TNO = min(1024, D)
    TF = 512 if ffn_w1.shape[-1] % 512 == 0 else 256
    TMF = min(1024, T)

    cvec = jnp.concatenate([c, c_ctx[None, :], jnp.zeros((8 - B - 1, D), F32)], axis=0)
    mods = _ada(cvec, ada_w, ada_b)
    lat_row = lambda tm: (lambda i: i // (T // tm))
    ctx_row = lambda i: B

    xl = x.reshape(B * T, D)
    xc = ctx.reshape(B * C, D)

    for i in range(depth):
        last = i == depth - 1
        j = i // n_mix
        mod = mods[i].reshape(8, 1, 6 * D)
        g1 = norm1[i].reshape(1, D)
        g2 = norm2[i].reshape(1, D)
        w1 = _block_cols(ffn_w1[i].astype(BF16), TF)
        w3 = _block_cols(ffn_w3[i].astype(BF16), TF)
        w2 = ffn_w2[i].astype(BF16)
        if i % n_mix == 0:
            lw = rwkv_w1.shape[-1]
            la = rwkv_a1.shape[-1]
            lg = rwkv_g1.shape[-1]
            wcat = jnp.concatenate([
                rwkv_w_r[j], rwkv_w_k[j], rwkv_w_v[j],
                _pad_cols(jnp.concatenate([rwkv_w1[j, 0], rwkv_w1[j, 1]], axis=1), LORA_BLK),
                _pad_cols(jnp.concatenate([rwkv_a1[j, 0], rwkv_a1[j, 1]], axis=1), LORA_BLK),
                _pad_cols(rwkv_g1[j], 2 * LORA_BLK)], axis=1).astype(BF16)
            wcat = _block_cols(wcat, TN1)
            z = lambda n: jnp.zeros((n, D), F32)
            w2d = jnp.stack([_pad_rows(rwkv_w2[j, 0], LORA_BLK),
                             _pad_rows(jnp.concatenate([z(lw), rwkv_w2[j, 1]], axis=0), LORA_BLK)]).astype(BF16)
            a2d = jnp.stack([_pad_rows(rwkv_a2[j, 0], LORA_BLK),
                             _pad_rows(jnp.concatenate([z(la), rwkv_a2[j, 1]], axis=0), LORA_BLK)]).astype(BF16)
            wg2 = _pad_rows(rwkv_g2[j], LORA_BLK).astype(BF16)
            w0 = rwkv_w0[j].reshape(2, 1, D)
            a0 = rwkv_a0[j].reshape(2, 1, D)
            k_k = rwkv_k_k[j].reshape(1, D)
            k_a = rwkv_k_a[j].reshape(1, D)
            r_k = rwkv_r_k[j].reshape(1, D)
            ln_w = rwkv_ln_w[j].reshape(1, D)
            ln_b = rwkv_ln_b[j].reshape(1, D)
            w_o = _block_cols(rwkv_w_o[j].astype(BF16), TNO)

            s1c, kkc = _rwkv_stage1(xc, mod, ctx_row, g1, rwkv_mu[j], k_k, wcat, latent=False, rows=rows, tm=C)
            h0 = jnp.zeros((2, B, D // PAIR, PAIR, PAIR), F32)
            yc, ac, hc = _wkv_scan(s1c, kkc, w2d, a2d, w0, a0, k_a, h0, B, C, D)
            s1l, kkl = _rwkv_stage1(xl, mod, lat_row(TM), g1, rwkv_mu[j], k_k, wcat, latent=True, rows=rows, tm=TM)
            yl, al, _ = _wkv_scan(s1l, kkl, w2d, a2d, w0, a0, k_a, hc, B, T, D)
            xl = _rwkv_out(yl, al, s1l, wg2, xl, mod, lat_row(TMO), ln_w, ln_b, r_k, k_a, w_o, TMO)
            if not last:
                xc = _rwkv_out(yc, ac, s1c, wg2, xc, mod, ctx_row, ln_w, ln_b, r_k, k_a, w_o, min(TMO, C))
        else:
            H = D // NA_HEAD
            cos, sin = _rope_tables(T)
            scale = NA_HEAD ** -0.5
            hg = jnp.stack([jnp.broadcast_to(na_q_gain[j] * scale, (NA_HEAD,)), na_k_gain[j]]).reshape(2, 1, NA_HEAD)
            tn = 512
            nq = D // tn
            wqkv = _block_cols(na_w_qkv[j].astype(BF16), tn)
            hg_blocks = jnp.concatenate([jnp.broadcast_to(hg[0], (nq, 1, NA_HEAD)),
                                         jnp.broadcast_to(hg[1], (nq, 1, NA_HEAD))], axis=0)
            qkv = _na_qkv(xl, mod, lat_row(TM), g1, wqkv, hg_blocks, cos, sin, nrope=2 * nq, nnorm=2 * nq, tm=TM)
            kg_blocks = jnp.broadcast_to(hg[1], (nq, 1, NA_HEAD))
            kvc = _na_qkv(xc, mod, ctx_row, g1, wqkv[nq:], kg_blocks, cos[:C], sin[:C], nrope=0, nnorm=nq, tm=C)
            bias = _na_bias(na_rpb[j], rows)
            o = _na_attention(qkv.reshape(B, T, 3 * D), kvc.reshape(B, C, 2 * D), bias, B, T, D, 8)
            xl = _proj_residual(o.reshape(B * T, D), _block_cols(na_w_o[j].astype(BF16), TNO), xl, mod,
                                lat_row(TM), 2, TM)
            if not last:
                raise NotImplementedError("attention layer with context output")
        xl = _ffn(xl, mod, lat_row(TMF), g2, w1, w3, w2, TMF)
        if not last:
            xc = _ffn(xc, mod, ctx_row, g2, w1, w3, w2, C)
    return xl.reshape(B, T, D)
```

```python
import functools
import math

import numpy as np
import jax
import jax.numpy as jnp
from jax import lax
from jax.experimental import pallas as pl
from jax.experimental.pallas import tpu as pltpu

F32 = jnp.float32
BF16 = jnp.bfloat16

GRID_W = 64
RMS_EPS = 1e-6
RWKV_HEAD = 64
GN_EPS = 64e-5
NA_HEAD = 128
WIN_H = 8
WIN_W = 16
ROPE_THETA = 10000.0
NEG_INF = -1e30

LANE = 128
MXU_DIM = 256
VMEM_LIMIT = 56 * 1024 * 1024

CHUNK = 64
PAIR = 2 * RWKV_HEAD
SCAN_PAIRS = 8
SCAN_CHUNKS = 4
LORA_BLK = 256
TN1 = 512


def _cparams(sem):
    return pltpu.CompilerParams(dimension_semantics=sem, vmem_limit_bytes=VMEM_LIMIT)


def _dot(a, b):
    return jnp.dot(a, b, preferred_element_type=F32)


def _split2(x):
    hi = x.astype(BF16)
    lo = (x - hi.astype(F32)).astype(BF16)
    return hi, lo


def _dot_hilo(a, b_bf16):
    hi, lo = _split2(a)
    return _dot(hi, b_bf16) + _dot(lo, b_bf16)


def _rms_scale(x):
    return lax.rsqrt(jnp.mean(x * x, axis=-1, keepdims=True) + RMS_EPS)


def _modulate(x, gain, shift, scale):
    return (x * _rms_scale(x) * gain) * (1.0 + scale) + shift


def _sigmoid(x):
    return 1.0 / (1.0 + jnp.exp(-x))


def _silu(x):
    return x * _sigmoid(x)


def _group_ones(n, group):
    idx = np.arange(n) // group
    return jnp.asarray((idx[:, None] == idx[None, :]).astype(np.float32), dtype=BF16)


def _ada_kernel(c_ref, w_ref, b_ref, o_ref):
    s = _silu(c_ref[...]).astype(BF16)
    o_ref[...] = _dot(s, w_ref[...].astype(BF16)) + b_ref[...]


def _ada(cvec, ada_w, ada_b):
    depth, D, N = ada_w.shape
    R = cvec.shape[0]
    tn = 1024
    return pl.pallas_call(
        _ada_kernel,
        grid=(depth, N // tn),
        in_specs=[
            pl.BlockSpec((R, D), lambda l, n: (0, 0)),
            pl.BlockSpec((None, D, tn), lambda l, n: (l, 0, n)),
            pl.BlockSpec((None, 1, tn), lambda l, n: (l, 0, n)),
        ],
        out_specs=pl.BlockSpec((None, R, tn), lambda l, n: (l, 0, n)),
        out_shape=jax.ShapeDtypeStruct((depth, R, N), F32),
        compiler_params=_cparams(("arbitrary", "arbitrary")),
        name="ada",
    )(cvec, ada_w, ada_b.reshape(depth, 1, N))


def _s1_kernel(xp_ref, x_ref, xn_ref, g_ref, sh_ref, sc_ref, mu_ref, kkw_ref, ones_ref, w_ref,
               o_ref, kk_ref, a_scr, h_scr, *, latent, rows, tm, D, nb):
    i = pl.program_id(0)
    n = pl.program_id(1)
    R = GRID_W
    pad = h_scr.shape[0] - tm

    @pl.when(n == 0)
    def _():
        g = g_ref[...]
        sh = sh_ref[...]
        sc = sc_ref[...]
        lo = pad // 2
        h_scr[lo:lo + tm, :] = _modulate(x_ref[...], g, sh, sc)
        if latent:
            h_scr[0:lo, :] = _modulate(xp_ref[...], g, sh, sc)
            h_scr[lo + tm:, :] = _modulate(xn_ref[...], g, sh, sc)
        else:
            h_scr[0:lo, :] = jnp.zeros((lo, D), F32)
            h_scr[lo + tm:, :] = jnp.zeros((pad - lo, D), F32)
        for rr in range(tm // R):
            base = lo + rr * R
            hc = h_scr[base:base + R, :]
            if latent:
                q = D // 4
                col = lax.broadcasted_iota(jnp.int32, (R, q), 0)
                grow = (i * (tm // R) + rr) & (rows - 1)
                hs = jnp.concatenate([
                    jnp.where(col > 0, h_scr[base - 1:base - 1 + R, 0:q], 0.0),
                    jnp.where(col < R - 1, h_scr[base + 1:base + 1 + R, q:2 * q], 0.0),
                    jnp.where(grow > 0, h_scr[base - R:base, 2 * q:3 * q], 0.0),
                    jnp.where(grow < rows - 1, h_scr[base + R:base + 2 * R, 3 * q:], 0.0)], axis=1)
            else:
                half = D // 2
                hs = jnp.concatenate([h_scr[base - 1:base - 1 + R, :half],
                                      h_scr[base + 1:base + 1 + R, half:]], axis=1)
            xx = hs - hc
            for j in range(6):
                a_scr[j, rr * R:(rr + 1) * R, :] = (hc + xx * mu_ref[j:j + 1, :]).astype(BF16)

    @pl.when(n < nb)
    def _():
        o_ref[...] = _dot(a_scr[0], w_ref[...]).astype(o_ref.dtype)

    @pl.when((n >= nb) & (n < 2 * nb))
    def _():
        acc = _dot(a_scr[2], w_ref[...])
        o_ref[...] = acc.astype(o_ref.dtype)
        kkf = acc * kkw_ref[...]
        sq = kkf * kkf
        ones = ones_ref[...]
        parts = []
        for s in range(TN1 // MXU_DIM):
            parts.append(_dot_hilo(sq[:, s * MXU_DIM:(s + 1) * MXU_DIM], ones))
        ss = jnp.concatenate(parts, axis=1)
        kk_ref[...] = (kkf / jnp.maximum(jnp.sqrt(ss), 1e-12)).astype(kk_ref.dtype)

    @pl.when((n >= 2 * nb) & (n < 3 * nb))
    def _():
        o_ref[...] = _dot(a_scr[3], w_ref[...]).astype(o_ref.dtype)

    @pl.when(n == 3 * nb)
    def _():
        o_ref[:, :LORA_BLK] = jnp.tanh(_dot(a_scr[1], w_ref[:, :LORA_BLK])).astype(o_ref.dtype)
        o_ref[:, LORA_BLK:] = _dot(a_scr[4], w_ref[:, LORA_BLK:]).astype(o_ref.dtype)

    @pl.when(n == 3 * nb + 1)
    def _():
        o_ref[...] = _sigmoid(_dot(a_scr[5], w_ref[...])).astype(o_ref.dtype)


def _rwkv_stage1(x, mod, row_fn, gain, mu, k_k, wcat, *, latent, rows, tm):
    M, D = x.shape
    nb = D // TN1
    N1 = wcat.shape[1]
    nsteps = N1 // TN1
    hb = tm // GRID_W if latent else 1
    hrows = GRID_W if latent else 8
    nhb = M // hrows
    kcol = lambda n: jnp.clip(n - nb, 0, nb - 1)
    kern = functools.partial(_s1_kernel, latent=latent, rows=rows, tm=tm, D=D, nb=nb)
    return pl.pallas_call(
        kern,
        grid=(M // tm, nsteps),
        in_specs=[
            pl.BlockSpec((hrows, D), lambda i, n: (jnp.maximum(i * hb - 1, 0), 0)),
            pl.BlockSpec((tm, D), lambda i, n: (i, 0)),
            pl.BlockSpec((hrows, D), lambda i, n: (jnp.minimum(i * hb + hb, nhb - 1), 0)),
            pl.BlockSpec((1, D), lambda i, n: (0, 0)),
            pl.BlockSpec((None, 1, D), lambda i, n: (row_fn(i), 0, 0)),
            pl.BlockSpec((None, 1, D), lambda i, n: (row_fn(i), 0, 1)),
            pl.BlockSpec((6, D), lambda i, n: (0, 0)),
            pl.BlockSpec((1, TN1), lambda i, n: (0, kcol(n))),
            pl.BlockSpec((MXU_DIM, MXU_DIM), lambda i, n: (0, 0)),
            pl.BlockSpec((D, TN1), lambda i, n: (0, n)),
        ],
        out_specs=[
            pl.BlockSpec((tm, TN1), lambda i, n: (i, n)),
            pl.BlockSpec((tm, TN1), lambda i, n: (i, kcol(n))),
        ],
        out_shape=[jax.ShapeDtypeStruct((M, N1), BF16), jax.ShapeDtypeStruct((M, D), BF16)],
        scratch_shapes=[pltpu.VMEM((6, tm, D), BF16),
                        pltpu.VMEM((tm + 2 * hrows, D), F32)],
        compiler_params=_cparams(("arbitrary", "arbitrary")),
        name="rwkv_stage1",
    )(x, x, x, gain, mod, mod, mu, k_k, _group_ones(MXU_DIM, RWKV_HEAD), wcat)


_NT = (((1,), (1,)), ((), ()))
_TN = (((0,), (0,)), ((), ()))


def _mm(a, b, dims=None):
    a = a.astype(BF16)
    b = b.astype(BF16)
    if dims is None:
        return jnp.dot(a, b, preferred_element_type=F32)
    return lax.dot_general(a, b, dims, preferred_element_type=F32)


_DONE = object()
_FRONT_LEAD = 1


def _scan_kernel(r_ref, k_ref, v_ref, kk_ref, lo_ref, w2_ref, a2_ref, w0_ref, a0_ref, ka_ref, h0_ref,
                 y_ref, a_ref, hT_ref, h_scr, *, G, nsub):
    d = pl.program_id(0)
    c = pl.program_id(3)
    nc = pl.num_programs(3)
    L = CHUNK
    pairs = range(G)

    @pl.when(c == 0)
    def _():
        h_scr[...] = h0_ref[...]

    sgn = 1 - 2 * d
    lane = lax.broadcasted_iota(jnp.int32, (L, PAIR), 1)
    m0 = lane < RWKV_HEAD
    ti = lax.broadcasted_iota(jnp.int32, (L, L), 0)
    si = lax.broadcasted_iota(jnp.int32, (L, L), 1)
    tri = jnp.where((si - ti) * sgn <= 0, 1.0, 0.0).astype(BF16)
    i2 = lax.broadcasted_iota(jnp.int32, (2 * L, 2 * L), 0)
    j2 = lax.broadcasted_iota(jnp.int32, (2 * L, 2 * L), 1)
    dif = jnp.where((i2 // L) == (j2 // L), ((j2 % L) - (i2 % L)) * sgn, 1)
    strict = dif < 0
    incl = dif <= 0
    r2 = lax.broadcasted_iota(jnp.int32, (PAIR, PAIR), 0)
    c2 = lax.broadcasted_iota(jnp.int32, (PAIR, PAIR), 1)
    bd = (r2 // RWKV_HEAD) == (c2 // RWKV_HEAD)

    def sm(x):
        return jnp.concatenate([jnp.where(m0, x, 0.0), jnp.where(m0, 0.0, x)], axis=0)

    def sl(x, g):
        return x[:, g * PAIR:(g + 1) * PAIR]

    def rows(q):
        return pl.ds(pl.multiple_of((q + d * (nsub - 1 - 2 * q)) * L, L), L)

    res = [None] * nsub

    def front(q):
        rw = rows(q)
        u = -(_dot(lo_ref[rw, :LORA_BLK], w2_ref[...]) + w0_ref[...])
        za = _dot(lo_ref[rw, LORA_BLK:], a2_ref[...]) + a0_ref[...]
        yield
        ld = -jnp.exp(-(jnp.maximum(u, 0.0) + jnp.log(1.0 + jnp.exp(-jnp.abs(u)))) - 0.5)
        a = _sigmoid(za)
        a_ref[rw, :] = a.astype(a_ref.dtype)
        p0 = ld.astype(BF16)
        r1 = ld - p0.astype(F32)
        p1 = r1.astype(BF16)
        p2 = (r1 - p1.astype(F32)).astype(BF16)
        yield
        cum = _dot(tri, p0) + _dot(tri, p1) + _dot(tri, p2)
        PL = jnp.exp(jnp.sum(ld, axis=0, keepdims=True))
        yield
        invP = jnp.exp(-cum)
        kk = kk_ref[rw, :].astype(F32)
        v = v_ref[rw, :].astype(F32)
        At = -kk * jnp.exp(cum - ld)
        Bt = (kk * a) * invP
        Kt = (k_ref[rw, :].astype(F32) * (1.0 + (a - 1.0) * ka_ref[...])) * invP
        Rt = r_ref[rw, :].astype(F32) * jnp.exp(cum)
        Bh = Bt * PL
        Kh = Kt * PL
        yield
        As = [sm(sl(At, g)) for g in pairs]
        Rs = [sm(sl(Rt, g)) for g in pairs]
        Vs = [sm(sl(v, g)) for g in pairs]
        Gm = [_mm(jnp.concatenate([As[g], Rs[g]], axis=0),
                  jnp.concatenate([sm(sl(Bt, g)), sm(sl(Kt, g))], axis=0), _NT) for g in pairs]
        yield
        Np = [jnp.where(strict, Gm[g][:2 * L, :2 * L], 0.0).astype(BF16) for g in pairs]
        Aak = [jnp.where(strict, Gm[g][:2 * L, 2 * L:], 0.0) for g in pairs]
        Mr = [jnp.concatenate([jnp.where(incl, Gm[g][2 * L:, :2 * L], 0.0),
                               jnp.where(incl, Gm[g][2 * L:, 2 * L:], 0.0)], axis=1).astype(BF16) for g in pairs]
        X = [jnp.concatenate([As[g], _mm(Aak[g], Vs[g])], axis=1) for g in pairs]
        yield
        nsteps = int(math.log2(L))
        for s in range(nsteps):
            X = [X[g] + _mm(Np[g], X[g]) for g in pairs]
            if s < nsteps - 1:
                Np = [_mm(Np[g], Np[g]).astype(BF16) for g in pairs]
            yield
        plc = [jnp.transpose(jnp.broadcast_to(sl(PL, g), (PAIR, PAIR))) for g in pairs]
        res[q] = (X, Rs, Vs, Mr, Bh, Kh, v, plc)
        yield

    state = [[h_scr[g] for g in pairs]]

    def back(q):
        X, Rs, Vs, Mr, Bh, Kh, v, plc = res[q]
        H = state[0]
        AH = [_mm(jnp.concatenate([X[g][:, :PAIR], Rs[g]], axis=0), H[g]) for g in pairs]
        yield
        Us = [AH[g][:2 * L] + X[g][:, PAIR:] for g in pairs]
        Ys = [AH[g][2 * L:] + _mm(Mr[g], jnp.concatenate([Us[g], Vs[g]], axis=0)) for g in pairs]
        upd = [_mm(jnp.concatenate([sl(Bh, g), sl(Kh, g)], axis=0),
                   jnp.concatenate([Us[g][:L] + Us[g][L:], sl(v, g)], axis=0), _TN) for g in pairs]
        yield
        rw = rows(q)
        for g in pairs:
            y_ref[rw, g * PAIR:(g + 1) * PAIR] = (Ys[g][:L] + Ys[g][L:]).astype(y_ref.dtype)
        state[0] = [H[g] * plc[g] + jnp.where(bd, upd[g], 0.0) for g in pairs]
        yield

    def run(primary, others):
        while next(primary, _DONE) is not _DONE:
            for o in others:
                next(o, _DONE)

    fronts = [front(q) for q in range(nsub)]
    for _ in range(_FRONT_LEAD):
        next(fronts[0])
    for q in range(nsub):
        nxt = fronts[q + 1:q + 2]
        run(fronts[q], nxt)
        run(back(q), nxt)
    for g in pairs:
        h_scr[g] = state[0][g]

    @pl.when(c == nc - 1)
    def _():
        hT_ref[...] = h_scr[...]


def _wkv_scan(s1, kk, w2d, a2d, w0, a0, k_a, h0, B, T, D):
    G = min(SCAN_PAIRS, D // PAIR)
    W = G * PAIR
    nsub = SCAN_CHUNKS
    L = nsub * CHUNK
    nc = T // L
    npg = D // W
    N1 = s1.shape[1]
    s1 = s1.reshape(B, T, N1)
    kk = kk.reshape(B, T, D)
    chunk = lambda d, c: c + d * (nc - 1 - 2 * c)
    tok = lambda off: pl.BlockSpec((None, L, W), lambda d, b, p, c: (b, chunk(d, c), off + p))
    dirw = lambda rows: pl.BlockSpec((None, rows, W), lambda d, b, p, c: (d, 0, p))
    state = pl.BlockSpec((None, None, G, PAIR, PAIR), lambda d, b, p, c: (d, b, p, 0, 0))
    out_tok = pl.BlockSpec((None, None, L, W), lambda d, b, p, c: (d, b, chunk(d, c), p))
    y, a, hT = pl.pallas_call(
        functools.partial(_scan_kernel, G=G, nsub=nsub),
        grid=(2, B, npg, nc),
        in_specs=[
            tok(0), tok(npg), tok(2 * npg),
            pl.BlockSpec((None, L, W), lambda d, b, p, c: (b, chunk(d, c), p)),
            pl.BlockSpec((None, L, 2 * LORA_BLK), lambda d, b, p, c: (b, chunk(d, c), 3 * D // (2 * LORA_BLK))),
            dirw(LORA_BLK), dirw(LORA_BLK), dirw(1), dirw(1),
            pl.BlockSpec((1, W), lambda d, b, p, c: (0, p)),
            state,
        ],
        out_specs=[out_tok, out_tok, state],
        out_shape=[jax.ShapeDtypeStruct((2, B, T, D), BF16),
                   jax.ShapeDtypeStruct((2, B, T, D), BF16),
                   jax.ShapeDtypeStruct((2, B, D // PAIR, PAIR, PAIR), F32)],
        scratch_shapes=[pltpu.VMEM((G, PAIR, PAIR), F32)],
        compiler_params=_cparams(("arbitrary", "arbitrary", "arbitrary", "arbitrary")),
        name="wkv_scan",
    )(s1, s1, s1, kk, s1, w2d, a2d, w0, a0, k_a, h0)
    return y.reshape(2, B * T, D), a.reshape(2, B * T, D), hT


def _rwkv_out_kernel(y0_ref, y1_ref, r_ref, k_ref, v_ref, a0_ref, a1_ref, sg_ref, g2_ref,
                     lnw_ref, lnb_ref, rk_ref, ka_ref, ones_ref, w_ref, x_ref, gate_ref,
                     o_ref, *, D):
    ones = ones_ref[...]
    inv = 1.0 / RWKV_HEAD
    sg = sg_ref[:, :LORA_BLK]
    acc = None
    for s in range(D // MXU_DIM):
        cs = slice(s * MXU_DIM, (s + 1) * MXU_DIM)
        f = lambda ref: ref[:, cs].astype(F32)
        wkv = f(y0_ref) + f(y1_ref)
        cen = wkv - _dot_hilo(wkv, ones) * inv
        var = _dot((cen * cen).astype(BF16), ones) * inv
        yn = cen * lax.rsqrt(var + GN_EPS) * lnw_ref[:, cs] + lnb_ref[:, cs]
        ks = f(k_ref) * (2.0 + (f(a0_ref) + f(a1_ref) - 2.0) * ka_ref[:, cs])
        bonus = _dot((f(r_ref) * ks * rk_ref[:, cs]).astype(BF16), ones) * f(v_ref)
        gate = _dot(sg, g2_ref[:, cs])
        part = _dot(((yn + bonus) * gate).astype(BF16), w_ref[cs, :])
        acc = part if acc is None else acc + part
    o_ref[...] = x_ref[...] + gate_ref[...] * acc


def _rwkv_out(y, a, s1, g2, x, mod, row_fn, ln_w, ln_b, r_k, k_a, w_o, tm):
    M, D = x.shape
    big = lambda col: pl.BlockSpec((tm, D), lambda i: (i, col))
    dirblk = lambda d: pl.BlockSpec((None, tm, D), lambda i: (d, i, 0))
    row = pl.BlockSpec((1, D), lambda i: (0, 0))
    return pl.pallas_call(
        functools.partial(_rwkv_out_kernel, D=D),
        grid=(M // tm,),
        in_specs=[
            dirblk(0), dirblk(1),
            big(0), big(1), big(2),
            dirblk(0), dirblk(1),
            pl.BlockSpec((tm, 2 * LORA_BLK), lambda i: (i, 3 * D // (2 * LORA_BLK) + 1)),
            pl.BlockSpec((LORA_BLK, D), lambda i: (0, 0)),
            row, row, row, row,
            pl.BlockSpec((MXU_DIM, MXU_DIM), lambda i: (0, 0)),
            pl.BlockSpec((D, D), lambda i: (0, 0)),
            pl.BlockSpec((tm, D), lambda i: (i, 0)),
            pl.BlockSpec((None, 1, D), lambda i: (row_fn(i), 0, 2)),
        ],
        out_specs=pl.BlockSpec((tm, D), lambda i: (i, 0)),
        out_shape=jax.ShapeDtypeStruct((M, D), F32),
        compiler_params=_cparams(("arbitrary",)),
        name="rwkv_out",
    )(y, y, s1, s1, s1, a, a, s1, g2, ln_w, ln_b, r_k, k_a, _group_ones(MXU_DIM, RWKV_HEAD), w_o, x, mod)


def _ffn_kernel(x_ref, g_ref, sh_ref, sc_ref, gate_ref, w1_ref, w3_ref, w2_ref, o_ref, a_scr, *, tf):
    f = pl.program_id(1)
    nf = pl.num_programs(1)

    @pl.when(f == 0)
    def _():
        a_scr[...] = _modulate(x_ref[...], g_ref[...], sh_ref[...], sc_ref[...]).astype(BF16)
        o_ref[...] = jnp.zeros_like(o_ref)

    a = a_scr[...]
    acts = []
    for s in range(tf // MXU_DIM):
        cs = slice(s * MXU_DIM, (s + 1) * MXU_DIM)
        h1 = _dot(a, w1_ref[:, cs])
        h3 = _dot(a, w3_ref[:, cs])
        acts.append((_silu(h1) * h3).astype(BF16))
    o_ref[...] += _dot(jnp.concatenate(acts, axis=1), w2_ref[...])

    @pl.when(f == nf - 1)
    def _():
        o_ref[...] = x_ref[...] + gate_ref[...] * o_ref[...]


def _ffn(x, mod, row_fn, gain, w1b, w3b, w2, tm, tf):
    M, D = x.shape
    nf = w1b.shape[1] // tf
    modspec = lambda ch: pl.BlockSpec((None, 1, D), lambda i, f: (row_fn(i), 0, ch))
    return pl.pallas_call(
        functools.partial(_ffn_kernel, tf=tf),
        grid=(M // tm, nf),
        in_specs=[
            pl.BlockSpec((tm, D), lambda i, f: (i, 0), pipeline_mode=pl.Buffered(1)),
            pl.BlockSpec((1, D), lambda i, f: (0, 0)),
            modspec(3), modspec(4), modspec(5),
            pl.BlockSpec((D, tf), lambda i, f: (0, f)),
            pl.BlockSpec((D, tf), lambda i, f: (0, f)),
            pl.BlockSpec((tf, D), lambda i, f: (f, 0)),
        ],
        out_specs=pl.BlockSpec((tm, D), lambda i, f: (i, 0)),
        out_shape=jax.ShapeDtypeStruct((M, D), F32),
        scratch_shapes=[pltpu.VMEM((tm, D), BF16)],
        compiler_params=_cparams(("arbitrary", "arbitrary")),
        name="ffn",
    )(x, gain, mod, mod, mod, w1b, w3b, w2)


def _qkv_kernel(x_ref, g_ref, sh_ref, sc_ref, w_ref, hg_ref, cos_ref, sin_ref, ones_ref, o_ref, a_scr,
                *, nrope, nnorm, tn):
    n = pl.program_id(1)

    @pl.when(n == 0)
    def _():
        a_scr[...] = _modulate(x_ref[...], g_ref[...], sh_ref[...], sc_ref[...]).astype(BF16)

    @pl.when(n < nnorm)
    def _():
        acc = _dot(a_scr[...], w_ref[...])
        hg = hg_ref[...]
        ones = ones_ref[...]
        for h in range(tn // MXU_DIM):
            blk = acc[:, h * MXU_DIM:(h + 1) * MXU_DIM]
            ms = _dot((blk * blk).astype(BF16), ones) * (1.0 / NA_HEAD)
            blk = blk * lax.rsqrt(ms + RMS_EPS)
            for s in range(MXU_DIM // NA_HEAD):
                t = blk[:, s * NA_HEAD:(s + 1) * NA_HEAD] * hg
                if nrope:
                    t = t * cos_ref[...] + pltpu.roll(t, NA_HEAD // 2, axis=1) * sin_ref[...]
                c0 = h * MXU_DIM + s * NA_HEAD
                o_ref[:, c0:c0 + NA_HEAD] = t.astype(o_ref.dtype)

    @pl.when(n >= nnorm)
    def _():
        o_ref[...] = _dot(a_scr[...], w_ref[...]).astype(o_ref.dtype)


def _na_qkv(x, mod, row_fn, gain, w, head_gain, cos, sin, *, nrope, nnorm, tm, tn):
    assert nrope in (0, nnorm)
    M, D = x.shape
    N = w.shape[1]
    tpb = cos.shape[0] // tm
    hrow = lambda n: jnp.minimum(n, head_gain.shape[0] - 1)
    return pl.pallas_call(
        functools.partial(_qkv_kernel, nrope=nrope, nnorm=nnorm, tn=tn),
        grid=(M // tm, N // tn),
        in_specs=[
            pl.BlockSpec((tm, D), lambda i, n: (i, 0)),
            pl.BlockSpec((1, D), lambda i, n: (0, 0)),
            pl.BlockSpec((None, 1, D), lambda i, n: (row_fn(i), 0, 0)),
            pl.BlockSpec((None, 1, D), lambda i, n: (row_fn(i), 0, 1)),
            pl.BlockSpec((D, tn), lambda i, n: (0, n)),
            pl.BlockSpec((None, 1, NA_HEAD), lambda i, n: (hrow(n), 0, 0)),
            pl.BlockSpec((tm, NA_HEAD), lambda i, n: (i % tpb, 0)),
            pl.BlockSpec((tm, NA_HEAD), lambda i, n: (i % tpb, 0)),
            pl.BlockSpec((MXU_DIM, MXU_DIM), lambda i, n: (0, 0)),
        ],
        out_specs=pl.BlockSpec((tm, tn), lambda i, n: (i, n)),
        out_shape=jax.ShapeDtypeStruct((M, N), BF16),
        scratch_shapes=[pltpu.VMEM((tm, D), BF16)],
        compiler_params=_cparams(("arbitrary", "arbitrary")),
        name="na_qkv",
    )(x, gain, mod, mod, w, head_gain, cos, sin, _group_ones(MXU_DIM, NA_HEAD))


def _na_kernel(q_ref, k_ref, v_ref, kc_ref, vc_ref, bias_ref, o_ref, *, rows, rb):
    j0 = pl.program_id(2) * rb
    win_h = min(WIN_H, rows)
    nk = win_h * GRID_W
    qrows = range(rb)
    rs = lambda t, j: t[j * GRID_W:(j + 1) * GRID_W]

    qa = q_ref[...]
    s_ctx = lax.dot_general(qa, kc_ref[...], _NT, preferred_element_type=F32)
    r0 = [jnp.clip(j0 + j - WIN_H // 2, 0, rows - win_h) for j in qrows]
    off = [pl.multiple_of(r0[j] * GRID_W, GRID_W) for j in qrows]
    s_lat = [lax.dot_general(rs(qa, j), k_ref[pl.ds(off[j], nk), :], _NT, preferred_element_type=F32)
             + bias_ref[r0[j] - (j0 + j) + WIN_H - 1] for j in qrows]
    m = [jnp.maximum(jnp.max(s_lat[j], axis=-1, keepdims=True),
                     jnp.max(rs(s_ctx, j), axis=-1, keepdims=True)) for j in qrows]
    p_lat = [jnp.exp(s_lat[j] - m[j]) for j in qrows]
    p_ctx = jnp.exp(s_ctx - jnp.concatenate(m, axis=0))
    den = [jnp.sum(p_lat[j], axis=-1, keepdims=True) + jnp.sum(rs(p_ctx, j), axis=-1, keepdims=True)
           for j in qrows]
    o_ctx = _dot(p_ctx.astype(BF16), vc_ref[...])
    for j in qrows:
        o = _dot(p_lat[j].astype(BF16), v_ref[pl.ds(off[j], nk), :]) + rs(o_ctx, j)
        o_ref[j * GRID_W:(j + 1) * GRID_W, :] = (o / den[j]).astype(o_ref.dtype)


def _na_attention(qkv, kvc, bias, B, T, D, rb):
    H = D // NA_HEAD
    rows = T // GRID_W
    C = kvc.shape[1]
    nbias = bias.shape[1]
    return pl.pallas_call(
        functools.partial(_na_kernel, rows=rows, rb=rb),
        grid=(B, H, rows // rb),
        in_specs=[
            pl.BlockSpec((None, rb * GRID_W, NA_HEAD), lambda b, h, j: (b, j, h)),
            pl.BlockSpec((None, T, NA_HEAD), lambda b, h, j: (b, 0, H + h)),
            pl.BlockSpec((None, T, NA_HEAD), lambda b, h, j: (b, 0, 2 * H + h)),
            pl.BlockSpec((None, C, NA_HEAD), lambda b, h, j: (b, 0, h)),
            pl.BlockSpec((None, C, NA_HEAD), lambda b, h, j: (b, 0, H + h)),
            pl.BlockSpec((None, nbias, GRID_W, bias.shape[3]), lambda b, h, j: (h, 0, 0, 0)),
        ],
        out_specs=pl.BlockSpec((None, rb * GRID_W, NA_HEAD), lambda b, h, j: (b, j, h)),
        out_shape=jax.ShapeDtypeStruct((B, T, D), BF16),
        compiler_params=_cparams(("arbitrary", "arbitrary", "arbitrary")),
        name="na_attention",
    )(qkv, qkv, qkv, kvc, kvc, bias)


def _proj_res_kernel(a_ref, w_ref, x_ref, gate_ref, o_ref):
    o_ref[...] = x_ref[...] + gate_ref[...] * _dot(a_ref[...], w_ref[...])


def _proj_residual(a, w, x, mod, row_fn, chunk, tm):
    M, K = a.shape
    N = w.shape[1]
    return pl.pallas_call(
        _proj_res_kernel,
        grid=(M // tm,),
        in_specs=[
            pl.BlockSpec((tm, K), lambda i: (i, 0)),
            pl.BlockSpec((K, N), lambda i: (0, 0)),
            pl.BlockSpec((tm, N), lambda i: (i, 0)),
            pl.BlockSpec((None, 1, N), lambda i: (row_fn(i), 0, chunk)),
        ],
        out_specs=pl.BlockSpec((tm, N), lambda i: (i, 0)),
        out_shape=jax.ShapeDtypeStruct((M, N), F32),
        compiler_params=_cparams(("arbitrary",)),
        name="proj_residual",
    )(a, w, x, mod)


_QUART = NA_HEAD // 4
_ROPE_PERM = np.concatenate([np.arange(0, _QUART), np.arange(2 * _QUART, 3 * _QUART),
                             np.arange(_QUART, 2 * _QUART), np.arange(3 * _QUART, NA_HEAD)])


def _rope_layout(w_qk):
    K, N = w_qk.shape
    return w_qk.reshape(K, N // NA_HEAD, NA_HEAD)[:, :, _ROPE_PERM].reshape(K, N)


def _rope_tables(T):
    t = np.arange(T)
    half = NA_HEAD // 2
    freqs = ROPE_THETA ** (-np.arange(0, half, 2, dtype=np.float64) / half)
    ang_r = (t // GRID_W).astype(np.float64)[:, None] * freqs
    ang_c = (t % GRID_W).astype(np.float64)[:, None] * freqs
    cos = np.concatenate([np.cos(ang_r), np.cos(ang_c), np.cos(ang_r), np.cos(ang_c)], axis=1)
    sin = np.concatenate([-np.sin(ang_r), -np.sin(ang_c), np.sin(ang_r), np.sin(ang_c)], axis=1)
    return jnp.asarray(cos, F32), jnp.asarray(sin, F32)


def _na_bias(rpb, rows):
    win_h = min(WIN_H, rows)
    qc = np.arange(GRID_W)[:, None]
    kc = np.arange(GRID_W)[None, :]
    wstart = np.clip(qc - WIN_W // 2, 0, GRID_W - WIN_W)
    valid = (kc >= wstart) & (kc < wstart + WIN_W)
    rel = np.clip(kc - qc, -(WIN_W - 1), WIN_W - 1) + WIN_W - 1
    g = rpb[:, :, rel]
    g = jnp.where(valid[None, None], g, NEG_INF)
    tabs = []
    for i0 in range(2 * WIN_H - win_h):
        blk = g[:, i0:i0 + win_h]
        tabs.append(jnp.transpose(blk, (0, 2, 1, 3)).reshape(g.shape[0], GRID_W, win_h * GRID_W))
    return jnp.stack(tabs, axis=1)


def _pad_cols(w, n):
    return jnp.pad(w, ((0, 0), (0, n - w.shape[1])))


def _pad_rows(w, n):
    return jnp.pad(w, ((0, n - w.shape[0]), (0, 0)))


def kernel(x, c, ctx, c_ctx, ada_w, ada_b, norm1, norm2, rwkv_mu, rwkv_w_r, rwkv_w_k, rwkv_w_v, rwkv_w_o, rwkv_w0, rwkv_w1, rwkv_w2, rwkv_a0, rwkv_a1, rwkv_a2, rwkv_g1, rwkv_g2, rwkv_k_k, rwkv_k_a, rwkv_r_k, rwkv_ln_w, rwkv_ln_b, na_w_qkv, na_w_o, na_q_gain, na_k_gain, na_rpb, ffn_w1, ffn_w3, ffn_w2):
    B, T, D = x.shape
    C = ctx.shape[1]
    rows = T // GRID_W
    depth = ada_w.shape[0]
    n_mix = 2
    TM = 512
    TMO = 256
    TF = 512 if ffn_w1.shape[-1] % 512 == 0 else 256
    TMF = min(1024, T)

    cvec = jnp.concatenate([c, c_ctx[None, :], jnp.zeros((8 - B - 1, D), F32)], axis=0)
    mods = _ada(cvec, ada_w, ada_b)
    lat_row = lambda tm: (lambda i: i // (T // tm))
    ctx_row = lambda i: B

    xl = x.reshape(B * T, D)
    xc = ctx.reshape(B * C, D)

    for i in range(depth):
        last = i == depth - 1
        j = i // n_mix
        mod = mods[i].reshape(8, 1, 6 * D)
        g1 = norm1[i].reshape(1, D)
        g2 = norm2[i].reshape(1, D)
        w1 = ffn_w1[i].astype(BF16)
        w3 = ffn_w3[i].astype(BF16)
        w2 = ffn_w2[i].astype(BF16)
        if i % n_mix == 0:
            lw = rwkv_w1.shape[-1]
            la = rwkv_a1.shape[-1]
            lg = rwkv_g1.shape[-1]
            wcat = jnp.concatenate([
                rwkv_w_r[j], rwkv_w_k[j], rwkv_w_v[j],
                _pad_cols(jnp.concatenate([rwkv_w1[j, 0], rwkv_w1[j, 1]], axis=1), LORA_BLK),
                _pad_cols(jnp.concatenate([rwkv_a1[j, 0], rwkv_a1[j, 1]], axis=1), LORA_BLK),
                _pad_cols(rwkv_g1[j], 2 * LORA_BLK)], axis=1).astype(BF16)
            z = lambda n: jnp.zeros((n, D), F32)
            w2d = jnp.stack([_pad_rows(rwkv_w2[j, 0], LORA_BLK),
                             _pad_rows(jnp.concatenate([z(lw), rwkv_w2[j, 1]], axis=0), LORA_BLK)]).astype(BF16)
            a2d = jnp.stack([_pad_rows(rwkv_a2[j, 0], LORA_BLK),
                             _pad_rows(jnp.concatenate([z(la), rwkv_a2[j, 1]], axis=0), LORA_BLK)]).astype(BF16)
            wg2 = _pad_rows(rwkv_g2[j], LORA_BLK).astype(BF16)
            w0 = rwkv_w0[j].reshape(2, 1, D)
            a0 = rwkv_a0[j].reshape(2, 1, D)
            k_k = rwkv_k_k[j].reshape(1, D)
            k_a = rwkv_k_a[j].reshape(1, D)
            r_k = rwkv_r_k[j].reshape(1, D)
            ln_w = rwkv_ln_w[j].reshape(1, D)
            ln_b = rwkv_ln_b[j].reshape(1, D)
            w_o = rwkv_w_o[j].astype(BF16)

            s1c, kkc = _rwkv_stage1(xc, mod, ctx_row, g1, rwkv_mu[j], k_k, wcat, latent=False, rows=rows, tm=C)
            h0 = jnp.zeros((2, B, D // PAIR, PAIR, PAIR), F32)
            yc, ac, hc = _wkv_scan(s1c, kkc, w2d, a2d, w0, a0, k_a, h0, B, C, D)
            s1l, kkl = _rwkv_stage1(xl, mod, lat_row(TM), g1, rwkv_mu[j], k_k, wcat, latent=True, rows=rows, tm=TM)
            yl, al, _ = _wkv_scan(s1l, kkl, w2d, a2d, w0, a0, k_a, hc, B, T, D)
            xl = _rwkv_out(yl, al, s1l, wg2, xl, mod, lat_row(TMO), ln_w, ln_b, r_k, k_a, w_o, TMO)
            if not last:
                xc = _rwkv_out(yc, ac, s1c, wg2, xc, mod, ctx_row, ln_w, ln_b, r_k, k_a, w_o, min(TMO, C))
        else:
            H = D // NA_HEAD
            cos, sin = _rope_tables(T)
            scale = NA_HEAD ** -0.5
            hg = jnp.stack([(na_q_gain[j] * scale)[_ROPE_PERM], na_k_gain[j][_ROPE_PERM]]).reshape(2, 1, NA_HEAD)
            tn = 512
            nq = D // tn
            wqkv = jnp.concatenate([_rope_layout(na_w_qkv[j][:, :2 * D]), na_w_qkv[j][:, 2 * D:]],
                                   axis=1).astype(BF16)
            hg_blocks = jnp.concatenate([jnp.broadcast_to(hg[0], (nq, 1, NA_HEAD)),
                                         jnp.broadcast_to(hg[1], (nq, 1, NA_HEAD))], axis=0)
            qkv = _na_qkv(xl, mod, lat_row(TM), g1, wqkv, hg_blocks, cos, sin, nrope=2 * nq, nnorm=2 * nq,
                          tm=TM, tn=tn)
            kg_blocks = jnp.broadcast_to(hg[1], (nq, 1, NA_HEAD))
            kvc = _na_qkv(xc, mod, ctx_row, g1, wqkv[:, D:], kg_blocks, cos[:C], sin[:C], nrope=0, nnorm=nq,
                          tm=C, tn=tn)
            bias = _na_bias(na_rpb[j], rows)
            o = _na_attention(qkv.reshape(B, T, 3 * D), kvc.reshape(B, C, 2 * D), bias, B, T, D, 8)
            xl = _proj_residual(o.reshape(B * T, D), na_w_o[j].astype(BF16), xl, mod,
                                lat_row(TM), 2, TM)
            if not last:
                raise NotImplementedError("attention layer with context output")
        xl = _ffn(xl, mod, lat_row(TMF), g2, w1, w3, w2, TMF, TF)
        if not last:
            xc = _ffn(xc, mod, ctx_row, g2, w1, w3, w2, C, TF)
    return xl.reshape(B, T, D)
```

```python
import functools
import math

import numpy as np
import jax
import jax.numpy as jnp
from jax import lax
from jax.experimental import pallas as pl
from jax.experimental.pallas import tpu as pltpu

F32 = jnp.float32
BF16 = jnp.bfloat16

GRID_W = 64
RMS_EPS = 1e-6
RWKV_HEAD = 64
GN_EPS = 64e-5
NA_HEAD = 128
WIN_H = 8
WIN_W = 16
ROPE_THETA = 10000.0
NEG_INF = -1e30

LANE = 128
MXU_DIM = 256
VMEM_LIMIT = 56 * 1024 * 1024

CHUNK = 64
PAIR = 2 * RWKV_HEAD
SCAN_PAIRS = 16
SCAN_CHUNKS = 4
LORA_BLK = 256
TN1 = 512


def _cparams(sem):
    return pltpu.CompilerParams(dimension_semantics=sem, vmem_limit_bytes=VMEM_LIMIT)


def _dot(a, b):
    return jnp.dot(a, b, preferred_element_type=F32)


def _split2(x):
    hi = x.astype(BF16)
    lo = (x - hi.astype(F32)).astype(BF16)
    return hi, lo


def _dot_hilo(a, b_bf16):
    hi, lo = _split2(a)
    return _dot(hi, b_bf16) + _dot(lo, b_bf16)


def _rms_scale(x):
    return lax.rsqrt(jnp.mean(x * x, axis=-1, keepdims=True) + RMS_EPS)


def _modulate(x, gain, shift, scale):
    return (x * _rms_scale(x) * gain) * (1.0 + scale) + shift


def _sigmoid(x):
    return 1.0 / (1.0 + jnp.exp(-x))


def _silu(x):
    return x * _sigmoid(x)


def _group_ones(n, group):
    idx = np.arange(n) // group
    return jnp.asarray((idx[:, None] == idx[None, :]).astype(np.float32), dtype=BF16)


def _ada_kernel(c_ref, w_ref, b_ref, o_ref):
    s = _silu(c_ref[...]).astype(BF16)
    o_ref[...] = _dot(s, w_ref[...].astype(BF16)) + b_ref[...]


def _ada(cvec, ada_w, ada_b):
    depth, D, N = ada_w.shape
    R = cvec.shape[0]
    tn = 1024
    return pl.pallas_call(
        _ada_kernel,
        grid=(depth, N // tn),
        in_specs=[
            pl.BlockSpec((R, D), lambda l, n: (0, 0)),
            pl.BlockSpec((None, D, tn), lambda l, n: (l, 0, n)),
            pl.BlockSpec((None, 1, tn), lambda l, n: (l, 0, n)),
        ],
        out_specs=pl.BlockSpec((None, R, tn), lambda l, n: (l, 0, n)),
        out_shape=jax.ShapeDtypeStruct((depth, R, N), F32),
        compiler_params=_cparams(("arbitrary", "arbitrary")),
        name="ada",
    )(cvec, ada_w, ada_b.reshape(depth, 1, N))


def _s1_kernel(xp_ref, x_ref, xn_ref, g_ref, sh_ref, sc_ref, mu_ref, kkw_ref, ones_ref, w_ref,
               o_ref, kk_ref, a_scr, h_scr, *, latent, rows, tm, D, nb):
    i = pl.program_id(0)
    n = pl.program_id(1)
    R = GRID_W
    pad = h_scr.shape[0] - tm

    @pl.when(n == 0)
    def _():
        g = g_ref[...]
        sh = sh_ref[...]
        sc = sc_ref[...]
        lo = pad // 2
        h_scr[lo:lo + tm, :] = _modulate(x_ref[...], g, sh, sc)
        if latent:
            h_scr[0:lo, :] = _modulate(xp_ref[...], g, sh, sc)
            h_scr[lo + tm:, :] = _modulate(xn_ref[...], g, sh, sc)
        else:
            h_scr[0:lo, :] = jnp.zeros((lo, D), F32)
            h_scr[lo + tm:, :] = jnp.zeros((pad - lo, D), F32)
        for rr in range(tm // R):
            base = lo + rr * R
            hc = h_scr[base:base + R, :]
            if latent:
                q = D // 4
                col = lax.broadcasted_iota(jnp.int32, (R, q), 0)
                grow = (i * (tm // R) + rr) & (rows - 1)
                hs = jnp.concatenate([
                    jnp.where(col > 0, h_scr[base - 1:base - 1 + R, 0:q], 0.0),
                    jnp.where(col < R - 1, h_scr[base + 1:base + 1 + R, q:2 * q], 0.0),
                    jnp.where(grow > 0, h_scr[base - R:base, 2 * q:3 * q], 0.0),
                    jnp.where(grow < rows - 1, h_scr[base + R:base + 2 * R, 3 * q:], 0.0)], axis=1)
            else:
                half = D // 2
                hs = jnp.concatenate([h_scr[base - 1:base - 1 + R, :half],
                                      h_scr[base + 1:base + 1 + R, half:]], axis=1)
            xx = hs - hc
            for j in range(6):
                a_scr[j, rr * R:(rr + 1) * R, :] = (hc + xx * mu_ref[j:j + 1, :]).astype(BF16)

    @pl.when(n < nb)
    def _():
        o_ref[...] = _dot(a_scr[0], w_ref[...]).astype(o_ref.dtype)

    @pl.when((n >= nb) & (n < 2 * nb))
    def _():
        acc = _dot(a_scr[2], w_ref[...])
        o_ref[...] = acc.astype(o_ref.dtype)
        kkf = acc * kkw_ref[...]
        sq = kkf * kkf
        ones = ones_ref[...]
        parts = []
        for s in range(TN1 // MXU_DIM):
            parts.append(_dot_hilo(sq[:, s * MXU_DIM:(s + 1) * MXU_DIM], ones))
        ss = jnp.concatenate(parts, axis=1)
        kk_ref[...] = (kkf / jnp.maximum(jnp.sqrt(ss), 1e-12)).astype(kk_ref.dtype)

    @pl.when((n >= 2 * nb) & (n < 3 * nb))
    def _():
        o_ref[...] = _dot(a_scr[3], w_ref[...]).astype(o_ref.dtype)

    @pl.when(n == 3 * nb)
    def _():
        o_ref[:, :LORA_BLK] = jnp.tanh(_dot(a_scr[1], w_ref[:, :LORA_BLK])).astype(o_ref.dtype)
        o_ref[:, LORA_BLK:] = _dot(a_scr[4], w_ref[:, LORA_BLK:]).astype(o_ref.dtype)

    @pl.when(n == 3 * nb + 1)
    def _():
        o_ref[...] = _sigmoid(_dot(a_scr[5], w_ref[...])).astype(o_ref.dtype)


def _rwkv_stage1(x, mod, row_fn, gain, mu, k_k, wcat, *, latent, rows, tm):
    M, D = x.shape
    nb = D // TN1
    N1 = wcat.shape[1]
    nsteps = N1 // TN1
    hb = tm // GRID_W if latent else 1
    hrows = GRID_W if latent else 8
    nhb = M // hrows
    kcol = lambda n: jnp.clip(n - nb, 0, nb - 1)
    kern = functools.partial(_s1_kernel, latent=latent, rows=rows, tm=tm, D=D, nb=nb)
    return pl.pallas_call(
        kern,
        grid=(M // tm, nsteps),
        in_specs=[
            pl.BlockSpec((hrows, D), lambda i, n: (jnp.maximum(i * hb - 1, 0), 0)),
            pl.BlockSpec((tm, D), lambda i, n: (i, 0)),
            pl.BlockSpec((hrows, D), lambda i, n: (jnp.minimum(i * hb + hb, nhb - 1), 0)),
            pl.BlockSpec((1, D), lambda i, n: (0, 0)),
            pl.BlockSpec((None, 1, D), lambda i, n: (row_fn(i), 0, 0)),
            pl.BlockSpec((None, 1, D), lambda i, n: (row_fn(i), 0, 1)),
            pl.BlockSpec((6, D), lambda i, n: (0, 0)),
            pl.BlockSpec((1, TN1), lambda i, n: (0, kcol(n))),
            pl.BlockSpec((MXU_DIM, MXU_DIM), lambda i, n: (0, 0)),
            pl.BlockSpec((D, TN1), lambda i, n: (0, n)),
        ],
        out_specs=[
            pl.BlockSpec((tm, TN1), lambda i, n: (i, n)),
            pl.BlockSpec((tm, TN1), lambda i, n: (i, kcol(n))),
        ],
        out_shape=[jax.ShapeDtypeStruct((M, N1), BF16), jax.ShapeDtypeStruct((M, D), BF16)],
        scratch_shapes=[pltpu.VMEM((6, tm, D), BF16),
                        pltpu.VMEM((tm + 2 * hrows, D), F32)],
        compiler_params=_cparams(("arbitrary", "arbitrary")),
        name="rwkv_stage1",
    )(x, x, x, gain, mod, mod, mu, k_k, _group_ones(MXU_DIM, RWKV_HEAD), wcat)


_NT = (((1,), (1,)), ((), ()))
_TN = (((0,), (0,)), ((), ()))


def _mm(a, b, dims=None):
    a = a.astype(BF16)
    b = b.astype(BF16)
    if dims is None:
        return jnp.dot(a, b, preferred_element_type=F32)
    return lax.dot_general(a, b, dims, preferred_element_type=F32)


_DONE = object()
_FRONT_LEAD = 1
_ROW_STAGGER = 1


def _scan_kernel(r_ref, k_ref, v_ref, kk_ref, lo_ref, w2_ref, a2_ref, w0_ref, a0_ref, ka_ref, h0_ref,
                 y_ref, a_ref, hT_ref, h_scr, *, G, nsub):
    d = pl.program_id(0)
    c = pl.program_id(3)
    nc = pl.num_programs(3)
    L = CHUNK
    pairs = range(G)

    @pl.when(c == 0)
    def _():
        h_scr[...] = h0_ref[...]

    sgn = 1 - 2 * d
    lane = lax.broadcasted_iota(jnp.int32, (L, PAIR), 1)
    m0 = lane < RWKV_HEAD
    ti = lax.broadcasted_iota(jnp.int32, (L, L), 0)
    si = lax.broadcasted_iota(jnp.int32, (L, L), 1)
    tri = jnp.where((si - ti) * sgn <= 0, 1.0, 0.0).astype(BF16)
    i2 = lax.broadcasted_iota(jnp.int32, (2 * L, 2 * L), 0)
    j2 = lax.broadcasted_iota(jnp.int32, (2 * L, 2 * L), 1)
    dif = jnp.where((i2 // L) == (j2 // L), ((j2 % L) - (i2 % L)) * sgn, 1)
    strict = dif < 0
    incl = dif <= 0
    r2 = lax.broadcasted_iota(jnp.int32, (PAIR, PAIR), 0)
    c2 = lax.broadcasted_iota(jnp.int32, (PAIR, PAIR), 1)
    bd = (r2 // RWKV_HEAD) == (c2 // RWKV_HEAD)

    def sm(x):
        return jnp.concatenate([jnp.where(m0, x, 0.0), jnp.where(m0, 0.0, x)], axis=0)

    def sl(x, g):
        return x[:, g * PAIR:(g + 1) * PAIR]

    def rows(q):
        return pl.ds(pl.multiple_of((q + d * (nsub - 1 - 2 * q)) * L, L), L)

    res = [None] * nsub

    def front(q):
        rw = rows(q)
        z = _dot(lo_ref[rw, :LORA_BLK], w2_ref[...]) + w0_ref[...]
        za = _dot(lo_ref[rw, LORA_BLK:], a2_ref[...]) + a0_ref[...]
        yield
        ld = -math.exp(-0.5) * _sigmoid(z)
        a = _sigmoid(za)
        a_ref[rw, :] = a.astype(a_ref.dtype)
        p0 = ld.astype(BF16)
        r1 = ld - p0.astype(F32)
        p1 = r1.astype(BF16)
        p2 = (r1 - p1.astype(F32)).astype(BF16)
        yield
        cum = _dot(tri, p0) + _dot(tri, p1) + _dot(tri, p2)
        PL = jnp.exp(jnp.sum(ld, axis=0, keepdims=True))
        yield
        invP = jnp.exp(-cum)
        kk = kk_ref[rw, :].astype(F32)
        v = v_ref[rw, :].astype(F32)
        At = -kk * jnp.exp(cum - ld)
        Bt = (kk * a) * invP
        Kt = (k_ref[rw, :].astype(F32) * (1.0 + (a - 1.0) * ka_ref[...])) * invP
        Rt = r_ref[rw, :].astype(F32) * jnp.exp(cum)
        Bh = Bt * PL
        Kh = Kt * PL
        yield
        As = [sm(sl(At, g)) for g in pairs]
        Rs = [sm(sl(Rt, g)) for g in pairs]
        Vs = [sm(sl(v, g)) for g in pairs]
        Gm = [_mm(jnp.concatenate([As[g], Rs[g]], axis=0),
                  jnp.concatenate([sm(sl(Bt, g)), sm(sl(Kt, g))], axis=0), _NT) for g in pairs]
        yield
        Np = [jnp.where(strict, Gm[g][:2 * L, :2 * L], 0.0).astype(BF16) for g in pairs]
        Aak = [jnp.where(strict, Gm[g][:2 * L, 2 * L:], 0.0) for g in pairs]
        Mr = [jnp.concatenate([jnp.where(incl, Gm[g][2 * L:, :2 * L], 0.0),
                               jnp.where(incl, Gm[g][2 * L:, 2 * L:], 0.0)], axis=1).astype(BF16) for g in pairs]
        X = [jnp.concatenate([As[g], _mm(Aak[g], Vs[g])], axis=1) for g in pairs]
        yield
        nsteps = int(math.log2(L))
        for s in range(nsteps):
            X = [X[g] + _mm(Np[g], X[g]) for g in pairs]
            if s < nsteps - 1:
                Np = [_mm(Np[g], Np[g]).astype(BF16) for g in pairs]
            yield
        plc = [jnp.transpose(jnp.broadcast_to(sl(PL, g), (PAIR, PAIR))) for g in pairs]
        res[q] = (X, Rs, Vs, Mr, Bh, Kh, v, plc)
        yield

    state = [[h_scr[g] for g in pairs]]

    def back(q):
        X, Rs, Vs, Mr, Bh, Kh, v, plc = res[q]
        H = state[0]
        AH = [_mm(jnp.concatenate([X[g][:, :PAIR], Rs[g]], axis=0), H[g]) for g in pairs]
        yield
        Us = [AH[g][:2 * L] + X[g][:, PAIR:] for g in pairs]
        Ys = [AH[g][2 * L:] + _mm(Mr[g], jnp.concatenate([Us[g], Vs[g]], axis=0)) for g in pairs]
        upd = [_mm(jnp.concatenate([sl(Bh, g), sl(Kh, g)], axis=0),
                   jnp.concatenate([Us[g][:L] + Us[g][L:], sl(v, g)], axis=0), _TN) for g in pairs]
        yield
        rw = rows(q)
        for g in pairs:
            y_ref[rw, g * PAIR:(g + 1) * PAIR] = (Ys[g][:L] + Ys[g][L:]).astype(y_ref.dtype)
        state[0] = [H[g] * plc[g] + jnp.where(bd, upd[g], 0.0) for g in pairs]
        yield

    def run(primary, others):
        while next(primary, _DONE) is not _DONE:
            for o in others:
                next(o, _DONE)

    fronts = [front(q) for q in range(nsub)]
    for _ in range(_FRONT_LEAD):
        next(fronts[0])
    for q in range(nsub):
        nxt = fronts[q + 1:q + 2]
        run(fronts[q], nxt)
        run(back(q), nxt)
    for g in pairs:
        h_scr[g] = state[0][g]

    @pl.when(c == nc - 1)
    def _():
        hT_ref[...] = h_scr[...]


def _wkv_scan(s1, kk, w2d, a2d, w0, a0, k_a, h0, B, T, D):
    G = min(SCAN_PAIRS, D // PAIR)
    W = G * PAIR
    nsub = SCAN_CHUNKS
    L = nsub * CHUNK
    nc = T // L
    npg = D // W
    N1 = s1.shape[1]
    s1 = s1.reshape(B, T, N1)
    kk = kk.reshape(B, T, D)
    chunk = lambda d, c: c + d * (nc - 1 - 2 * c)
    tok = lambda off: pl.BlockSpec((None, L, W), lambda d, b, p, c: (b, chunk(d, c), off + p))
    dirw = lambda rows: pl.BlockSpec((None, rows, W), lambda d, b, p, c: (d, 0, p))
    state = pl.BlockSpec((None, None, G, PAIR, PAIR), lambda d, b, p, c: (d, b, p, 0, 0))
    out_tok = pl.BlockSpec((None, None, L, W), lambda d, b, p, c: (d, b, chunk(d, c), p))
    y, a, hT = pl.pallas_call(
        functools.partial(_scan_kernel, G=G, nsub=nsub),
        grid=(2, B, npg, nc),
        in_specs=[
            tok(0), tok(npg), tok(2 * npg),
            pl.BlockSpec((None, L, W), lambda d, b, p, c: (b, chunk(d, c), p)),
            pl.BlockSpec((None, L, 2 * LORA_BLK), lambda d, b, p, c: (b, chunk(d, c), 3 * D // (2 * LORA_BLK))),
            dirw(LORA_BLK), dirw(LORA_BLK), dirw(1), dirw(1),
            pl.BlockSpec((1, W), lambda d, b, p, c: (0, p)),
            state,
        ],
        out_specs=[out_tok, out_tok, state],
        out_shape=[jax.ShapeDtypeStruct((2, B, T, D), BF16),
                   jax.ShapeDtypeStruct((2, B, T, D), BF16),
                   jax.ShapeDtypeStruct((2, B, D // PAIR, PAIR, PAIR), F32)],
        scratch_shapes=[pltpu.VMEM((G, PAIR, PAIR), F32)],
        compiler_params=_cparams(("arbitrary", "arbitrary", "arbitrary", "arbitrary")),
        name="wkv_scan",
    )(s1, s1, s1, kk, s1, w2d, a2d, w0, a0, k_a, h0)
    return y.reshape(2, B * T, D), a.reshape(2, B * T, D), hT


def _rwkv_out_kernel(y0_ref, y1_ref, r_ref, k_ref, v_ref, a0_ref, a1_ref, sg_ref, g2_ref,
                     lnw_ref, lnb_ref, rk_ref, ka_ref, ones_ref, w_ref, x_ref, gate_ref,
                     o_ref, *, D):
    ones = ones_ref[...]
    inv = 1.0 / RWKV_HEAD
    sg = sg_ref[:, :LORA_BLK]
    acc = None
    for s in range(D // MXU_DIM):
        cs = slice(s * MXU_DIM, (s + 1) * MXU_DIM)
        f = lambda ref: ref[:, cs].astype(F32)
        wkv = f(y0_ref) + f(y1_ref)
        cen = wkv - _dot_hilo(wkv, ones) * inv
        var = _dot((cen * cen).astype(BF16), ones) * inv
        yn = cen * lax.rsqrt(var + GN_EPS) * lnw_ref[:, cs] + lnb_ref[:, cs]
        ks = f(k_ref) * (2.0 + (f(a0_ref) + f(a1_ref) - 2.0) * ka_ref[:, cs])
        bonus = _dot((f(r_ref) * ks * rk_ref[:, cs]).astype(BF16), ones) * f(v_ref)
        gate = _dot(sg, g2_ref[:, cs])
        part = _dot(((yn + bonus) * gate).astype(BF16), w_ref[cs, :])
        acc = part if acc is None else acc + part
    o_ref[...] = x_ref[...] + gate_ref[...] * acc


def _rwkv_out(y, a, s1, g2, x, mod, row_fn, ln_w, ln_b, r_k, k_a, w_o, tm):
    M, D = x.shape
    big = lambda col: pl.BlockSpec((tm, D), lambda i: (i, col))
    dirblk = lambda d: pl.BlockSpec((None, tm, D), lambda i: (d, i, 0))
    row = pl.BlockSpec((1, D), lambda i: (0, 0))
    return pl.pallas_call(
        functools.partial(_rwkv_out_kernel, D=D),
        grid=(M // tm,),
        in_specs=[
            dirblk(0), dirblk(1),
            big(0), big(1), big(2),
            dirblk(0), dirblk(1),
            pl.BlockSpec((tm, 2 * LORA_BLK), lambda i: (i, 3 * D // (2 * LORA_BLK) + 1)),
            pl.BlockSpec((LORA_BLK, D), lambda i: (0, 0)),
            row, row, row, row,
            pl.BlockSpec((MXU_DIM, MXU_DIM), lambda i: (0, 0)),
            pl.BlockSpec((D, D), lambda i: (0, 0)),
            pl.BlockSpec((tm, D), lambda i: (i, 0)),
            pl.BlockSpec((None, 1, D), lambda i: (row_fn(i), 0, 2)),
        ],
        out_specs=pl.BlockSpec((tm, D), lambda i: (i, 0)),
        out_shape=jax.ShapeDtypeStruct((M, D), F32),
        compiler_params=_cparams(("arbitrary",)),
        name="rwkv_out",
    )(y, y, s1, s1, s1, a, a, s1, g2, ln_w, ln_b, r_k, k_a, _group_ones(MXU_DIM, RWKV_HEAD), w_o, x, mod)


def _ffn_kernel(x_ref, g_ref, sh_ref, sc_ref, gate_ref, w1_ref, w3_ref, w2_ref, o_ref, a_scr, *, tf):
    f = pl.program_id(1)
    nf = pl.num_programs(1)

    @pl.when(f == 0)
    def _():
        a_scr[...] = _modulate(x_ref[...], g_ref[...], sh_ref[...], sc_ref[...]).astype(BF16)
        o_ref[...] = jnp.zeros_like(o_ref)

    a = a_scr[...]
    acts = []
    for s in range(tf // MXU_DIM):
        cs = slice(s * MXU_DIM, (s + 1) * MXU_DIM)
        h1 = _dot(a, w1_ref[:, cs])
        h3 = _dot(a, w3_ref[:, cs])
        acts.append((_silu(h1) * h3).astype(BF16))
    o_ref[...] += _dot(jnp.concatenate(acts, axis=1), w2_ref[...])

    @pl.when(f == nf - 1)
    def _():
        o_ref[...] = x_ref[...] + gate_ref[...] * o_ref[...]


def _ffn(x, mod, row_fn, gain, w1b, w3b, w2, tm, tf):
    M, D = x.shape
    nf = w1b.shape[1] // tf
    modspec = lambda ch: pl.BlockSpec((None, 1, D), lambda i, f: (row_fn(i), 0, ch))
    return pl.pallas_call(
        functools.partial(_ffn_kernel, tf=tf),
        grid=(M // tm, nf),
        in_specs=[
            pl.BlockSpec((tm, D), lambda i, f: (i, 0), pipeline_mode=pl.Buffered(1)),
            pl.BlockSpec((1, D), lambda i, f: (0, 0)),
            modspec(3), modspec(4), modspec(5),
            pl.BlockSpec((D, tf), lambda i, f: (0, f)),
            pl.BlockSpec((D, tf), lambda i, f: (0, f)),
            pl.BlockSpec((tf, D), lambda i, f: (f, 0)),
        ],
        out_specs=pl.BlockSpec((tm, D), lambda i, f: (i, 0)),
        out_shape=jax.ShapeDtypeStruct((M, D), F32),
        scratch_shapes=[pltpu.VMEM((tm, D), BF16)],
        compiler_params=_cparams(("arbitrary", "arbitrary")),
        name="ffn",
    )(x, gain, mod, mod, mod, w1b, w3b, w2)


def _qkv_kernel(x_ref, g_ref, sh_ref, sc_ref, w_ref, hg_ref, cos_ref, sin_ref, ones_ref, o_ref, a_scr,
                *, nrope, nnorm, tn):
    n = pl.program_id(1)

    @pl.when(n == 0)
    def _():
        a_scr[...] = _modulate(x_ref[...], g_ref[...], sh_ref[...], sc_ref[...]).astype(BF16)

    @pl.when(n < nnorm)
    def _():
        acc = _dot(a_scr[...], w_ref[...])
        hg = hg_ref[...]
        ones = ones_ref[...]
        for h in range(tn // MXU_DIM):
            blk = acc[:, h * MXU_DIM:(h + 1) * MXU_DIM]
            ms = _dot((blk * blk).astype(BF16), ones) * (1.0 / NA_HEAD)
            blk = blk * lax.rsqrt(ms + RMS_EPS)
            for s in range(MXU_DIM // NA_HEAD):
                t = blk[:, s * NA_HEAD:(s + 1) * NA_HEAD] * hg
                if nrope:
                    t = t * cos_ref[...] + pltpu.roll(t, NA_HEAD // 2, axis=1) * sin_ref[...]
                c0 = h * MXU_DIM + s * NA_HEAD
                o_ref[:, c0:c0 + NA_HEAD] = t.astype(o_ref.dtype)

    @pl.when(n >= nnorm)
    def _():
        o_ref[...] = _dot(a_scr[...], w_ref[...]).astype(o_ref.dtype)


def _na_qkv(x, mod, row_fn, gain, w, head_gain, cos, sin, *, nrope, nnorm, tm, tn):
    assert nrope in (0, nnorm)
    M, D = x.shape
    N = w.shape[1]
    tpb = cos.shape[0] // tm
    hrow = lambda n: jnp.minimum(n, head_gain.shape[0] - 1)
    return pl.pallas_call(
        functools.partial(_qkv_kernel, nrope=nrope, nnorm=nnorm, tn=tn),
        grid=(M // tm, N // tn),
        in_specs=[
            pl.BlockSpec((tm, D), lambda i, n: (i, 0)),
            pl.BlockSpec((1, D), lambda i, n: (0, 0)),
            pl.BlockSpec((None, 1, D), lambda i, n: (row_fn(i), 0, 0)),
            pl.BlockSpec((None, 1, D), lambda i, n: (row_fn(i), 0, 1)),
            pl.BlockSpec((D, tn), lambda i, n: (0, n)),
            pl.BlockSpec((None, 1, NA_HEAD), lambda i, n: (hrow(n), 0, 0)),
            pl.BlockSpec((tm, NA_HEAD), lambda i, n: (i % tpb, 0)),
            pl.BlockSpec((tm, NA_HEAD), lambda i, n: (i % tpb, 0)),
            pl.BlockSpec((MXU_DIM, MXU_DIM), lambda i, n: (0, 0)),
        ],
        out_specs=pl.BlockSpec((tm, tn), lambda i, n: (i, n)),
        out_shape=jax.ShapeDtypeStruct((M, N), BF16),
        scratch_shapes=[pltpu.VMEM((tm, D), BF16)],
        compiler_params=_cparams(("arbitrary", "arbitrary")),
        name="na_qkv",
    )(x, gain, mod, mod, w, head_gain, cos, sin, _group_ones(MXU_DIM, NA_HEAD))


def _na_kernel(q_ref, k_ref, v_ref, kc_ref, vc_ref, bias_ref, o_ref, *, rows, rb):
    j0 = pl.program_id(2) * rb
    win_h = min(WIN_H, rows)
    nk = win_h * GRID_W
    qrows = range(rb)
    rs = lambda t, j: t[j * GRID_W:(j + 1) * GRID_W]

    qa = q_ref[...]
    s_ctx = lax.dot_general(qa, kc_ref[...], _NT, preferred_element_type=F32)

    def row(j):
        r0 = jnp.clip(j0 + j - WIN_H // 2, 0, rows - win_h)
        off = pl.multiple_of(r0 * GRID_W, GRID_W)
        s_lat = (lax.dot_general(rs(qa, j), k_ref[pl.ds(off, nk), :], _NT, preferred_element_type=F32)
                 + bias_ref[r0 - (j0 + j) + WIN_H - 1])
        yield
        sc = rs(s_ctx, j)
        m = jnp.maximum(jnp.max(s_lat, axis=-1, keepdims=True), jnp.max(sc, axis=-1, keepdims=True))
        yield
        p_lat = jnp.exp(s_lat - m)
        p_ctx = jnp.exp(sc - m)
        yield
        den = jnp.sum(p_lat, axis=-1, keepdims=True) + jnp.sum(p_ctx, axis=-1, keepdims=True)
        o = _dot(p_lat.astype(BF16), v_ref[pl.ds(off, nk), :]) + _dot(p_ctx.astype(BF16), vc_ref[...])
        yield
        o_ref[j * GRID_W:(j + 1) * GRID_W, :] = (o / den).astype(o_ref.dtype)
        yield

    gens = [row(j) for j in qrows]
    done = [False] * rb
    tick = 0
    while not all(done):
        for j in qrows:
            if not done[j] and tick >= j * _ROW_STAGGER:
                done[j] = next(gens[j], _DONE) is _DONE
        tick += 1


def _na_attention(qkv, kvc, bias, B, T, D, rb):
    H = D // NA_HEAD
    rows = T // GRID_W
    C = kvc.shape[1]
    nbias = bias.shape[1]
    return pl.pallas_call(
        functools.partial(_na_kernel, rows=rows, rb=rb),
        grid=(B, H, rows // rb),
        in_specs=[
            pl.BlockSpec((None, rb * GRID_W, NA_HEAD), lambda b, h, j: (b, j, h)),
            pl.BlockSpec((None, T, NA_HEAD), lambda b, h, j: (b, 0, H + h)),
            pl.BlockSpec((None, T, NA_HEAD), lambda b, h, j: (b, 0, 2 * H + h)),
            pl.BlockSpec((None, C, NA_HEAD), lambda b, h, j: (b, 0, h)),
            pl.BlockSpec((None, C, NA_HEAD), lambda b, h, j: (b, 0, H + h)),
            pl.BlockSpec((None, nbias, GRID_W, bias.shape[3]), lambda b, h, j: (h, 0, 0, 0)),
        ],
        out_specs=pl.BlockSpec((None, rb * GRID_W, NA_HEAD), lambda b, h, j: (b, j, h)),
        out_shape=jax.ShapeDtypeStruct((B, T, D), BF16),
        compiler_params=_cparams(("arbitrary", "arbitrary", "arbitrary")),
        name="na_attention",
    )(qkv, qkv, qkv, kvc, kvc, bias)


def _proj_res_kernel(a_ref, w_ref, x_ref, gate_ref, o_ref):
    o_ref[...] = x_ref[...] + gate_ref[...] * _dot(a_ref[...], w_ref[...])


def _proj_residual(a, w, x, mod, row_fn, chunk, tm):
    M, K = a.shape
    N = w.shape[1]
    return pl.pallas_call(
        _proj_res_kernel,
        grid=(M // tm,),
        in_specs=[
            pl.BlockSpec((tm, K), lambda i: (i, 0)),
            pl.BlockSpec((K, N), lambda i: (0, 0)),
            pl.BlockSpec((tm, N), lambda i: (i, 0)),
            pl.BlockSpec((None, 1, N), lambda i: (row_fn(i), 0, chunk)),
        ],
        out_specs=pl.BlockSpec((tm, N), lambda i: (i, 0)),
        out_shape=jax.ShapeDtypeStruct((M, N), F32),
        compiler_params=_cparams(("arbitrary",)),
        name="proj_residual",
    )(a, w, x, mod)


_QUART = NA_HEAD // 4
_ROPE_PERM = np.concatenate([np.arange(0, _QUART), np.arange(2 * _QUART, 3 * _QUART),
                             np.arange(_QUART, 2 * _QUART), np.arange(3 * _QUART, NA_HEAD)])


def _rope_layout(w_qk):
    K, N = w_qk.shape
    return w_qk.reshape(K, N // NA_HEAD, NA_HEAD)[:, :, _ROPE_PERM].reshape(K, N)


def _rope_tables(T):
    t = np.arange(T)
    half = NA_HEAD // 2
    freqs = ROPE_THETA ** (-np.arange(0, half, 2, dtype=np.float64) / half)
    ang_r = (t // GRID_W).astype(np.float64)[:, None] * freqs
    ang_c = (t % GRID_W).astype(np.float64)[:, None] * freqs
    cos = np.concatenate([np.cos(ang_r), np.cos(ang_c), np.cos(ang_r), np.cos(ang_c)], axis=1)
    sin = np.concatenate([-np.sin(ang_r), -np.sin(ang_c), np.sin(ang_r), np.sin(ang_c)], axis=1)
    return jnp.asarray(cos, F32), jnp.asarray(sin, F32)


def _na_bias(rpb, rows):
    win_h = min(WIN_H, rows)
    qc = np.arange(GRID_W)[:, None]
    kc = np.arange(GRID_W)[None, :]
    wstart = np.clip(qc - WIN_W // 2, 0, GRID_W - WIN_W)
    valid = (kc >= wstart) & (kc < wstart + WIN_W)
    rel = np.clip(kc - qc, -(WIN_W - 1), WIN_W - 1) + WIN_W - 1
    g = rpb[:, :, rel]
    g = jnp.where(valid[None, None], g, NEG_INF)
    tabs = []
    for i0 in range(2 * WIN_H - win_h):
        blk = g[:, i0:i0 + win_h]
        tabs.append(jnp.transpose(blk, (0, 2, 1, 3)).reshape(g.shape[0], GRID_W, win_h * GRID_W))
    return jnp.stack(tabs, axis=1)


def _pad_cols(w, n):
    return jnp.pad(w, ((0, 0), (0, n - w.shape[1])))


def _pad_rows(w, n):
    return jnp.pad(w, ((0, n - w.shape[0]), (0, 0)))


def kernel(x, c, ctx, c_ctx, ada_w, ada_b, norm1, norm2, rwkv_mu, rwkv_w_r, rwkv_w_k, rwkv_w_v, rwkv_w_o, rwkv_w0, rwkv_w1, rwkv_w2, rwkv_a0, rwkv_a1, rwkv_a2, rwkv_g1, rwkv_g2, rwkv_k_k, rwkv_k_a, rwkv_r_k, rwkv_ln_w, rwkv_ln_b, na_w_qkv, na_w_o, na_q_gain, na_k_gain, na_rpb, ffn_w1, ffn_w3, ffn_w2):
    B, T, D = x.shape
    C = ctx.shape[1]
    rows = T // GRID_W
    depth = ada_w.shape[0]
    n_mix = 2
    TM = 512
    TMO = 256
    TF = 512 if ffn_w1.shape[-1] % 512 == 0 else 256
    TMF = min(1024, T)

    cvec = jnp.concatenate([c, c_ctx[None, :], jnp.zeros((8 - B - 1, D), F32)], axis=0)
    mods = _ada(cvec, ada_w, ada_b)
    lat_row = lambda tm: (lambda i: i // (T // tm))
    ctx_row = lambda i: B

    xl = x.reshape(B * T, D)
    xc = ctx.reshape(B * C, D)

    for i in range(depth):
        last = i == depth - 1
        j = i // n_mix
        mod = mods[i].reshape(8, 1, 6 * D)
        g1 = norm1[i].reshape(1, D)
        g2 = norm2[i].reshape(1, D)
        w1 = ffn_w1[i].astype(BF16)
        w3 = ffn_w3[i].astype(BF16)
        w2 = ffn_w2[i].astype(BF16)
        if i % n_mix == 0:
            lw = rwkv_w1.shape[-1]
            la = rwkv_a1.shape[-1]
            lg = rwkv_g1.shape[-1]
            wcat = jnp.concatenate([
                rwkv_w_r[j], rwkv_w_k[j], rwkv_w_v[j],
                _pad_cols(jnp.concatenate([rwkv_w1[j, 0], rwkv_w1[j, 1]], axis=1), LORA_BLK),
                _pad_cols(jnp.concatenate([rwkv_a1[j, 0], rwkv_a1[j, 1]], axis=1), LORA_BLK),
                _pad_cols(rwkv_g1[j], 2 * LORA_BLK)], axis=1).astype(BF16)
            z = lambda n: jnp.zeros((n, D), F32)
            w2d = jnp.stack([_pad_rows(rwkv_w2[j, 0], LORA_BLK),
                             _pad_rows(jnp.concatenate([z(lw), rwkv_w2[j, 1]], axis=0), LORA_BLK)]).astype(BF16)
            a2d = jnp.stack([_pad_rows(rwkv_a2[j, 0], LORA_BLK),
                             _pad_rows(jnp.concatenate([z(la), rwkv_a2[j, 1]], axis=0), LORA_BLK)]).astype(BF16)
            wg2 = _pad_rows(rwkv_g2[j], LORA_BLK).astype(BF16)
            w0 = rwkv_w0[j].reshape(2, 1, D)
            a0 = rwkv_a0[j].reshape(2, 1, D)
            k_k = rwkv_k_k[j].reshape(1, D)
            k_a = rwkv_k_a[j].reshape(1, D)
            r_k = rwkv_r_k[j].reshape(1, D)
            ln_w = rwkv_ln_w[j].reshape(1, D)
            ln_b = rwkv_ln_b[j].reshape(1, D)
            w_o = rwkv_w_o[j].astype(BF16)

            s1c, kkc = _rwkv_stage1(xc, mod, ctx_row, g1, rwkv_mu[j], k_k, wcat, latent=False, rows=rows, tm=C)
            h0 = jnp.zeros((2, B, D // PAIR, PAIR, PAIR), F32)
            yc, ac, hc = _wkv_scan(s1c, kkc, w2d, a2d, w0, a0, k_a, h0, B, C, D)
            s1l, kkl = _rwkv_stage1(xl, mod, lat_row(TM), g1, rwkv_mu[j], k_k, wcat, latent=True, rows=rows, tm=TM)
            yl, al, _ = _wkv_scan(s1l, kkl, w2d, a2d, w0, a0, k_a, hc, B, T, D)
            xl = _rwkv_out(yl, al, s1l, wg2, xl, mod, lat_row(TMO), ln_w, ln_b, r_k, k_a, w_o, TMO)
            if not last:
                xc = _rwkv_out(yc, ac, s1c, wg2, xc, mod, ctx_row, ln_w, ln_b, r_k, k_a, w_o, min(TMO, C))
        else:
            H = D // NA_HEAD
            cos, sin = _rope_tables(T)
            scale = NA_HEAD ** -0.5
            hg = jnp.stack([(na_q_gain[j] * scale)[_ROPE_PERM], na_k_gain[j][_ROPE_PERM]]).reshape(2, 1, NA_HEAD)
            tn = 512
            nq = D // tn
            wqkv = jnp.concatenate([_rope_layout(na_w_qkv[j][:, :2 * D]), na_w_qkv[j][:, 2 * D:]],
                                   axis=1).astype(BF16)
            hg_blocks = jnp.concatenate([jnp.broadcast_to(hg[0], (nq, 1, NA_HEAD)),
                                         jnp.broadcast_to(hg[1], (nq, 1, NA_HEAD))], axis=0)
            qkv = _na_qkv(xl, mod, lat_row(TM), g1, wqkv, hg_blocks, cos, sin, nrope=2 * nq, nnorm=2 * nq,
                          tm=TM, tn=tn)
            kg_blocks = jnp.broadcast_to(hg[1], (nq, 1, NA_HEAD))
            kvc = _na_qkv(xc, mod, ctx_row, g1, wqkv[:, D:], kg_blocks, cos[:C], sin[:C], nrope=0, nnorm=nq,
                          tm=C, tn=tn)
            bias = _na_bias(na_rpb[j], rows)
            o = _na_attention(qkv.reshape(B, T, 3 * D), kvc.reshape(B, C, 2 * D), bias, B, T, D, min(16, rows))
            xl = _proj_residual(o.reshape(B * T, D), na_w_o[j].astype(BF16), xl, mod,
                                lat_row(TM), 2, TM)
            if not last:
                raise NotImplementedError("attention layer with context output")
        xl = _ffn(xl, mod, lat_row(TMF), g2, w1, w3, w2, TMF, TF)
        if not last:
            xc = _ffn(xc, mod, ctx_row, g2, w1, w3, w2, C, TF)
    return xl.reshape(B, T, D)
```

```python
import functools
import math

import numpy as np
import jax
import jax.numpy as jnp
from jax import lax
from jax.experimental import pallas as pl
from jax.experimental.pallas import tpu as pltpu

F32 = jnp.float32
BF16 = jnp.bfloat16

GRID_W = 64
RMS_EPS = 1e-6
RWKV_HEAD = 64
GN_EPS = 64e-5
NA_HEAD = 128
WIN_H = 8
WIN_W = 16
ROPE_THETA = 10000.0
NEG_INF = -1e30

LANE = 128
MXU_DIM = 256
VMEM_LIMIT = 56 * 1024 * 1024

CHUNK = 64
PAIR = 2 * RWKV_HEAD
SCAN_PAIRS = 16
SCAN_CHUNKS = 4
LORA_BLK = 256
TN1 = 512


def _cparams(sem):
    return pltpu.CompilerParams(dimension_semantics=sem, vmem_limit_bytes=VMEM_LIMIT)


def _dot(a, b):
    return jnp.dot(a, b, preferred_element_type=F32)


def _split2(x):
    hi = x.astype(BF16)
    lo = (x - hi.astype(F32)).astype(BF16)
    return hi, lo


def _dot_hilo(a, b_bf16):
    hi, lo = _split2(a)
    return _dot(hi, b_bf16) + _dot(lo, b_bf16)


def _rms_scale(x):
    return lax.rsqrt(jnp.mean(x * x, axis=-1, keepdims=True) + RMS_EPS)


def _modulate(x, gain, shift, scale):
    return (x * _rms_scale(x) * gain) * (1.0 + scale) + shift


def _sigmoid(x):
    return 1.0 / (1.0 + jnp.exp(-x))


def _silu(x):
    return x * _sigmoid(x)


def _group_ones(n, group):
    idx = np.arange(n) // group
    return jnp.asarray((idx[:, None] == idx[None, :]).astype(np.float32), dtype=BF16)


def _ada_kernel(c_ref, w_ref, b_ref, o_ref):
    s = _silu(c_ref[...]).astype(BF16)
    o_ref[...] = _dot(s, w_ref[...].astype(BF16)) + b_ref[...]


def _ada(cvec, ada_w, ada_b):
    depth, D, N = ada_w.shape
    R = cvec.shape[0]
    tn = 1024
    return pl.pallas_call(
        _ada_kernel,
        grid=(depth, N // tn),
        in_specs=[
            pl.BlockSpec((R, D), lambda l, n: (0, 0)),
            pl.BlockSpec((None, D, tn), lambda l, n: (l, 0, n)),
            pl.BlockSpec((None, 1, tn), lambda l, n: (l, 0, n)),
        ],
        out_specs=pl.BlockSpec((None, R, tn), lambda l, n: (l, 0, n)),
        out_shape=jax.ShapeDtypeStruct((depth, R, N), F32),
        compiler_params=_cparams(("arbitrary", "arbitrary")),
        name="ada",
    )(cvec, ada_w, ada_b.reshape(depth, 1, N))


def _s1_kernel(xp_ref, x_ref, xn_ref, g_ref, sh_ref, sc_ref, mu_ref, kkw_ref, ones_ref, w_ref,
               o_ref, kk_ref, a_scr, h_scr, *, latent, rows, tm, D, nb):
    i = pl.program_id(0)
    n = pl.program_id(1)
    R = GRID_W
    pad = h_scr.shape[0] - tm

    @pl.when(n == 0)
    def _():
        g = g_ref[...]
        sh = sh_ref[...]
        sc = sc_ref[...]
        lo = pad // 2
        h_scr[lo:lo + tm, :] = _modulate(x_ref[...], g, sh, sc)
        if latent:
            h_scr[0:lo, :] = _modulate(xp_ref[...], g, sh, sc)
            h_scr[lo + tm:, :] = _modulate(xn_ref[...], g, sh, sc)
        else:
            h_scr[0:lo, :] = jnp.zeros((lo, D), F32)
            h_scr[lo + tm:, :] = jnp.zeros((pad - lo, D), F32)
        for rr in range(tm // R):
            base = lo + rr * R
            hc = h_scr[base:base + R, :]
            if latent:
                q = D // 4
                col = lax.broadcasted_iota(jnp.int32, (R, q), 0)
                grow = (i * (tm // R) + rr) & (rows - 1)
                hs = jnp.concatenate([
                    jnp.where(col > 0, h_scr[base - 1:base - 1 + R, 0:q], 0.0),
                    jnp.where(col < R - 1, h_scr[base + 1:base + 1 + R, q:2 * q], 0.0),
                    jnp.where(grow > 0, h_scr[base - R:base, 2 * q:3 * q], 0.0),
                    jnp.where(grow < rows - 1, h_scr[base + R:base + 2 * R, 3 * q:], 0.0)], axis=1)
            else:
                half = D // 2
                hs = jnp.concatenate([h_scr[base - 1:base - 1 + R, :half],
                                      h_scr[base + 1:base + 1 + R, half:]], axis=1)
            xx = hs - hc
            for j in range(6):
                a_scr[j, rr * R:(rr + 1) * R, :] = (hc + xx * mu_ref[j:j + 1, :]).astype(BF16)

    @pl.when(n < nb)
    def _():
        o_ref[...] = _dot(a_scr[0], w_ref[...]).astype(o_ref.dtype)

    @pl.when((n >= nb) & (n < 2 * nb))
    def _():
        acc = _dot(a_scr[2], w_ref[...])
        o_ref[...] = acc.astype(o_ref.dtype)
        kkf = acc * kkw_ref[...]
        sq = kkf * kkf
        ones = ones_ref[...]
        parts = []
        for s in range(TN1 // MXU_DIM):
            parts.append(_dot_hilo(sq[:, s * MXU_DIM:(s + 1) * MXU_DIM], ones))
        ss = jnp.concatenate(parts, axis=1)
        kk_ref[...] = (kkf / jnp.maximum(jnp.sqrt(ss), 1e-12)).astype(kk_ref.dtype)

    @pl.when((n >= 2 * nb) & (n < 3 * nb))
    def _():
        o_ref[...] = _dot(a_scr[3], w_ref[...]).astype(o_ref.dtype)

    @pl.when(n == 3 * nb)
    def _():
        o_ref[:, :LORA_BLK] = jnp.tanh(_dot(a_scr[1], w_ref[:, :LORA_BLK])).astype(o_ref.dtype)
        o_ref[:, LORA_BLK:] = _dot(a_scr[4], w_ref[:, LORA_BLK:]).astype(o_ref.dtype)

    @pl.when(n == 3 * nb + 1)
    def _():
        o_ref[...] = _sigmoid(_dot(a_scr[5], w_ref[...])).astype(o_ref.dtype)


def _rwkv_stage1(x, mod, row_fn, gain, mu, k_k, wcat, *, latent, rows, tm):
    M, D = x.shape
    nb = D // TN1
    N1 = wcat.shape[1]
    nsteps = N1 // TN1
    hb = tm // GRID_W if latent else 1
    hrows = GRID_W if latent else 8
    nhb = M // hrows
    kcol = lambda n: jnp.clip(n - nb, 0, nb - 1)
    kern = functools.partial(_s1_kernel, latent=latent, rows=rows, tm=tm, D=D, nb=nb)
    return pl.pallas_call(
        kern,
        grid=(M // tm, nsteps),
        in_specs=[
            pl.BlockSpec((hrows, D), lambda i, n: (jnp.maximum(i * hb - 1, 0), 0)),
            pl.BlockSpec((tm, D), lambda i, n: (i, 0)),
            pl.BlockSpec((hrows, D), lambda i, n: (jnp.minimum(i * hb + hb, nhb - 1), 0)),
            pl.BlockSpec((1, D), lambda i, n: (0, 0)),
            pl.BlockSpec((None, 1, D), lambda i, n: (row_fn(i), 0, 0)),
            pl.BlockSpec((None, 1, D), lambda i, n: (row_fn(i), 0, 1)),
            pl.BlockSpec((6, D), lambda i, n: (0, 0)),
            pl.BlockSpec((1, TN1), lambda i, n: (0, kcol(n))),
            pl.BlockSpec((MXU_DIM, MXU_DIM), lambda i, n: (0, 0)),
            pl.BlockSpec((D, TN1), lambda i, n: (0, n)),
        ],
        out_specs=[
            pl.BlockSpec((tm, TN1), lambda i, n: (i, n)),
            pl.BlockSpec((tm, TN1), lambda i, n: (i, kcol(n))),
        ],
        out_shape=[jax.ShapeDtypeStruct((M, N1), BF16), jax.ShapeDtypeStruct((M, D), BF16)],
        scratch_shapes=[pltpu.VMEM((6, tm, D), BF16),
                        pltpu.VMEM((tm + 2 * hrows, D), F32)],
        compiler_params=_cparams(("arbitrary", "arbitrary")),
        name="rwkv_stage1",
    )(x, x, x, gain, mod, mod, mu, k_k, _group_ones(MXU_DIM, RWKV_HEAD), wcat)


_NT = (((1,), (1,)), ((), ()))
_TN = (((0,), (0,)), ((), ()))


def _mm(a, b, dims=None):
    a = a.astype(BF16)
    b = b.astype(BF16)
    if dims is None:
        return jnp.dot(a, b, preferred_element_type=F32)
    return lax.dot_general(a, b, dims, preferred_element_type=F32)


_DONE = object()
_FRONT_LEAD = 1
_ROW_STAGGER = 1


def _scan_kernel(r_ref, k_ref, v_ref, kk_ref, lo_ref, w2_ref, a2_ref, w0_ref, a0_ref, ka_ref, h0_ref,
                 y_ref, a_ref, hT_ref, h_scr, *, G, nsub):
    d = pl.program_id(0)
    c = pl.program_id(3)
    nc = pl.num_programs(3)
    L = CHUNK
    pairs = range(G)
    nsteps = int(math.log2(L))

    @pl.when(c == 0)
    def _():
        h_scr[...] = h0_ref[...]

    sgn = 1 - 2 * d
    lane = lax.broadcasted_iota(jnp.int32, (L, PAIR), 1)
    m0 = lane < RWKV_HEAD
    ti = lax.broadcasted_iota(jnp.int32, (L, L), 0)
    si = lax.broadcasted_iota(jnp.int32, (L, L), 1)
    tri = jnp.where((si - ti) * sgn <= 0, 1.0, 0.0).astype(BF16)
    i2 = lax.broadcasted_iota(jnp.int32, (2 * L, 2 * L), 0)
    j2 = lax.broadcasted_iota(jnp.int32, (2 * L, 2 * L), 1)
    dif = jnp.where((i2 // L) == (j2 // L), ((j2 % L) - (i2 % L)) * sgn, 1)
    strict = dif < 0
    incl = dif <= 0
    r2 = lax.broadcasted_iota(jnp.int32, (PAIR, PAIR), 0)
    c2 = lax.broadcasted_iota(jnp.int32, (PAIR, PAIR), 1)
    bd = (r2 // RWKV_HEAD) == (c2 // RWKV_HEAD)

    def sm(x):
        return jnp.concatenate([jnp.where(m0, x, 0.0), jnp.where(m0, 0.0, x)], axis=0)

    def sl(x, g):
        return x[:, g * PAIR:(g + 1) * PAIR]

    def mm2(lhs, rhs):
        out = []
        for g in range(0, len(lhs), 2):
            ra, rb = rhs[g].astype(BF16), rhs[g + 1].astype(BF16)
            z = jnp.zeros_like(ra)
            bd2 = jnp.concatenate([jnp.concatenate([ra, z], axis=1), jnp.concatenate([z, rb], axis=1)], axis=0)
            both = _mm(jnp.concatenate([lhs[g].astype(BF16), lhs[g + 1].astype(BF16)], axis=1), bd2)
            out += [both[:, :PAIR], both[:, PAIR:]]
        return out

    def rows(q):
        return pl.ds(pl.multiple_of((q + d * (nsub - 1 - 2 * q)) * L, L), L)

    res = [None] * nsub

    def front(q):
        rw = rows(q)
        z = _dot(lo_ref[rw, :LORA_BLK], w2_ref[...]) + w0_ref[...]
        za = _dot(lo_ref[rw, LORA_BLK:], a2_ref[...]) + a0_ref[...]
        yield
        ld = -math.exp(-0.5) * _sigmoid(z)
        a = _sigmoid(za)
        a_ref[rw, :] = a.astype(a_ref.dtype)
        p0 = ld.astype(BF16)
        r1 = ld - p0.astype(F32)
        p1 = r1.astype(BF16)
        p2 = (r1 - p1.astype(F32)).astype(BF16)
        yield
        cum = _dot(tri, p0) + _dot(tri, p1) + _dot(tri, p2)
        PL = jnp.exp(jnp.sum(ld, axis=0, keepdims=True))
        yield
        invP = jnp.exp(-cum)
        kk = kk_ref[rw, :].astype(F32)
        v = v_ref[rw, :].astype(F32)
        At = -kk * jnp.exp(cum - ld)
        Bt = (kk * a) * invP
        Kt = (k_ref[rw, :].astype(F32) * (1.0 + (a - 1.0) * ka_ref[...])) * invP
        Rt = r_ref[rw, :].astype(F32) * jnp.exp(cum)
        Bh = Bt * PL
        Kh = Kt * PL
        yield
        As = [sm(sl(At, g)) for g in pairs]
        Rs = [sm(sl(Rt, g)) for g in pairs]
        Vs = [sm(sl(v, g)) for g in pairs]
        Gm = [_mm(jnp.concatenate([As[g], Rs[g]], axis=0),
                  jnp.concatenate([sm(sl(Bt, g)), sm(sl(Kt, g))], axis=0), _NT) for g in pairs]
        yield
        Np = [jnp.where(strict, Gm[g][:2 * L, :2 * L], 0.0).astype(BF16) for g in pairs]
        Aak = [jnp.where(strict, Gm[g][:2 * L, 2 * L:], 0.0) for g in pairs]
        Mr = [jnp.concatenate([jnp.where(incl, Gm[g][2 * L:, :2 * L], 0.0),
                               jnp.where(incl, Gm[g][2 * L:, 2 * L:], 0.0)], axis=1).astype(BF16) for g in pairs]
        AV = mm2(Aak, Vs)
        yield
        pw = [Np]
        for s in range(nsteps - 1):
            pw.append([t.astype(BF16) for t in mm2(pw[-1], pw[-1])])
            yield
        plc = [jnp.transpose(jnp.broadcast_to(sl(PL, g), (PAIR, PAIR))) for g in pairs]
        res[q] = (As, Rs, Vs, Mr, AV, pw, Bh, Kh, v, plc)
        yield

    state = [[h_scr[g] for g in pairs]]

    def back(q):
        As, Rs, Vs, Mr, AV, pw, Bh, Kh, v, plc = res[q]
        H = state[0]
        AH = mm2([jnp.concatenate([As[g], Rs[g]], axis=0) for g in pairs], H)
        yield
        Us = [AH[g][:2 * L] + AV[g] for g in pairs]
        for s in range(nsteps):
            Us = [Us[g] + _mm(pw[s][g], Us[g]) for g in pairs]
            yield
        Ys = [AH[g][2 * L:] + _mm(Mr[g], jnp.concatenate([Us[g], Vs[g]], axis=0)) for g in pairs]
        upd = [_mm(jnp.concatenate([sl(Bh, g), sl(Kh, g)], axis=0),
                   jnp.concatenate([Us[g][:L] + Us[g][L:], sl(v, g)], axis=0), _TN) for g in pairs]
        yield
        rw = rows(q)
        for g in pairs:
            y_ref[rw, g * PAIR:(g + 1) * PAIR] = (Ys[g][:L] + Ys[g][L:]).astype(y_ref.dtype)
        state[0] = [H[g] * plc[g] + jnp.where(bd, upd[g], 0.0) for g in pairs]
        yield

    def run(primary, others):
        while next(primary, _DONE) is not _DONE:
            for o in others:
                next(o, _DONE)

    fronts = [front(q) for q in range(nsub)]
    for _ in range(_FRONT_LEAD):
        next(fronts[0])
    for q in range(nsub):
        nxt = fronts[q + 1:q + 2]
        run(fronts[q], nxt)
        run(back(q), nxt)
    for g in pairs:
        h_scr[g] = state[0][g]

    @pl.when(c == nc - 1)
    def _():
        hT_ref[...] = h_scr[...]


def _wkv_scan(s1, kk, w2d, a2d, w0, a0, k_a, h0, B, T, D):
    G = min(SCAN_PAIRS, D // PAIR)
    W = G * PAIR
    nsub = SCAN_CHUNKS
    L = nsub * CHUNK
    nc = T // L
    npg = D // W
    N1 = s1.shape[1]
    s1 = s1.reshape(B, T, N1)
    kk = kk.reshape(B, T, D)
    chunk = lambda d, c: c + d * (nc - 1 - 2 * c)
    tok = lambda off: pl.BlockSpec((None, L, W), lambda d, b, p, c: (b, chunk(d, c), off + p))
    dirw = lambda rows: pl.BlockSpec((None, rows, W), lambda d, b, p, c: (d, 0, p))
    state = pl.BlockSpec((None, None, G, PAIR, PAIR), lambda d, b, p, c: (d, b, p, 0, 0))
    out_tok = pl.BlockSpec((None, None, L, W), lambda d, b, p, c: (d, b, chunk(d, c), p))
    y, a, hT = pl.pallas_call(
        functools.partial(_scan_kernel, G=G, nsub=nsub),
        grid=(2, B, npg, nc),
        in_specs=[
            tok(0), tok(npg), tok(2 * npg),
            pl.BlockSpec((None, L, W), lambda d, b, p, c: (b, chunk(d, c), p)),
            pl.BlockSpec((None, L, 2 * LORA_BLK), lambda d, b, p, c: (b, chunk(d, c), 3 * D // (2 * LORA_BLK))),
            dirw(LORA_BLK), dirw(LORA_BLK), dirw(1), dirw(1),
            pl.BlockSpec((1, W), lambda d, b, p, c: (0, p)),
            state,
        ],
        out_specs=[out_tok, out_tok, state],
        out_shape=[jax.ShapeDtypeStruct((2, B, T, D), BF16),
                   jax.ShapeDtypeStruct((2, B, T, D), BF16),
                   jax.ShapeDtypeStruct((2, B, D // PAIR, PAIR, PAIR), F32)],
        scratch_shapes=[pltpu.VMEM((G, PAIR, PAIR), F32)],
        compiler_params=_cparams(("arbitrary", "arbitrary", "arbitrary", "arbitrary")),
        name="wkv_scan",
    )(s1, s1, s1, kk, s1, w2d, a2d, w0, a0, k_a, h0)
    return y.reshape(2, B * T, D), a.reshape(2, B * T, D), hT


def _rwkv_out_kernel(y0_ref, y1_ref, r_ref, k_ref, v_ref, a0_ref, a1_ref, sg_ref, g2_ref,
                     lnw_ref, lnb_ref, rk_ref, ka_ref, ones_ref, w_ref, x_ref, gate_ref,
                     o_ref, *, D):
    ones = ones_ref[...]
    inv = 1.0 / RWKV_HEAD
    sg = sg_ref[:, :LORA_BLK]
    acc = None
    for s in range(D // MXU_DIM):
        cs = slice(s * MXU_DIM, (s + 1) * MXU_DIM)
        f = lambda ref: ref[:, cs].astype(F32)
        wkv = f(y0_ref) + f(y1_ref)
        cen = wkv - _dot_hilo(wkv, ones) * inv
        var = _dot((cen * cen).astype(BF16), ones) * inv
        yn = cen * lax.rsqrt(var + GN_EPS) * lnw_ref[:, cs] + lnb_ref[:, cs]
        ks = f(k_ref) * (2.0 + (f(a0_ref) + f(a1_ref) - 2.0) * ka_ref[:, cs])
        bonus = _dot((f(r_ref) * ks * rk_ref[:, cs]).astype(BF16), ones) * f(v_ref)
        gate = _dot(sg, g2_ref[:, cs])
        part = _dot(((yn + bonus) * gate).astype(BF16), w_ref[cs, :])
        acc = part if acc is None else acc + part
    o_ref[...] = x_ref[...] + gate_ref[...] * acc


def _rwkv_out(y, a, s1, g2, x, mod, row_fn, ln_w, ln_b, r_k, k_a, w_o, tm):
    M, D = x.shape
    big = lambda col: pl.BlockSpec((tm, D), lambda i: (i, col))
    dirblk = lambda d: pl.BlockSpec((None, tm, D), lambda i: (d, i, 0))
    row = pl.BlockSpec((1, D), lambda i: (0, 0))
    return pl.pallas_call(
        functools.partial(_rwkv_out_kernel, D=D),
        grid=(M // tm,),
        in_specs=[
            dirblk(0), dirblk(1),
            big(0), big(1), big(2),
            dirblk(0), dirblk(1),
            pl.BlockSpec((tm, 2 * LORA_BLK), lambda i: (i, 3 * D // (2 * LORA_BLK) + 1)),
            pl.BlockSpec((LORA_BLK, D), lambda i: (0, 0)),
            row, row, row, row,
            pl.BlockSpec((MXU_DIM, MXU_DIM), lambda i: (0, 0)),
            pl.BlockSpec((D, D), lambda i: (0, 0)),
            pl.BlockSpec((tm, D), lambda i: (i, 0)),
            pl.BlockSpec((None, 1, D), lambda i: (row_fn(i), 0, 2)),
        ],
        out_specs=pl.BlockSpec((tm, D), lambda i: (i, 0)),
        out_shape=jax.ShapeDtypeStruct((M, D), F32),
        compiler_params=_cparams(("arbitrary",)),
        name="rwkv_out",
    )(y, y, s1, s1, s1, a, a, s1, g2, ln_w, ln_b, r_k, k_a, _group_ones(MXU_DIM, RWKV_HEAD), w_o, x, mod)


def _ffn_kernel(x_ref, g_ref, sh_ref, sc_ref, gate_ref, w1_ref, w3_ref, w2_ref, o_ref, a_scr, *, tf):
    f = pl.program_id(1)
    nf = pl.num_programs(1)

    @pl.when(f == 0)
    def _():
        a_scr[...] = _modulate(x_ref[...], g_ref[...], sh_ref[...], sc_ref[...]).astype(BF16)
        o_ref[...] = jnp.zeros_like(o_ref)

    a = a_scr[...]
    acts = []
    for s in range(tf // MXU_DIM):
        cs = slice(s * MXU_DIM, (s + 1) * MXU_DIM)
        h1 = _dot(a, w1_ref[:, cs])
        h3 = _dot(a, w3_ref[:, cs])
        acts.append((_silu(h1) * h3).astype(BF16))
    o_ref[...] += _dot(jnp.concatenate(acts, axis=1), w2_ref[...])

    @pl.when(f == nf - 1)
    def _():
        o_ref[...] = x_ref[...] + gate_ref[...] * o_ref[...]


def _ffn(x, mod, row_fn, gain, w1b, w3b, w2, tm, tf):
    M, D = x.shape
    nf = w1b.shape[1] // tf
    modspec = lambda ch: pl.BlockSpec((None, 1, D), lambda i, f: (row_fn(i), 0, ch))
    return pl.pallas_call(
        functools.partial(_ffn_kernel, tf=tf),
        grid=(M // tm, nf),
        in_specs=[
            pl.BlockSpec((tm, D), lambda i, f: (i, 0), pipeline_mode=pl.Buffered(1)),
            pl.BlockSpec((1, D), lambda i, f: (0, 0)),
            modspec(3), modspec(4), modspec(5),
            pl.BlockSpec((D, tf), lambda i, f: (0, f)),
            pl.BlockSpec((D, tf), lambda i, f: (0, f)),
            pl.BlockSpec((tf, D), lambda i, f: (f, 0)),
        ],
        out_specs=pl.BlockSpec((tm, D), lambda i, f: (i, 0)),
        out_shape=jax.ShapeDtypeStruct((M, D), F32),
        scratch_shapes=[pltpu.VMEM((tm, D), BF16)],
        compiler_params=_cparams(("arbitrary", "arbitrary")),
        name="ffn",
    )(x, gain, mod, mod, mod, w1b, w3b, w2)


def _qkv_kernel(x_ref, g_ref, sh_ref, sc_ref, w_ref, hg_ref, cos_ref, sin_ref, ones_ref, o_ref, a_scr,
                *, nrope, nnorm, tn):
    n = pl.program_id(1)

    @pl.when(n == 0)
    def _():
        a_scr[...] = _modulate(x_ref[...], g_ref[...], sh_ref[...], sc_ref[...]).astype(BF16)

    @pl.when(n < nnorm)
    def _():
        acc = _dot(a_scr[...], w_ref[...])
        hg = hg_ref[...]
        ones = ones_ref[...]
        for h in range(tn // MXU_DIM):
            blk = acc[:, h * MXU_DIM:(h + 1) * MXU_DIM]
            ms = _dot((blk * blk).astype(BF16), ones) * (1.0 / NA_HEAD)
            blk = blk * lax.rsqrt(ms + RMS_EPS)
            for s in range(MXU_DIM // NA_HEAD):
                t = blk[:, s * NA_HEAD:(s + 1) * NA_HEAD] * hg
                if nrope:
                    t = t * cos_ref[...] + pltpu.roll(t, NA_HEAD // 2, axis=1) * sin_ref[...]
                c0 = h * MXU_DIM + s * NA_HEAD
                o_ref[:, c0:c0 + NA_HEAD] = t.astype(o_ref.dtype)

    @pl.when(n >= nnorm)
    def _():
        o_ref[...] = _dot(a_scr[...], w_ref[...]).astype(o_ref.dtype)


def _na_qkv(x, mod, row_fn, gain, w, head_gain, cos, sin, *, nrope, nnorm, tm, tn):
    assert nrope in (0, nnorm)
    M, D = x.shape
    N = w.shape[1]
    tpb = cos.shape[0] // tm
    hrow = lambda n: jnp.minimum(n, head_gain.shape[0] - 1)
    return pl.pallas_call(
        functools.partial(_qkv_kernel, nrope=nrope, nnorm=nnorm, tn=tn),
        grid=(M // tm, N // tn),
        in_specs=[
            pl.BlockSpec((tm, D), lambda i, n: (i, 0)),
            pl.BlockSpec((1, D), lambda i, n: (0, 0)),
            pl.BlockSpec((None, 1, D), lambda i, n: (row_fn(i), 0, 0)),
            pl.BlockSpec((None, 1, D), lambda i, n: (row_fn(i), 0, 1)),
            pl.BlockSpec((D, tn), lambda i, n: (0, n)),
            pl.BlockSpec((None, 1, NA_HEAD), lambda i, n: (hrow(n), 0, 0)),
            pl.BlockSpec((tm, NA_HEAD), lambda i, n: (i % tpb, 0)),
            pl.BlockSpec((tm, NA_HEAD), lambda i, n: (i % tpb, 0)),
            pl.BlockSpec((MXU_DIM, MXU_DIM), lambda i, n: (0, 0)),
        ],
        out_specs=pl.BlockSpec((tm, tn), lambda i, n: (i, n)),
        out_shape=jax.ShapeDtypeStruct((M, N), BF16),
        scratch_shapes=[pltpu.VMEM((tm, D), BF16)],
        compiler_params=_cparams(("arbitrary", "arbitrary")),
        name="na_qkv",
    )(x, gain, mod, mod, w, head_gain, cos, sin, _group_ones(MXU_DIM, NA_HEAD))


def _na_kernel(q_ref, k_ref, v_ref, kc_ref, vc_ref, bias_ref, o_ref, *, rows, rb):
    j0 = pl.program_id(2) * rb
    win_h = min(WIN_H, rows)
    nk = win_h * GRID_W
    qrows = range(rb)
    rs = lambda t, j: t[j * GRID_W:(j + 1) * GRID_W]

    qa = q_ref[...]
    s_ctx = lax.dot_general(qa, kc_ref[...], _NT, preferred_element_type=F32)

    def row(j):
        r0 = jnp.clip(j0 + j - WIN_H // 2, 0, rows - win_h)
        off = pl.multiple_of(r0 * GRID_W, GRID_W)
        s_lat = (lax.dot_general(rs(qa, j), k_ref[pl.ds(off, nk), :], _NT, preferred_element_type=F32)
                 + bias_ref[r0 - (j0 + j) + WIN_H - 1])
        yield
        sc = rs(s_ctx, j)
        m = jnp.maximum(jnp.max(s_lat, axis=-1, keepdims=True), jnp.max(sc, axis=-1, keepdims=True))
        yield
        p_lat = jnp.exp(s_lat - m)
        p_ctx = jnp.exp(sc - m)
        yield
        den = jnp.sum(p_lat, axis=-1, keepdims=True) + jnp.sum(p_ctx, axis=-1, keepdims=True)
        o = _dot(p_lat.astype(BF16), v_ref[pl.ds(off, nk), :]) + _dot(p_ctx.astype(BF16), vc_ref[...])
        yield
        o_ref[j * GRID_W:(j + 1) * GRID_W, :] = (o / den).astype(o_ref.dtype)
        yield

    gens = [row(j) for j in qrows]
    done = [False] * rb
    tick = 0
    while not all(done):
        for j in qrows:
            if not done[j] and tick >= j * _ROW_STAGGER:
                done[j] = next(gens[j], _DONE) is _DONE
        tick += 1


def _na_attention(qkv, kvc, bias, B, T, D, rb):
    H = D // NA_HEAD
    rows = T // GRID_W
    C = kvc.shape[1]
    nbias = bias.shape[1]
    return pl.pallas_call(
        functools.partial(_na_kernel, rows=rows, rb=rb),
        grid=(B, H, rows // rb),
        in_specs=[
            pl.BlockSpec((None, rb * GRID_W, NA_HEAD), lambda b, h, j: (b, j, h)),
            pl.BlockSpec((None, T, NA_HEAD), lambda b, h, j: (b, 0, H + h)),
            pl.BlockSpec((None, T, NA_HEAD), lambda b, h, j: (b, 0, 2 * H + h)),
            pl.BlockSpec((None, C, NA_HEAD), lambda b, h, j: (b, 0, h)),
            pl.BlockSpec((None, C, NA_HEAD), lambda b, h, j: (b, 0, H + h)),
            pl.BlockSpec((None, nbias, GRID_W, bias.shape[3]), lambda b, h, j: (h, 0, 0, 0)),
        ],
        out_specs=pl.BlockSpec((None, rb * GRID_W, NA_HEAD), lambda b, h, j: (b, j, h)),
        out_shape=jax.ShapeDtypeStruct((B, T, D), BF16),
        compiler_params=_cparams(("arbitrary", "arbitrary", "arbitrary")),
        name="na_attention",
    )(qkv, qkv, qkv, kvc, kvc, bias)


def _proj_res_kernel(a_ref, w_ref, x_ref, gate_ref, o_ref):
    o_ref[...] = x_ref[...] + gate_ref[...] * _dot(a_ref[...], w_ref[...])


def _proj_residual(a, w, x, mod, row_fn, chunk, tm):
    M, K = a.shape
    N = w.shape[1]
    return pl.pallas_call(
        _proj_res_kernel,
        grid=(M // tm,),
        in_specs=[
            pl.BlockSpec((tm, K), lambda i: (i, 0)),
            pl.BlockSpec((K, N), lambda i: (0, 0)),
            pl.BlockSpec((tm, N), lambda i: (i, 0)),
            pl.BlockSpec((None, 1, N), lambda i: (row_fn(i), 0, chunk)),
        ],
        out_specs=pl.BlockSpec((tm, N), lambda i: (i, 0)),
        out_shape=jax.ShapeDtypeStruct((M, N), F32),
        compiler_params=_cparams(("arbitrary",)),
        name="proj_residual",
    )(a, w, x, mod)


_QUART = NA_HEAD // 4
_ROPE_PERM = np.concatenate([np.arange(0, _QUART), np.arange(2 * _QUART, 3 * _QUART),
                             np.arange(_QUART, 2 * _QUART), np.arange(3 * _QUART, NA_HEAD)])


def _rope_layout(w_qk):
    K, N = w_qk.shape
    return w_qk.reshape(K, N // NA_HEAD, NA_HEAD)[:, :, _ROPE_PERM].reshape(K, N)


def _rope_tables(T):
    t = np.arange(T)
    half = NA_HEAD // 2
    freqs = ROPE_THETA ** (-np.arange(0, half, 2, dtype=np.float64) / half)
    ang_r = (t // GRID_W).astype(np.float64)[:, None] * freqs
    ang_c = (t % GRID_W).astype(np.float64)[:, None] * freqs
    cos = np.concatenate([np.cos(ang_r), np.cos(ang_c), np.cos(ang_r), np.cos(ang_c)], axis=1)
    sin = np.concatenate([-np.sin(ang_r), -np.sin(ang_c), np.sin(ang_r), np.sin(ang_c)], axis=1)
    return jnp.asarray(cos, F32), jnp.asarray(sin, F32)


def _na_bias(rpb, rows):
    win_h = min(WIN_H, rows)
    qc = np.arange(GRID_W)[:, None]
    kc = np.arange(GRID_W)[None, :]
    wstart = np.clip(qc - WIN_W // 2, 0, GRID_W - WIN_W)
    valid = (kc >= wstart) & (kc < wstart + WIN_W)
    rel = np.clip(kc - qc, -(WIN_W - 1), WIN_W - 1) + WIN_W - 1
    g = rpb[:, :, rel]
    g = jnp.where(valid[None, None], g, NEG_INF)
    tabs = []
    for i0 in range(2 * WIN_H - win_h):
        blk = g[:, i0:i0 + win_h]
        tabs.append(jnp.transpose(blk, (0, 2, 1, 3)).reshape(g.shape[0], GRID_W, win_h * GRID_W))
    return jnp.stack(tabs, axis=1)


def _pad_cols(w, n):
    return jnp.pad(w, ((0, 0), (0, n - w.shape[1])))


def _pad_rows(w, n):
    return jnp.pad(w, ((0, n - w.shape[0]), (0, 0)))


def kernel(x, c, ctx, c_ctx, ada_w, ada_b, norm1, norm2, rwkv_mu, rwkv_w_r, rwkv_w_k, rwkv_w_v, rwkv_w_o, rwkv_w0, rwkv_w1, rwkv_w2, rwkv_a0, rwkv_a1, rwkv_a2, rwkv_g1, rwkv_g2, rwkv_k_k, rwkv_k_a, rwkv_r_k, rwkv_ln_w, rwkv_ln_b, na_w_qkv, na_w_o, na_q_gain, na_k_gain, na_rpb, ffn_w1, ffn_w3, ffn_w2):
    B, T, D = x.shape
    C = ctx.shape[1]
    rows = T // GRID_W
    depth = ada_w.shape[0]
    n_mix = 2
    TM = 512
    TMO = 256
    TF = 512 if ffn_w1.shape[-1] % 512 == 0 else 256
    TMF = min(1024, T)

    cvec = jnp.concatenate([c, c_ctx[None, :], jnp.zeros((8 - B - 1, D), F32)], axis=0)
    mods = _ada(cvec, ada_w, ada_b)
    lat_row = lambda tm: (lambda i: i // (T // tm))
    ctx_row = lambda i: B

    xl = x.reshape(B * T, D)
    xc = ctx.reshape(B * C, D)

    for i in range(depth):
        last = i == depth - 1
        j = i // n_mix
        mod = mods[i].reshape(8, 1, 6 * D)
        g1 = norm1[i].reshape(1, D)
        g2 = norm2[i].reshape(1, D)
        w1 = ffn_w1[i].astype(BF16)
        w3 = ffn_w3[i].astype(BF16)
        w2 = ffn_w2[i].astype(BF16)
        if i % n_mix == 0:
            lw = rwkv_w1.shape[-1]
            la = rwkv_a1.shape[-1]
            lg = rwkv_g1.shape[-1]
            wcat = jnp.concatenate([
                rwkv_w_r[j], rwkv_w_k[j], rwkv_w_v[j],
                _pad_cols(jnp.concatenate([rwkv_w1[j, 0], rwkv_w1[j, 1]], axis=1), LORA_BLK),
                _pad_cols(jnp.concatenate([rwkv_a1[j, 0], rwkv_a1[j, 1]], axis=1), LORA_BLK),
                _pad_cols(rwkv_g1[j], 2 * LORA_BLK)], axis=1).astype(BF16)
            z = lambda n: jnp.zeros((n, D), F32)
            w2d = jnp.stack([_pad_rows(rwkv_w2[j, 0], LORA_BLK),
                             _pad_rows(jnp.concatenate([z(lw), rwkv_w2[j, 1]], axis=0), LORA_BLK)]).astype(BF16)
            a2d = jnp.stack([_pad_rows(rwkv_a2[j, 0], LORA_BLK),
                             _pad_rows(jnp.concatenate([z(la), rwkv_a2[j, 1]], axis=0), LORA_BLK)]).astype(BF16)
            wg2 = _pad_rows(rwkv_g2[j], LORA_BLK).astype(BF16)
            w0 = rwkv_w0[j].reshape(2, 1, D)
            a0 = rwkv_a0[j].reshape(2, 1, D)
            k_k = rwkv_k_k[j].reshape(1, D)
            k_a = rwkv_k_a[j].reshape(1, D)
            r_k = rwkv_r_k[j].reshape(1, D)
            ln_w = rwkv_ln_w[j].reshape(1, D)
            ln_b = rwkv_ln_b[j].reshape(1, D)
            w_o = rwkv_w_o[j].astype(BF16)

            s1c, kkc = _rwkv_stage1(xc, mod, ctx_row, g1, rwkv_mu[j], k_k, wcat, latent=False, rows=rows, tm=C)
            h0 = jnp.zeros((2, B, D // PAIR, PAIR, PAIR), F32)
            yc, ac, hc = _wkv_scan(s1c, kkc, w2d, a2d, w0, a0, k_a, h0, B, C, D)
            s1l, kkl = _rwkv_stage1(xl, mod, lat_row(TM), g1, rwkv_mu[j], k_k, wcat, latent=True, rows=rows, tm=TM)
            yl, al, _ = _wkv_scan(s1l, kkl, w2d, a2d, w0, a0, k_a, hc, B, T, D)
            xl = _rwkv_out(yl, al, s1l, wg2, xl, mod, lat_row(TMO), ln_w, ln_b, r_k, k_a, w_o, TMO)
            if not last:
                xc = _rwkv_out(yc, ac, s1c, wg2, xc, mod, ctx_row, ln_w, ln_b, r_k, k_a, w_o, min(TMO, C))
        else:
            H = D // NA_HEAD
            cos, sin = _rope_tables(T)
            scale = NA_HEAD ** -0.5
            hg = jnp.stack([(na_q_gain[j] * scale)[_ROPE_PERM], na_k_gain[j][_ROPE_PERM]]).reshape(2, 1, NA_HEAD)
            tn = 512
            nq = D // tn
            wqkv = jnp.concatenate([_rope_layout(na_w_qkv[j][:, :2 * D]), na_w_qkv[j][:, 2 * D:]],
                                   axis=1).astype(BF16)
            hg_blocks = jnp.concatenate([jnp.broadcast_to(hg[0], (nq, 1, NA_HEAD)),
                                         jnp.broadcast_to(hg[1], (nq, 1, NA_HEAD))], axis=0)
            qkv = _na_qkv(xl, mod, lat_row(TM), g1, wqkv, hg_blocks, cos, sin, nrope=2 * nq, nnorm=2 * nq,
                          tm=TM, tn=tn)
            kg_blocks = jnp.broadcast_to(hg[1], (nq, 1, NA_HEAD))
            kvc = _na_qkv(xc, mod, ctx_row, g1, wqkv[:, D:], kg_blocks, cos[:C], sin[:C], nrope=0, nnorm=nq,
                          tm=C, tn=tn)
            bias = _na_bias(na_rpb[j], rows)
            o = _na_attention(qkv.reshape(B, T, 3 * D), kvc.reshape(B, C, 2 * D), bias, B, T, D, min(16, rows))
            xl = _proj_residual(o.reshape(B * T, D), na_w_o[j].astype(BF16), xl, mod,
                                lat_row(TM), 2, TM)
            if not last:
                raise NotImplementedError("attention layer with context output")
        xl = _ffn(xl, mod, lat_row(TMF), g2, w1, w3, w2, TMF, TF)
        if not last:
            xc = _ffn(xc, mod, ctx_row, g2, w1, w3, w2, C, TF)
    return xl.reshape(B, T, D)
```

```python
import functools
import math

import numpy as np
import jax
import jax.numpy as jnp
from jax import lax
from jax.experimental import pallas as pl
from jax.experimental.pallas import tpu as pltpu

F32 = jnp.float32
BF16 = jnp.bfloat16

GRID_W = 64
RMS_EPS = 1e-6
RWKV_HEAD = 64
GN_EPS = 64e-5
NA_HEAD = 128
WIN_H = 8
WIN_W = 16
ROPE_THETA = 10000.0
NEG_INF = -1e30

LANE = 128
MXU_DIM = 256
VMEM_LIMIT = 56 * 1024 * 1024

CHUNK = 64
PAIR = 2 * RWKV_HEAD
SCAN_PAIRS = 16
SCAN_CHUNKS = 4
LORA_BLK = 256
TN1 = 4 * LORA_BLK


def _cparams(sem):
    return pltpu.CompilerParams(dimension_semantics=sem, vmem_limit_bytes=VMEM_LIMIT)


def _dot(a, b):
    return jnp.dot(a, b, preferred_element_type=F32)


def _split2(x):
    hi = x.astype(BF16)
    lo = (x - hi.astype(F32)).astype(BF16)
    return hi, lo


def _dot_hilo(a, b_bf16):
    hi, lo = _split2(a)
    return _dot(hi, b_bf16) + _dot(lo, b_bf16)


def _rms_scale(x):
    return lax.rsqrt(jnp.mean(x * x, axis=-1, keepdims=True) + RMS_EPS)


def _modulate(x, gain, shift, scale):
    return (x * _rms_scale(x) * gain) * (1.0 + scale) + shift


def _sigmoid(x):
    return 1.0 / (1.0 + jnp.exp(-x))


def _silu(x):
    return x * _sigmoid(x)


def _group_ones(n, group):
    idx = np.arange(n) // group
    return jnp.asarray((idx[:, None] == idx[None, :]).astype(np.float32), dtype=BF16)


def _ada_kernel(c_ref, w_ref, b_ref, o_ref):
    s = _silu(c_ref[...]).astype(BF16)
    o_ref[...] = _dot(s, w_ref[...].astype(BF16)) + b_ref[...]


def _ada(cvec, ada_w, ada_b):
    depth, D, N = ada_w.shape
    R = cvec.shape[0]
    tn = 1024
    return pl.pallas_call(
        _ada_kernel,
        grid=(depth, N // tn),
        in_specs=[
            pl.BlockSpec((R, D), lambda l, n: (0, 0)),
            pl.BlockSpec((None, D, tn), lambda l, n: (l, 0, n)),
            pl.BlockSpec((None, 1, tn), lambda l, n: (l, 0, n)),
        ],
        out_specs=pl.BlockSpec((None, R, tn), lambda l, n: (l, 0, n)),
        out_shape=jax.ShapeDtypeStruct((depth, R, N), F32),
        compiler_params=_cparams(("arbitrary", "arbitrary")),
        name="ada",
    )(cvec, ada_w, ada_b.reshape(depth, 1, N))


def _s1_kernel(xp_ref, x_ref, xn_ref, g_ref, sh_ref, sc_ref, mu_ref, kkw_ref, ones_ref, w_ref,
               o_ref, kk_ref, a_scr, h_scr, *, latent, rows, tm, D, nb):
    i = pl.program_id(0)
    n = pl.program_id(1)
    R = GRID_W
    pad = h_scr.shape[0] - tm

    @pl.when(n == 0)
    def _():
        g = g_ref[...]
        sh = sh_ref[...]
        sc = sc_ref[...]
        lo = pad // 2
        h_scr[lo:lo + tm, :] = _modulate(x_ref[...], g, sh, sc)
        if latent:
            h_scr[0:lo, :] = _modulate(xp_ref[...], g, sh, sc)
            h_scr[lo + tm:, :] = _modulate(xn_ref[...], g, sh, sc)
        else:
            h_scr[0:lo, :] = jnp.zeros((lo, D), F32)
            h_scr[lo + tm:, :] = jnp.zeros((pad - lo, D), F32)
        for rr in range(tm // R):
            base = lo + rr * R
            hc = h_scr[base:base + R, :]
            if latent:
                q = D // 4
                col = lax.broadcasted_iota(jnp.int32, (R, q), 0)
                grow = (i * (tm // R) + rr) & (rows - 1)
                hs = jnp.concatenate([
                    jnp.where(col > 0, h_scr[base - 1:base - 1 + R, 0:q], 0.0),
                    jnp.where(col < R - 1, h_scr[base + 1:base + 1 + R, q:2 * q], 0.0),
                    jnp.where(grow > 0, h_scr[base - R:base, 2 * q:3 * q], 0.0),
                    jnp.where(grow < rows - 1, h_scr[base + R:base + 2 * R, 3 * q:], 0.0)], axis=1)
            else:
                half = D // 2
                hs = jnp.concatenate([h_scr[base - 1:base - 1 + R, :half],
                                      h_scr[base + 1:base + 1 + R, half:]], axis=1)
            xx = hs - hc
            for j in range(6):
                a_scr[j, rr * R:(rr + 1) * R, :] = (hc + xx * mu_ref[j:j + 1, :]).astype(BF16)

    @pl.when(n < nb)
    def _():
        o_ref[...] = _dot(a_scr[0], w_ref[...]).astype(o_ref.dtype)

    @pl.when((n >= nb) & (n < 2 * nb))
    def _():
        acc = _dot(a_scr[2], w_ref[...])
        o_ref[...] = acc.astype(o_ref.dtype)
        kkf = acc * kkw_ref[...]
        sq = kkf * kkf
        ones = ones_ref[...]
        parts = []
        for s in range(TN1 // MXU_DIM):
            parts.append(_dot_hilo(sq[:, s * MXU_DIM:(s + 1) * MXU_DIM], ones))
        ss = jnp.concatenate(parts, axis=1)
        kk_ref[...] = (kkf / jnp.maximum(jnp.sqrt(ss), 1e-12)).astype(kk_ref.dtype)

    @pl.when((n >= 2 * nb) & (n < 3 * nb))
    def _():
        o_ref[...] = _dot(a_scr[3], w_ref[...]).astype(o_ref.dtype)

    @pl.when(n == 3 * nb)
    def _():
        c1, c2, c3 = LORA_BLK, 2 * LORA_BLK, 3 * LORA_BLK
        o_ref[:, :c1] = jnp.tanh(_dot(a_scr[1], w_ref[:, :c1])).astype(o_ref.dtype)
        o_ref[:, c1:c2] = _dot(a_scr[4], w_ref[:, c1:c2]).astype(o_ref.dtype)
        o_ref[:, c2:c3] = _sigmoid(_dot(a_scr[5], w_ref[:, c2:c3])).astype(o_ref.dtype)
        o_ref[:, c3:] = jnp.zeros((o_ref.shape[0], TN1 - c3), o_ref.dtype)


def _rwkv_stage1(x, mod, row_fn, gain, mu, k_k, wcat, *, latent, rows, tm):
    M, D = x.shape
    nb = D // TN1
    N1 = wcat.shape[1]
    nsteps = N1 // TN1
    hb = tm // GRID_W if latent else 1
    hrows = GRID_W if latent else 8
    nhb = M // hrows
    kcol = lambda n: jnp.clip(n - nb, 0, nb - 1)
    kern = functools.partial(_s1_kernel, latent=latent, rows=rows, tm=tm, D=D, nb=nb)
    return pl.pallas_call(
        kern,
        grid=(M // tm, nsteps),
        in_specs=[
            pl.BlockSpec((hrows, D), lambda i, n: (jnp.maximum(i * hb - 1, 0), 0)),
            pl.BlockSpec((tm, D), lambda i, n: (i, 0)),
            pl.BlockSpec((hrows, D), lambda i, n: (jnp.minimum(i * hb + hb, nhb - 1), 0)),
            pl.BlockSpec((1, D), lambda i, n: (0, 0)),
            pl.BlockSpec((None, 1, D), lambda i, n: (row_fn(i), 0, 0)),
            pl.BlockSpec((None, 1, D), lambda i, n: (row_fn(i), 0, 1)),
            pl.BlockSpec((6, D), lambda i, n: (0, 0)),
            pl.BlockSpec((1, TN1), lambda i, n: (0, kcol(n))),
            pl.BlockSpec((MXU_DIM, MXU_DIM), lambda i, n: (0, 0)),
            pl.BlockSpec((D, TN1), lambda i, n: (0, n)),
        ],
        out_specs=[
            pl.BlockSpec((tm, TN1), lambda i, n: (i, n)),
            pl.BlockSpec((tm, TN1), lambda i, n: (i, kcol(n))),
        ],
        out_shape=[jax.ShapeDtypeStruct((M, N1), BF16), jax.ShapeDtypeStruct((M, D), BF16)],
        scratch_shapes=[pltpu.VMEM((6, tm, D), BF16),
                        pltpu.VMEM((tm + 2 * hrows, D), F32)],
        compiler_params=_cparams(("arbitrary", "arbitrary")),
        name="rwkv_stage1",
    )(x, x, x, gain, mod, mod, mu, k_k, _group_ones(MXU_DIM, RWKV_HEAD), wcat)


_NT = (((1,), (1,)), ((), ()))
_TN = (((0,), (0,)), ((), ()))


def _mm(a, b, dims=None):
    a = a.astype(BF16)
    b = b.astype(BF16)
    if dims is None:
        return jnp.dot(a, b, preferred_element_type=F32)
    return lax.dot_general(a, b, dims, preferred_element_type=F32)


_DONE = object()
_FRONT_LEAD = 1
_ROW_STAGGER = 1


def _scan_kernel(r_ref, k_ref, v_ref, kk_ref, lo_ref, w2_ref, a2_ref, w0_ref, a0_ref, ka_ref, h0_ref,
                 y_ref, a_ref, hT_ref, h_scr, *, G, nsub):
    d = pl.program_id(0)
    c = pl.program_id(3)
    nc = pl.num_programs(3)
    L = CHUNK
    pairs = range(G)
    nsteps = int(math.log2(L))

    @pl.when(c == 0)
    def _():
        h_scr[...] = h0_ref[...]

    sgn = 1 - 2 * d
    lane = lax.broadcasted_iota(jnp.int32, (L, PAIR), 1)
    m0 = lane < RWKV_HEAD
    ti = lax.broadcasted_iota(jnp.int32, (L, L), 0)
    si = lax.broadcasted_iota(jnp.int32, (L, L), 1)
    tri = jnp.where((si - ti) * sgn <= 0, 1.0, 0.0).astype(BF16)
    i2 = lax.broadcasted_iota(jnp.int32, (2 * L, 2 * L), 0)
    j2 = lax.broadcasted_iota(jnp.int32, (2 * L, 2 * L), 1)
    dif = jnp.where((i2 // L) == (j2 // L), ((j2 % L) - (i2 % L)) * sgn, 1)
    strict = dif < 0
    incl = dif <= 0
    r2 = lax.broadcasted_iota(jnp.int32, (PAIR, PAIR), 0)
    c2 = lax.broadcasted_iota(jnp.int32, (PAIR, PAIR), 1)
    bd = (r2 // RWKV_HEAD) == (c2 // RWKV_HEAD)

    def sm(x):
        return jnp.concatenate([jnp.where(m0, x, 0.0), jnp.where(m0, 0.0, x)], axis=0)

    def sl(x, g):
        return x[:, g * PAIR:(g + 1) * PAIR]

    def mm2(lhs, rhs):
        out = []
        for g in range(0, len(lhs), 2):
            ra, rb = rhs[g].astype(BF16), rhs[g + 1].astype(BF16)
            z = jnp.zeros_like(ra)
            bd2 = jnp.concatenate([jnp.concatenate([ra, z], axis=1), jnp.concatenate([z, rb], axis=1)], axis=0)
            both = _mm(jnp.concatenate([lhs[g].astype(BF16), lhs[g + 1].astype(BF16)], axis=1), bd2)
            out += [both[:, :PAIR], both[:, PAIR:]]
        return out

    def rows(q):
        return pl.ds(pl.multiple_of((q + d * (nsub - 1 - 2 * q)) * L, L), L)

    res = [None] * nsub

    def front(q):
        rw = rows(q)
        z = _dot(lo_ref[rw, :LORA_BLK], w2_ref[...]) + w0_ref[...]
        za = _dot(lo_ref[rw, LORA_BLK:], a2_ref[...]) + a0_ref[...]
        yield
        ld = -math.exp(-0.5) * _sigmoid(z)
        a = _sigmoid(za)
        a_ref[rw, :] = a.astype(a_ref.dtype)
        p0 = ld.astype(BF16)
        r1 = ld - p0.astype(F32)
        p1 = r1.astype(BF16)
        p2 = (r1 - p1.astype(F32)).astype(BF16)
        yield
        cum = _dot(tri, p0) + _dot(tri, p1) + _dot(tri, p2)
        PL = jnp.exp(jnp.sum(ld, axis=0, keepdims=True))
        yield
        invP = jnp.exp(-cum)
        kk = kk_ref[rw, :].astype(F32)
        v = v_ref[rw, :].astype(F32)
        At = -kk * jnp.exp(cum - ld)
        Bt = (kk * a) * invP
        Kt = (k_ref[rw, :].astype(F32) * (1.0 + (a - 1.0) * ka_ref[...])) * invP
        Rt = r_ref[rw, :].astype(F32) * jnp.exp(cum)
        Bh = Bt * PL
        Kh = Kt * PL
        yield
        As = [sm(sl(At, g)) for g in pairs]
        Rs = [sm(sl(Rt, g)) for g in pairs]
        Vs = [sm(sl(v, g)) for g in pairs]
        Gm = [_mm(jnp.concatenate([As[g], Rs[g]], axis=0),
                  jnp.concatenate([sm(sl(Bt, g)), sm(sl(Kt, g))], axis=0), _NT) for g in pairs]
        yield
        Np = [jnp.where(strict, Gm[g][:2 * L, :2 * L], 0.0).astype(BF16) for g in pairs]
        Aak = [jnp.where(strict, Gm[g][:2 * L, 2 * L:], 0.0) for g in pairs]
        Mr = [jnp.concatenate([jnp.where(incl, Gm[g][2 * L:, :2 * L], 0.0),
                               jnp.where(incl, Gm[g][2 * L:, 2 * L:], 0.0)], axis=1).astype(BF16) for g in pairs]
        AV = mm2(Aak, Vs)
        yield
        pw = [Np]
        for s in range(nsteps - 1):
            pw.append([t.astype(BF16) for t in mm2(pw[-1], pw[-1])])
            yield
        plc = [jnp.transpose(jnp.broadcast_to(sl(PL, g), (PAIR, PAIR))) for g in pairs]
        res[q] = (As, Rs, Vs, Mr, AV, pw, Bh, Kh, v, plc)
        yield

    state = [[h_scr[g] for g in pairs]]

    def back(q):
        As, Rs, Vs, Mr, AV, pw, Bh, Kh, v, plc = res[q]
        H = state[0]
        AH = mm2([jnp.concatenate([As[g], Rs[g]], axis=0) for g in pairs], H)
        yield
        Us = [AH[g][:2 * L] + AV[g] for g in pairs]
        for s in range(nsteps):
            Us = [Us[g] + _mm(pw[s][g], Us[g]) for g in pairs]
            yield
        Ys = [AH[g][2 * L:] + _mm(Mr[g], jnp.concatenate([Us[g], Vs[g]], axis=0)) for g in pairs]
        upd = [_mm(jnp.concatenate([sl(Bh, g), sl(Kh, g)], axis=0),
                   jnp.concatenate([Us[g][:L] + Us[g][L:], sl(v, g)], axis=0), _TN) for g in pairs]
        yield
        rw = rows(q)
        for g in pairs:
            y_ref[rw, g * PAIR:(g + 1) * PAIR] = (Ys[g][:L] + Ys[g][L:]).astype(y_ref.dtype)
        state[0] = [H[g] * plc[g] + jnp.where(bd, upd[g], 0.0) for g in pairs]
        yield

    def run(primary, others):
        while next(primary, _DONE) is not _DONE:
            for o in others:
                next(o, _DONE)

    fronts = [front(q) for q in range(nsub)]
    for _ in range(_FRONT_LEAD):
        next(fronts[0])
    for q in range(nsub):
        nxt = fronts[q + 1:q + 2]
        run(fronts[q], nxt)
        run(back(q), nxt)
    for g in pairs:
        h_scr[g] = state[0][g]

    @pl.when(c == nc - 1)
    def _():
        hT_ref[...] = h_scr[...]


def _wkv_scan(s1, kk, w2d, a2d, w0, a0, k_a, h0, B, T, D):
    G = min(SCAN_PAIRS, D // PAIR)
    W = G * PAIR
    nsub = SCAN_CHUNKS
    L = nsub * CHUNK
    nc = T // L
    npg = D // W
    N1 = s1.shape[1]
    s1 = s1.reshape(B, T, N1)
    kk = kk.reshape(B, T, D)
    chunk = lambda d, c: c + d * (nc - 1 - 2 * c)
    tok = lambda off: pl.BlockSpec((None, L, W), lambda d, b, p, c: (b, chunk(d, c), off + p))
    dirw = lambda rows: pl.BlockSpec((None, rows, W), lambda d, b, p, c: (d, 0, p))
    state = pl.BlockSpec((None, None, G, PAIR, PAIR), lambda d, b, p, c: (d, b, p, 0, 0))
    out_tok = pl.BlockSpec((None, None, L, W), lambda d, b, p, c: (d, b, chunk(d, c), p))
    y, a, hT = pl.pallas_call(
        functools.partial(_scan_kernel, G=G, nsub=nsub),
        grid=(2, B, npg, nc),
        in_specs=[
            tok(0), tok(npg), tok(2 * npg),
            pl.BlockSpec((None, L, W), lambda d, b, p, c: (b, chunk(d, c), p)),
            pl.BlockSpec((None, L, 2 * LORA_BLK), lambda d, b, p, c: (b, chunk(d, c), 3 * D // (2 * LORA_BLK))),
            dirw(LORA_BLK), dirw(LORA_BLK), dirw(1), dirw(1),
            pl.BlockSpec((1, W), lambda d, b, p, c: (0, p)),
            state,
        ],
        out_specs=[out_tok, out_tok, state],
        out_shape=[jax.ShapeDtypeStruct((2, B, T, D), BF16),
                   jax.ShapeDtypeStruct((2, B, T, D), BF16),
                   jax.ShapeDtypeStruct((2, B, D // PAIR, PAIR, PAIR), F32)],
        scratch_shapes=[pltpu.VMEM((G, PAIR, PAIR), F32)],
        compiler_params=_cparams(("arbitrary", "arbitrary", "arbitrary", "arbitrary")),
        name="wkv_scan",
    )(s1, s1, s1, kk, s1, w2d, a2d, w0, a0, k_a, h0)
    return y.reshape(2, B * T, D), a.reshape(2, B * T, D), hT


def _rwkv_out_kernel(y0_ref, y1_ref, r_ref, k_ref, v_ref, a0_ref, a1_ref, sg_ref, g2_ref,
                     lnw_ref, lnb_ref, rk_ref, ka_ref, ones_ref, w_ref, x_ref, gate_ref,
                     o_ref, *, D):
    ones = ones_ref[...]
    inv = 1.0 / RWKV_HEAD
    sg = sg_ref[:, :LORA_BLK]
    acc = None
    for s in range(D // MXU_DIM):
        cs = slice(s * MXU_DIM, (s + 1) * MXU_DIM)
        f = lambda ref: ref[:, cs].astype(F32)
        wkv = f(y0_ref) + f(y1_ref)
        cen = wkv - _dot_hilo(wkv, ones) * inv
        var = _dot((cen * cen).astype(BF16), ones) * inv
        yn = cen * lax.rsqrt(var + GN_EPS) * lnw_ref[:, cs] + lnb_ref[:, cs]
        ks = f(k_ref) * (2.0 + (f(a0_ref) + f(a1_ref) - 2.0) * ka_ref[:, cs])
        bonus = _dot((f(r_ref) * ks * rk_ref[:, cs]).astype(BF16), ones) * f(v_ref)
        gate = _dot(sg, g2_ref[:, cs])
        part = _dot(((yn + bonus) * gate).astype(BF16), w_ref[cs, :])
        acc = part if acc is None else acc + part
    o_ref[...] = x_ref[...] + gate_ref[...] * acc


def _rwkv_out(y, a, s1, g2, x, mod, row_fn, ln_w, ln_b, r_k, k_a, w_o, tm):
    M, D = x.shape
    big = lambda col: pl.BlockSpec((tm, D), lambda i: (i, col))
    dirblk = lambda d: pl.BlockSpec((None, tm, D), lambda i: (d, i, 0))
    row = pl.BlockSpec((1, D), lambda i: (0, 0))
    return pl.pallas_call(
        functools.partial(_rwkv_out_kernel, D=D),
        grid=(M // tm,),
        in_specs=[
            dirblk(0), dirblk(1),
            big(0), big(1), big(2),
            dirblk(0), dirblk(1),
            pl.BlockSpec((tm, 2 * LORA_BLK), lambda i: (i, 3 * D // (2 * LORA_BLK) + 1)),
            pl.BlockSpec((LORA_BLK, D), lambda i: (0, 0)),
            row, row, row, row,
            pl.BlockSpec((MXU_DIM, MXU_DIM), lambda i: (0, 0)),
            pl.BlockSpec((D, D), lambda i: (0, 0)),
            pl.BlockSpec((tm, D), lambda i: (i, 0)),
            pl.BlockSpec((None, 1, D), lambda i: (row_fn(i), 0, 2)),
        ],
        out_specs=pl.BlockSpec((tm, D), lambda i: (i, 0)),
        out_shape=jax.ShapeDtypeStruct((M, D), F32),
        compiler_params=_cparams(("arbitrary",)),
        name="rwkv_out",
    )(y, y, s1, s1, s1, a, a, s1, g2, ln_w, ln_b, r_k, k_a, _group_ones(MXU_DIM, RWKV_HEAD), w_o, x, mod)


def _ffn_kernel(x_ref, g_ref, sh_ref, sc_ref, gate_ref, w1_ref, w3_ref, w2_ref, o_ref, a_scr, *, tf):
    f = pl.program_id(1)
    nf = pl.num_programs(1)

    @pl.when(f == 0)
    def _():
        a_scr[...] = _modulate(x_ref[...], g_ref[...], sh_ref[...], sc_ref[...]).astype(BF16)
        o_ref[...] = jnp.zeros_like(o_ref)

    a = a_scr[...]
    acts = []
    for s in range(tf // MXU_DIM):
        cs = slice(s * MXU_DIM, (s + 1) * MXU_DIM)
        h1 = _dot(a, w1_ref[:, cs])
        h3 = _dot(a, w3_ref[:, cs])
        acts.append((_silu(h1) * h3).astype(BF16))
    o_ref[...] += _dot(jnp.concatenate(acts, axis=1), w2_ref[...])

    @pl.when(f == nf - 1)
    def _():
        o_ref[...] = x_ref[...] + gate_ref[...] * o_ref[...]


def _ffn(x, mod, row_fn, gain, w1b, w3b, w2, tm, tf):
    M, D = x.shape
    nf = w1b.shape[1] // tf
    modspec = lambda ch: pl.BlockSpec((None, 1, D), lambda i, f: (row_fn(i), 0, ch))
    return pl.pallas_call(
        functools.partial(_ffn_kernel, tf=tf),
        grid=(M // tm, nf),
        in_specs=[
            pl.BlockSpec((tm, D), lambda i, f: (i, 0), pipeline_mode=pl.Buffered(1)),
            pl.BlockSpec((1, D), lambda i, f: (0, 0)),
            modspec(3), modspec(4), modspec(5),
            pl.BlockSpec((D, tf), lambda i, f: (0, f)),
            pl.BlockSpec((D, tf), lambda i, f: (0, f)),
            pl.BlockSpec((tf, D), lambda i, f: (f, 0)),
        ],
        out_specs=pl.BlockSpec((tm, D), lambda i, f: (i, 0)),
        out_shape=jax.ShapeDtypeStruct((M, D), F32),
        scratch_shapes=[pltpu.VMEM((tm, D), BF16)],
        compiler_params=_cparams(("arbitrary", "arbitrary")),
        name="ffn",
    )(x, gain, mod, mod, mod, w1b, w3b, w2)


def _qkv_kernel(x_ref, g_ref, sh_ref, sc_ref, w_ref, hg_ref, cos_ref, sin_ref, ones_ref, o_ref, a_scr,
                *, nrope, nnorm, tn):
    n = pl.program_id(1)

    @pl.when(n == 0)
    def _():
        a_scr[...] = _modulate(x_ref[...], g_ref[...], sh_ref[...], sc_ref[...]).astype(BF16)

    @pl.when(n < nnorm)
    def _():
        acc = _dot(a_scr[...], w_ref[...])
        hg = hg_ref[...]
        ones = ones_ref[...]
        for h in range(tn // MXU_DIM):
            blk = acc[:, h * MXU_DIM:(h + 1) * MXU_DIM]
            ms = _dot((blk * blk).astype(BF16), ones) * (1.0 / NA_HEAD)
            blk = blk * lax.rsqrt(ms + RMS_EPS)
            for s in range(MXU_DIM // NA_HEAD):
                t = blk[:, s * NA_HEAD:(s + 1) * NA_HEAD] * hg
                if nrope:
                    t = t * cos_ref[...] + pltpu.roll(t, NA_HEAD // 2, axis=1) * sin_ref[...]
                c0 = h * MXU_DIM + s * NA_HEAD
                o_ref[:, c0:c0 + NA_HEAD] = t.astype(o_ref.dtype)

    @pl.when(n >= nnorm)
    def _():
        o_ref[...] = _dot(a_scr[...], w_ref[...]).astype(o_ref.dtype)


def _na_qkv(x, mod, row_fn, gain, w, head_gain, cos, sin, *, nrope, nnorm, tm, tn):
    assert nrope in (0, nnorm)
    M, D = x.shape
    N = w.shape[1]
    tpb = cos.shape[0] // tm
    hrow = lambda n: jnp.minimum(n, head_gain.shape[0] - 1)
    return pl.pallas_call(
        functools.partial(_qkv_kernel, nrope=nrope, nnorm=nnorm, tn=tn),
        grid=(M // tm, N // tn),
        in_specs=[
            pl.BlockSpec((tm, D), lambda i, n: (i, 0)),
            pl.BlockSpec((1, D), lambda i, n: (0, 0)),
            pl.BlockSpec((None, 1, D), lambda i, n: (row_fn(i), 0, 0)),
            pl.BlockSpec((None, 1, D), lambda i, n: (row_fn(i), 0, 1)),
            pl.BlockSpec((D, tn), lambda i, n: (0, n)),
            pl.BlockSpec((None, 1, NA_HEAD), lambda i, n: (hrow(n), 0, 0)),
            pl.BlockSpec((tm, NA_HEAD), lambda i, n: (i % tpb, 0)),
            pl.BlockSpec((tm, NA_HEAD), lambda i, n: (i % tpb, 0)),
            pl.BlockSpec((MXU_DIM, MXU_DIM), lambda i, n: (0, 0)),
        ],
        out_specs=pl.BlockSpec((tm, tn), lambda i, n: (i, n)),
        out_shape=jax.ShapeDtypeStruct((M, N), BF16),
        scratch_shapes=[pltpu.VMEM((tm, D), BF16)],
        compiler_params=_cparams(("arbitrary", "arbitrary")),
        name="na_qkv",
    )(x, gain, mod, mod, w, head_gain, cos, sin, _group_ones(MXU_DIM, NA_HEAD))


def _na_kernel(q_ref, k_ref, v_ref, kc_ref, vc_ref, bias_ref, o_ref, *, rows, rb):
    j0 = pl.program_id(2) * rb
    win_h = min(WIN_H, rows)
    nk = win_h * GRID_W
    qrows = range(rb)
    rs = lambda t, j: t[j * GRID_W:(j + 1) * GRID_W]

    qa = q_ref[...]
    s_ctx = lax.dot_general(qa, kc_ref[...], _NT, preferred_element_type=F32)

    def row(j):
        r0 = jnp.clip(j0 + j - WIN_H // 2, 0, rows - win_h)
        off = pl.multiple_of(r0 * GRID_W, GRID_W)
        s_lat = (lax.dot_general(rs(qa, j), k_ref[pl.ds(off, nk), :], _NT, preferred_element_type=F32)
                 + bias_ref[r0 - (j0 + j) + WIN_H - 1])
        yield
        sc = rs(s_ctx, j)
        m = jnp.maximum(jnp.max(s_lat, axis=-1, keepdims=True), jnp.max(sc, axis=-1, keepdims=True))
        yield
        p_lat = jnp.exp(s_lat - m)
        p_ctx = jnp.exp(sc - m)
        yield
        den = jnp.sum(p_lat, axis=-1, keepdims=True) + jnp.sum(p_ctx, axis=-1, keepdims=True)
        o = _dot(p_lat.astype(BF16), v_ref[pl.ds(off, nk), :]) + _dot(p_ctx.astype(BF16), vc_ref[...])
        yield
        o_ref[j * GRID_W:(j + 1) * GRID_W, :] = (o / den).astype(o_ref.dtype)
        yield

    gens = [row(j) for j in qrows]
    done = [False] * rb
    tick = 0
    while not all(done):
        for j in qrows:
            if not done[j] and tick >= j * _ROW_STAGGER:
                done[j] = next(gens[j], _DONE) is _DONE
        tick += 1


def _na_attention(qkv, kvc, bias, B, T, D, rb):
    H = D // NA_HEAD
    rows = T // GRID_W
    C = kvc.shape[1]
    nbias = bias.shape[1]
    return pl.pallas_call(
        functools.partial(_na_kernel, rows=rows, rb=rb),
        grid=(B, H, rows // rb),
        in_specs=[
            pl.BlockSpec((None, rb * GRID_W, NA_HEAD), lambda b, h, j: (b, j, h)),
            pl.BlockSpec((None, T, NA_HEAD), lambda b, h, j: (b, 0, H + h)),
            pl.BlockSpec((None, T, NA_HEAD), lambda b, h, j: (b, 0, 2 * H + h)),
            pl.BlockSpec((None, C, NA_HEAD), lambda b, h, j: (b, 0, h)),
            pl.BlockSpec((None, C, NA_HEAD), lambda b, h, j: (b, 0, H + h)),
            pl.BlockSpec((None, nbias, GRID_W, bias.shape[3]), lambda b, h, j: (h, 0, 0, 0)),
        ],
        out_specs=pl.BlockSpec((None, rb * GRID_W, NA_HEAD), lambda b, h, j: (b, j, h)),
        out_shape=jax.ShapeDtypeStruct((B, T, D), BF16),
        compiler_params=_cparams(("arbitrary", "arbitrary", "arbitrary")),
        name="na_attention",
    )(qkv, qkv, qkv, kvc, kvc, bias)


def _proj_res_kernel(a_ref, w_ref, x_ref, gate_ref, o_ref):
    o_ref[...] = x_ref[...] + gate_ref[...] * _dot(a_ref[...], w_ref[...])


def _proj_residual(a, w, x, mod, row_fn, chunk, tm):
    M, K = a.shape
    N = w.shape[1]
    return pl.pallas_call(
        _proj_res_kernel,
        grid=(M // tm,),
        in_specs=[
            pl.BlockSpec((tm, K), lambda i: (i, 0)),
            pl.BlockSpec((K, N), lambda i: (0, 0)),
            pl.BlockSpec((tm, N), lambda i: (i, 0)),
            pl.BlockSpec((None, 1, N), lambda i: (row_fn(i), 0, chunk)),
        ],
        out_specs=pl.BlockSpec((tm, N), lambda i: (i, 0)),
        out_shape=jax.ShapeDtypeStruct((M, N), F32),
        compiler_params=_cparams(("arbitrary",)),
        name="proj_residual",
    )(a, w, x, mod)


_QUART = NA_HEAD // 4
_ROPE_PERM = np.concatenate([np.arange(0, _QUART), np.arange(2 * _QUART, 3 * _QUART),
                             np.arange(_QUART, 2 * _QUART), np.arange(3 * _QUART, NA_HEAD)])


def _rope_layout(w_qk):
    K, N = w_qk.shape
    return w_qk.reshape(K, N // NA_HEAD, NA_HEAD)[:, :, _ROPE_PERM].reshape(K, N)


def _rope_tables(T):
    t = np.arange(T)
    half = NA_HEAD // 2
    freqs = ROPE_THETA ** (-np.arange(0, half, 2, dtype=np.float64) / half)
    ang_r = (t // GRID_W).astype(np.float64)[:, None] * freqs
    ang_c = (t % GRID_W).astype(np.float64)[:, None] * freqs
    cos = np.concatenate([np.cos(ang_r), np.cos(ang_c), np.cos(ang_r), np.cos(ang_c)], axis=1)
    sin = np.concatenate([-np.sin(ang_r), -np.sin(ang_c), np.sin(ang_r), np.sin(ang_c)], axis=1)
    return jnp.asarray(cos, F32), jnp.asarray(sin, F32)


def _na_bias(rpb, rows):
    win_h = min(WIN_H, rows)
    qc = np.arange(GRID_W)[:, None]
    kc = np.arange(GRID_W)[None, :]
    wstart = np.clip(qc - WIN_W // 2, 0, GRID_W - WIN_W)
    valid = (kc >= wstart) & (kc < wstart + WIN_W)
    rel = np.clip(kc - qc, -(WIN_W - 1), WIN_W - 1) + WIN_W - 1
    g = rpb[:, :, rel]
    g = jnp.where(valid[None, None], g, NEG_INF)
    tabs = []
    for i0 in range(2 * WIN_H - win_h):
        blk = g[:, i0:i0 + win_h]
        tabs.append(jnp.transpose(blk, (0, 2, 1, 3)).reshape(g.shape[0], GRID_W, win_h * GRID_W))
    return jnp.stack(tabs, axis=1)


def _pad_cols(w, n):
    return jnp.pad(w, ((0, 0), (0, n - w.shape[1])))


def _pad_rows(w, n):
    return jnp.pad(w, ((0, n - w.shape[0]), (0, 0)))


def kernel(x, c, ctx, c_ctx, ada_w, ada_b, norm1, norm2, rwkv_mu, rwkv_w_r, rwkv_w_k, rwkv_w_v, rwkv_w_o, rwkv_w0, rwkv_w1, rwkv_w2, rwkv_a0, rwkv_a1, rwkv_a2, rwkv_g1, rwkv_g2, rwkv_k_k, rwkv_k_a, rwkv_r_k, rwkv_ln_w, rwkv_ln_b, na_w_qkv, na_w_o, na_q_gain, na_k_gain, na_rpb, ffn_w1, ffn_w3, ffn_w2):
    B, T, D = x.shape
    C = ctx.shape[1]
    rows = T // GRID_W
    depth = ada_w.shape[0]
    n_mix = 2
    TM = 512
    TMO = 256
    TF = 512 if ffn_w1.shape[-1] % 512 == 0 else 256
    TMF = min(1024, T)

    cvec = jnp.concatenate([c, c_ctx[None, :], jnp.zeros((8 - B - 1, D), F32)], axis=0)
    mods = _ada(cvec, ada_w, ada_b)
    lat_row = lambda tm: (lambda i: i // (T // tm))
    ctx_row = lambda i: B

    xl = x.reshape(B * T, D)
    xc = ctx.reshape(B * C, D)

    for i in range(depth):
        last = i == depth - 1
        j = i // n_mix
        mod = mods[i].reshape(8, 1, 6 * D)
        g1 = norm1[i].reshape(1, D)
        g2 = norm2[i].reshape(1, D)
        w1 = ffn_w1[i].astype(BF16)
        w3 = ffn_w3[i].astype(BF16)
        w2 = ffn_w2[i].astype(BF16)
        if i % n_mix == 0:
            lw = rwkv_w1.shape[-1]
            la = rwkv_a1.shape[-1]
            lg = rwkv_g1.shape[-1]
            wcat = jnp.concatenate([
                rwkv_w_r[j], rwkv_w_k[j], rwkv_w_v[j],
                _pad_cols(jnp.concatenate([rwkv_w1[j, 0], rwkv_w1[j, 1]], axis=1), LORA_BLK),
                _pad_cols(jnp.concatenate([rwkv_a1[j, 0], rwkv_a1[j, 1]], axis=1), LORA_BLK),
                _pad_cols(rwkv_g1[j], 2 * LORA_BLK)], axis=1).astype(BF16)
            z = lambda n: jnp.zeros((n, D), F32)
            w2d = jnp.stack([_pad_rows(rwkv_w2[j, 0], LORA_BLK),
                             _pad_rows(jnp.concatenate([z(lw), rwkv_w2[j, 1]], axis=0), LORA_BLK)]).astype(BF16)
            a2d = jnp.stack([_pad_rows(rwkv_a2[j, 0], LORA_BLK),
                             _pad_rows(jnp.concatenate([z(la), rwkv_a2[j, 1]], axis=0), LORA_BLK)]).astype(BF16)
            wg2 = _pad_rows(rwkv_g2[j], LORA_BLK).astype(BF16)
            w0 = rwkv_w0[j].reshape(2, 1, D)
            a0 = rwkv_a0[j].reshape(2, 1, D)
            k_k = rwkv_k_k[j].reshape(1, D)
            k_a = rwkv_k_a[j].reshape(1, D)
            r_k = rwkv_r_k[j].reshape(1, D)
            ln_w = rwkv_ln_w[j].reshape(1, D)
            ln_b = rwkv_ln_b[j].reshape(1, D)
            w_o = rwkv_w_o[j].astype(BF16)

            s1c, kkc = _rwkv_stage1(xc, mod, ctx_row, g1, rwkv_mu[j], k_k, wcat, latent=False, rows=rows, tm=C)
            h0 = jnp.zeros((2, B, D // PAIR, PAIR, PAIR), F32)
            yc, ac, hc = _wkv_scan(s1c, kkc, w2d, a2d, w0, a0, k_a, h0, B, C, D)
            s1l, kkl = _rwkv_stage1(xl, mod, lat_row(TM), g1, rwkv_mu[j], k_k, wcat, latent=True, rows=rows, tm=TM)
            yl, al, _ = _wkv_scan(s1l, kkl, w2d, a2d, w0, a0, k_a, hc, B, T, D)
            xl = _rwkv_out(yl, al, s1l, wg2, xl, mod, lat_row(TMO), ln_w, ln_b, r_k, k_a, w_o, TMO)
            if not last:
                xc = _rwkv_out(yc, ac, s1c, wg2, xc, mod, ctx_row, ln_w, ln_b, r_k, k_a, w_o, min(TMO, C))
        else:
            H = D // NA_HEAD
            cos, sin = _rope_tables(T)
            scale = NA_HEAD ** -0.5
            hg = jnp.stack([(na_q_gain[j] * scale)[_ROPE_PERM], na_k_gain[j][_ROPE_PERM]]).reshape(2, 1, NA_HEAD)
            tn = min(1024, D)
            nq = D // tn
            wqkv = jnp.concatenate([_rope_layout(na_w_qkv[j][:, :2 * D]), na_w_qkv[j][:, 2 * D:]],
                                   axis=1).astype(BF16)
            hg_blocks = jnp.concatenate([jnp.broadcast_to(hg[0], (nq, 1, NA_HEAD)),
                                         jnp.broadcast_to(hg[1], (nq, 1, NA_HEAD))], axis=0)
            qkv = _na_qkv(xl, mod, lat_row(TM), g1, wqkv, hg_blocks, cos, sin, nrope=2 * nq, nnorm=2 * nq,
                          tm=TM, tn=tn)
            kg_blocks = jnp.broadcast_to(hg[1], (nq, 1, NA_HEAD))
            kvc = _na_qkv(xc, mod, ctx_row, g1, wqkv[:, D:], kg_blocks, cos[:C], sin[:C], nrope=0, nnorm=nq,
                          tm=C, tn=tn)
            bias = _na_bias(na_rpb[j], rows)
            o = _na_attention(qkv.reshape(B, T, 3 * D), kvc.reshape(B, C, 2 * D), bias, B, T, D, min(32, rows))
            xl = _proj_residual(o.reshape(B * T, D), na_w_o[j].astype(BF16), xl, mod,
                                lat_row(TM), 2, TM)
            if not last:
                raise NotImplementedError("attention layer with context output")
        xl = _ffn(xl, mod, lat_row(TMF), g2, w1, w3, w2, TMF, TF)
        if not last:
            xc = _ffn(xc, mod, ctx_row, g2, w1, w3, w2, C, TF)
    return xl.reshape(B, T, D)
```

```python
import functools
import math

import numpy as np
import jax
import jax.numpy as jnp
from jax import lax
from jax.experimental import pallas as pl
from jax.experimental.pallas import tpu as pltpu

F32 = jnp.float32
BF16 = jnp.bfloat16

GRID_W = 64
RMS_EPS = 1e-6
RWKV_HEAD = 64
GN_EPS = 64e-5
NA_HEAD = 128
WIN_H = 8
WIN_W = 16
ROPE_THETA = 10000.0
NEG_INF = -1e30

LANE = 128
MXU_DIM = 256
VMEM_LIMIT = 56 * 1024 * 1024

CHUNK = 64
PAIR = 2 * RWKV_HEAD
SCAN_PAIRS = 16
SCAN_CHUNKS = 4
LORA_BLK = 256
TN1 = 4 * LORA_BLK


def _cparams(sem):
    return pltpu.CompilerParams(dimension_semantics=sem, vmem_limit_bytes=VMEM_LIMIT)


def _dot(a, b):
    return jnp.dot(a, b, preferred_element_type=F32)


def _split2(x):
    hi = x.astype(BF16)
    lo = (x - hi.astype(F32)).astype(BF16)
    return hi, lo


def _dot_hilo(a, b_bf16):
    hi, lo = _split2(a)
    return _dot(hi, b_bf16) + _dot(lo, b_bf16)


def _rms_scale(x):
    return lax.rsqrt(jnp.mean(x * x, axis=-1, keepdims=True) + RMS_EPS)


def _modulate(x, gain, shift, scale):
    return (x * _rms_scale(x) * gain) * (1.0 + scale) + shift


def _sigmoid(x):
    return 1.0 / (1.0 + jnp.exp(-x))


def _silu(x):
    return x * _sigmoid(x)


def _group_ones(n, group):
    idx = np.arange(n) // group
    return jnp.asarray((idx[:, None] == idx[None, :]).astype(np.float32), dtype=BF16)


def _ada_kernel(c_ref, w_ref, b_ref, o_ref):
    s = _silu(c_ref[...]).astype(BF16)
    o_ref[...] = _dot(s, w_ref[...].astype(BF16)) + b_ref[...]


def _ada(cvec, ada_w, ada_b):
    depth, D, N = ada_w.shape
    R = cvec.shape[0]
    tn = 1024
    return pl.pallas_call(
        _ada_kernel,
        grid=(depth, N // tn),
        in_specs=[
            pl.BlockSpec((R, D), lambda l, n: (0, 0)),
            pl.BlockSpec((None, D, tn), lambda l, n: (l, 0, n)),
            pl.BlockSpec((None, 1, tn), lambda l, n: (l, 0, n)),
        ],
        out_specs=pl.BlockSpec((None, R, tn), lambda l, n: (l, 0, n)),
        out_shape=jax.ShapeDtypeStruct((depth, R, N), F32),
        compiler_params=_cparams(("arbitrary", "arbitrary")),
        name="ada",
    )(cvec, ada_w, ada_b.reshape(depth, 1, N))


def _s1_kernel(xp_ref, x_ref, xn_ref, g_ref, sh_ref, sc_ref, mu_ref, kkw_ref, ones_ref, w_ref,
               o_ref, kk_ref, a_scr, h_scr, xx_scr, *, latent, rows, tm, D, nb):
    i = pl.program_id(0)
    n = pl.program_id(1)
    R = GRID_W
    pad = h_scr.shape[0] - tm

    @pl.when(n == 0)
    def _():
        g = g_ref[...]
        sh = sh_ref[...]
        sc = sc_ref[...]
        lo = pad // 2
        h_scr[lo:lo + tm, :] = _modulate(x_ref[...], g, sh, sc)
        if latent:
            h_scr[0:lo, :] = _modulate(xp_ref[...], g, sh, sc)
            h_scr[lo + tm:, :] = _modulate(xn_ref[...], g, sh, sc)
        else:
            h_scr[0:lo, :] = jnp.zeros((lo, D), F32)
            h_scr[lo + tm:, :] = jnp.zeros((pad - lo, D), F32)
        for rr in range(tm // R):
            base = lo + rr * R
            hc = h_scr[base:base + R, :]
            if latent:
                q = D // 4
                col = lax.broadcasted_iota(jnp.int32, (R, q), 0)
                grow = (i * (tm // R) + rr) & (rows - 1)
                hs = jnp.concatenate([
                    jnp.where(col > 0, h_scr[base - 1:base - 1 + R, 0:q], 0.0),
                    jnp.where(col < R - 1, h_scr[base + 1:base + 1 + R, q:2 * q], 0.0),
                    jnp.where(grow > 0, h_scr[base - R:base, 2 * q:3 * q], 0.0),
                    jnp.where(grow < rows - 1, h_scr[base + R:base + 2 * R, 3 * q:], 0.0)], axis=1)
            else:
                half = D // 2
                hs = jnp.concatenate([h_scr[base - 1:base - 1 + R, :half],
                                      h_scr[base + 1:base + 1 + R, half:]], axis=1)
            xx = hs - hc
            xx_scr[rr * R:(rr + 1) * R, :] = xx
            a_scr[0, rr * R:(rr + 1) * R, :] = (hc + xx * mu_ref[0:1, :]).astype(BF16)

    def mix(j):
        lo = pad // 2
        for rr in range(tm // R):
            rs_ = slice(rr * R, (rr + 1) * R)
            a_scr[j, rs_, :] = (h_scr[lo + rr * R:lo + (rr + 1) * R, :] + xx_scr[rs_, :] * mu_ref[j:j + 1, :]).astype(BF16)

    def r_step():
        o_ref[...] = _dot(a_scr[0], w_ref[...]).astype(o_ref.dtype)

    def k_step():
        acc = _dot(a_scr[2], w_ref[...])
        o_ref[...] = acc.astype(o_ref.dtype)
        kkf = acc * kkw_ref[...]
        sq = kkf * kkf
        ones = ones_ref[...]
        parts = []
        for s in range(TN1 // MXU_DIM):
            parts.append(_dot_hilo(sq[:, s * MXU_DIM:(s + 1) * MXU_DIM], ones))
        ss = jnp.concatenate(parts, axis=1)
        kk_ref[...] = (kkf / jnp.maximum(jnp.sqrt(ss), 1e-12)).astype(kk_ref.dtype)

    def v_step():
        o_ref[...] = _dot(a_scr[3], w_ref[...]).astype(o_ref.dtype)

    def lora_step():
        c1, c2, c3 = LORA_BLK, 2 * LORA_BLK, 3 * LORA_BLK
        o_ref[:, :c1] = jnp.tanh(_dot(a_scr[1], w_ref[:, :c1])).astype(o_ref.dtype)
        o_ref[:, c1:c2] = _dot(a_scr[4], w_ref[:, c1:c2]).astype(o_ref.dtype)
        o_ref[:, c2:c3] = _sigmoid(_dot(a_scr[5], w_ref[:, c2:c3])).astype(o_ref.dtype)
        o_ref[:, c3:] = jnp.zeros((o_ref.shape[0], TN1 - c3), o_ref.dtype)

    steps = [r_step] * nb + [k_step] * nb + [v_step] * nb + [lora_step]
    first_use = {2: nb, 3: 2 * nb, 1: 3 * nb, 4: 3 * nb, 5: 3 * nb}
    pending = sorted(first_use, key=lambda j: first_use[j])
    for t, step in enumerate(steps):
        side = [j for j in pending if first_use[j] == t + 1] or pending[:1]
        pending = [j for j in pending if j not in side]

        @pl.when(n == t)
        def _(step=step, side=side):
            step()
            for j in side:
                mix(j)


def _rwkv_stage1(x, mod, row_fn, gain, mu, k_k, wcat, *, latent, rows, tm):
    M, D = x.shape
    nb = D // TN1
    N1 = wcat.shape[1]
    nsteps = N1 // TN1
    hb = tm // GRID_W if latent else 1
    hrows = GRID_W if latent else 8
    nhb = M // hrows
    kcol = lambda n: jnp.clip(n - nb, 0, nb - 1)
    kern = functools.partial(_s1_kernel, latent=latent, rows=rows, tm=tm, D=D, nb=nb)
    return pl.pallas_call(
        kern,
        grid=(M // tm, nsteps),
        in_specs=[
            pl.BlockSpec((hrows, D), lambda i, n: (jnp.maximum(i * hb - 1, 0), 0)),
            pl.BlockSpec((tm, D), lambda i, n: (i, 0)),
            pl.BlockSpec((hrows, D), lambda i, n: (jnp.minimum(i * hb + hb, nhb - 1), 0)),
            pl.BlockSpec((1, D), lambda i, n: (0, 0)),
            pl.BlockSpec((None, 1, D), lambda i, n: (row_fn(i), 0, 0)),
            pl.BlockSpec((None, 1, D), lambda i, n: (row_fn(i), 0, 1)),
            pl.BlockSpec((6, D), lambda i, n: (0, 0)),
            pl.BlockSpec((1, TN1), lambda i, n: (0, kcol(n))),
            pl.BlockSpec((MXU_DIM, MXU_DIM), lambda i, n: (0, 0)),
            pl.BlockSpec((D, TN1), lambda i, n: (0, n)),
        ],
        out_specs=[
            pl.BlockSpec((tm, TN1), lambda i, n: (i, n)),
            pl.BlockSpec((tm, TN1), lambda i, n: (i, kcol(n))),
        ],
        out_shape=[jax.ShapeDtypeStruct((M, N1), BF16), jax.ShapeDtypeStruct((M, D), BF16)],
        scratch_shapes=[pltpu.VMEM((6, tm, D), BF16),
                        pltpu.VMEM((tm + 2 * hrows, D), F32),
                        pltpu.VMEM((tm, D), F32)],
        compiler_params=_cparams(("arbitrary", "arbitrary")),
        name="rwkv_stage1",
    )(x, x, x, gain, mod, mod, mu, k_k, _group_ones(MXU_DIM, RWKV_HEAD), wcat)


_NT = (((1,), (1,)), ((), ()))
_TN = (((0,), (0,)), ((), ()))


def _mm(a, b, dims=None):
    a = a.astype(BF16)
    b = b.astype(BF16)
    if dims is None:
        return jnp.dot(a, b, preferred_element_type=F32)
    return lax.dot_general(a, b, dims, preferred_element_type=F32)


_DONE = object()
_FRONT_LEAD = 1
_ROW_STAGGER = 1


def _scan_kernel(r_ref, k_ref, v_ref, kk_ref, lo_ref, w2_ref, a2_ref, w0_ref, a0_ref, ka_ref, h0_ref,
                 y_ref, a_ref, hT_ref, h_scr, *, G, nsub):
    d = pl.program_id(0)
    c = pl.program_id(3)
    nc = pl.num_programs(3)
    L = CHUNK
    pairs = range(G)
    nsteps = int(math.log2(L))

    @pl.when(c == 0)
    def _():
        h_scr[...] = h0_ref[...]

    sgn = 1 - 2 * d
    lane = lax.broadcasted_iota(jnp.int32, (L, PAIR), 1)
    m0 = lane < RWKV_HEAD
    ti = lax.broadcasted_iota(jnp.int32, (L, L), 0)
    si = lax.broadcasted_iota(jnp.int32, (L, L), 1)
    tri = jnp.where((si - ti) * sgn <= 0, 1.0, 0.0).astype(BF16)
    i2 = lax.broadcasted_iota(jnp.int32, (2 * L, 2 * L), 0)
    j2 = lax.broadcasted_iota(jnp.int32, (2 * L, 2 * L), 1)
    dif = jnp.where((i2 // L) == (j2 // L), ((j2 % L) - (i2 % L)) * sgn, 1)
    strict = dif < 0
    incl = dif <= 0
    r2 = lax.broadcasted_iota(jnp.int32, (PAIR, PAIR), 0)
    c2 = lax.broadcasted_iota(jnp.int32, (PAIR, PAIR), 1)
    bd = (r2 // RWKV_HEAD) == (c2 // RWKV_HEAD)

    def sm(x):
        return jnp.concatenate([jnp.where(m0, x, 0.0), jnp.where(m0, 0.0, x)], axis=0)

    def sl(x, g):
        return x[:, g * PAIR:(g + 1) * PAIR]

    def mm2(lhs, rhs):
        out = []
        for g in range(0, len(lhs), 2):
            ra, rb = rhs[g].astype(BF16), rhs[g + 1].astype(BF16)
            z = jnp.zeros_like(ra)
            bd2 = jnp.concatenate([jnp.concatenate([ra, z], axis=1), jnp.concatenate([z, rb], axis=1)], axis=0)
            both = _mm(jnp.concatenate([lhs[g].astype(BF16), lhs[g + 1].astype(BF16)], axis=1), bd2)
            out += [both[:, :PAIR], both[:, PAIR:]]
        return out

    def rows(q):
        return pl.ds(pl.multiple_of((q + d * (nsub - 1 - 2 * q)) * L, L), L)

    res = [None] * nsub

    def front(q):
        rw = rows(q)
        z = _dot(lo_ref[rw, :LORA_BLK], w2_ref[...]) + w0_ref[...]
        za = _dot(lo_ref[rw, LORA_BLK:], a2_ref[...]) + a0_ref[...]
        yield
        ld = -math.exp(-0.5) * _sigmoid(z)
        a = _sigmoid(za)
        a_ref[rw, :] = a.astype(a_ref.dtype)
        p0 = ld.astype(BF16)
        r1 = ld - p0.astype(F32)
        p1 = r1.astype(BF16)
        p2 = (r1 - p1.astype(F32)).astype(BF16)
        yield
        cum = _dot(tri, p0) + _dot(tri, p1) + _dot(tri, p2)
        PL = jnp.exp(jnp.sum(ld, axis=0, keepdims=True))
        yield
        invP = jnp.exp(-cum)
        kk = kk_ref[rw, :].astype(F32)
        v = v_ref[rw, :].astype(F32)
        At = -kk * jnp.exp(cum - ld)
        Bt = (kk * a) * invP
        Kt = (k_ref[rw, :].astype(F32) * (1.0 + (a - 1.0) * ka_ref[...])) * invP
        Rt = r_ref[rw, :].astype(F32) * jnp.exp(cum)
        Bh = Bt * PL
        Kh = Kt * PL
        yield
        As = [sm(sl(At, g)) for g in pairs]
        Rs = [sm(sl(Rt, g)) for g in pairs]
        Vs = [sm(sl(v, g)) for g in pairs]
        Gm = [_mm(jnp.concatenate([As[g], Rs[g]], axis=0),
                  jnp.concatenate([sm(sl(Bt, g)), sm(sl(Kt, g))], axis=0), _NT) for g in pairs]
        yield
        Np = [jnp.where(strict, Gm[g][:2 * L, :2 * L], 0.0).astype(BF16) for g in pairs]
        Aak = [jnp.where(strict, Gm[g][:2 * L, 2 * L:], 0.0) for g in pairs]
        Mr = [jnp.concatenate([jnp.where(incl, Gm[g][2 * L:, :2 * L], 0.0),
                               jnp.where(incl, Gm[g][2 * L:, 2 * L:], 0.0)], axis=1).astype(BF16) for g in pairs]
        AV = mm2(Aak, Vs)
        yield
        pw = [Np]
        for s in range(nsteps - 1):
            pw.append([t.astype(BF16) for t in mm2(pw[-1], pw[-1])])
            yield
        plc = [jnp.transpose(jnp.broadcast_to(sl(PL, g), (PAIR, PAIR))) for g in pairs]
        res[q] = (As, Rs, Vs, Mr, AV, pw, Bh, Kh, v, plc)
        yield

    state = [[h_scr[g] for g in pairs]]

    def back(q):
        As, Rs, Vs, Mr, AV, pw, Bh, Kh, v, plc = res[q]
        H = state[0]
        AH = mm2([jnp.concatenate([As[g], Rs[g]], axis=0) for g in pairs], H)
        yield
        Us = [AH[g][:2 * L] + AV[g] for g in pairs]
        for s in range(nsteps):
            Us = [Us[g] + _mm(pw[s][g], Us[g]) for g in pairs]
            yield
        Ys = [AH[g][2 * L:] + _mm(Mr[g], jnp.concatenate([Us[g], Vs[g]], axis=0)) for g in pairs]
        upd = [_mm(jnp.concatenate([sl(Bh, g), sl(Kh, g)], axis=0),
                   jnp.concatenate([Us[g][:L] + Us[g][L:], sl(v, g)], axis=0), _TN) for g in pairs]
        yield
        rw = rows(q)
        for g in pairs:
            y_ref[rw, g * PAIR:(g + 1) * PAIR] = (Ys[g][:L] + Ys[g][L:]).astype(y_ref.dtype)
        state[0] = [H[g] * plc[g] + jnp.where(bd, upd[g], 0.0) for g in pairs]
        yield

    def run(primary, others):
        while next(primary, _DONE) is not _DONE:
            for o in others:
                next(o, _DONE)

    fronts = [front(q) for q in range(nsub)]
    for _ in range(_FRONT_LEAD):
        next(fronts[0])
    for q in range(nsub):
        nxt = fronts[q + 1:q + 2]
        run(fronts[q], nxt)
        run(back(q), nxt)
    for g in pairs:
        h_scr[g] = state[0][g]

    @pl.when(c == nc - 1)
    def _():
        hT_ref[...] = h_scr[...]


def _wkv_scan(s1, kk, w2d, a2d, w0, a0, k_a, h0, B, T, D):
    G = min(SCAN_PAIRS, D // PAIR)
    W = G * PAIR
    nsub = SCAN_CHUNKS
    L = nsub * CHUNK
    nc = T // L
    npg = D // W
    N1 = s1.shape[1]
    s1 = s1.reshape(B, T, N1)
    kk = kk.reshape(B, T, D)
    chunk = lambda d, c: c + d * (nc - 1 - 2 * c)
    tok = lambda off: pl.BlockSpec((None, L, W), lambda d, b, p, c: (b, chunk(d, c), off + p))
    dirw = lambda rows: pl.BlockSpec((None, rows, W), lambda d, b, p, c: (d, 0, p))
    state = pl.BlockSpec((None, None, G, PAIR, PAIR), lambda d, b, p, c: (d, b, p, 0, 0))
    out_tok = pl.BlockSpec((None, None, L, W), lambda d, b, p, c: (d, b, chunk(d, c), p))
    y, a, hT = pl.pallas_call(
        functools.partial(_scan_kernel, G=G, nsub=nsub),
        grid=(2, B, npg, nc),
        in_specs=[
            tok(0), tok(npg), tok(2 * npg),
            pl.BlockSpec((None, L, W), lambda d, b, p, c: (b, chunk(d, c), p)),
            pl.BlockSpec((None, L, 2 * LORA_BLK), lambda d, b, p, c: (b, chunk(d, c), 3 * D // (2 * LORA_BLK))),
            dirw(LORA_BLK), dirw(LORA_BLK), dirw(1), dirw(1),
            pl.BlockSpec((1, W), lambda d, b, p, c: (0, p)),
            state,
        ],
        out_specs=[out_tok, out_tok, state],
        out_shape=[jax.ShapeDtypeStruct((2, B, T, D), BF16),
                   jax.ShapeDtypeStruct((2, B, T, D), BF16),
                   jax.ShapeDtypeStruct((2, B, D // PAIR, PAIR, PAIR), F32)],
        scratch_shapes=[pltpu.VMEM((G, PAIR, PAIR), F32)],
        compiler_params=_cparams(("arbitrary", "arbitrary", "arbitrary", "arbitrary")),
        name="wkv_scan",
    )(s1, s1, s1, kk, s1, w2d, a2d, w0, a0, k_a, h0)
    return y.reshape(2, B * T, D), a.reshape(2, B * T, D), hT


def _rwkv_out_kernel(y0_ref, y1_ref, r_ref, k_ref, v_ref, a0_ref, a1_ref, sg_ref, g2_ref,
                     lnw_ref, lnb_ref, rk_ref, ka_ref, ones_ref, w_ref, x_ref, gate_ref,
                     o_ref, *, D):
    ones = ones_ref[...]
    inv = 1.0 / RWKV_HEAD
    sg = sg_ref[:, :LORA_BLK]
    acc = None
    for s in range(D // MXU_DIM):
        cs = slice(s * MXU_DIM, (s + 1) * MXU_DIM)
        f = lambda ref: ref[:, cs].astype(F32)
        wkv = f(y0_ref) + f(y1_ref)
        cen = wkv - _dot_hilo(wkv, ones) * inv
        var = _dot((cen * cen).astype(BF16), ones) * inv
        yn = cen * lax.rsqrt(var + GN_EPS) * lnw_ref[:, cs] + lnb_ref[:, cs]
        ks = f(k_ref) * (2.0 + (f(a0_ref) + f(a1_ref) - 2.0) * ka_ref[:, cs])
        bonus = _dot((f(r_ref) * ks * rk_ref[:, cs]).astype(BF16), ones) * f(v_ref)
        gate = _dot(sg, g2_ref[:, cs])
        part = _dot(((yn + bonus) * gate).astype(BF16), w_ref[cs, :])
        acc = part if acc is None else acc + part
    o_ref[...] = x_ref[...] + gate_ref[...] * acc


def _rwkv_out(y, a, s1, g2, x, mod, row_fn, ln_w, ln_b, r_k, k_a, w_o, tm):
    M, D = x.shape
    big = lambda col: pl.BlockSpec((tm, D), lambda i: (i, col))
    dirblk = lambda d: pl.BlockSpec((None, tm, D), lambda i: (d, i, 0))
    row = pl.BlockSpec((1, D), lambda i: (0, 0))
    return pl.pallas_call(
        functools.partial(_rwkv_out_kernel, D=D),
        grid=(M // tm,),
        in_specs=[
            dirblk(0), dirblk(1),
            big(0), big(1), big(2),
            dirblk(0), dirblk(1),
            pl.BlockSpec((tm, 2 * LORA_BLK), lambda i: (i, 3 * D // (2 * LORA_BLK) + 1)),
            pl.BlockSpec((LORA_BLK, D), lambda i: (0, 0)),
            row, row, row, row,
            pl.BlockSpec((MXU_DIM, MXU_DIM), lambda i: (0, 0)),
            pl.BlockSpec((D, D), lambda i: (0, 0)),
            pl.BlockSpec((tm, D), lambda i: (i, 0)),
            pl.BlockSpec((None, 1, D), lambda i: (row_fn(i), 0, 2)),
        ],
        out_specs=pl.BlockSpec((tm, D), lambda i: (i, 0)),
        out_shape=jax.ShapeDtypeStruct((M, D), F32),
        compiler_params=_cparams(("arbitrary",)),
        name="rwkv_out",
    )(y, y, s1, s1, s1, a, a, s1, g2, ln_w, ln_b, r_k, k_a, _group_ones(MXU_DIM, RWKV_HEAD), w_o, x, mod)


def _ffn_kernel(x_ref, g_ref, sh_ref, sc_ref, gate_ref, w1_ref, w3_ref, w2_ref, o_ref, a_scr, *, tf):
    f = pl.program_id(1)
    nf = pl.num_programs(1)

    def hidden_block(a):
        acts = []
        for s in range(tf // MXU_DIM):
            cs = slice(s * MXU_DIM, (s + 1) * MXU_DIM)
            h1 = _dot(a, w1_ref[:, cs])
            h3 = _dot(a, w3_ref[:, cs])
            acts.append((_silu(h1) * h3).astype(BF16))
        return _dot(jnp.concatenate(acts, axis=1), w2_ref[...])

    @pl.when(f == 0)
    def _():
        a = _modulate(x_ref[...], g_ref[...], sh_ref[...], sc_ref[...]).astype(BF16)
        a_scr[...] = a
        o_ref[...] = hidden_block(a)

    @pl.when(f > 0)
    def _():
        o_ref[...] += hidden_block(a_scr[...])

    @pl.when(f == nf - 1)
    def _():
        o_ref[...] = x_ref[...] + gate_ref[...] * o_ref[...]


def _ffn(x, mod, row_fn, gain, w1b, w3b, w2, tm, tf):
    M, D = x.shape
    nf = w1b.shape[1] // tf
    modspec = lambda ch: pl.BlockSpec((None, 1, D), lambda i, f: (row_fn(i), 0, ch))
    return pl.pallas_call(
        functools.partial(_ffn_kernel, tf=tf),
        grid=(M // tm, nf),
        in_specs=[
            pl.BlockSpec((tm, D), lambda i, f: (i, 0), pipeline_mode=pl.Buffered(1)),
            pl.BlockSpec((1, D), lambda i, f: (0, 0)),
            modspec(3), modspec(4), modspec(5),
            pl.BlockSpec((D, tf), lambda i, f: (0, f)),
            pl.BlockSpec((D, tf), lambda i, f: (0, f)),
            pl.BlockSpec((tf, D), lambda i, f: (f, 0)),
        ],
        out_specs=pl.BlockSpec((tm, D), lambda i, f: (i, 0)),
        out_shape=jax.ShapeDtypeStruct((M, D), F32),
        scratch_shapes=[pltpu.VMEM((tm, D), BF16)],
        compiler_params=_cparams(("arbitrary", "arbitrary")),
        name="ffn",
    )(x, gain, mod, mod, mod, w1b, w3b, w2)


def _qkv_kernel(x_ref, g_ref, sh_ref, sc_ref, w_ref, hg_ref, cos_ref, sin_ref, ones_ref, o_ref, a_scr,
                *, nrope, nnorm, tn):
    n = pl.program_id(1)

    def normed_block(a):
        acc = _dot(a, w_ref[...])
        hg = hg_ref[...]
        ones = ones_ref[...]
        for h in range(tn // MXU_DIM):
            blk = acc[:, h * MXU_DIM:(h + 1) * MXU_DIM]
            ms = _dot((blk * blk).astype(BF16), ones) * (1.0 / NA_HEAD)
            blk = blk * lax.rsqrt(ms + RMS_EPS)
            for s in range(MXU_DIM // NA_HEAD):
                t = blk[:, s * NA_HEAD:(s + 1) * NA_HEAD] * hg
                if nrope:
                    t = t * cos_ref[...] + pltpu.roll(t, NA_HEAD // 2, axis=1) * sin_ref[...]
                c0 = h * MXU_DIM + s * NA_HEAD
                o_ref[:, c0:c0 + NA_HEAD] = t.astype(o_ref.dtype)

    @pl.when(n == 0)
    def _():
        a = _modulate(x_ref[...], g_ref[...], sh_ref[...], sc_ref[...]).astype(BF16)
        a_scr[...] = a
        normed_block(a)

    @pl.when((n > 0) & (n < nnorm))
    def _():
        normed_block(a_scr[...])

    @pl.when(n >= nnorm)
    def _():
        o_ref[...] = _dot(a_scr[...], w_ref[...]).astype(o_ref.dtype)


def _na_qkv(x, mod, row_fn, gain, w, head_gain, cos, sin, *, nrope, nnorm, tm, tn):
    assert nrope in (0, nnorm)
    M, D = x.shape
    N = w.shape[1]
    tpb = cos.shape[0] // tm
    hrow = lambda n: jnp.minimum(n, head_gain.shape[0] - 1)
    return pl.pallas_call(
        functools.partial(_qkv_kernel, nrope=nrope, nnorm=nnorm, tn=tn),
        grid=(M // tm, N // tn),
        in_specs=[
            pl.BlockSpec((tm, D), lambda i, n: (i, 0)),
            pl.BlockSpec((1, D), lambda i, n: (0, 0)),
            pl.BlockSpec((None, 1, D), lambda i, n: (row_fn(i), 0, 0)),
            pl.BlockSpec((None, 1, D), lambda i, n: (row_fn(i), 0, 1)),
            pl.BlockSpec((D, tn), lambda i, n: (0, n)),
            pl.BlockSpec((None, 1, NA_HEAD), lambda i, n: (hrow(n), 0, 0)),
            pl.BlockSpec((tm, NA_HEAD), lambda i, n: (i % tpb, 0)),
            pl.BlockSpec((tm, NA_HEAD), lambda i, n: (i % tpb, 0)),
            pl.BlockSpec((MXU_DIM, MXU_DIM), lambda i, n: (0, 0)),
        ],
        out_specs=pl.BlockSpec((tm, tn), lambda i, n: (i, n)),
        out_shape=jax.ShapeDtypeStruct((M, N), BF16),
        scratch_shapes=[pltpu.VMEM((tm, D), BF16)],
        compiler_params=_cparams(("arbitrary", "arbitrary")),
        name="na_qkv",
    )(x, gain, mod, mod, w, head_gain, cos, sin, _group_ones(MXU_DIM, NA_HEAD))


def _na_kernel(q_ref, k_ref, v_ref, kc_ref, vc_ref, bias_ref, o_ref, *, rows, rb):
    j0 = pl.program_id(2) * rb
    win_h = min(WIN_H, rows)
    nk = win_h * GRID_W
    qrows = range(rb)
    rs = lambda t, j: t[j * GRID_W:(j + 1) * GRID_W]

    qa = q_ref[...]
    s_ctx = lax.dot_general(qa, kc_ref[...], _NT, preferred_element_type=F32)

    def row(j):
        r0 = jnp.clip(j0 + j - WIN_H // 2, 0, rows - win_h)
        off = pl.multiple_of(r0 * GRID_W, GRID_W)
        s_lat = (lax.dot_general(rs(qa, j), k_ref[pl.ds(off, nk), :], _NT, preferred_element_type=F32)
                 + bias_ref[r0 - (j0 + j) + WIN_H - 1])
        yield
        sc = rs(s_ctx, j)
        m = jnp.maximum(jnp.max(s_lat, axis=-1, keepdims=True), jnp.max(sc, axis=-1, keepdims=True))
        yield
        p_lat = jnp.exp(s_lat - m)
        p_ctx = jnp.exp(sc - m)
        yield
        den = jnp.sum(p_lat, axis=-1, keepdims=True) + jnp.sum(p_ctx, axis=-1, keepdims=True)
        o = _dot(p_lat.astype(BF16), v_ref[pl.ds(off, nk), :]) + _dot(p_ctx.astype(BF16), vc_ref[...])
        yield
        o_ref[j * GRID_W:(j + 1) * GRID_W, :] = (o / den).astype(o_ref.dtype)
        yield

    gens = [row(j) for j in qrows]
    done = [False] * rb
    tick = 0
    while not all(done):
        for j in qrows:
            if not done[j] and tick >= j * _ROW_STAGGER:
                done[j] = next(gens[j], _DONE) is _DONE
        tick += 1


def _na_attention(qkv, kvc, bias, B, T, D, rb):
    H = D // NA_HEAD
    rows = T // GRID_W
    C = kvc.shape[1]
    nbias = bias.shape[1]
    return pl.pallas_call(
        functools.partial(_na_kernel, rows=rows, rb=rb),
        grid=(B, H, rows // rb),
        in_specs=[
            pl.BlockSpec((None, rb * GRID_W, NA_HEAD), lambda b, h, j: (b, j, h)),
            pl.BlockSpec((None, T, NA_HEAD), lambda b, h, j: (b, 0, H + h)),
            pl.BlockSpec((None, T, NA_HEAD), lambda b, h, j: (b, 0, 2 * H + h)),
            pl.BlockSpec((None, C, NA_HEAD), lambda b, h, j: (b, 0, h)),
            pl.BlockSpec((None, C, NA_HEAD), lambda b, h, j: (b, 0, H + h)),
            pl.BlockSpec((None, nbias, GRID_W, bias.shape[3]), lambda b, h, j: (h, 0, 0, 0)),
        ],
        out_specs=pl.BlockSpec((None, rb * GRID_W, NA_HEAD), lambda b, h, j: (b, j, h)),
        out_shape=jax.ShapeDtypeStruct((B, T, D), BF16),
        compiler_params=_cparams(("arbitrary", "arbitrary", "arbitrary")),
        name="na_attention",
    )(qkv, qkv, qkv, kvc, kvc, bias)


def _proj_res_kernel(a_ref, w_ref, x_ref, gate_ref, o_ref):
    o_ref[...] = x_ref[...] + gate_ref[...] * _dot(a_ref[...], w_ref[...])


def _proj_residual(a, w, x, mod, row_fn, chunk, tm):
    M, K = a.shape
    N = w.shape[1]
    return pl.pallas_call(
        _proj_res_kernel,
        grid=(M // tm,),
        in_specs=[
            pl.BlockSpec((tm, K), lambda i: (i, 0)),
            pl.BlockSpec((K, N), lambda i: (0, 0)),
            pl.BlockSpec((tm, N), lambda i: (i, 0)),
            pl.BlockSpec((None, 1, N), lambda i: (row_fn(i), 0, chunk)),
        ],
        out_specs=pl.BlockSpec((tm, N), lambda i: (i, 0)),
        out_shape=jax.ShapeDtypeStruct((M, N), F32),
        compiler_params=_cparams(("arbitrary",)),
        name="proj_residual",
    )(a, w, x, mod)


_QUART = NA_HEAD // 4
_ROPE_PERM = np.concatenate([np.arange(0, _QUART), np.arange(2 * _QUART, 3 * _QUART),
                             np.arange(_QUART, 2 * _QUART), np.arange(3 * _QUART, NA_HEAD)])


def _rope_layout(w_qk):
    K, N = w_qk.shape
    return w_qk.reshape(K, N // NA_HEAD, NA_HEAD)[:, :, _ROPE_PERM].reshape(K, N)


def _rope_tables(T):
    t = np.arange(T)
    half = NA_HEAD // 2
    freqs = ROPE_THETA ** (-np.arange(0, half, 2, dtype=np.float64) / half)
    ang_r = (t // GRID_W).astype(np.float64)[:, None] * freqs
    ang_c = (t % GRID_W).astype(np.float64)[:, None] * freqs
    cos = np.concatenate([np.cos(ang_r), np.cos(ang_c), np.cos(ang_r), np.cos(ang_c)], axis=1)
    sin = np.concatenate([-np.sin(ang_r), -np.sin(ang_c), np.sin(ang_r), np.sin(ang_c)], axis=1)
    return jnp.asarray(cos, F32), jnp.asarray(sin, F32)


def _na_bias(rpb, rows):
    win_h = min(WIN_H, rows)
    qc = np.arange(GRID_W)[:, None]
    kc = np.arange(GRID_W)[None, :]
    wstart = np.clip(qc - WIN_W // 2, 0, GRID_W - WIN_W)
    valid = (kc >= wstart) & (kc < wstart + WIN_W)
    rel = np.clip(kc - qc, -(WIN_W - 1), WIN_W - 1) + WIN_W - 1
    g = rpb[:, :, rel]
    g = jnp.where(valid[None, None], g, NEG_INF)
    tabs = []
    for i0 in range(2 * WIN_H - win_h):
        blk = g[:, i0:i0 + win_h]
        tabs.append(jnp.transpose(blk, (0, 2, 1, 3)).reshape(g.shape[0], GRID_W, win_h * GRID_W))
    return jnp.stack(tabs, axis=1)


def _pad_cols(w, n):
    return jnp.pad(w, ((0, 0), (0, n - w.shape[1])))


def _pad_rows(w, n):
    return jnp.pad(w, ((0, n - w.shape[0]), (0, 0)))


def kernel(x, c, ctx, c_ctx, ada_w, ada_b, norm1, norm2, rwkv_mu, rwkv_w_r, rwkv_w_k, rwkv_w_v, rwkv_w_o, rwkv_w0, rwkv_w1, rwkv_w2, rwkv_a0, rwkv_a1, rwkv_a2, rwkv_g1, rwkv_g2, rwkv_k_k, rwkv_k_a, rwkv_r_k, rwkv_ln_w, rwkv_ln_b, na_w_qkv, na_w_o, na_q_gain, na_k_gain, na_rpb, ffn_w1, ffn_w3, ffn_w2):
    B, T, D = x.shape
    C = ctx.shape[1]
    rows = T // GRID_W
    depth = ada_w.shape[0]
    n_mix = 2
    TM = 512
    TMO = 256
    TF = 512 if ffn_w1.shape[-1] % 512 == 0 else 256
    TMF = min(1024, T)

    cvec = jnp.concatenate([c, c_ctx[None, :], jnp.zeros((8 - B - 1, D), F32)], axis=0)
    mods = _ada(cvec, ada_w, ada_b)
    lat_row = lambda tm: (lambda i: i // (T // tm))
    ctx_row = lambda i: B

    xl = x.reshape(B * T, D)
    xc = ctx.reshape(B * C, D)

    for i in range(depth):
        last = i == depth - 1
        j = i // n_mix
        mod = mods[i].reshape(8, 1, 6 * D)
        g1 = norm1[i].reshape(1, D)
        g2 = norm2[i].reshape(1, D)
        w1 = ffn_w1[i].astype(BF16)
        w3 = ffn_w3[i].astype(BF16)
        w2 = ffn_w2[i].astype(BF16)
        if i % n_mix == 0:
            lw = rwkv_w1.shape[-1]
            la = rwkv_a1.shape[-1]
            lg = rwkv_g1.shape[-1]
            wcat = jnp.concatenate([
                rwkv_w_r[j], rwkv_w_k[j], rwkv_w_v[j],
                _pad_cols(jnp.concatenate([rwkv_w1[j, 0], rwkv_w1[j, 1]], axis=1), LORA_BLK),
                _pad_cols(jnp.concatenate([rwkv_a1[j, 0], rwkv_a1[j, 1]], axis=1), LORA_BLK),
                _pad_cols(rwkv_g1[j], 2 * LORA_BLK)], axis=1).astype(BF16)
            z = lambda n: jnp.zeros((n, D), F32)
            w2d = jnp.stack([_pad_rows(rwkv_w2[j, 0], LORA_BLK),
                             _pad_rows(jnp.concatenate([z(lw), rwkv_w2[j, 1]], axis=0), LORA_BLK)]).astype(BF16)
            a2d = jnp.stack([_pad_rows(rwkv_a2[j, 0], LORA_BLK),
                             _pad_rows(jnp.concatenate([z(la), rwkv_a2[j, 1]], axis=0), LORA_BLK)]).astype(BF16)
            wg2 = _pad_rows(rwkv_g2[j], LORA_BLK).astype(BF16)
            w0 = rwkv_w0[j].reshape(2, 1, D)
            a0 = rwkv_a0[j].reshape(2, 1, D)
            k_k = rwkv_k_k[j].reshape(1, D)
            k_a = rwkv_k_a[j].reshape(1, D)
            r_k = rwkv_r_k[j].reshape(1, D)
            ln_w = rwkv_ln_w[j].reshape(1, D)
            ln_b = rwkv_ln_b[j].reshape(1, D)
            w_o = rwkv_w_o[j].astype(BF16)

            s1c, kkc = _rwkv_stage1(xc, mod, ctx_row, g1, rwkv_mu[j], k_k, wcat, latent=False, rows=rows, tm=C)
            h0 = jnp.zeros((2, B, D // PAIR, PAIR, PAIR), F32)
            yc, ac, hc = _wkv_scan(s1c, kkc, w2d, a2d, w0, a0, k_a, h0, B, C, D)
            s1l, kkl = _rwkv_stage1(xl, mod, lat_row(TM), g1, rwkv_mu[j], k_k, wcat, latent=True, rows=rows, tm=TM)
            yl, al, _ = _wkv_scan(s1l, kkl, w2d, a2d, w0, a0, k_a, hc, B, T, D)
            xl = _rwkv_out(yl, al, s1l, wg2, xl, mod, lat_row(TMO), ln_w, ln_b, r_k, k_a, w_o, TMO)
            if not last:
                xc = _rwkv_out(yc, ac, s1c, wg2, xc, mod, ctx_row, ln_w, ln_b, r_k, k_a, w_o, min(TMO, C))
        else:
            H = D // NA_HEAD
            cos, sin = _rope_tables(T)
            scale = NA_HEAD ** -0.5
            hg = jnp.stack([(na_q_gain[j] * scale)[_ROPE_PERM], na_k_gain[j][_ROPE_PERM]]).reshape(2, 1, NA_HEAD)
            tn = min(1024, D)
            nq = D // tn
            wqkv = jnp.concatenate([_rope_layout(na_w_qkv[j][:, :2 * D]), na_w_qkv[j][:, 2 * D:]],
                                   axis=1).astype(BF16)
            hg_blocks = jnp.concatenate([jnp.broadcast_to(hg[0], (nq, 1, NA_HEAD)),
                                         jnp.broadcast_to(hg[1], (nq, 1, NA_HEAD))], axis=0)
            qkv = _na_qkv(xl, mod, lat_row(TM), g1, wqkv, hg_blocks, cos, sin, nrope=2 * nq, nnorm=2 * nq,
                          tm=TM, tn=tn)
            kg_blocks = jnp.broadcast_to(hg[1], (nq, 1, NA_HEAD))
            kvc = _na_qkv(xc, mod, ctx_row, g1, wqkv[:, D:], kg_blocks, cos[:C], sin[:C], nrope=0, nnorm=nq,
                          tm=C, tn=tn)
            bias = _na_bias(na_rpb[j], rows)
            o = _na_attention(qkv.reshape(B, T, 3 * D), kvc.reshape(B, C, 2 * D), bias, B, T, D, min(32, rows))
            xl = _proj_residual(o.reshape(B * T, D), na_w_o[j].astype(BF16), xl, mod,
                                lat_row(TM), 2, TM)
            if not last:
                raise NotImplementedError("attention layer with context output")
        xl = _ffn(xl, mod, lat_row(TMF), g2, w1, w3, w2, TMF, TF)
        if not last:
            xc = _ffn(xc, mod, ctx_row, g2, w1, w3, w2, C, TF)
    return xl.reshape(B, T, D)
```

```python
import functools
import math

import numpy as np
import jax
import jax.numpy as jnp
from jax import lax
from jax.experimental import pallas as pl
from jax.experimental.pallas import tpu as pltpu

F32 = jnp.float32
BF16 = jnp.bfloat16

GRID_W = 64
RMS_EPS = 1e-6
RWKV_HEAD = 64
GN_EPS = 64e-5
NA_HEAD = 128
WIN_H = 8
WIN_W = 16
ROPE_THETA = 10000.0
NEG_INF = -1e30

LANE = 128
MXU_DIM = 256
VMEM_LIMIT = 56 * 1024 * 1024

CHUNK = 64
PAIR = 2 * RWKV_HEAD
SCAN_PAIRS = 16
SCAN_CHUNKS = 4
LORA_BLK = 256
TN1 = 4 * LORA_BLK


def _cparams(sem):
    return pltpu.CompilerParams(dimension_semantics=sem, vmem_limit_bytes=VMEM_LIMIT)


def _dot(a, b):
    return jnp.dot(a, b, preferred_element_type=F32)


def _split2(x):
    hi = x.astype(BF16)
    lo = (x - hi.astype(F32)).astype(BF16)
    return hi, lo


def _dot_hilo(a, b_bf16):
    hi, lo = _split2(a)
    return _dot(hi, b_bf16) + _dot(lo, b_bf16)


def _rms_scale(x):
    return lax.rsqrt(jnp.mean(x * x, axis=-1, keepdims=True) + RMS_EPS)


def _modulate(x, gain, shift, scale):
    return (x * _rms_scale(x) * gain) * (1.0 + scale) + shift


def _sigmoid(x):
    return 1.0 / (1.0 + jnp.exp(-x))


def _silu(x):
    return x * _sigmoid(x)


def _group_ones(n, group):
    idx = np.arange(n) // group
    return jnp.asarray((idx[:, None] == idx[None, :]).astype(np.float32), dtype=BF16)


def _ada_kernel(c_ref, w_ref, b_ref, o_ref):
    s = _silu(c_ref[...]).astype(BF16)
    o_ref[...] = _dot(s, w_ref[...].astype(BF16)) + b_ref[...]


def _ada(cvec, ada_w, ada_b):
    depth, D, N = ada_w.shape
    R = cvec.shape[0]
    tn = 1024
    return pl.pallas_call(
        _ada_kernel,
        grid=(depth, N // tn),
        in_specs=[
            pl.BlockSpec((R, D), lambda l, n: (0, 0)),
            pl.BlockSpec((None, D, tn), lambda l, n: (l, 0, n)),
            pl.BlockSpec((None, 1, tn), lambda l, n: (l, 0, n)),
        ],
        out_specs=pl.BlockSpec((None, R, tn), lambda l, n: (l, 0, n)),
        out_shape=jax.ShapeDtypeStruct((depth, R, N), F32),
        compiler_params=_cparams(("arbitrary", "arbitrary")),
        name="ada",
    )(cvec, ada_w, ada_b.reshape(depth, 1, N))


def _s1_kernel(xp_ref, x_ref, xn_ref, g_ref, sh_ref, sc_ref, mu_ref, kkw_ref, ones_ref, w_ref,
               o_ref, kk_ref, a_scr, h_scr, xx_scr, *, latent, rows, tm, D, nb):
    i = pl.program_id(0)
    n = pl.program_id(1)
    R = GRID_W
    pad = h_scr.shape[0] - tm

    @pl.when(n == 0)
    def _():
        g = g_ref[...]
        sh = sh_ref[...]
        sc = sc_ref[...]
        lo = pad // 2
        h_scr[lo:lo + tm, :] = _modulate(x_ref[...], g, sh, sc)
        if latent:
            h_scr[0:lo, :] = _modulate(xp_ref[...], g, sh, sc)
            h_scr[lo + tm:, :] = _modulate(xn_ref[...], g, sh, sc)
        else:
            h_scr[0:lo, :] = jnp.zeros((lo, D), F32)
            h_scr[lo + tm:, :] = jnp.zeros((pad - lo, D), F32)
        for rr in range(tm // R):
            base = lo + rr * R
            hc = h_scr[base:base + R, :]
            if latent:
                q = D // 4
                col = lax.broadcasted_iota(jnp.int32, (R, q), 0)
                grow = (i * (tm // R) + rr) & (rows - 1)
                hs = jnp.concatenate([
                    jnp.where(col > 0, h_scr[base - 1:base - 1 + R, 0:q], 0.0),
                    jnp.where(col < R - 1, h_scr[base + 1:base + 1 + R, q:2 * q], 0.0),
                    jnp.where(grow > 0, h_scr[base - R:base, 2 * q:3 * q], 0.0),
                    jnp.where(grow < rows - 1, h_scr[base + R:base + 2 * R, 3 * q:], 0.0)], axis=1)
            else:
                half = D // 2
                hs = jnp.concatenate([h_scr[base - 1:base - 1 + R, :half],
                                      h_scr[base + 1:base + 1 + R, half:]], axis=1)
            xx = hs - hc
            xx_scr[rr * R:(rr + 1) * R, :] = xx
            a_scr[0, rr * R:(rr + 1) * R, :] = (hc + xx * mu_ref[0:1, :]).astype(BF16)

    def mix(j):
        lo = pad // 2
        for rr in range(tm // R):
            rs_ = slice(rr * R, (rr + 1) * R)
            a_scr[j, rs_, :] = (h_scr[lo + rr * R:lo + (rr + 1) * R, :] + xx_scr[rs_, :] * mu_ref[j:j + 1, :]).astype(BF16)

    def r_step():
        o_ref[...] = _dot(a_scr[0], w_ref[...]).astype(o_ref.dtype)

    def k_step():
        acc = _dot(a_scr[2], w_ref[...])
        o_ref[...] = acc.astype(o_ref.dtype)
        kkf = acc * kkw_ref[...]
        sq = kkf * kkf
        ones = ones_ref[...]
        parts = []
        for s in range(TN1 // MXU_DIM):
            parts.append(_dot_hilo(sq[:, s * MXU_DIM:(s + 1) * MXU_DIM], ones))
        ss = jnp.concatenate(parts, axis=1)
        kk_ref[...] = (kkf / jnp.maximum(jnp.sqrt(ss), 1e-12)).astype(kk_ref.dtype)

    def v_step():
        o_ref[...] = _dot(a_scr[3], w_ref[...]).astype(o_ref.dtype)

    def lora_step():
        c1, c2, c3 = LORA_BLK, 2 * LORA_BLK, 3 * LORA_BLK
        o_ref[:, :c1] = jnp.tanh(_dot(a_scr[1], w_ref[:, :c1])).astype(o_ref.dtype)
        o_ref[:, c1:c2] = _dot(a_scr[4], w_ref[:, c1:c2]).astype(o_ref.dtype)
        o_ref[:, c2:c3] = _sigmoid(_dot(a_scr[5], w_ref[:, c2:c3])).astype(o_ref.dtype)
        o_ref[:, c3:] = jnp.zeros((o_ref.shape[0], TN1 - c3), o_ref.dtype)

    steps = [r_step] * nb + [k_step] * nb + [v_step] * nb + [lora_step]
    first_use = {2: nb, 3: 2 * nb, 1: 3 * nb, 4: 3 * nb, 5: 3 * nb}
    pending = sorted(first_use, key=lambda j: first_use[j])
    for t, step in enumerate(steps):
        side = [j for j in pending if first_use[j] == t + 1] or pending[:1]
        pending = [j for j in pending if j not in side]

        @pl.when(n == t)
        def _(step=step, side=side):
            step()
            for j in side:
                mix(j)


def _rwkv_stage1(x, mod, row_fn, gain, mu, k_k, wcat, *, latent, rows, tm):
    M, D = x.shape
    nb = D // TN1
    N1 = wcat.shape[1]
    nsteps = N1 // TN1
    hb = tm // GRID_W if latent else 1
    hrows = GRID_W if latent else 8
    nhb = M // hrows
    kcol = lambda n: jnp.clip(n - nb, 0, nb - 1)
    kern = functools.partial(_s1_kernel, latent=latent, rows=rows, tm=tm, D=D, nb=nb)
    return pl.pallas_call(
        kern,
        grid=(M // tm, nsteps),
        in_specs=[
            pl.BlockSpec((hrows, D), lambda i, n: (jnp.maximum(i * hb - 1, 0), 0)),
            pl.BlockSpec((tm, D), lambda i, n: (i, 0)),
            pl.BlockSpec((hrows, D), lambda i, n: (jnp.minimum(i * hb + hb, nhb - 1), 0)),
            pl.BlockSpec((1, D), lambda i, n: (0, 0)),
            pl.BlockSpec((None, 1, D), lambda i, n: (row_fn(i), 0, 0)),
            pl.BlockSpec((None, 1, D), lambda i, n: (row_fn(i), 0, 1)),
            pl.BlockSpec((6, D), lambda i, n: (0, 0)),
            pl.BlockSpec((1, TN1), lambda i, n: (0, kcol(n))),
            pl.BlockSpec((MXU_DIM, MXU_DIM), lambda i, n: (0, 0)),
            pl.BlockSpec((D, TN1), lambda i, n: (0, n)),
        ],
        out_specs=[
            pl.BlockSpec((tm, TN1), lambda i, n: (i, n)),
            pl.BlockSpec((tm, TN1), lambda i, n: (i, kcol(n))),
        ],
        out_shape=[jax.ShapeDtypeStruct((M, N1), BF16), jax.ShapeDtypeStruct((M, D), BF16)],
        scratch_shapes=[pltpu.VMEM((6, tm, D), BF16),
                        pltpu.VMEM((tm + 2 * hrows, D), F32),
                        pltpu.VMEM((tm, D), F32)],
        compiler_params=_cparams(("arbitrary", "arbitrary")),
        name="rwkv_stage1",
    )(x, x, x, gain, mod, mod, mu, k_k, _group_ones(MXU_DIM, RWKV_HEAD), wcat)


_NT = (((1,), (1,)), ((), ()))
_TN = (((0,), (0,)), ((), ()))


def _mm(a, b, dims=None):
    a = a.astype(BF16)
    b = b.astype(BF16)
    if dims is None:
        return jnp.dot(a, b, preferred_element_type=F32)
    return lax.dot_general(a, b, dims, preferred_element_type=F32)


_DONE = object()
_FRONT_LEAD = 1
_ROW_STAGGER = 1


def _scan_kernel(r_ref, k_ref, v_ref, kk_ref, lo_ref, w2_ref, a2_ref, w0_ref, a0_ref, ka_ref, h0_ref,
                 y_ref, a_ref, hT_ref, h_scr, *, G, nsub):
    d = pl.program_id(0)
    c = pl.program_id(3)
    nc = pl.num_programs(3)
    L = CHUNK
    pairs = range(G)
    nsteps = int(math.log2(L))

    @pl.when(c == 0)
    def _():
        h_scr[...] = h0_ref[...]

    sgn = 1 - 2 * d
    lane = lax.broadcasted_iota(jnp.int32, (L, PAIR), 1)
    m0 = lane < RWKV_HEAD
    ti = lax.broadcasted_iota(jnp.int32, (L, L), 0)
    si = lax.broadcasted_iota(jnp.int32, (L, L), 1)
    tri = jnp.where((si - ti) * sgn <= 0, 1.0, 0.0).astype(BF16)
    i2 = lax.broadcasted_iota(jnp.int32, (2 * L, 2 * L), 0)
    j2 = lax.broadcasted_iota(jnp.int32, (2 * L, 2 * L), 1)
    dif = jnp.where((i2 // L) == (j2 // L), ((j2 % L) - (i2 % L)) * sgn, 1)
    strict = dif < 0
    incl = dif <= 0
    r2 = lax.broadcasted_iota(jnp.int32, (PAIR, PAIR), 0)
    c2 = lax.broadcasted_iota(jnp.int32, (PAIR, PAIR), 1)
    bd = (r2 // RWKV_HEAD) == (c2 // RWKV_HEAD)

    def sm(x):
        return jnp.concatenate([jnp.where(m0, x, 0.0), jnp.where(m0, 0.0, x)], axis=0)

    def sl(x, g):
        return x[:, g * PAIR:(g + 1) * PAIR]

    def mm2(lhs, rhs):
        out = []
        for g in range(0, len(lhs), 2):
            ra, rb = rhs[g].astype(BF16), rhs[g + 1].astype(BF16)
            z = jnp.zeros_like(ra)
            bd2 = jnp.concatenate([jnp.concatenate([ra, z], axis=1), jnp.concatenate([z, rb], axis=1)], axis=0)
            both = _mm(jnp.concatenate([lhs[g].astype(BF16), lhs[g + 1].astype(BF16)], axis=1), bd2)
            out += [both[:, :PAIR], both[:, PAIR:]]
        return out

    def rows(q):
        return pl.ds(pl.multiple_of((q + d * (nsub - 1 - 2 * q)) * L, L), L)

    res = [None] * nsub

    def front(q):
        rw = rows(q)
        z = _dot(lo_ref[rw, :LORA_BLK], w2_ref[...]) + w0_ref[...]
        za = _dot(lo_ref[rw, LORA_BLK:], a2_ref[...]) + a0_ref[...]
        yield
        ld = -math.exp(-0.5) * _sigmoid(z)
        a = _sigmoid(za)
        a_ref[rw, :] = a.astype(a_ref.dtype)
        p0 = ld.astype(BF16)
        r1 = ld - p0.astype(F32)
        p1 = r1.astype(BF16)
        p2 = (r1 - p1.astype(F32)).astype(BF16)
        yield
        cum = _dot(tri, p0) + _dot(tri, p1) + _dot(tri, p2)
        PL = jnp.exp(jnp.sum(ld, axis=0, keepdims=True))
        yield
        invP = jnp.exp(-cum)
        kk = kk_ref[rw, :].astype(F32)
        v = v_ref[rw, :].astype(F32)
        At = -kk * jnp.exp(cum - ld)
        Bt = (kk * a) * invP
        Kt = (k_ref[rw, :].astype(F32) * (1.0 + (a - 1.0) * ka_ref[...])) * invP
        Rt = r_ref[rw, :].astype(F32) * jnp.exp(cum)
        Bh = Bt * PL
        Kh = Kt * PL
        yield
        As = [sm(sl(At, g)) for g in pairs]
        Rs = [sm(sl(Rt, g)) for g in pairs]
        Vs = [sm(sl(v, g)) for g in pairs]
        Gc = [_mm(jnp.concatenate([As[g], Rs[g]], axis=0),
                  jnp.concatenate([sl(Bt, g), sl(Kt, g)], axis=0), _NT) for g in pairs]
        yield
        Gr = [pltpu.roll(Gc[g], L, axis=1) for g in pairs]

        def place(g, r0, first):
            a, b = (Gc[g], Gr[g]) if first else (Gr[g], Gc[g])
            return jnp.concatenate([a[r0:r0 + L], b[r0 + L:r0 + 2 * L]], axis=0)

        Np = [jnp.where(strict, place(g, 0, True), 0.0).astype(BF16) for g in pairs]
        Aak = [jnp.where(strict, place(g, 0, False), 0.0) for g in pairs]
        Mr = [jnp.concatenate([jnp.where(incl, place(g, 2 * L, True), 0.0),
                               jnp.where(incl, place(g, 2 * L, False), 0.0)], axis=1).astype(BF16) for g in pairs]
        AV = mm2(Aak, Vs)
        yield
        pw = [Np]
        for s in range(nsteps - 1):
            pw.append([t.astype(BF16) for t in mm2(pw[-1], pw[-1])])
            yield
        plc = [jnp.transpose(jnp.broadcast_to(sl(PL, g), (PAIR, PAIR))) for g in pairs]
        res[q] = (As, Rs, Vs, Mr, AV, pw, Bh, Kh, v, plc)
        yield

    state = [[h_scr[g] for g in pairs]]

    def back(q):
        As, Rs, Vs, Mr, AV, pw, Bh, Kh, v, plc = res[q]
        H = state[0]
        AH = mm2([jnp.concatenate([As[g], Rs[g]], axis=0) for g in pairs], H)
        yield
        Us = [AH[g][:2 * L] + AV[g] for g in pairs]
        for s in range(nsteps):
            Us = [Us[g] + _mm(pw[s][g], Us[g]) for g in pairs]
            yield
        Ys = [AH[g][2 * L:] + _mm(Mr[g], jnp.concatenate([Us[g], Vs[g]], axis=0)) for g in pairs]
        upd = [_mm(jnp.concatenate([sl(Bh, g), sl(Kh, g)], axis=0),
                   jnp.concatenate([Us[g][:L] + Us[g][L:], sl(v, g)], axis=0), _TN) for g in pairs]
        yield
        rw = rows(q)
        for g in pairs:
            y_ref[rw, g * PAIR:(g + 1) * PAIR] = (Ys[g][:L] + Ys[g][L:]).astype(y_ref.dtype)
        state[0] = [H[g] * plc[g] + jnp.where(bd, upd[g], 0.0) for g in pairs]
        yield

    def run(primary, others):
        while next(primary, _DONE) is not _DONE:
            for o in others:
                next(o, _DONE)

    fronts = [front(q) for q in range(nsub)]
    for _ in range(_FRONT_LEAD):
        next(fronts[0])
    for q in range(nsub):
        nxt = fronts[q + 1:q + 2]
        run(fronts[q], nxt)
        run(back(q), nxt)
    for g in pairs:
        h_scr[g] = state[0][g]

    @pl.when(c == nc - 1)
    def _():
        hT_ref[...] = h_scr[...]


def _wkv_scan(s1, kk, w2d, a2d, w0, a0, k_a, h0, B, T, D):
    G = min(SCAN_PAIRS, D // PAIR)
    W = G * PAIR
    nsub = SCAN_CHUNKS
    L = nsub * CHUNK
    nc = T // L
    npg = D // W
    N1 = s1.shape[1]
    s1 = s1.reshape(B, T, N1)
    kk = kk.reshape(B, T, D)
    chunk = lambda d, c: c + d * (nc - 1 - 2 * c)
    tok = lambda off: pl.BlockSpec((None, L, W), lambda d, b, p, c: (b, chunk(d, c), off + p))
    dirw = lambda rows: pl.BlockSpec((None, rows, W), lambda d, b, p, c: (d, 0, p))
    state = pl.BlockSpec((None, None, G, PAIR, PAIR), lambda d, b, p, c: (d, b, p, 0, 0))
    out_tok = pl.BlockSpec((None, None, L, W), lambda d, b, p, c: (d, b, chunk(d, c), p))
    y, a, hT = pl.pallas_call(
        functools.partial(_scan_kernel, G=G, nsub=nsub),
        grid=(2, B, npg, nc),
        in_specs=[
            tok(0), tok(npg), tok(2 * npg),
            pl.BlockSpec((None, L, W), lambda d, b, p, c: (b, chunk(d, c), p)),
            pl.BlockSpec((None, L, 2 * LORA_BLK), lambda d, b, p, c: (b, chunk(d, c), 3 * D // (2 * LORA_BLK))),
            dirw(LORA_BLK), dirw(LORA_BLK), dirw(1), dirw(1),
            pl.BlockSpec((1, W), lambda d, b, p, c: (0, p)),
            state,
        ],
        out_specs=[out_tok, out_tok, state],
        out_shape=[jax.ShapeDtypeStruct((2, B, T, D), BF16),
                   jax.ShapeDtypeStruct((2, B, T, D), BF16),
                   jax.ShapeDtypeStruct((2, B, D // PAIR, PAIR, PAIR), F32)],
        scratch_shapes=[pltpu.VMEM((G, PAIR, PAIR), F32)],
        compiler_params=_cparams(("arbitrary", "arbitrary", "arbitrary", "arbitrary")),
        name="wkv_scan",
    )(s1, s1, s1, kk, s1, w2d, a2d, w0, a0, k_a, h0)
    return y.reshape(2, B * T, D), a.reshape(2, B * T, D), hT


def _rwkv_out_kernel(y0_ref, y1_ref, r_ref, k_ref, v_ref, a0_ref, a1_ref, sg_ref, g2_ref,
                     lnw_ref, lnb_ref, rk_ref, ka_ref, ones_ref, w_ref, x_ref, gate_ref,
                     o_ref, *, D):
    ones = ones_ref[...]
    inv = 1.0 / RWKV_HEAD
    sg = sg_ref[:, :LORA_BLK]
    acc = None
    for s in range(D // MXU_DIM):
        cs = slice(s * MXU_DIM, (s + 1) * MXU_DIM)
        f = lambda ref: ref[:, cs].astype(F32)
        wkv = f(y0_ref) + f(y1_ref)
        cen = wkv - _dot(wkv.astype(BF16), ones) * inv
        var = _dot((cen * cen).astype(BF16), ones) * inv
        yn = cen * lax.rsqrt(var + GN_EPS) * lnw_ref[:, cs] + lnb_ref[:, cs]
        ks = f(k_ref) * (2.0 + (f(a0_ref) + f(a1_ref) - 2.0) * ka_ref[:, cs])
        bonus = _dot((f(r_ref) * ks * rk_ref[:, cs]).astype(BF16), ones) * f(v_ref)
        gate = _dot(sg, g2_ref[:, cs])
        part = _dot(((yn + bonus) * gate).astype(BF16), w_ref[cs, :])
        acc = part if acc is None else acc + part
    o_ref[...] = x_ref[...] + gate_ref[...] * acc


def _rwkv_out(y, a, s1, g2, x, mod, row_fn, ln_w, ln_b, r_k, k_a, w_o, tm):
    M, D = x.shape
    big = lambda col: pl.BlockSpec((tm, D), lambda i: (i, col))
    dirblk = lambda d: pl.BlockSpec((None, tm, D), lambda i: (d, i, 0))
    row = pl.BlockSpec((1, D), lambda i: (0, 0))
    return pl.pallas_call(
        functools.partial(_rwkv_out_kernel, D=D),
        grid=(M // tm,),
        in_specs=[
            dirblk(0), dirblk(1),
            big(0), big(1), big(2),
            dirblk(0), dirblk(1),
            pl.BlockSpec((tm, 2 * LORA_BLK), lambda i: (i, 3 * D // (2 * LORA_BLK) + 1)),
            pl.BlockSpec((LORA_BLK, D), lambda i: (0, 0)),
            row, row, row, row,
            pl.BlockSpec((MXU_DIM, MXU_DIM), lambda i: (0, 0)),
            pl.BlockSpec((D, D), lambda i: (0, 0)),
            pl.BlockSpec((tm, D), lambda i: (i, 0)),
            pl.BlockSpec((None, 1, D), lambda i: (row_fn(i), 0, 2)),
        ],
        out_specs=pl.BlockSpec((tm, D), lambda i: (i, 0)),
        out_shape=jax.ShapeDtypeStruct((M, D), F32),
        compiler_params=_cparams(("arbitrary",)),
        name="rwkv_out",
    )(y, y, s1, s1, s1, a, a, s1, g2, ln_w, ln_b, r_k, k_a, _group_ones(MXU_DIM, RWKV_HEAD), w_o, x, mod)


def _ffn_kernel(x_ref, g_ref, sh_ref, sc_ref, gate_ref, w1_ref, w3_ref, w2_ref, o_ref, a_scr, *, tf):
    f = pl.program_id(1)
    nf = pl.num_programs(1)

    def hidden_block(a):
        acts = []
        for s in range(tf // MXU_DIM):
            cs = slice(s * MXU_DIM, (s + 1) * MXU_DIM)
            h1 = _dot(a, w1_ref[:, cs])
            h3 = _dot(a, w3_ref[:, cs])
            acts.append((_silu(h1) * h3).astype(BF16))
        return _dot(jnp.concatenate(acts, axis=1), w2_ref[...])

    @pl.when(f == 0)
    def _():
        a = _modulate(x_ref[...], g_ref[...], sh_ref[...], sc_ref[...]).astype(BF16)
        a_scr[...] = a
        o_ref[...] = hidden_block(a)

    @pl.when(f > 0)
    def _():
        o_ref[...] += hidden_block(a_scr[...])

    @pl.when(f == nf - 1)
    def _():
        o_ref[...] = x_ref[...] + gate_ref[...] * o_ref[...]


def _ffn(x, mod, row_fn, gain, w1b, w3b, w2, tm, tf):
    M, D = x.shape
    nf = w1b.shape[1] // tf
    modspec = lambda ch: pl.BlockSpec((None, 1, D), lambda i, f: (row_fn(i), 0, ch))
    return pl.pallas_call(
        functools.partial(_ffn_kernel, tf=tf),
        grid=(M // tm, nf),
        in_specs=[
            pl.BlockSpec((tm, D), lambda i, f: (i, 0), pipeline_mode=pl.Buffered(1)),
            pl.BlockSpec((1, D), lambda i, f: (0, 0)),
            modspec(3), modspec(4), modspec(5),
            pl.BlockSpec((D, tf), lambda i, f: (0, f)),
            pl.BlockSpec((D, tf), lambda i, f: (0, f)),
            pl.BlockSpec((tf, D), lambda i, f: (f, 0)),
        ],
        out_specs=pl.BlockSpec((tm, D), lambda i, f: (i, 0)),
        out_shape=jax.ShapeDtypeStruct((M, D), F32),
        scratch_shapes=[pltpu.VMEM((tm, D), BF16)],
        compiler_params=_cparams(("arbitrary", "arbitrary")),
        name="ffn",
    )(x, gain, mod, mod, mod, w1b, w3b, w2)


def _qkv_kernel(x_ref, g_ref, sh_ref, sc_ref, w_ref, hg_ref, cos_ref, sin_ref, ones_ref, o_ref, a_scr,
                *, nrope, nnorm, tn):
    n = pl.program_id(1)

    def normed_block(a):
        acc = _dot(a, w_ref[...])
        hg = hg_ref[...]
        ones = ones_ref[...]
        for h in range(tn // MXU_DIM):
            blk = acc[:, h * MXU_DIM:(h + 1) * MXU_DIM]
            ms = _dot((blk * blk).astype(BF16), ones) * (1.0 / NA_HEAD)
            blk = blk * lax.rsqrt(ms + RMS_EPS)
            for s in range(MXU_DIM // NA_HEAD):
                t = blk[:, s * NA_HEAD:(s + 1) * NA_HEAD] * hg
                if nrope:
                    t = t * cos_ref[...] + pltpu.roll(t, NA_HEAD // 2, axis=1) * sin_ref[...]
                c0 = h * MXU_DIM + s * NA_HEAD
                o_ref[:, c0:c0 + NA_HEAD] = t.astype(o_ref.dtype)

    @pl.when(n == 0)
    def _():
        a = _modulate(x_ref[...], g_ref[...], sh_ref[...], sc_ref[...]).astype(BF16)
        a_scr[...] = a
        normed_block(a)

    @pl.when((n > 0) & (n < nnorm))
    def _():
        normed_block(a_scr[...])

    @pl.when(n >= nnorm)
    def _():
        o_ref[...] = _dot(a_scr[...], w_ref[...]).astype(o_ref.dtype)


def _na_qkv(x, mod, row_fn, gain, w, head_gain, cos, sin, *, nrope, nnorm, tm, tn):
    assert nrope in (0, nnorm)
    M, D = x.shape
    N = w.shape[1]
    tpb = cos.shape[0] // tm
    hrow = lambda n: jnp.minimum(n, head_gain.shape[0] - 1)
    return pl.pallas_call(
        functools.partial(_qkv_kernel, nrope=nrope, nnorm=nnorm, tn=tn),
        grid=(M // tm, N // tn),
        in_specs=[
            pl.BlockSpec((tm, D), lambda i, n: (i, 0)),
            pl.BlockSpec((1, D), lambda i, n: (0, 0)),
            pl.BlockSpec((None, 1, D), lambda i, n: (row_fn(i), 0, 0)),
            pl.BlockSpec((None, 1, D), lambda i, n: (row_fn(i), 0, 1)),
            pl.BlockSpec((D, tn), lambda i, n: (0, n)),
            pl.BlockSpec((None, 1, NA_HEAD), lambda i, n: (hrow(n), 0, 0)),
            pl.BlockSpec((tm, NA_HEAD), lambda i, n: (i % tpb, 0)),
            pl.BlockSpec((tm, NA_HEAD), lambda i, n: (i % tpb, 0)),
            pl.BlockSpec((MXU_DIM, MXU_DIM), lambda i, n: (0, 0)),
        ],
        out_specs=pl.BlockSpec((tm, tn), lambda i, n: (i, n)),
        out_shape=jax.ShapeDtypeStruct((M, N), BF16),
        scratch_shapes=[pltpu.VMEM((tm, D), BF16)],
        compiler_params=_cparams(("arbitrary", "arbitrary")),
        name="na_qkv",
    )(x, gain, mod, mod, w, head_gain, cos, sin, _group_ones(MXU_DIM, NA_HEAD))


def _na_kernel(q_ref, k_ref, v_ref, kc_ref, vc_ref, bias_ref, o_ref, *, rows, rb):
    j0 = pl.program_id(2) * rb
    win_h = min(WIN_H, rows)
    nk = win_h * GRID_W
    qrows = range(rb)
    rs = lambda t, j: t[j * GRID_W:(j + 1) * GRID_W]

    qa = q_ref[...]
    s_ctx = lax.dot_general(qa, kc_ref[...], _NT, preferred_element_type=F32)

    def row(j):
        r0 = jnp.clip(j0 + j - WIN_H // 2, 0, rows - win_h)
        off = pl.multiple_of(r0 * GRID_W, GRID_W)
        s_lat = (lax.dot_general(rs(qa, j), k_ref[pl.ds(off, nk), :], _NT, preferred_element_type=F32)
                 + bias_ref[r0 - (j0 + j) + WIN_H - 1])
        yield
        sc = rs(s_ctx, j)
        m = jnp.maximum(jnp.max(s_lat, axis=-1, keepdims=True), jnp.max(sc, axis=-1, keepdims=True))
        yield
        p_lat = jnp.exp(s_lat - m)
        p_ctx = jnp.exp(sc - m)
        yield
        den = jnp.sum(p_lat, axis=-1, keepdims=True) + jnp.sum(p_ctx, axis=-1, keepdims=True)
        o = _dot(p_lat.astype(BF16), v_ref[pl.ds(off, nk), :]) + _dot(p_ctx.astype(BF16), vc_ref[...])
        yield
        o_ref[j * GRID_W:(j + 1) * GRID_W, :] = (o / den).astype(o_ref.dtype)
        yield

    gens = [row(j) for j in qrows]
    done = [False] * rb
    tick = 0
    while not all(done):
        for j in qrows:
            if not done[j] and tick >= j * _ROW_STAGGER:
                done[j] = next(gens[j], _DONE) is _DONE
        tick += 1


def _na_attention(qkv, kvc, bias, B, T, D, rb):
    H = D // NA_HEAD
    rows = T // GRID_W
    C = kvc.shape[1]
    nbias = bias.shape[1]
    return pl.pallas_call(
        functools.partial(_na_kernel, rows=rows, rb=rb),
        grid=(B, H, rows // rb),
        in_specs=[
            pl.BlockSpec((None, rb * GRID_W, NA_HEAD), lambda b, h, j: (b, j, h)),
            pl.BlockSpec((None, T, NA_HEAD), lambda b, h, j: (b, 0, H + h)),
            pl.BlockSpec((None, T, NA_HEAD), lambda b, h, j: (b, 0, 2 * H + h)),
            pl.BlockSpec((None, C, NA_HEAD), lambda b, h, j: (b, 0, h)),
            pl.BlockSpec((None, C, NA_HEAD), lambda b, h, j: (b, 0, H + h)),
            pl.BlockSpec((None, nbias, GRID_W, bias.shape[3]), lambda b, h, j: (h, 0, 0, 0)),
        ],
        out_specs=pl.BlockSpec((None, rb * GRID_W, NA_HEAD), lambda b, h, j: (b, j, h)),
        out_shape=jax.ShapeDtypeStruct((B, T, D), BF16),
        compiler_params=_cparams(("arbitrary", "arbitrary", "arbitrary")),
        name="na_attention",
    )(qkv, qkv, qkv, kvc, kvc, bias)


def _proj_res_kernel(a_ref, w_ref, x_ref, gate_ref, o_ref):
    o_ref[...] = x_ref[...] + gate_ref[...] * _dot(a_ref[...], w_ref[...])


def _proj_residual(a, w, x, mod, row_fn, chunk, tm):
    M, K = a.shape
    N = w.shape[1]
    return pl.pallas_call(
        _proj_res_kernel,
        grid=(M // tm,),
        in_specs=[
            pl.BlockSpec((tm, K), lambda i: (i, 0)),
            pl.BlockSpec((K, N), lambda i: (0, 0)),
            pl.BlockSpec((tm, N), lambda i: (i, 0)),
            pl.BlockSpec((None, 1, N), lambda i: (row_fn(i), 0, chunk)),
        ],
        out_specs=pl.BlockSpec((tm, N), lambda i: (i, 0)),
        out_shape=jax.ShapeDtypeStruct((M, N), F32),
        compiler_params=_cparams(("arbitrary",)),
        name="proj_residual",
    )(a, w, x, mod)


_QUART = NA_HEAD // 4
_ROPE_PERM = np.concatenate([np.arange(0, _QUART), np.arange(2 * _QUART, 3 * _QUART),
                             np.arange(_QUART, 2 * _QUART), np.arange(3 * _QUART, NA_HEAD)])


def _rope_layout(w_qk):
    K, N = w_qk.shape
    return w_qk.reshape(K, N // NA_HEAD, NA_HEAD)[:, :, _ROPE_PERM].reshape(K, N)


def _rope_tables(T):
    t = np.arange(T)
    half = NA_HEAD // 2
    freqs = ROPE_THETA ** (-np.arange(0, half, 2, dtype=np.float64) / half)
    ang_r = (t // GRID_W).astype(np.float64)[:, None] * freqs
    ang_c = (t % GRID_W).astype(np.float64)[:, None] * freqs
    cos = np.concatenate([np.cos(ang_r), np.cos(ang_c), np.cos(ang_r), np.cos(ang_c)], axis=1)
    sin = np.concatenate([-np.sin(ang_r), -np.sin(ang_c), np.sin(ang_r), np.sin(ang_c)], axis=1)
    return jnp.asarray(cos, F32), jnp.asarray(sin, F32)


def _na_bias(rpb, rows):
    win_h = min(WIN_H, rows)
    qc = np.arange(GRID_W)[:, None]
    kc = np.arange(GRID_W)[None, :]
    wstart = np.clip(qc - WIN_W // 2, 0, GRID_W - WIN_W)
    valid = (kc >= wstart) & (kc < wstart + WIN_W)
    rel = np.clip(kc - qc, -(WIN_W - 1), WIN_W - 1) + WIN_W - 1
    g = rpb[:, :, rel]
    g = jnp.where(valid[None, None], g, NEG_INF)
    tabs = []
    for i0 in range(2 * WIN_H - win_h):
        blk = g[:, i0:i0 + win_h]
        tabs.append(jnp.transpose(blk, (0, 2, 1, 3)).reshape(g.shape[0], GRID_W, win_h * GRID_W))
    return jnp.stack(tabs, axis=1)


def _pad_cols(w, n):
    return jnp.pad(w, ((0, 0), (0, n - w.shape[1])))


def _pad_rows(w, n):
    return jnp.pad(w, ((0, n - w.shape[0]), (0, 0)))


def kernel(x, c, ctx, c_ctx, ada_w, ada_b, norm1, norm2, rwkv_mu, rwkv_w_r, rwkv_w_k, rwkv_w_v, rwkv_w_o, rwkv_w0, rwkv_w1, rwkv_w2, rwkv_a0, rwkv_a1, rwkv_a2, rwkv_g1, rwkv_g2, rwkv_k_k, rwkv_k_a, rwkv_r_k, rwkv_ln_w, rwkv_ln_b, na_w_qkv, na_w_o, na_q_gain, na_k_gain, na_rpb, ffn_w1, ffn_w3, ffn_w2):
    B, T, D = x.shape
    C = ctx.shape[1]
    rows = T // GRID_W
    depth = ada_w.shape[0]
    n_mix = 2
    TM = 512
    TMO = 256
    TF = 512 if ffn_w1.shape[-1] % 512 == 0 else 256
    TMF = min(1024, T)

    cvec = jnp.concatenate([c, c_ctx[None, :], jnp.zeros((8 - B - 1, D), F32)], axis=0)
    mods = _ada(cvec, ada_w, ada_b)
    lat_row = lambda tm: (lambda i: i // (T // tm))
    ctx_row = lambda i: B

    xl = x.reshape(B * T, D)
    xc = ctx.reshape(B * C, D)

    for i in range(depth):
        last = i == depth - 1
        j = i // n_mix
        mod = mods[i].reshape(8, 1, 6 * D)
        g1 = norm1[i].reshape(1, D)
        g2 = norm2[i].reshape(1, D)
        w1 = ffn_w1[i].astype(BF16)
        w3 = ffn_w3[i].astype(BF16)
        w2 = ffn_w2[i].astype(BF16)
        if i % n_mix == 0:
            lw = rwkv_w1.shape[-1]
            la = rwkv_a1.shape[-1]
            assert max(2 * lw, 2 * la, rwkv_g1.shape[-1]) <= LORA_BLK and D % TN1 == 0
            wcat = jnp.concatenate([
                rwkv_w_r[j], rwkv_w_k[j], rwkv_w_v[j],
                _pad_cols(jnp.concatenate([rwkv_w1[j, 0], rwkv_w1[j, 1]], axis=1), LORA_BLK),
                _pad_cols(jnp.concatenate([rwkv_a1[j, 0], rwkv_a1[j, 1]], axis=1), LORA_BLK),
                _pad_cols(rwkv_g1[j], 2 * LORA_BLK)], axis=1).astype(BF16)
            z = lambda n: jnp.zeros((n, D), F32)
            w2d = jnp.stack([_pad_rows(rwkv_w2[j, 0], LORA_BLK),
                             _pad_rows(jnp.concatenate([z(lw), rwkv_w2[j, 1]], axis=0), LORA_BLK)]).astype(BF16)
            a2d = jnp.stack([_pad_rows(rwkv_a2[j, 0], LORA_BLK),
                             _pad_rows(jnp.concatenate([z(la), rwkv_a2[j, 1]], axis=0), LORA_BLK)]).astype(BF16)
            wg2 = _pad_rows(rwkv_g2[j], LORA_BLK).astype(BF16)
            w0 = rwkv_w0[j].reshape(2, 1, D)
            a0 = rwkv_a0[j].reshape(2, 1, D)
            k_k = rwkv_k_k[j].reshape(1, D)
            k_a = rwkv_k_a[j].reshape(1, D)
            r_k = rwkv_r_k[j].reshape(1, D)
            ln_w = rwkv_ln_w[j].reshape(1, D)
            ln_b = rwkv_ln_b[j].reshape(1, D)
            w_o = rwkv_w_o[j].astype(BF16)

            s1c, kkc = _rwkv_stage1(xc, mod, ctx_row, g1, rwkv_mu[j], k_k, wcat, latent=False, rows=rows, tm=C)
            h0 = jnp.zeros((2, B, D // PAIR, PAIR, PAIR), F32)
            yc, ac, hc = _wkv_scan(s1c, kkc, w2d, a2d, w0, a0, k_a, h0, B, C, D)
            s1l, kkl = _rwkv_stage1(xl, mod, lat_row(TM), g1, rwkv_mu[j], k_k, wcat, latent=True, rows=rows, tm=TM)
            yl, al, _ = _wkv_scan(s1l, kkl, w2d, a2d, w0, a0, k_a, hc, B, T, D)
            xl = _rwkv_out(yl, al, s1l, wg2, xl, mod, lat_row(TMO), ln_w, ln_b, r_k, k_a, w_o, TMO)
            if not last:
                xc = _rwkv_out(yc, ac, s1c, wg2, xc, mod, ctx_row, ln_w, ln_b, r_k, k_a, w_o, min(TMO, C))
        else:
            H = D // NA_HEAD
            cos, sin = _rope_tables(T)
            scale = NA_HEAD ** -0.5
            hg = jnp.stack([(na_q_gain[j] * scale)[_ROPE_PERM], na_k_gain[j][_ROPE_PERM]]).reshape(2, 1, NA_HEAD)
            tn = min(1024, D)
            nq = D // tn
            wqkv = jnp.concatenate([_rope_layout(na_w_qkv[j][:, :2 * D]), na_w_qkv[j][:, 2 * D:]],
                                   axis=1).astype(BF16)
            hg_blocks = jnp.concatenate([jnp.broadcast_to(hg[0], (nq, 1, NA_HEAD)),
                                         jnp.broadcast_to(hg[1], (nq, 1, NA_HEAD))], axis=0)
            qkv = _na_qkv(xl, mod, lat_row(TM), g1, wqkv, hg_blocks, cos, sin, nrope=2 * nq, nnorm=2 * nq,
                          tm=TM, tn=tn)
            kg_blocks = jnp.broadcast_to(hg[1], (nq, 1, NA_HEAD))
            kvc = _na_qkv(xc, mod, ctx_row, g1, wqkv[:, D:], kg_blocks, cos[:C], sin[:C], nrope=0, nnorm=nq,
                          tm=C, tn=tn)
            bias = _na_bias(na_rpb[j], rows)
            o = _na_attention(qkv.reshape(B, T, 3 * D), kvc.reshape(B, C, 2 * D), bias, B, T, D, min(32, rows))
            xl = _proj_residual(o.reshape(B * T, D), na_w_o[j].astype(BF16), xl, mod,
                                lat_row(TM), 2, TM)
            if not last:
                raise NotImplementedError("attention layer with context output")
        xl = _ffn(xl, mod, lat_row(TMF), g2, w1, w3, w2, TMF, TF)
        if not last:
            xc = _ffn(xc, mod, ctx_row, g2, w1, w3, w2, C, TF)
    return xl.reshape(B, T, D)
```

```python
import functools
import math

import numpy as np
import jax
import jax.numpy as jnp
from jax import lax
from jax.experimental import pallas as pl
from jax.experimental.pallas import tpu as pltpu

F32 = jnp.float32
BF16 = jnp.bfloat16

GRID_W = 64
RMS_EPS = 1e-6
RWKV_HEAD = 64
GN_EPS = 64e-5
NA_HEAD = 128
WIN_H = 8
WIN_W = 16
ROPE_THETA = 10000.0
NEG_INF = -1e30

LANE = 128
MXU_DIM = 256
VMEM_LIMIT = 56 * 1024 * 1024

CHUNK = 64
PAIR = 2 * RWKV_HEAD
SCAN_PAIRS = 16
SCAN_CHUNKS = 4
LORA_BLK = 256
TN1 = 4 * LORA_BLK


def _cparams(sem):
    return pltpu.CompilerParams(dimension_semantics=sem, vmem_limit_bytes=VMEM_LIMIT)


def _dot(a, b):
    return jnp.dot(a, b, preferred_element_type=F32)


def _split2(x):
    hi = x.astype(BF16)
    lo = (x - hi.astype(F32)).astype(BF16)
    return hi, lo


def _rms_scale(x):
    return lax.rsqrt(jnp.mean(x * x, axis=-1, keepdims=True) + RMS_EPS)


def _modulate(x, gain, shift, scale):
    return (x * _rms_scale(x) * gain) * (1.0 + scale) + shift


def _sigmoid(x):
    return 1.0 / (1.0 + jnp.exp(-x))


def _silu(x):
    return x * _sigmoid(x)


def _group_ones(n, group):
    idx = np.arange(n) // group
    return jnp.asarray((idx[:, None] == idx[None, :]).astype(np.float32), dtype=BF16)


def _ada_kernel(c_ref, w_ref, b_ref, o_ref):
    s = _silu(c_ref[...]).astype(BF16)
    o_ref[...] = _dot(s, w_ref[...].astype(BF16)) + b_ref[...]


def _ada(cvec, ada_w, ada_b):
    depth, D, N = ada_w.shape
    R = cvec.shape[0]
    tn = 1024
    return pl.pallas_call(
        _ada_kernel,
        grid=(depth, N // tn),
        in_specs=[
            pl.BlockSpec((R, D), lambda l, n: (0, 0)),
            pl.BlockSpec((None, D, tn), lambda l, n: (l, 0, n)),
            pl.BlockSpec((None, 1, tn), lambda l, n: (l, 0, n)),
        ],
        out_specs=pl.BlockSpec((None, R, tn), lambda l, n: (l, 0, n)),
        out_shape=jax.ShapeDtypeStruct((depth, R, N), F32),
        compiler_params=_cparams(("arbitrary", "arbitrary")),
        name="ada",
    )(cvec, ada_w, ada_b.reshape(depth, 1, N))


def _s1_kernel(xp_ref, x_ref, xn_ref, g_ref, sh_ref, sc_ref, mu_ref, kkw_ref, ones_ref, w_ref,
               o_ref, kk_ref, a_scr, h_scr, xx_scr, *, latent, rows, tm, D, nb):
    i = pl.program_id(0)
    n = pl.program_id(1)
    R = GRID_W
    pad = h_scr.shape[0] - tm

    @pl.when(n == 0)
    def _():
        g = g_ref[...]
        sh = sh_ref[...]
        sc = sc_ref[...]
        lo = pad // 2
        h_scr[lo:lo + tm, :] = _modulate(x_ref[...], g, sh, sc)
        if latent:
            h_scr[0:lo, :] = _modulate(xp_ref[...], g, sh, sc)
            h_scr[lo + tm:, :] = _modulate(xn_ref[...], g, sh, sc)
        else:
            h_scr[0:lo, :] = jnp.zeros((lo, D), F32)
            h_scr[lo + tm:, :] = jnp.zeros((pad - lo, D), F32)
        for rr in range(tm // R):
            base = lo + rr * R
            hc = h_scr[base:base + R, :]
            if latent:
                q = D // 4
                col = lax.broadcasted_iota(jnp.int32, (R, q), 0)
                grow = (i * (tm // R) + rr) & (rows - 1)
                hs = jnp.concatenate([
                    jnp.where(col > 0, h_scr[base - 1:base - 1 + R, 0:q], 0.0),
                    jnp.where(col < R - 1, h_scr[base + 1:base + 1 + R, q:2 * q], 0.0),
                    jnp.where(grow > 0, h_scr[base - R:base, 2 * q:3 * q], 0.0),
                    jnp.where(grow < rows - 1, h_scr[base + R:base + 2 * R, 3 * q:], 0.0)], axis=1)
            else:
                half = D // 2
                hs = jnp.concatenate([h_scr[base - 1:base - 1 + R, :half],
                                      h_scr[base + 1:base + 1 + R, half:]], axis=1)
            xx = hs - hc
            xx_scr[rr * R:(rr + 1) * R, :] = xx
            a_scr[0, rr * R:(rr + 1) * R, :] = (hc + xx * mu_ref[0:1, :]).astype(BF16)

    def mix(j):
        lo = pad // 2
        for rr in range(tm // R):
            rs_ = slice(rr * R, (rr + 1) * R)
            a_scr[j, rs_, :] = (h_scr[lo + rr * R:lo + (rr + 1) * R, :] + xx_scr[rs_, :] * mu_ref[j:j + 1, :]).astype(BF16)

    def r_step():
        o_ref[...] = _dot(a_scr[0], w_ref[...]).astype(o_ref.dtype)

    def k_step():
        acc = _dot(a_scr[2], w_ref[...])
        o_ref[...] = acc.astype(o_ref.dtype)
        kkf = acc * kkw_ref[...]
        sq = kkf * kkf
        ones = ones_ref[...]
        parts = []
        for s in range(TN1 // MXU_DIM):
            parts.append(_dot(sq[:, s * MXU_DIM:(s + 1) * MXU_DIM].astype(BF16), ones))
        ss = jnp.concatenate(parts, axis=1)
        kk_ref[...] = (kkf / jnp.maximum(jnp.sqrt(ss), 1e-12)).astype(kk_ref.dtype)

    def v_step():
        o_ref[...] = _dot(a_scr[3], w_ref[...]).astype(o_ref.dtype)

    def lora_step():
        c1, c2, c3 = LORA_BLK, 2 * LORA_BLK, 3 * LORA_BLK
        o_ref[:, :c1] = jnp.tanh(_dot(a_scr[1], w_ref[:, :c1])).astype(o_ref.dtype)
        o_ref[:, c1:c2] = _dot(a_scr[4], w_ref[:, c1:c2]).astype(o_ref.dtype)
        o_ref[:, c2:c3] = _sigmoid(_dot(a_scr[5], w_ref[:, c2:c3])).astype(o_ref.dtype)
        o_ref[:, c3:] = jnp.zeros((o_ref.shape[0], TN1 - c3), o_ref.dtype)

    steps = [r_step] * nb + [k_step] * nb + [v_step] * nb + [lora_step]
    first_use = {2: nb, 3: 2 * nb, 1: 3 * nb, 4: 3 * nb, 5: 3 * nb}
    pending = sorted(first_use, key=lambda j: first_use[j])
    for t, step in enumerate(steps):
        side = [j for j in pending if first_use[j] == t + 1] or pending[:1]
        pending = [j for j in pending if j not in side]

        @pl.when(n == t)
        def _(step=step, side=side):
            step()
            for j in side:
                mix(j)


def _rwkv_stage1(x, mod, row_fn, gain, mu, k_k, wcat, *, latent, rows, tm):
    M, D = x.shape
    nb = D // TN1
    N1 = wcat.shape[1]
    nsteps = N1 // TN1
    hb = tm // GRID_W if latent else 1
    hrows = GRID_W if latent else 8
    nhb = M // hrows
    kcol = lambda n: jnp.clip(n - nb, 0, nb - 1)
    kern = functools.partial(_s1_kernel, latent=latent, rows=rows, tm=tm, D=D, nb=nb)
    return pl.pallas_call(
        kern,
        grid=(M // tm, nsteps),
        in_specs=[
            pl.BlockSpec((hrows, D), lambda i, n: (jnp.maximum(i * hb - 1, 0), 0)),
            pl.BlockSpec((tm, D), lambda i, n: (i, 0)),
            pl.BlockSpec((hrows, D), lambda i, n: (jnp.minimum(i * hb + hb, nhb - 1), 0)),
            pl.BlockSpec((1, D), lambda i, n: (0, 0)),
            pl.BlockSpec((None, 1, D), lambda i, n: (row_fn(i), 0, 0)),
            pl.BlockSpec((None, 1, D), lambda i, n: (row_fn(i), 0, 1)),
            pl.BlockSpec((6, D), lambda i, n: (0, 0)),
            pl.BlockSpec((1, TN1), lambda i, n: (0, kcol(n))),
            pl.BlockSpec((MXU_DIM, MXU_DIM), lambda i, n: (0, 0)),
            pl.BlockSpec((D, TN1), lambda i, n: (0, n)),
        ],
        out_specs=[
            pl.BlockSpec((tm, TN1), lambda i, n: (i, n)),
            pl.BlockSpec((tm, TN1), lambda i, n: (i, kcol(n))),
        ],
        out_shape=[jax.ShapeDtypeStruct((M, N1), BF16), jax.ShapeDtypeStruct((M, D), BF16)],
        scratch_shapes=[pltpu.VMEM((6, tm, D), BF16),
                        pltpu.VMEM((tm + 2 * hrows, D), F32),
                        pltpu.VMEM((tm, D), F32)],
        compiler_params=_cparams(("arbitrary", "arbitrary")),
        name="rwkv_stage1",
    )(x, x, x, gain, mod, mod, mu, k_k, _group_ones(MXU_DIM, RWKV_HEAD), wcat)


_NT = (((1,), (1,)), ((), ()))
_TN = (((0,), (0,)), ((), ()))


def _mm(a, b, dims=None):
    a = a.astype(BF16)
    b = b.astype(BF16)
    if dims is None:
        return jnp.dot(a, b, preferred_element_type=F32)
    return lax.dot_general(a, b, dims, preferred_element_type=F32)


_DONE = object()
_FRONT_LEAD = 1
_ROW_STAGGER = 1


def _scan_kernel(r_ref, k_ref, v_ref, kk_ref, lo_ref, w2_ref, a2_ref, w0_ref, a0_ref, ka_ref, h0_ref,
                 y_ref, a_ref, hT_ref, h_scr, *, G, nsub):
    d = pl.program_id(0)
    c = pl.program_id(3)
    nc = pl.num_programs(3)
    L = CHUNK
    pairs = range(G)
    nsteps = int(math.log2(L))

    @pl.when(c == 0)
    def _():
        h_scr[...] = h0_ref[...]

    sgn = 1 - 2 * d
    lane = lax.broadcasted_iota(jnp.int32, (L, PAIR), 1)
    m0 = lane < RWKV_HEAD
    ti = lax.broadcasted_iota(jnp.int32, (L, L), 0)
    si = lax.broadcasted_iota(jnp.int32, (L, L), 1)
    tri = jnp.where((si - ti) * sgn <= 0, 1.0, 0.0).astype(BF16)
    i2 = lax.broadcasted_iota(jnp.int32, (2 * L, 2 * L), 0)
    j2 = lax.broadcasted_iota(jnp.int32, (2 * L, 2 * L), 1)
    dif = jnp.where((i2 // L) == (j2 // L), ((j2 % L) - (i2 % L)) * sgn, 1)
    strict = dif < 0
    incl = dif <= 0
    r2 = lax.broadcasted_iota(jnp.int32, (PAIR, PAIR), 0)
    c2 = lax.broadcasted_iota(jnp.int32, (PAIR, PAIR), 1)
    bd = (r2 // RWKV_HEAD) == (c2 // RWKV_HEAD)

    def sm(x):
        return jnp.concatenate([jnp.where(m0, x, 0.0), jnp.where(m0, 0.0, x)], axis=0)

    def sl(x, g):
        return x[:, g * PAIR:(g + 1) * PAIR]

    def mm2(lhs, rhs):
        out = []
        for g in range(0, len(lhs), 2):
            ra, rb = rhs[g].astype(BF16), rhs[g + 1].astype(BF16)
            z = jnp.zeros_like(ra)
            bd2 = jnp.concatenate([jnp.concatenate([ra, z], axis=1), jnp.concatenate([z, rb], axis=1)], axis=0)
            both = _mm(jnp.concatenate([lhs[g].astype(BF16), lhs[g + 1].astype(BF16)], axis=1), bd2)
            out += [both[:, :PAIR], both[:, PAIR:]]
        return out

    def rows(q):
        return pl.ds(pl.multiple_of((q + d * (nsub - 1 - 2 * q)) * L, L), L)

    res = [None] * nsub

    def front(q):
        rw = rows(q)
        z = _dot(lo_ref[rw, :LORA_BLK], w2_ref[...]) + w0_ref[...]
        za = _dot(lo_ref[rw, LORA_BLK:], a2_ref[...]) + a0_ref[...]
        yield
        ld = -math.exp(-0.5) * _sigmoid(z)
        a = _sigmoid(za)
        a_ref[rw, :] = a.astype(a_ref.dtype)
        p0, p1 = _split2(ld)
        yield
        cum = _dot(tri, p0) + _dot(tri, p1)
        PL = jnp.exp(jnp.sum(ld, axis=0, keepdims=True))
        yield
        invP = jnp.exp(-cum)
        kk = kk_ref[rw, :].astype(F32)
        v = v_ref[rw, :].astype(F32)
        At = -kk * jnp.exp(cum - ld)
        Bt = (kk * a) * invP
        Kt = (k_ref[rw, :].astype(F32) * (1.0 + (a - 1.0) * ka_ref[...])) * invP
        Rt = r_ref[rw, :].astype(F32) * jnp.exp(cum)
        Bh = Bt * PL
        Kh = Kt * PL
        yield
        As = [sm(sl(At, g)) for g in pairs]
        Rs = [sm(sl(Rt, g)) for g in pairs]
        Vs = [sm(sl(v, g)) for g in pairs]
        Gc = [_mm(jnp.concatenate([As[g], Rs[g]], axis=0),
                  jnp.concatenate([sl(Bt, g), sl(Kt, g)], axis=0), _NT) for g in pairs]
        yield
        Gr = [pltpu.roll(Gc[g], L, axis=1) for g in pairs]

        def place(g, r0, first):
            a, b = (Gc[g], Gr[g]) if first else (Gr[g], Gc[g])
            return jnp.concatenate([a[r0:r0 + L], b[r0 + L:r0 + 2 * L]], axis=0)

        Np = [jnp.where(strict, place(g, 0, True), 0.0).astype(BF16) for g in pairs]
        Aak = [jnp.where(strict, place(g, 0, False), 0.0) for g in pairs]
        Mr = [jnp.concatenate([jnp.where(incl, place(g, 2 * L, True), 0.0),
                               jnp.where(incl, place(g, 2 * L, False), 0.0)], axis=1).astype(BF16) for g in pairs]
        AV = mm2(Aak, Vs)
        yield
        pw = [Np]
        for s in range(nsteps - 1):
            pw.append([t.astype(BF16) for t in mm2(pw[-1], pw[-1])])
            yield
        plc = [jnp.transpose(jnp.broadcast_to(sl(PL, g), (PAIR, PAIR))) for g in pairs]
        res[q] = (As, Rs, Vs, Mr, AV, pw, Bh, Kh, v, plc)
        yield

    state = [[h_scr[g] for g in pairs]]

    def back(q):
        As, Rs, Vs, Mr, AV, pw, Bh, Kh, v, plc = res[q]
        H = state[0]
        AH = mm2([jnp.concatenate([As[g], Rs[g]], axis=0) for g in pairs], H)
        yield
        Us = [AH[g][:2 * L] + AV[g] for g in pairs]
        for s in range(nsteps):
            Us = [Us[g] + _mm(pw[s][g], Us[g]) for g in pairs]
            yield
        Ys = [AH[g][2 * L:] + _mm(Mr[g], jnp.concatenate([Us[g], Vs[g]], axis=0)) for g in pairs]
        upd = [_mm(jnp.concatenate([sl(Bh, g), sl(Kh, g)], axis=0),
                   jnp.concatenate([Us[g][:L] + Us[g][L:], sl(v, g)], axis=0), _TN) for g in pairs]
        yield
        rw = rows(q)
        for g in pairs:
            y_ref[rw, g * PAIR:(g + 1) * PAIR] = (Ys[g][:L] + Ys[g][L:]).astype(y_ref.dtype)
        state[0] = [H[g] * plc[g] + jnp.where(bd, upd[g], 0.0) for g in pairs]
        yield

    def run(primary, others):
        while next(primary, _DONE) is not _DONE:
            for o in others:
                next(o, _DONE)

    fronts = [front(q) for q in range(nsub)]
    for _ in range(_FRONT_LEAD):
        next(fronts[0])
    for q in range(nsub):
        nxt = fronts[q + 1:q + 2]
        run(fronts[q], nxt)
        run(back(q), nxt)
    for g in pairs:
        h_scr[g] = state[0][g]

    @pl.when(c == nc - 1)
    def _():
        hT_ref[...] = h_scr[...]


def _wkv_scan(s1, kk, w2d, a2d, w0, a0, k_a, h0, B, T, D):
    G = min(SCAN_PAIRS, D // PAIR)
    W = G * PAIR
    nsub = SCAN_CHUNKS
    L = nsub * CHUNK
    nc = T // L
    npg = D // W
    N1 = s1.shape[1]
    s1 = s1.reshape(B, T, N1)
    kk = kk.reshape(B, T, D)
    chunk = lambda d, c: c + d * (nc - 1 - 2 * c)
    tok = lambda off: pl.BlockSpec((None, L, W), lambda d, b, p, c: (b, chunk(d, c), off + p))
    dirw = lambda rows: pl.BlockSpec((None, rows, W), lambda d, b, p, c: (d, 0, p))
    state = pl.BlockSpec((None, None, G, PAIR, PAIR), lambda d, b, p, c: (d, b, p, 0, 0))
    out_tok = pl.BlockSpec((None, None, L, W), lambda d, b, p, c: (d, b, chunk(d, c), p))
    y, a, hT = pl.pallas_call(
        functools.partial(_scan_kernel, G=G, nsub=nsub),
        grid=(2, B, npg, nc),
        in_specs=[
            tok(0), tok(npg), tok(2 * npg),
            pl.BlockSpec((None, L, W), lambda d, b, p, c: (b, chunk(d, c), p)),
            pl.BlockSpec((None, L, 2 * LORA_BLK), lambda d, b, p, c: (b, chunk(d, c), 3 * D // (2 * LORA_BLK))),
            dirw(LORA_BLK), dirw(LORA_BLK), dirw(1), dirw(1),
            pl.BlockSpec((1, W), lambda d, b, p, c: (0, p)),
            state,
        ],
        out_specs=[out_tok, out_tok, state],
        out_shape=[jax.ShapeDtypeStruct((2, B, T, D), BF16),
                   jax.ShapeDtypeStruct((2, B, T, D), BF16),
                   jax.ShapeDtypeStruct((2, B, D // PAIR, PAIR, PAIR), F32)],
        scratch_shapes=[pltpu.VMEM((G, PAIR, PAIR), F32)],
        compiler_params=_cparams(("arbitrary", "arbitrary", "arbitrary", "arbitrary")),
        name="wkv_scan",
    )(s1, s1, s1, kk, s1, w2d, a2d, w0, a0, k_a, h0)
    return y.reshape(2, B * T, D), a.reshape(2, B * T, D), hT


def _rwkv_out_kernel(y0_ref, y1_ref, r_ref, k_ref, v_ref, a0_ref, a1_ref, sg_ref, g2_ref,
                     lnw_ref, lnb_ref, rk_ref, ka_ref, ones_ref, w_ref, x_ref, gate_ref,
                     o_ref, *, D):
    ones = ones_ref[...]
    inv = 1.0 / RWKV_HEAD
    sg = sg_ref[:, :LORA_BLK]
    acc = None
    for s in range(D // MXU_DIM):
        cs = slice(s * MXU_DIM, (s + 1) * MXU_DIM)
        f = lambda ref: ref[:, cs].astype(F32)
        wkv = f(y0_ref) + f(y1_ref)
        cen = wkv - _dot(wkv.astype(BF16), ones) * inv
        var = _dot((cen * cen).astype(BF16), ones) * inv
        yn = cen * lax.rsqrt(var + GN_EPS) * lnw_ref[:, cs] + lnb_ref[:, cs]
        ks = f(k_ref) * (2.0 + (f(a0_ref) + f(a1_ref) - 2.0) * ka_ref[:, cs])
        bonus = _dot((f(r_ref) * ks * rk_ref[:, cs]).astype(BF16), ones) * f(v_ref)
        gate = _dot(sg, g2_ref[:, cs])
        part = _dot(((yn + bonus) * gate).astype(BF16), w_ref[cs, :])
        acc = part if acc is None else acc + part
    o_ref[...] = x_ref[...] + gate_ref[...] * acc


def _rwkv_out(y, a, s1, g2, x, mod, row_fn, ln_w, ln_b, r_k, k_a, w_o, tm):
    M, D = x.shape
    big = lambda col: pl.BlockSpec((tm, D), lambda i: (i, col))
    dirblk = lambda d: pl.BlockSpec((None, tm, D), lambda i: (d, i, 0))
    row = pl.BlockSpec((1, D), lambda i: (0, 0))
    return pl.pallas_call(
        functools.partial(_rwkv_out_kernel, D=D),
        grid=(M // tm,),
        in_specs=[
            dirblk(0), dirblk(1),
            big(0), big(1), big(2),
            dirblk(0), dirblk(1),
            pl.BlockSpec((tm, 2 * LORA_BLK), lambda i: (i, 3 * D // (2 * LORA_BLK) + 1)),
            pl.BlockSpec((LORA_BLK, D), lambda i: (0, 0)),
            row, row, row, row,
            pl.BlockSpec((MXU_DIM, MXU_DIM), lambda i: (0, 0)),
            pl.BlockSpec((D, D), lambda i: (0, 0)),
            pl.BlockSpec((tm, D), lambda i: (i, 0)),
            pl.BlockSpec((None, 1, D), lambda i: (row_fn(i), 0, 2)),
        ],
        out_specs=pl.BlockSpec((tm, D), lambda i: (i, 0)),
        out_shape=jax.ShapeDtypeStruct((M, D), F32),
        compiler_params=_cparams(("arbitrary",)),
        name="rwkv_out",
    )(y, y, s1, s1, s1, a, a, s1, g2, ln_w, ln_b, r_k, k_a, _group_ones(MXU_DIM, RWKV_HEAD), w_o, x, mod)


def _ffn_kernel(x_ref, g_ref, sh_ref, sc_ref, gate_ref, w1_ref, w3_ref, w2_ref, o_ref, a_scr, *, tf):
    f = pl.program_id(1)
    nf = pl.num_programs(1)

    def hidden_block(a):
        acts = []
        for s in range(tf // MXU_DIM):
            cs = slice(s * MXU_DIM, (s + 1) * MXU_DIM)
            h1 = _dot(a, w1_ref[:, cs])
            h3 = _dot(a, w3_ref[:, cs])
            acts.append((_silu(h1) * h3).astype(BF16))
        return _dot(jnp.concatenate(acts, axis=1), w2_ref[...])

    @pl.when(f == 0)
    def _():
        a = _modulate(x_ref[...], g_ref[...], sh_ref[...], sc_ref[...]).astype(BF16)
        a_scr[...] = a
        o_ref[...] = hidden_block(a)

    @pl.when((f > 0) & (f < nf - 1))
    def _():
        o_ref[...] += hidden_block(a_scr[...])

    @pl.when(f == nf - 1)
    def _():
        o_ref[...] = x_ref[...] + gate_ref[...] * (o_ref[...] + hidden_block(a_scr[...]))


def _ffn(x, mod, row_fn, gain, w1b, w3b, w2, tm, tf):
    M, D = x.shape
    nf = w1b.shape[1] // tf
    modspec = lambda ch: pl.BlockSpec((None, 1, D), lambda i, f: (row_fn(i), 0, ch))
    return pl.pallas_call(
        functools.partial(_ffn_kernel, tf=tf),
        grid=(M // tm, nf),
        in_specs=[
            pl.BlockSpec((tm, D), lambda i, f: (i, 0), pipeline_mode=pl.Buffered(1)),
            pl.BlockSpec((1, D), lambda i, f: (0, 0)),
            modspec(3), modspec(4), modspec(5),
            pl.BlockSpec((D, tf), lambda i, f: (0, f)),
            pl.BlockSpec((D, tf), lambda i, f: (0, f)),
            pl.BlockSpec((tf, D), lambda i, f: (f, 0)),
        ],
        out_specs=pl.BlockSpec((tm, D), lambda i, f: (i, 0)),
        out_shape=jax.ShapeDtypeStruct((M, D), F32),
        scratch_shapes=[pltpu.VMEM((tm, D), BF16)],
        compiler_params=_cparams(("arbitrary", "arbitrary")),
        name="ffn",
    )(x, gain, mod, mod, mod, w1b, w3b, w2)


def _qkv_kernel(x_ref, g_ref, sh_ref, sc_ref, w_ref, hg_ref, cos_ref, sin_ref, ones_ref, o_ref, a_scr,
                *, nrope, nnorm, tn):
    n = pl.program_id(1)

    def normed_block(a):
        acc = _dot(a, w_ref[...])
        hg = hg_ref[...]
        ones = ones_ref[...]
        for h in range(tn // MXU_DIM):
            blk = acc[:, h * MXU_DIM:(h + 1) * MXU_DIM]
            ms = _dot((blk * blk).astype(BF16), ones) * (1.0 / NA_HEAD)
            blk = blk * lax.rsqrt(ms + RMS_EPS)
            for s in range(MXU_DIM // NA_HEAD):
                t = blk[:, s * NA_HEAD:(s + 1) * NA_HEAD] * hg
                if nrope:
                    t = t * cos_ref[...] + pltpu.roll(t, NA_HEAD // 2, axis=1) * sin_ref[...]
                c0 = h * MXU_DIM + s * NA_HEAD
                o_ref[:, c0:c0 + NA_HEAD] = t.astype(o_ref.dtype)

    @pl.when(n == 0)
    def _():
        a = _modulate(x_ref[...], g_ref[...], sh_ref[...], sc_ref[...]).astype(BF16)
        a_scr[...] = a
        normed_block(a)

    @pl.when((n > 0) & (n < nnorm))
    def _():
        normed_block(a_scr[...])

    @pl.when(n >= nnorm)
    def _():
        o_ref[...] = _dot(a_scr[...], w_ref[...]).astype(o_ref.dtype)


def _na_qkv(x, mod, row_fn, gain, w, head_gain, cos, sin, *, nrope, nnorm, tm, tn):
    assert nrope in (0, nnorm)
    M, D = x.shape
    N = w.shape[1]
    tpb = cos.shape[0] // tm
    hrow = lambda n: jnp.minimum(n, head_gain.shape[0] - 1)
    return pl.pallas_call(
        functools.partial(_qkv_kernel, nrope=nrope, nnorm=nnorm, tn=tn),
        grid=(M // tm, N // tn),
        in_specs=[
            pl.BlockSpec((tm, D), lambda i, n: (i, 0)),
            pl.BlockSpec((1, D), lambda i, n: (0, 0)),
            pl.BlockSpec((None, 1, D), lambda i, n: (row_fn(i), 0, 0)),
            pl.BlockSpec((None, 1, D), lambda i, n: (row_fn(i), 0, 1)),
            pl.BlockSpec((D, tn), lambda i, n: (0, n)),
            pl.BlockSpec((None, 1, NA_HEAD), lambda i, n: (hrow(n), 0, 0)),
            pl.BlockSpec((tm, NA_HEAD), lambda i, n: (i % tpb, 0)),
            pl.BlockSpec((tm, NA_HEAD), lambda i, n: (i % tpb, 0)),
            pl.BlockSpec((MXU_DIM, MXU_DIM), lambda i, n: (0, 0)),
        ],
        out_specs=pl.BlockSpec((tm, tn), lambda i, n: (i, n)),
        out_shape=jax.ShapeDtypeStruct((M, N), BF16),
        scratch_shapes=[pltpu.VMEM((tm, D), BF16)],
        compiler_params=_cparams(("arbitrary", "arbitrary")),
        name="na_qkv",
    )(x, gain, mod, mod, w, head_gain, cos, sin, _group_ones(MXU_DIM, NA_HEAD))


def _na_kernel(q_ref, k_ref, v_ref, kc_ref, vc_ref, bias_ref, o_ref, *, rows, rb):
    j0 = pl.program_id(2) * rb
    win_h = min(WIN_H, rows)
    nk = win_h * GRID_W
    qrows = range(rb)
    rs = lambda t, j: t[j * GRID_W:(j + 1) * GRID_W]

    qa = q_ref[...]
    s_ctx = lax.dot_general(qa, kc_ref[...], _NT, preferred_element_type=F32)

    def row(j):
        r0 = jnp.clip(j0 + j - WIN_H // 2, 0, rows - win_h)
        off = pl.multiple_of(r0 * GRID_W, GRID_W)
        s_lat = (lax.dot_general(rs(qa, j), k_ref[pl.ds(off, nk), :], _NT, preferred_element_type=F32)
                 + bias_ref[r0 - (j0 + j) + WIN_H - 1])
        yield
        sc = rs(s_ctx, j)
        m = jnp.maximum(jnp.max(s_lat, axis=-1, keepdims=True), jnp.max(sc, axis=-1, keepdims=True))
        yield
        p_lat = jnp.exp(s_lat - m)
        p_ctx = jnp.exp(sc - m)
        yield
        den = jnp.sum(p_lat, axis=-1, keepdims=True) + jnp.sum(p_ctx, axis=-1, keepdims=True)
        o = _dot(p_lat.astype(BF16), v_ref[pl.ds(off, nk), :]) + _dot(p_ctx.astype(BF16), vc_ref[...])
        yield
        o_ref[j * GRID_W:(j + 1) * GRID_W, :] = (o / den).astype(o_ref.dtype)
        yield

    gens = [row(j) for j in qrows]
    done = [False] * rb
    tick = 0
    while not all(done):
        for j in qrows:
            if not done[j] and tick >= j * _ROW_STAGGER:
                done[j] = next(gens[j], _DONE) is _DONE
        tick += 1


def _na_attention(qkv, kvc, bias, B, T, D, rb):
    H = D // NA_HEAD
    rows = T // GRID_W
    C = kvc.shape[1]
    nbias = bias.shape[1]
    return pl.pallas_call(
        functools.partial(_na_kernel, rows=rows, rb=rb),
        grid=(B, H, rows // rb),
        in_specs=[
            pl.BlockSpec((None, rb * GRID_W, NA_HEAD), lambda b, h, j: (b, j, h)),
            pl.BlockSpec((None, T, NA_HEAD), lambda b, h, j: (b, 0, H + h)),
            pl.BlockSpec((None, T, NA_HEAD), lambda b, h, j: (b, 0, 2 * H + h)),
            pl.BlockSpec((None, C, NA_HEAD), lambda b, h, j: (b, 0, h)),
            pl.BlockSpec((None, C, NA_HEAD), lambda b, h, j: (b, 0, H + h)),
            pl.BlockSpec((None, nbias, GRID_W, bias.shape[3]), lambda b, h, j: (h, 0, 0, 0)),
        ],
        out_specs=pl.BlockSpec((None, rb * GRID_W, NA_HEAD), lambda b, h, j: (b, j, h)),
        out_shape=jax.ShapeDtypeStruct((B, T, D), BF16),
        compiler_params=_cparams(("arbitrary", "arbitrary", "arbitrary")),
        name="na_attention",
    )(qkv, qkv, qkv, kvc, kvc, bias)


def _proj_res_kernel(a_ref, w_ref, x_ref, gate_ref, o_ref):
    o_ref[...] = x_ref[...] + gate_ref[...] * _dot(a_ref[...], w_ref[...])


def _proj_residual(a, w, x, mod, row_fn, chunk, tm):
    M, K = a.shape
    N = w.shape[1]
    return pl.pallas_call(
        _proj_res_kernel,
        grid=(M // tm,),
        in_specs=[
            pl.BlockSpec((tm, K), lambda i: (i, 0)),
            pl.BlockSpec((K, N), lambda i: (0, 0)),
            pl.BlockSpec((tm, N), lambda i: (i, 0)),
            pl.BlockSpec((None, 1, N), lambda i: (row_fn(i), 0, chunk)),
        ],
        out_specs=pl.BlockSpec((tm, N), lambda i: (i, 0)),
        out_shape=jax.ShapeDtypeStruct((M, N), F32),
        compiler_params=_cparams(("arbitrary",)),
        name="proj_residual",
    )(a, w, x, mod)


_QUART = NA_HEAD // 4
_ROPE_PERM = np.concatenate([np.arange(0, _QUART), np.arange(2 * _QUART, 3 * _QUART),
                             np.arange(_QUART, 2 * _QUART), np.arange(3 * _QUART, NA_HEAD)])


def _rope_layout(w_qk):
    K, N = w_qk.shape
    return w_qk.reshape(K, N // NA_HEAD, NA_HEAD)[:, :, _ROPE_PERM].reshape(K, N)


def _rope_tables(T):
    t = np.arange(T)
    half = NA_HEAD // 2
    freqs = ROPE_THETA ** (-np.arange(0, half, 2, dtype=np.float64) / half)
    ang_r = (t // GRID_W).astype(np.float64)[:, None] * freqs
    ang_c = (t % GRID_W).astype(np.float64)[:, None] * freqs
    cos = np.concatenate([np.cos(ang_r), np.cos(ang_c), np.cos(ang_r), np.cos(ang_c)], axis=1)
    sin = np.concatenate([-np.sin(ang_r), -np.sin(ang_c), np.sin(ang_r), np.sin(ang_c)], axis=1)
    return jnp.asarray(cos, F32), jnp.asarray(sin, F32)


def _na_bias(rpb, rows):
    win_h = min(WIN_H, rows)
    qc = np.arange(GRID_W)[:, None]
    kc = np.arange(GRID_W)[None, :]
    wstart = np.clip(qc - WIN_W // 2, 0, GRID_W - WIN_W)
    valid = (kc >= wstart) & (kc < wstart + WIN_W)
    rel = np.clip(kc - qc, -(WIN_W - 1), WIN_W - 1) + WIN_W - 1
    g = rpb[:, :, rel]
    g = jnp.where(valid[None, None], g, NEG_INF)
    tabs = []
    for i0 in range(2 * WIN_H - win_h):
        blk = g[:, i0:i0 + win_h]
        tabs.append(jnp.transpose(blk, (0, 2, 1, 3)).reshape(g.shape[0], GRID_W, win_h * GRID_W))
    return jnp.stack(tabs, axis=1)


def _pad_cols(w, n):
    return jnp.pad(w, ((0, 0), (0, n - w.shape[1])))


def _pad_rows(w, n):
    return jnp.pad(w, ((0, n - w.shape[0]), (0, 0)))


def kernel(x, c, ctx, c_ctx, ada_w, ada_b, norm1, norm2, rwkv_mu, rwkv_w_r, rwkv_w_k, rwkv_w_v, rwkv_w_o, rwkv_w0, rwkv_w1, rwkv_w2, rwkv_a0, rwkv_a1, rwkv_a2, rwkv_g1, rwkv_g2, rwkv_k_k, rwkv_k_a, rwkv_r_k, rwkv_ln_w, rwkv_ln_b, na_w_qkv, na_w_o, na_q_gain, na_k_gain, na_rpb, ffn_w1, ffn_w3, ffn_w2):
    B, T, D = x.shape
    C = ctx.shape[1]
    rows = T // GRID_W
    depth = ada_w.shape[0]
    n_mix = 2
    TM = 512
    TMO = 256
    TF = 512 if ffn_w1.shape[-1] % 512 == 0 else 256
    TMF = min(1024, T)

    cvec = jnp.concatenate([c, c_ctx[None, :], jnp.zeros((8 - B - 1, D), F32)], axis=0)
    mods = _ada(cvec, ada_w, ada_b)
    lat_row = lambda tm: (lambda i: i // (T // tm))
    ctx_row = lambda i: B

    xl = x.reshape(B * T, D)
    xc = ctx.reshape(B * C, D)

    for i in range(depth):
        last = i == depth - 1
        j = i // n_mix
        mod = mods[i].reshape(8, 1, 6 * D)
        g1 = norm1[i].reshape(1, D)
        g2 = norm2[i].reshape(1, D)
        w1 = ffn_w1[i].astype(BF16)
        w3 = ffn_w3[i].astype(BF16)
        w2 = ffn_w2[i].astype(BF16)
        if i % n_mix == 0:
            lw = rwkv_w1.shape[-1]
            la = rwkv_a1.shape[-1]
            assert max(2 * lw, 2 * la, rwkv_g1.shape[-1]) <= LORA_BLK and D % TN1 == 0
            wcat = jnp.concatenate([
                rwkv_w_r[j], rwkv_w_k[j], rwkv_w_v[j],
                _pad_cols(jnp.concatenate([rwkv_w1[j, 0], rwkv_w1[j, 1]], axis=1), LORA_BLK),
                _pad_cols(jnp.concatenate([rwkv_a1[j, 0], rwkv_a1[j, 1]], axis=1), LORA_BLK),
                _pad_cols(rwkv_g1[j], 2 * LORA_BLK)], axis=1).astype(BF16)
            z = lambda n: jnp.zeros((n, D), F32)
            w2d = jnp.stack([_pad_rows(rwkv_w2[j, 0], LORA_BLK),
                             _pad_rows(jnp.concatenate([z(lw), rwkv_w2[j, 1]], axis=0), LORA_BLK)]).astype(BF16)
            a2d = jnp.stack([_pad_rows(rwkv_a2[j, 0], LORA_BLK),
                             _pad_rows(jnp.concatenate([z(la), rwkv_a2[j, 1]], axis=0), LORA_BLK)]).astype(BF16)
            wg2 = _pad_rows(rwkv_g2[j], LORA_BLK).astype(BF16)
            w0 = rwkv_w0[j].reshape(2, 1, D)
            a0 = rwkv_a0[j].reshape(2, 1, D)
            k_k = rwkv_k_k[j].reshape(1, D)
            k_a = rwkv_k_a[j].reshape(1, D)
            r_k = rwkv_r_k[j].reshape(1, D)
            ln_w = rwkv_ln_w[j].reshape(1, D)
            ln_b = rwkv_ln_b[j].reshape(1, D)
            w_o = rwkv_w_o[j].astype(BF16)

            s1c, kkc = _rwkv_stage1(xc, mod, ctx_row, g1, rwkv_mu[j], k_k, wcat, latent=False, rows=rows, tm=C)
            h0 = jnp.zeros((2, B, D // PAIR, PAIR, PAIR), F32)
            yc, ac, hc = _wkv_scan(s1c, kkc, w2d, a2d, w0, a0, k_a, h0, B, C, D)
            s1l, kkl = _rwkv_stage1(xl, mod, lat_row(TM), g1, rwkv_mu[j], k_k, wcat, latent=True, rows=rows, tm=TM)
            yl, al, _ = _wkv_scan(s1l, kkl, w2d, a2d, w0, a0, k_a, hc, B, T, D)
            xl = _rwkv_out(yl, al, s1l, wg2, xl, mod, lat_row(TMO), ln_w, ln_b, r_k, k_a, w_o, TMO)
            if not last:
                xc = _rwkv_out(yc, ac, s1c, wg2, xc, mod, ctx_row, ln_w, ln_b, r_k, k_a, w_o, min(TMO, C))
        else:
            H = D // NA_HEAD
            cos, sin = _rope_tables(T)
            scale = NA_HEAD ** -0.5
            hg = jnp.stack([(na_q_gain[j] * scale)[_ROPE_PERM], na_k_gain[j][_ROPE_PERM]]).reshape(2, 1, NA_HEAD)
            tn = min(1024, D)
            nq = D // tn
            wqkv = jnp.concatenate([_rope_layout(na_w_qkv[j][:, :2 * D]), na_w_qkv[j][:, 2 * D:]],
                                   axis=1).astype(BF16)
            hg_blocks = jnp.concatenate([jnp.broadcast_to(hg[0], (nq, 1, NA_HEAD)),
                                         jnp.broadcast_to(hg[1], (nq, 1, NA_HEAD))], axis=0)
            qkv = _na_qkv(xl, mod, lat_row(TM), g1, wqkv, hg_blocks, cos, sin, nrope=2 * nq, nnorm=2 * nq,
                          tm=TM, tn=tn)
            kg_blocks = jnp.broadcast_to(hg[1], (nq, 1, NA_HEAD))
            kvc = _na_qkv(xc, mod, ctx_row, g1, wqkv[:, D:], kg_blocks, cos[:C], sin[:C], nrope=0, nnorm=nq,
                          tm=C, tn=tn)
            bias = _na_bias(na_rpb[j], rows)
            o = _na_attention(qkv.reshape(B, T, 3 * D), kvc.reshape(B, C, 2 * D), bias, B, T, D, min(32, rows))
            xl = _proj_residual(o.reshape(B * T, D), na_w_o[j].astype(BF16), xl, mod,
                                lat_row(TM), 2, TM)
            if not last:
                raise NotImplementedError("attention layer with context output")
        xl = _ffn(xl, mod, lat_row(TMF), g2, w1, w3, w2, TMF, TF)
        if not last:
            xc = _ffn(xc, mod, ctx_row, g2, w1, w3, w2, C, TF)
    return xl.reshape(B, T, D)
```

```python
import functools
import math

import numpy as np
import jax
import jax.numpy as jnp
from jax import lax
from jax.experimental import pallas as pl
from jax.experimental.pallas import tpu as pltpu

F32 = jnp.float32
BF16 = jnp.bfloat16

GRID_W = 64
RMS_EPS = 1e-6
RWKV_HEAD = 64
GN_EPS = 64e-5
NA_HEAD = 128
WIN_H = 8
WIN_W = 16
ROPE_THETA = 10000.0
NEG_INF = -1e30

MXU_DIM = 256
VMEM_LIMIT = 56 * 1024 * 1024

CHUNK = 64
PAIR = 2 * RWKV_HEAD
SCAN_PAIRS = 16
SCAN_CHUNKS = 4
LORA_BLK = 256
TN1 = 4 * LORA_BLK


def _cparams(sem):
    return pltpu.CompilerParams(dimension_semantics=sem, vmem_limit_bytes=VMEM_LIMIT)


def _dot(a, b):
    return jnp.dot(a, b, preferred_element_type=F32)


def _split2(x):
    hi = x.astype(BF16)
    lo = (x - hi.astype(F32)).astype(BF16)
    return hi, lo


def _rms_scale(x):
    return lax.rsqrt(jnp.mean(x * x, axis=-1, keepdims=True) + RMS_EPS)


def _modulate(x, gain, shift, scale):
    return (x * _rms_scale(x) * gain) * (1.0 + scale) + shift


def _sigmoid(x):
    return 1.0 / (1.0 + jnp.exp(-x))


def _silu(x):
    return x * _sigmoid(x)


def _group_ones(n, group):
    idx = np.arange(n) // group
    return jnp.asarray((idx[:, None] == idx[None, :]).astype(np.float32), dtype=BF16)


def _ada_kernel(c_ref, w_ref, b_ref, o_ref):
    s = _silu(c_ref[...]).astype(BF16)
    o_ref[...] = _dot(s, w_ref[...].astype(BF16)) + b_ref[...]


def _ada(cvec, ada_w, ada_b):
    depth, D, N = ada_w.shape
    R = cvec.shape[0]
    tn = 1024
    return pl.pallas_call(
        _ada_kernel,
        grid=(depth, N // tn),
        in_specs=[
            pl.BlockSpec((R, D), lambda l, n: (0, 0)),
            pl.BlockSpec((None, D, tn), lambda l, n: (l, 0, n)),
            pl.BlockSpec((None, 1, tn), lambda l, n: (l, 0, n)),
        ],
        out_specs=pl.BlockSpec((None, R, tn), lambda l, n: (l, 0, n)),
        out_shape=jax.ShapeDtypeStruct((depth, R, N), F32),
        compiler_params=_cparams(("arbitrary", "arbitrary")),
        name="ada",
    )(cvec, ada_w, ada_b.reshape(depth, 1, N))


def _s1_kernel(xp_ref, x_ref, xn_ref, g_ref, sh_ref, sc_ref, mu_ref, kkw_ref, ones_ref, w_ref,
               o_ref, kk_ref, a_scr, h_scr, xx_scr, *, latent, rows, tm, D, nb):
    i = pl.program_id(0)
    n = pl.program_id(1)
    R = GRID_W
    pad = h_scr.shape[0] - tm

    @pl.when(n == 0)
    def _():
        g = g_ref[...]
        sh = sh_ref[...]
        sc = sc_ref[...]
        lo = pad // 2
        h_scr[lo:lo + tm, :] = _modulate(x_ref[...], g, sh, sc)
        if latent:
            h_scr[0:lo, :] = _modulate(xp_ref[...], g, sh, sc)
            h_scr[lo + tm:, :] = _modulate(xn_ref[...], g, sh, sc)
        else:
            h_scr[0:lo, :] = jnp.zeros((lo, D), F32)
            h_scr[lo + tm:, :] = jnp.zeros((pad - lo, D), F32)
        for rr in range(tm // R):
            base = lo + rr * R
            hc = h_scr[base:base + R, :]
            if latent:
                q = D // 4
                col = lax.broadcasted_iota(jnp.int32, (R, q), 0)
                grow = (i * (tm // R) + rr) & (rows - 1)
                hs = jnp.concatenate([
                    jnp.where(col > 0, h_scr[base - 1:base - 1 + R, 0:q], 0.0),
                    jnp.where(col < R - 1, h_scr[base + 1:base + 1 + R, q:2 * q], 0.0),
                    jnp.where(grow > 0, h_scr[base - R:base, 2 * q:3 * q], 0.0),
                    jnp.where(grow < rows - 1, h_scr[base + R:base + 2 * R, 3 * q:], 0.0)], axis=1)
            else:
                half = D // 2
                hs = jnp.concatenate([h_scr[base - 1:base - 1 + R, :half],
                                      h_scr[base + 1:base + 1 + R, half:]], axis=1)
            xx = hs - hc
            xx_scr[rr * R:(rr + 1) * R, :] = xx
            a_scr[0, rr * R:(rr + 1) * R, :] = (hc + xx * mu_ref[0:1, :]).astype(BF16)

    def mix(j):
        lo = pad // 2
        for rr in range(tm // R):
            rs_ = slice(rr * R, (rr + 1) * R)
            a_scr[j, rs_, :] = (h_scr[lo + rr * R:lo + (rr + 1) * R, :] + xx_scr[rs_, :] * mu_ref[j:j + 1, :]).astype(BF16)

    def r_step():
        o_ref[...] = _dot(a_scr[0], w_ref[...]).astype(o_ref.dtype)

    def k_step():
        acc = _dot(a_scr[2], w_ref[...])
        o_ref[...] = acc.astype(o_ref.dtype)
        kkf = acc * kkw_ref[...]
        sq = kkf * kkf
        ones = ones_ref[...]
        parts = []
        for s in range(TN1 // MXU_DIM):
            parts.append(_dot(sq[:, s * MXU_DIM:(s + 1) * MXU_DIM].astype(BF16), ones))
        ss = jnp.concatenate(parts, axis=1)
        kk_ref[...] = (kkf / jnp.maximum(jnp.sqrt(ss), 1e-12)).astype(kk_ref.dtype)

    def v_step():
        o_ref[...] = _dot(a_scr[3], w_ref[...]).astype(o_ref.dtype)

    def lora_step():
        c1, c2, c3 = LORA_BLK, 2 * LORA_BLK, 3 * LORA_BLK
        o_ref[:, :c1] = jnp.tanh(_dot(a_scr[1], w_ref[:, :c1])).astype(o_ref.dtype)
        o_ref[:, c1:c2] = _dot(a_scr[4], w_ref[:, c1:c2]).astype(o_ref.dtype)
        o_ref[:, c2:c3] = _sigmoid(_dot(a_scr[5], w_ref[:, c2:c3])).astype(o_ref.dtype)
        o_ref[:, c3:] = jnp.zeros((o_ref.shape[0], TN1 - c3), o_ref.dtype)

    steps = [r_step] * nb + [k_step] * nb + [v_step] * nb + [lora_step]
    first_use = {2: nb, 3: 2 * nb, 1: 3 * nb, 4: 3 * nb, 5: 3 * nb}
    pending = sorted(first_use, key=lambda j: first_use[j])
    for t, step in enumerate(steps):
        side = [j for j in pending if first_use[j] == t + 1] or pending[:1]
        pending = [j for j in pending if j not in side]

        @pl.when(n == t)
        def _(step=step, side=side):
            step()
            for j in side:
                mix(j)


def _rwkv_stage1(x, mod, row_fn, gain, mu, k_k, wcat, *, latent, rows, tm):
    M, D = x.shape
    nb = D // TN1
    N1 = wcat.shape[1]
    nsteps = N1 // TN1
    hb = tm // GRID_W if latent else 1
    hrows = GRID_W if latent else 8
    nhb = M // hrows
    kcol = lambda n: jnp.clip(n - nb, 0, nb - 1)
    kern = functools.partial(_s1_kernel, latent=latent, rows=rows, tm=tm, D=D, nb=nb)
    return pl.pallas_call(
        kern,
        grid=(M // tm, nsteps),
        in_specs=[
            pl.BlockSpec((hrows, D), lambda i, n: (jnp.maximum(i * hb - 1, 0), 0)),
            pl.BlockSpec((tm, D), lambda i, n: (i, 0)),
            pl.BlockSpec((hrows, D), lambda i, n: (jnp.minimum(i * hb + hb, nhb - 1), 0)),
            pl.BlockSpec((1, D), lambda i, n: (0, 0)),
            pl.BlockSpec((None, 1, D), lambda i, n: (row_fn(i), 0, 0)),
            pl.BlockSpec((None, 1, D), lambda i, n: (row_fn(i), 0, 1)),
            pl.BlockSpec((6, D), lambda i, n: (0, 0)),
            pl.BlockSpec((1, TN1), lambda i, n: (0, kcol(n))),
            pl.BlockSpec((MXU_DIM, MXU_DIM), lambda i, n: (0, 0)),
            pl.BlockSpec((D, TN1), lambda i, n: (0, n)),
        ],
        out_specs=[
            pl.BlockSpec((tm, TN1), lambda i, n: (i, n)),
            pl.BlockSpec((tm, TN1), lambda i, n: (i, kcol(n))),
        ],
        out_shape=[jax.ShapeDtypeStruct((M, N1), BF16), jax.ShapeDtypeStruct((M, D), BF16)],
        scratch_shapes=[pltpu.VMEM((6, tm, D), BF16),
                        pltpu.VMEM((tm + 2 * hrows, D), F32),
                        pltpu.VMEM((tm, D), F32)],
        compiler_params=_cparams(("arbitrary", "arbitrary")),
        name="rwkv_stage1",
    )(x, x, x, gain, mod, mod, mu, k_k, _group_ones(MXU_DIM, RWKV_HEAD), wcat)


_NT = (((1,), (1,)), ((), ()))
_TN = (((0,), (0,)), ((), ()))


def _mm(a, b, dims=None):
    a = a.astype(BF16)
    b = b.astype(BF16)
    if dims is None:
        return jnp.dot(a, b, preferred_element_type=F32)
    return lax.dot_general(a, b, dims, preferred_element_type=F32)


_DONE = object()
_FRONT_LEAD = 1
_ROW_STAGGER = 1


def _scan_kernel(r_ref, k_ref, v_ref, kk_ref, lo_ref, w2_ref, a2_ref, w0_ref, a0_ref, ka_ref, h0_ref,
                 y_ref, a_ref, hT_ref, h_scr, *, G, nsub):
    d = pl.program_id(0)
    c = pl.program_id(3)
    nc = pl.num_programs(3)
    L = CHUNK
    pairs = range(G)
    nsteps = int(math.log2(L))

    @pl.when(c == 0)
    def _():
        h_scr[...] = h0_ref[...]

    sgn = 1 - 2 * d
    lane = lax.broadcasted_iota(jnp.int32, (L, PAIR), 1)
    m0 = lane < RWKV_HEAD
    ti = lax.broadcasted_iota(jnp.int32, (L, L), 0)
    si = lax.broadcasted_iota(jnp.int32, (L, L), 1)
    tri = jnp.where((si - ti) * sgn <= 0, 1.0, 0.0).astype(BF16)
    i2 = lax.broadcasted_iota(jnp.int32, (2 * L, 2 * L), 0)
    j2 = lax.broadcasted_iota(jnp.int32, (2 * L, 2 * L), 1)
    dif = jnp.where((i2 // L) == (j2 // L), ((j2 % L) - (i2 % L)) * sgn, 1)
    strict = dif < 0
    incl = dif <= 0
    r2 = lax.broadcasted_iota(jnp.int32, (PAIR, PAIR), 0)
    c2 = lax.broadcasted_iota(jnp.int32, (PAIR, PAIR), 1)
    bd = (r2 // RWKV_HEAD) == (c2 // RWKV_HEAD)

    def sm(x):
        return jnp.concatenate([jnp.where(m0, x, 0.0), jnp.where(m0, 0.0, x)], axis=0)

    def sl(x, g):
        return x[:, g * PAIR:(g + 1) * PAIR]

    def mm2(lhs, rhs):
        out = []
        for g in range(0, len(lhs), 2):
            ra, rb = rhs[g].astype(BF16), rhs[g + 1].astype(BF16)
            z = jnp.zeros_like(ra)
            bd2 = jnp.concatenate([jnp.concatenate([ra, z], axis=1), jnp.concatenate([z, rb], axis=1)], axis=0)
            both = _mm(jnp.concatenate([lhs[g].astype(BF16), lhs[g + 1].astype(BF16)], axis=1), bd2)
            out += [both[:, :PAIR], both[:, PAIR:]]
        return out

    def rows(q):
        return pl.ds(pl.multiple_of((q + d * (nsub - 1 - 2 * q)) * L, L), L)

    res = [None] * nsub

    def front(q):
        rw = rows(q)
        z = _dot(lo_ref[rw, :LORA_BLK], w2_ref[...]) + w0_ref[...]
        za = _dot(lo_ref[rw, LORA_BLK:], a2_ref[...]) + a0_ref[...]
        yield
        ld = -math.exp(-0.5) * _sigmoid(z)
        a = _sigmoid(za)
        a_ref[rw, :] = a.astype(a_ref.dtype)
        p0, p1 = _split2(ld)
        yield
        cum = _dot(tri, p0) + _dot(tri, p1)
        PL = jnp.exp(jnp.sum(ld, axis=0, keepdims=True))
        yield
        invP = jnp.exp(-cum)
        kk = kk_ref[rw, :].astype(F32)
        v = v_ref[rw, :].astype(F32)
        At = -kk * jnp.exp(cum - ld)
        Bt = (kk * a) * invP
        Kt = (k_ref[rw, :].astype(F32) * (1.0 + (a - 1.0) * ka_ref[...])) * invP
        Rt = r_ref[rw, :].astype(F32) * jnp.exp(cum)
        Bh = Bt * PL
        Kh = Kt * PL
        yield
        As = [sm(sl(At, g)) for g in pairs]
        Rs = [sm(sl(Rt, g)) for g in pairs]
        Vs = [sm(sl(v, g)) for g in pairs]
        Gc = [_mm(jnp.concatenate([As[g], Rs[g]], axis=0),
                  jnp.concatenate([sl(Bt, g), sl(Kt, g)], axis=0), _NT) for g in pairs]
        yield
        Gr = [pltpu.roll(Gc[g], L, axis=1) for g in pairs]

        def place(g, r0, first):
            a, b = (Gc[g], Gr[g]) if first else (Gr[g], Gc[g])
            return jnp.concatenate([a[r0:r0 + L], b[r0 + L:r0 + 2 * L]], axis=0)

        Np = [jnp.where(strict, place(g, 0, True), 0.0).astype(BF16) for g in pairs]
        Aak = [jnp.where(strict, place(g, 0, False), 0.0) for g in pairs]
        Mr = [jnp.concatenate([jnp.where(incl, place(g, 2 * L, True), 0.0),
                               jnp.where(incl, place(g, 2 * L, False), 0.0)], axis=1).astype(BF16) for g in pairs]
        AV = mm2(Aak, Vs)
        yield
        pw = [Np]
        for s in range(nsteps - 1):
            pw.append([t.astype(BF16) for t in mm2(pw[-1], pw[-1])])
            yield
        plc = [jnp.transpose(jnp.broadcast_to(sl(PL, g), (PAIR, PAIR))) for g in pairs]
        res[q] = (As, Rs, Vs, Mr, AV, pw, Bh, Kh, v, plc)
        yield

    state = [[h_scr[g] for g in pairs]]

    def back(q):
        As, Rs, Vs, Mr, AV, pw, Bh, Kh, v, plc = res[q]
        H = state[0]
        AH = mm2([jnp.concatenate([As[g], Rs[g]], axis=0) for g in pairs], H)
        yield
        Us = [AH[g][:2 * L] + AV[g] for g in pairs]
        for s in range(nsteps):
            Us = [Us[g] + _mm(pw[s][g], Us[g]) for g in pairs]
            yield
        Ys = [AH[g][2 * L:] + _mm(Mr[g], jnp.concatenate([Us[g], Vs[g]], axis=0)) for g in pairs]
        upd = [_mm(jnp.concatenate([sl(Bh, g), sl(Kh, g)], axis=0),
                   jnp.concatenate([Us[g][:L] + Us[g][L:], sl(v, g)], axis=0), _TN) for g in pairs]
        yield
        rw = rows(q)
        for g in pairs:
            y_ref[rw, g * PAIR:(g + 1) * PAIR] = (Ys[g][:L] + Ys[g][L:]).astype(y_ref.dtype)
        state[0] = [H[g] * plc[g] + jnp.where(bd, upd[g], 0.0) for g in pairs]
        yield

    def run(primary, others):
        while next(primary, _DONE) is not _DONE:
            for o in others:
                next(o, _DONE)

    fronts = [front(q) for q in range(nsub)]
    for _ in range(_FRONT_LEAD):
        next(fronts[0])
    for q in range(nsub):
        nxt = fronts[q + 1:q + 2]
        run(fronts[q], nxt)
        run(back(q), nxt)
    for g in pairs:
        h_scr[g] = state[0][g]

    @pl.when(c == nc - 1)
    def _():
        hT_ref[...] = h_scr[...]


def _wkv_scan(s1, kk, w2d, a2d, w0, a0, k_a, h0, B, T, D):
    G = min(SCAN_PAIRS, D // PAIR)
    W = G * PAIR
    nsub = SCAN_CHUNKS
    L = nsub * CHUNK
    nc = T // L
    npg = D // W
    N1 = s1.shape[1]
    s1 = s1.reshape(B, T, N1)
    kk = kk.reshape(B, T, D)
    chunk = lambda d, c: c + d * (nc - 1 - 2 * c)
    tok = lambda off: pl.BlockSpec((None, L, W), lambda d, b, p, c: (b, chunk(d, c), off + p))
    dirw = lambda rows: pl.BlockSpec((None, rows, W), lambda d, b, p, c: (d, 0, p))
    state = pl.BlockSpec((None, None, G, PAIR, PAIR), lambda d, b, p, c: (d, b, p, 0, 0))
    out_tok = pl.BlockSpec((None, None, L, W), lambda d, b, p, c: (d, b, chunk(d, c), p))
    y, a, hT = pl.pallas_call(
        functools.partial(_scan_kernel, G=G, nsub=nsub),
        grid=(2, B, npg, nc),
        in_specs=[
            tok(0), tok(npg), tok(2 * npg),
            pl.BlockSpec((None, L, W), lambda d, b, p, c: (b, chunk(d, c), p)),
            pl.BlockSpec((None, L, 2 * LORA_BLK), lambda d, b, p, c: (b, chunk(d, c), 3 * D // (2 * LORA_BLK))),
            dirw(LORA_BLK), dirw(LORA_BLK), dirw(1), dirw(1),
            pl.BlockSpec((1, W), lambda d, b, p, c: (0, p)),
            state,
        ],
        out_specs=[out_tok, out_tok, state],
        out_shape=[jax.ShapeDtypeStruct((2, B, T, D), BF16),
                   jax.ShapeDtypeStruct((2, B, T, D), BF16),
                   jax.ShapeDtypeStruct((2, B, D // PAIR, PAIR, PAIR), F32)],
        scratch_shapes=[pltpu.VMEM((G, PAIR, PAIR), F32)],
        compiler_params=_cparams(("arbitrary", "arbitrary", "arbitrary", "arbitrary")),
        name="wkv_scan",
    )(s1, s1, s1, kk, s1, w2d, a2d, w0, a0, k_a, h0)
    return y.reshape(2, B * T, D), a.reshape(2, B * T, D), hT


def _rwkv_out_kernel(y0_ref, y1_ref, r_ref, k_ref, v_ref, a0_ref, a1_ref, sg_ref, g2_ref,
                     lnw_ref, lnb_ref, rk_ref, ka_ref, ones_ref, w_ref, x_ref, gate_ref,
                     o_ref, *, D):
    ones = ones_ref[...]
    inv = 1.0 / RWKV_HEAD
    sg = sg_ref[:, :LORA_BLK]
    acc = None
    for s in range(D // MXU_DIM):
        cs = slice(s * MXU_DIM, (s + 1) * MXU_DIM)
        f = lambda ref: ref[:, cs].astype(F32)
        wkv = f(y0_ref) + f(y1_ref)
        cen = wkv - _dot(wkv.astype(BF16), ones) * inv
        var = _dot((cen * cen).astype(BF16), ones) * inv
        yn = cen * lax.rsqrt(var + GN_EPS) * lnw_ref[:, cs] + lnb_ref[:, cs]
        ks = f(k_ref) * (2.0 + (f(a0_ref) + f(a1_ref) - 2.0) * ka_ref[:, cs])
        bonus = _dot((f(r_ref) * ks * rk_ref[:, cs]).astype(BF16), ones) * f(v_ref)
        gate = _dot(sg, g2_ref[:, cs])
        part = _dot(((yn + bonus) * gate).astype(BF16), w_ref[cs, :])
        acc = part if acc is None else acc + part
    o_ref[...] = x_ref[...] + gate_ref[...] * acc


def _rwkv_out(y, a, s1, g2, x, mod, row_fn, ln_w, ln_b, r_k, k_a, w_o, tm):
    M, D = x.shape
    big = lambda col: pl.BlockSpec((tm, D), lambda i: (i, col))
    dirblk = lambda d: pl.BlockSpec((None, tm, D), lambda i: (d, i, 0))
    row = pl.BlockSpec((1, D), lambda i: (0, 0))
    return pl.pallas_call(
        functools.partial(_rwkv_out_kernel, D=D),
        grid=(M // tm,),
        in_specs=[
            dirblk(0), dirblk(1),
            big(0), big(1), big(2),
            dirblk(0), dirblk(1),
            pl.BlockSpec((tm, 2 * LORA_BLK), lambda i: (i, 3 * D // (2 * LORA_BLK) + 1)),
            pl.BlockSpec((LORA_BLK, D), lambda i: (0, 0)),
            row, row, row, row,
            pl.BlockSpec((MXU_DIM, MXU_DIM), lambda i: (0, 0)),
            pl.BlockSpec((D, D), lambda i: (0, 0)),
            pl.BlockSpec((tm, D), lambda i: (i, 0)),
            pl.BlockSpec((None, 1, D), lambda i: (row_fn(i), 0, 2)),
        ],
        out_specs=pl.BlockSpec((tm, D), lambda i: (i, 0)),
        out_shape=jax.ShapeDtypeStruct((M, D), F32),
        compiler_params=_cparams(("arbitrary",)),
        name="rwkv_out",
    )(y, y, s1, s1, s1, a, a, s1, g2, ln_w, ln_b, r_k, k_a, _group_ones(MXU_DIM, RWKV_HEAD), w_o, x, mod)


def _ffn_kernel(x_ref, g_ref, sh_ref, sc_ref, gate_ref, w1_ref, w3_ref, w2_ref, o_ref, a_scr, *, tf):
    f = pl.program_id(1)
    nf = pl.num_programs(1)

    def hidden_block(a):
        acts = []
        for s in range(tf // MXU_DIM):
            cs = slice(s * MXU_DIM, (s + 1) * MXU_DIM)
            h1 = _dot(a, w1_ref[:, cs])
            h3 = _dot(a, w3_ref[:, cs])
            acts.append((_silu(h1) * h3).astype(BF16))
        return _dot(jnp.concatenate(acts, axis=1), w2_ref[...])

    @pl.when(f == 0)
    def _():
        a = _modulate(x_ref[...], g_ref[...], sh_ref[...], sc_ref[...]).astype(BF16)
        a_scr[...] = a
        o_ref[...] = hidden_block(a)

    @pl.when((f > 0) & (f < nf - 1))
    def _():
        o_ref[...] += hidden_block(a_scr[...])

    @pl.when(f == nf - 1)
    def _():
        o_ref[...] = x_ref[...] + gate_ref[...] * (o_ref[...] + hidden_block(a_scr[...]))


def _ffn(x, mod, row_fn, gain, w1b, w3b, w2, tm, tf):
    M, D = x.shape
    nf = w1b.shape[1] // tf
    modspec = lambda ch: pl.BlockSpec((None, 1, D), lambda i, f: (row_fn(i), 0, ch))
    return pl.pallas_call(
        functools.partial(_ffn_kernel, tf=tf),
        grid=(M // tm, nf),
        in_specs=[
            pl.BlockSpec((tm, D), lambda i, f: (i, 0), pipeline_mode=pl.Buffered(1)),
            pl.BlockSpec((1, D), lambda i, f: (0, 0)),
            modspec(3), modspec(4), modspec(5),
            pl.BlockSpec((D, tf), lambda i, f: (0, f)),
            pl.BlockSpec((D, tf), lambda i, f: (0, f)),
            pl.BlockSpec((tf, D), lambda i, f: (f, 0)),
        ],
        out_specs=pl.BlockSpec((tm, D), lambda i, f: (i, 0)),
        out_shape=jax.ShapeDtypeStruct((M, D), F32),
        scratch_shapes=[pltpu.VMEM((tm, D), BF16)],
        compiler_params=_cparams(("arbitrary", "arbitrary")),
        name="ffn",
    )(x, gain, mod, mod, mod, w1b, w3b, w2)


def _qkv_kernel(x_ref, g_ref, sh_ref, sc_ref, w_ref, hg_ref, cos_ref, sin_ref, ones_ref, o_ref, a_scr,
                *, nrope, nnorm, tn):
    n = pl.program_id(1)

    def normed_block(a):
        acc = _dot(a, w_ref[...])
        hg = hg_ref[...]
        ones = ones_ref[...]
        for h in range(tn // MXU_DIM):
            blk = acc[:, h * MXU_DIM:(h + 1) * MXU_DIM]
            ms = _dot((blk * blk).astype(BF16), ones) * (1.0 / NA_HEAD)
            blk = blk * lax.rsqrt(ms + RMS_EPS)
            for s in range(MXU_DIM // NA_HEAD):
                t = blk[:, s * NA_HEAD:(s + 1) * NA_HEAD] * hg
                if nrope:
                    t = t * cos_ref[...] + pltpu.roll(t, NA_HEAD // 2, axis=1) * sin_ref[...]
                c0 = h * MXU_DIM + s * NA_HEAD
                o_ref[:, c0:c0 + NA_HEAD] = t.astype(o_ref.dtype)

    @pl.when(n == 0)
    def _():
        a = _modulate(x_ref[...], g_ref[...], sh_ref[...], sc_ref[...]).astype(BF16)
        a_scr[...] = a
        normed_block(a)

    @pl.when((n > 0) & (n < nnorm))
    def _():
        normed_block(a_scr[...])

    @pl.when(n >= nnorm)
    def _():
        o_ref[...] = _dot(a_scr[...], w_ref[...]).astype(o_ref.dtype)


def _na_qkv(x, mod, row_fn, gain, w, head_gain, cos, sin, *, nrope, nnorm, tm, tn):
    assert nrope in (0, nnorm)
    M, D = x.shape
    N = w.shape[1]
    tpb = cos.shape[0] // tm
    hrow = lambda n: jnp.minimum(n, head_gain.shape[0] - 1)
    return pl.pallas_call(
        functools.partial(_qkv_kernel, nrope=nrope, nnorm=nnorm, tn=tn),
        grid=(M // tm, N // tn),
        in_specs=[
            pl.BlockSpec((tm, D), lambda i, n: (i, 0)),
            pl.BlockSpec((1, D), lambda i, n: (0, 0)),
            pl.BlockSpec((None, 1, D), lambda i, n: (row_fn(i), 0, 0)),
            pl.BlockSpec((None, 1, D), lambda i, n: (row_fn(i), 0, 1)),
            pl.BlockSpec((D, tn), lambda i, n: (0, n)),
            pl.BlockSpec((None, 1, NA_HEAD), lambda i, n: (hrow(n), 0, 0)),
            pl.BlockSpec((tm, NA_HEAD), lambda i, n: (i % tpb, 0)),
            pl.BlockSpec((tm, NA_HEAD), lambda i, n: (i % tpb, 0)),
            pl.BlockSpec((MXU_DIM, MXU_DIM), lambda i, n: (0, 0)),
        ],
        out_specs=pl.BlockSpec((tm, tn), lambda i, n: (i, n)),
        out_shape=jax.ShapeDtypeStruct((M, N), BF16),
        scratch_shapes=[pltpu.VMEM((tm, D), BF16)],
        compiler_params=_cparams(("arbitrary", "arbitrary")),
        name="na_qkv",
    )(x, gain, mod, mod, w, head_gain, cos, sin, _group_ones(MXU_DIM, NA_HEAD))


def _na_kernel(q_ref, k_ref, v_ref, kc_ref, vc_ref, bias_ref, o_ref, *, rows, rb):
    j0 = pl.program_id(2) * rb
    win_h = min(WIN_H, rows)
    nk = win_h * GRID_W
    qrows = range(rb)
    rs = lambda t, j: t[j * GRID_W:(j + 1) * GRID_W]

    qa = q_ref[...]
    s_ctx = lax.dot_general(qa, kc_ref[...], _NT, preferred_element_type=F32)

    def row(j):
        r0 = jnp.clip(j0 + j - WIN_H // 2, 0, rows - win_h)
        off = pl.multiple_of(r0 * GRID_W, GRID_W)
        s_lat = (lax.dot_general(rs(qa, j), k_ref[pl.ds(off, nk), :], _NT, preferred_element_type=F32)
                 + bias_ref[r0 - (j0 + j) + WIN_H - 1])
        yield
        sc = rs(s_ctx, j)
        m = jnp.maximum(jnp.max(s_lat, axis=-1, keepdims=True), jnp.max(sc, axis=-1, keepdims=True))
        yield
        p_lat = jnp.exp(s_lat - m)
        p_ctx = jnp.exp(sc - m)
        yield
        den = jnp.sum(p_lat, axis=-1, keepdims=True) + jnp.sum(p_ctx, axis=-1, keepdims=True)
        o = _dot(p_lat.astype(BF16), v_ref[pl.ds(off, nk), :]) + _dot(p_ctx.astype(BF16), vc_ref[...])
        yield
        o_ref[j * GRID_W:(j + 1) * GRID_W, :] = (o / den).astype(o_ref.dtype)
        yield

    gens = [row(j) for j in qrows]
    done = [False] * rb
    tick = 0
    while not all(done):
        for j in qrows:
            if not done[j] and tick >= j * _ROW_STAGGER:
                done[j] = next(gens[j], _DONE) is _DONE
        tick += 1


def _na_attention(qkv, kvc, bias, B, T, D, rb):
    H = D // NA_HEAD
    rows = T // GRID_W
    C = kvc.shape[1]
    nbias = bias.shape[1]
    return pl.pallas_call(
        functools.partial(_na_kernel, rows=rows, rb=rb),
        grid=(B, H, rows // rb),
        in_specs=[
            pl.BlockSpec((None, rb * GRID_W, NA_HEAD), lambda b, h, j: (b, j, h)),
            pl.BlockSpec((None, T, NA_HEAD), lambda b, h, j: (b, 0, H + h)),
            pl.BlockSpec((None, T, NA_HEAD), lambda b, h, j: (b, 0, 2 * H + h)),
            pl.BlockSpec((None, C, NA_HEAD), lambda b, h, j: (b, 0, h)),
            pl.BlockSpec((None, C, NA_HEAD), lambda b, h, j: (b, 0, H + h)),
            pl.BlockSpec((None, nbias, GRID_W, bias.shape[3]), lambda b, h, j: (h, 0, 0, 0)),
        ],
        out_specs=pl.BlockSpec((None, rb * GRID_W, NA_HEAD), lambda b, h, j: (b, j, h)),
        out_shape=jax.ShapeDtypeStruct((B, T, D), BF16),
        compiler_params=_cparams(("arbitrary", "arbitrary", "arbitrary")),
        name="na_attention",
    )(qkv, qkv, qkv, kvc, kvc, bias)


def _proj_res_kernel(a_ref, w_ref, x_ref, gate_ref, o_ref):
    o_ref[...] = x_ref[...] + gate_ref[...] * _dot(a_ref[...], w_ref[...])


def _proj_residual(a, w, x, mod, row_fn, chunk, tm):
    M, K = a.shape
    N = w.shape[1]
    return pl.pallas_call(
        _proj_res_kernel,
        grid=(M // tm,),
        in_specs=[
            pl.BlockSpec((tm, K), lambda i: (i, 0)),
            pl.BlockSpec((K, N), lambda i: (0, 0)),
            pl.BlockSpec((tm, N), lambda i: (i, 0)),
            pl.BlockSpec((None, 1, N), lambda i: (row_fn(i), 0, chunk)),
        ],
        out_specs=pl.BlockSpec((tm, N), lambda i: (i, 0)),
        out_shape=jax.ShapeDtypeStruct((M, N), F32),
        compiler_params=_cparams(("arbitrary",)),
        name="proj_residual",
    )(a, w, x, mod)


_QUART = NA_HEAD // 4
_ROPE_PERM = np.concatenate([np.arange(0, _QUART), np.arange(2 * _QUART, 3 * _QUART),
                             np.arange(_QUART, 2 * _QUART), np.arange(3 * _QUART, NA_HEAD)])


def _rope_layout(w_qk):
    K, N = w_qk.shape
    return w_qk.reshape(K, N // NA_HEAD, NA_HEAD)[:, :, _ROPE_PERM].reshape(K, N)


def _rope_tables(T):
    t = np.arange(T)
    half = NA_HEAD // 2
    freqs = ROPE_THETA ** (-np.arange(0, half, 2, dtype=np.float64) / half)
    ang_r = (t // GRID_W).astype(np.float64)[:, None] * freqs
    ang_c = (t % GRID_W).astype(np.float64)[:, None] * freqs
    cos = np.concatenate([np.cos(ang_r), np.cos(ang_c), np.cos(ang_r), np.cos(ang_c)], axis=1)
    sin = np.concatenate([-np.sin(ang_r), -np.sin(ang_c), np.sin(ang_r), np.sin(ang_c)], axis=1)
    return jnp.asarray(cos, F32), jnp.asarray(sin, F32)


def _na_bias(rpb, rows):
    win_h = min(WIN_H, rows)
    qc = np.arange(GRID_W)[:, None]
    kc = np.arange(GRID_W)[None, :]
    wstart = np.clip(qc - WIN_W // 2, 0, GRID_W - WIN_W)
    valid = (kc >= wstart) & (kc < wstart + WIN_W)
    rel = np.clip(kc - qc, -(WIN_W - 1), WIN_W - 1) + WIN_W - 1
    g = rpb[:, :, rel]
    g = jnp.where(valid[None, None], g, NEG_INF)
    tabs = []
    for i0 in range(2 * WIN_H - win_h):
        blk = g[:, i0:i0 + win_h]
        tabs.append(jnp.transpose(blk, (0, 2, 1, 3)).reshape(g.shape[0], GRID_W, win_h * GRID_W))
    return jnp.stack(tabs, axis=1)


def _pad_cols(w, n):
    return jnp.pad(w, ((0, 0), (0, n - w.shape[1])))


def _pad_rows(w, n):
    return jnp.pad(w, ((0, n - w.shape[0]), (0, 0)))


def kernel(x, c, ctx, c_ctx, ada_w, ada_b, norm1, norm2, rwkv_mu, rwkv_w_r, rwkv_w_k, rwkv_w_v, rwkv_w_o, rwkv_w0, rwkv_w1, rwkv_w2, rwkv_a0, rwkv_a1, rwkv_a2, rwkv_g1, rwkv_g2, rwkv_k_k, rwkv_k_a, rwkv_r_k, rwkv_ln_w, rwkv_ln_b, na_w_qkv, na_w_o, na_q_gain, na_k_gain, na_rpb, ffn_w1, ffn_w3, ffn_w2):
    B, T, D = x.shape
    C = ctx.shape[1]
    rows = T // GRID_W
    depth = ada_w.shape[0]
    n_mix = 2
    TM = 512
    TMO = 256
    TF = 512 if ffn_w1.shape[-1] % 512 == 0 else 256
    TMF = min(1024, T)

    cvec = jnp.concatenate([c, c_ctx[None, :], jnp.zeros((8 - B - 1, D), F32)], axis=0)
    mods = _ada(cvec, ada_w, ada_b)
    lat_row = lambda tm: (lambda i: i // (T // tm))
    ctx_row = lambda i: B

    xl = x.reshape(B * T, D)
    xc = ctx.reshape(B * C, D)

    for i in range(depth):
        last = i == depth - 1
        j = i // n_mix
        mod = mods[i].reshape(8, 1, 6 * D)
        g1 = norm1[i].reshape(1, D)
        g2 = norm2[i].reshape(1, D)
        w1 = ffn_w1[i].astype(BF16)
        w3 = ffn_w3[i].astype(BF16)
        w2 = ffn_w2[i].astype(BF16)
        if i % n_mix == 0:
            lw = rwkv_w1.shape[-1]
            la = rwkv_a1.shape[-1]
            assert max(2 * lw, 2 * la, rwkv_g1.shape[-1]) <= LORA_BLK and D % TN1 == 0
            wcat = jnp.concatenate([
                rwkv_w_r[j], rwkv_w_k[j], rwkv_w_v[j],
                _pad_cols(jnp.concatenate([rwkv_w1[j, 0], rwkv_w1[j, 1]], axis=1), LORA_BLK),
                _pad_cols(jnp.concatenate([rwkv_a1[j, 0], rwkv_a1[j, 1]], axis=1), LORA_BLK),
                _pad_cols(rwkv_g1[j], 2 * LORA_BLK)], axis=1).astype(BF16)
            z = lambda n: jnp.zeros((n, D), F32)
            w2d = jnp.stack([_pad_rows(rwkv_w2[j, 0], LORA_BLK),
                             _pad_rows(jnp.concatenate([z(lw), rwkv_w2[j, 1]], axis=0), LORA_BLK)]).astype(BF16)
            a2d = jnp.stack([_pad_rows(rwkv_a2[j, 0], LORA_BLK),
                             _pad_rows(jnp.concatenate([z(la), rwkv_a2[j, 1]], axis=0), LORA_BLK)]).astype(BF16)
            wg2 = _pad_rows(rwkv_g2[j], LORA_BLK).astype(BF16)
            w0 = rwkv_w0[j].reshape(2, 1, D)
            a0 = rwkv_a0[j].reshape(2, 1, D)
            k_k = rwkv_k_k[j].reshape(1, D)
            k_a = rwkv_k_a[j].reshape(1, D)
            r_k = rwkv_r_k[j].reshape(1, D)
            ln_w = rwkv_ln_w[j].reshape(1, D)
            ln_b = rwkv_ln_b[j].reshape(1, D)
            w_o = rwkv_w_o[j].astype(BF16)

            s1c, kkc = _rwkv_stage1(xc, mod, ctx_row, g1, rwkv_mu[j], k_k, wcat, latent=False, rows=rows, tm=C)
            h0 = jnp.zeros((2, B, D // PAIR, PAIR, PAIR), F32)
            yc, ac, hc = _wkv_scan(s1c, kkc, w2d, a2d, w0, a0, k_a, h0, B, C, D)
            s1l, kkl = _rwkv_stage1(xl, mod, lat_row(TM), g1, rwkv_mu[j], k_k, wcat, latent=True, rows=rows, tm=TM)
            yl, al, _ = _wkv_scan(s1l, kkl, w2d, a2d, w0, a0, k_a, hc, B, T, D)
            xl = _rwkv_out(yl, al, s1l, wg2, xl, mod, lat_row(TMO), ln_w, ln_b, r_k, k_a, w_o, TMO)
            if not last:
                xc = _rwkv_out(yc, ac, s1c, wg2, xc, mod, ctx_row, ln_w, ln_b, r_k, k_a, w_o, min(TMO, C))
        else:
            H = D // NA_HEAD
            cos, sin = _rope_tables(T)
            scale = NA_HEAD ** -0.5
            hg = jnp.stack([(na_q_gain[j] * scale)[_ROPE_PERM], na_k_gain[j][_ROPE_PERM]]).reshape(2, 1, NA_HEAD)
            tn = min(1024, D)
            nq = D // tn
            wqkv = jnp.concatenate([_rope_layout(na_w_qkv[j][:, :2 * D]), na_w_qkv[j][:, 2 * D:]],
                                   axis=1).astype(BF16)
            hg_blocks = jnp.concatenate([jnp.broadcast_to(hg[0], (nq, 1, NA_HEAD)),
                                         jnp.broadcast_to(hg[1], (nq, 1, NA_HEAD))], axis=0)
            qkv = _na_qkv(xl, mod, lat_row(TMF), g1, wqkv, hg_blocks, cos, sin, nrope=2 * nq, nnorm=2 * nq,
                          tm=TMF, tn=tn)
            kg_blocks = jnp.broadcast_to(hg[1], (nq, 1, NA_HEAD))
            kvc = _na_qkv(xc, mod, ctx_row, g1, wqkv[:, D:], kg_blocks, cos[:C], sin[:C], nrope=0, nnorm=nq,
                          tm=C, tn=tn)
            bias = _na_bias(na_rpb[j], rows)
            o = _na_attention(qkv.reshape(B, T, 3 * D), kvc.reshape(B, C, 2 * D), bias, B, T, D, min(32, rows))
            xl = _proj_residual(o.reshape(B * T, D), na_w_o[j].astype(BF16), xl, mod,
                                lat_row(TM), 2, TM)
            if not last:
                raise NotImplementedError("attention layer with context output")
        xl = _ffn(xl, mod, lat_row(TMF), g2, w1, w3, w2, TMF, TF)
        if not last:
            xc = _ffn(xc, mod, ctx_row, g2, w1, w3, w2, C, TF)
    return xl.reshape(B, T, D)
```

```python
import functools
import math

import numpy as np
import jax
import jax.numpy as jnp
from jax import lax
from jax.experimental import pallas as pl
from jax.experimental.pallas import tpu as pltpu

F32 = jnp.float32
BF16 = jnp.bfloat16

GRID_W = 64
RMS_EPS = 1e-6
RWKV_HEAD = 64
GN_EPS = 64e-5
NA_HEAD = 128
WIN_H = 8
WIN_W = 16
ROPE_THETA = 10000.0
NEG_INF = -1e30

MXU_DIM = 256
VMEM_LIMIT = 56 * 1024 * 1024

CHUNK = 64
PAIR = 2 * RWKV_HEAD
SCAN_PAIRS = 16
SCAN_CHUNKS = 4
LORA_BLK = 256
TN1 = 4 * LORA_BLK


def _cparams(sem):
    return pltpu.CompilerParams(dimension_semantics=sem, vmem_limit_bytes=VMEM_LIMIT)


def _dot(a, b):
    return jnp.dot(a, b, preferred_element_type=F32)


def _split2(x):
    hi = x.astype(BF16)
    lo = (x - hi.astype(F32)).astype(BF16)
    return hi, lo


def _rms_scale(x):
    return lax.rsqrt(jnp.mean(x * x, axis=-1, keepdims=True) + RMS_EPS)


def _modulate(x, gain, shift, scale):
    return (x * _rms_scale(x) * gain) * (1.0 + scale) + shift


def _sigmoid(x):
    return 1.0 / (1.0 + jnp.exp(-x))


def _silu(x):
    return x * _sigmoid(x)


def _group_ones(n, group):
    idx = np.arange(n) // group
    return jnp.asarray((idx[:, None] == idx[None, :]).astype(np.float32), dtype=BF16)


def _ada_kernel(c_ref, w_ref, b_ref, o_ref):
    s = _silu(c_ref[...]).astype(BF16)
    o_ref[...] = _dot(s, w_ref[...].astype(BF16)) + b_ref[...]


def _ada(cvec, ada_w, ada_b):
    depth, D, N = ada_w.shape
    R = cvec.shape[0]
    tn = 1024
    return pl.pallas_call(
        _ada_kernel,
        grid=(depth, N // tn),
        in_specs=[
            pl.BlockSpec((R, D), lambda l, n: (0, 0)),
            pl.BlockSpec((None, D, tn), lambda l, n: (l, 0, n)),
            pl.BlockSpec((None, 1, tn), lambda l, n: (l, 0, n)),
        ],
        out_specs=pl.BlockSpec((None, R, tn), lambda l, n: (l, 0, n)),
        out_shape=jax.ShapeDtypeStruct((depth, R, N), F32),
        compiler_params=_cparams(("arbitrary", "arbitrary")),
        name="ada",
    )(cvec, ada_w, ada_b.reshape(depth, 1, N))


def _s1_kernel(xp_ref, x_ref, xn_ref, g_ref, sh_ref, sc_ref, mu_ref, kkw_ref, ones_ref, w_ref,
               o_ref, kk_ref, a_scr, h_scr, xx_scr, *, latent, rows, tm, D, nb):
    i = pl.program_id(0)
    n = pl.program_id(1)
    R = GRID_W
    pad = h_scr.shape[0] - tm

    @pl.when(n == 0)
    def _():
        g = g_ref[...]
        sh = sh_ref[...]
        sc = sc_ref[...]
        lo = pad // 2
        h_scr[lo:lo + tm, :] = _modulate(x_ref[...], g, sh, sc)
        if latent:
            h_scr[0:lo, :] = _modulate(xp_ref[...], g, sh, sc)
            h_scr[lo + tm:, :] = _modulate(xn_ref[...], g, sh, sc)
        else:
            h_scr[0:lo, :] = jnp.zeros((lo, D), F32)
            h_scr[lo + tm:, :] = jnp.zeros((pad - lo, D), F32)
        for rr in range(tm // R):
            base = lo + rr * R
            hc = h_scr[base:base + R, :]
            if latent:
                q = D // 4
                col = lax.broadcasted_iota(jnp.int32, (R, q), 0)
                grow = (i * (tm // R) + rr) & (rows - 1)
                hs = jnp.concatenate([
                    jnp.where(col > 0, h_scr[base - 1:base - 1 + R, 0:q], 0.0),
                    jnp.where(col < R - 1, h_scr[base + 1:base + 1 + R, q:2 * q], 0.0),
                    jnp.where(grow > 0, h_scr[base - R:base, 2 * q:3 * q], 0.0),
                    jnp.where(grow < rows - 1, h_scr[base + R:base + 2 * R, 3 * q:], 0.0)], axis=1)
            else:
                half = D // 2
                hs = jnp.concatenate([h_scr[base - 1:base - 1 + R, :half],
                                      h_scr[base + 1:base + 1 + R, half:]], axis=1)
            xx = hs - hc
            xx_scr[rr * R:(rr + 1) * R, :] = xx
            a_scr[0, rr * R:(rr + 1) * R, :] = (hc + xx * mu_ref[0:1, :]).astype(BF16)

    def mix(j):
        lo = pad // 2
        for rr in range(tm // R):
            rs_ = slice(rr * R, (rr + 1) * R)
            a_scr[j, rs_, :] = (h_scr[lo + rr * R:lo + (rr + 1) * R, :] + xx_scr[rs_, :] * mu_ref[j:j + 1, :]).astype(BF16)

    def r_step():
        o_ref[...] = _dot(a_scr[0], w_ref[...]).astype(o_ref.dtype)

    def k_step():
        acc = _dot(a_scr[2], w_ref[...])
        o_ref[...] = acc.astype(o_ref.dtype)
        kkf = acc * kkw_ref[...]
        sq = kkf * kkf
        ones = ones_ref[...]
        parts = []
        for s in range(TN1 // MXU_DIM):
            parts.append(_dot(sq[:, s * MXU_DIM:(s + 1) * MXU_DIM].astype(BF16), ones))
        ss = jnp.concatenate(parts, axis=1)
        kk_ref[...] = (kkf / jnp.maximum(jnp.sqrt(ss), 1e-12)).astype(kk_ref.dtype)

    def v_step():
        o_ref[...] = _dot(a_scr[3], w_ref[...]).astype(o_ref.dtype)

    def lora_step():
        c1, c2, c3 = LORA_BLK, 2 * LORA_BLK, 3 * LORA_BLK
        o_ref[:, :c1] = jnp.tanh(_dot(a_scr[1], w_ref[:, :c1])).astype(o_ref.dtype)
        o_ref[:, c1:c2] = _dot(a_scr[4], w_ref[:, c1:c2]).astype(o_ref.dtype)
        o_ref[:, c2:c3] = _sigmoid(_dot(a_scr[5], w_ref[:, c2:c3])).astype(o_ref.dtype)
        o_ref[:, c3:] = jnp.zeros((o_ref.shape[0], TN1 - c3), o_ref.dtype)

    steps = [r_step] * nb + [k_step] * nb + [v_step] * nb + [lora_step]
    first_use = {2: nb, 3: 2 * nb, 1: 3 * nb, 4: 3 * nb, 5: 3 * nb}
    pending = sorted(first_use, key=lambda j: first_use[j])
    for t, step in enumerate(steps):
        side = [j for j in pending if first_use[j] == t + 1] or pending[:1]
        pending = [j for j in pending if j not in side]

        @pl.when(n == t)
        def _(step=step, side=side):
            step()
            for j in side:
                mix(j)


def _rwkv_stage1(x, mod, row_fn, gain, mu, k_k, wcat, *, latent, rows, tm):
    M, D = x.shape
    nb = D // TN1
    N1 = wcat.shape[1]
    nsteps = N1 // TN1
    hb = tm // GRID_W if latent else 1
    hrows = GRID_W if latent else 8
    nhb = M // hrows
    kcol = lambda n: jnp.clip(n - nb, 0, nb - 1)
    kern = functools.partial(_s1_kernel, latent=latent, rows=rows, tm=tm, D=D, nb=nb)
    return pl.pallas_call(
        kern,
        grid=(M // tm, nsteps),
        in_specs=[
            pl.BlockSpec((hrows, D), lambda i, n: (jnp.maximum(i * hb - 1, 0), 0)),
            pl.BlockSpec((tm, D), lambda i, n: (i, 0)),
            pl.BlockSpec((hrows, D), lambda i, n: (jnp.minimum(i * hb + hb, nhb - 1), 0)),
            pl.BlockSpec((1, D), lambda i, n: (0, 0)),
            pl.BlockSpec((None, 1, D), lambda i, n: (row_fn(i), 0, 0)),
            pl.BlockSpec((None, 1, D), lambda i, n: (row_fn(i), 0, 1)),
            pl.BlockSpec((6, D), lambda i, n: (0, 0)),
            pl.BlockSpec((1, TN1), lambda i, n: (0, kcol(n))),
            pl.BlockSpec((MXU_DIM, MXU_DIM), lambda i, n: (0, 0)),
            pl.BlockSpec((D, TN1), lambda i, n: (0, n)),
        ],
        out_specs=[
            pl.BlockSpec((tm, TN1), lambda i, n: (i, n)),
            pl.BlockSpec((tm, TN1), lambda i, n: (i, kcol(n))),
        ],
        out_shape=[jax.ShapeDtypeStruct((M, N1), BF16), jax.ShapeDtypeStruct((M, D), BF16)],
        scratch_shapes=[pltpu.VMEM((6, tm, D), BF16),
                        pltpu.VMEM((tm + 2 * hrows, D), F32),
                        pltpu.VMEM((tm, D), F32)],
        compiler_params=_cparams(("arbitrary", "arbitrary")),
        name="rwkv_stage1",
    )(x, x, x, gain, mod, mod, mu, k_k, _group_ones(MXU_DIM, RWKV_HEAD), wcat)


_NT = (((1,), (1,)), ((), ()))
_TN = (((0,), (0,)), ((), ()))


def _mm(a, b, dims=None):
    a = a.astype(BF16)
    b = b.astype(BF16)
    if dims is None:
        return jnp.dot(a, b, preferred_element_type=F32)
    return lax.dot_general(a, b, dims, preferred_element_type=F32)


_DONE = object()
_FRONT_LEAD = 1
_ROW_STAGGER = 1


def _scan_kernel(r_ref, k_ref, v_ref, kk_ref, lo_ref, w2_ref, a2_ref, w0_ref, a0_ref, ka_ref, h0_ref,
                 y_ref, a_ref, hT_ref, h_scr, *, G, nsub):
    d = pl.program_id(0)
    c = pl.program_id(3)
    nc = pl.num_programs(3)
    L = CHUNK
    pairs = range(G)
    nsteps = int(math.log2(L))

    @pl.when(c == 0)
    def _():
        h_scr[...] = h0_ref[...]

    sgn = 1 - 2 * d
    lane = lax.broadcasted_iota(jnp.int32, (L, PAIR), 1)
    m0 = lane < RWKV_HEAD
    ti = lax.broadcasted_iota(jnp.int32, (L, L), 0)
    si = lax.broadcasted_iota(jnp.int32, (L, L), 1)
    tri = jnp.where((si - ti) * sgn <= 0, 1.0, 0.0).astype(BF16)
    i2 = lax.broadcasted_iota(jnp.int32, (2 * L, 2 * L), 0)
    j2 = lax.broadcasted_iota(jnp.int32, (2 * L, 2 * L), 1)
    dif = jnp.where((i2 // L) == (j2 // L), ((j2 % L) - (i2 % L)) * sgn, 1)
    strict = dif < 0
    incl = dif <= 0
    r2 = lax.broadcasted_iota(jnp.int32, (PAIR, PAIR), 0)
    c2 = lax.broadcasted_iota(jnp.int32, (PAIR, PAIR), 1)
    bd = (r2 // RWKV_HEAD) == (c2 // RWKV_HEAD)

    def sm(x):
        return jnp.concatenate([jnp.where(m0, x, 0.0), jnp.where(m0, 0.0, x)], axis=0)

    def sl(x, g):
        return x[:, g * PAIR:(g + 1) * PAIR]

    def mm2(lhs, rhs):
        out = []
        for g in range(0, len(lhs), 2):
            ra, rb = rhs[g].astype(BF16), rhs[g + 1].astype(BF16)
            z = jnp.zeros_like(ra)
            bd2 = jnp.concatenate([jnp.concatenate([ra, z], axis=1), jnp.concatenate([z, rb], axis=1)], axis=0)
            both = _mm(jnp.concatenate([lhs[g].astype(BF16), lhs[g + 1].astype(BF16)], axis=1), bd2)
            out += [both[:, :PAIR], both[:, PAIR:]]
        return out

    def rows(q):
        return pl.ds(pl.multiple_of((q + d * (nsub - 1 - 2 * q)) * L, L), L)

    res = [None] * nsub

    def front(q):
        rw = rows(q)
        z = _dot(lo_ref[rw, :LORA_BLK], w2_ref[...]) + w0_ref[...]
        za = _dot(lo_ref[rw, LORA_BLK:], a2_ref[...]) + a0_ref[...]
        yield
        ld = -math.exp(-0.5) * _sigmoid(z)
        a = _sigmoid(za)
        a_ref[rw, :] = a.astype(a_ref.dtype)
        p0, p1 = _split2(ld)
        yield
        cum = _dot(tri, p0) + _dot(tri, p1)
        PL = jnp.exp(jnp.sum(ld, axis=0, keepdims=True))
        yield
        invP = jnp.exp(-cum)
        kk = kk_ref[rw, :].astype(F32)
        v = v_ref[rw, :].astype(F32)
        At = -kk * jnp.exp(cum - ld)
        Bt = (kk * a) * invP
        Kt = (k_ref[rw, :].astype(F32) * (1.0 + (a - 1.0) * ka_ref[...])) * invP
        Rt = r_ref[rw, :].astype(F32) * jnp.exp(cum)
        Bh = Bt * PL
        Kh = Kt * PL
        yield
        As = [sm(sl(At, g)) for g in pairs]
        Rs = [sm(sl(Rt, g)) for g in pairs]
        Vs = [sm(sl(v, g)) for g in pairs]
        Gc = [_mm(jnp.concatenate([As[g], Rs[g]], axis=0),
                  jnp.concatenate([sl(Bt, g), sl(Kt, g)], axis=0), _NT) for g in pairs]
        yield
        Gr = [pltpu.roll(Gc[g], L, axis=1) for g in pairs]

        def place(g, r0, first):
            a, b = (Gc[g], Gr[g]) if first else (Gr[g], Gc[g])
            return jnp.concatenate([a[r0:r0 + L], b[r0 + L:r0 + 2 * L]], axis=0)

        Np = [jnp.where(strict, place(g, 0, True), 0.0).astype(BF16) for g in pairs]
        Aak = [jnp.where(strict, place(g, 0, False), 0.0) for g in pairs]
        Mr = [jnp.concatenate([jnp.where(incl, place(g, 2 * L, True), 0.0),
                               jnp.where(incl, place(g, 2 * L, False), 0.0)], axis=1).astype(BF16) for g in pairs]
        AV = mm2(Aak, Vs)
        yield
        pw = [Np]
        for s in range(nsteps - 1):
            pw.append([t.astype(BF16) for t in mm2(pw[-1], pw[-1])])
            yield
        plc = [jnp.transpose(jnp.broadcast_to(sl(PL, g), (PAIR, PAIR))) for g in pairs]
        res[q] = (As, Rs, Vs, Mr, AV, pw, Bh, Kh, v, plc)
        yield

    state = [[h_scr[g] for g in pairs]]

    def back(q):
        As, Rs, Vs, Mr, AV, pw, Bh, Kh, v, plc = res[q]
        H = state[0]
        AH = mm2([jnp.concatenate([As[g], Rs[g]], axis=0) for g in pairs], H)
        yield
        Us = [AH[g][:2 * L] + AV[g] for g in pairs]
        for s in range(nsteps):
            Us = [Us[g] + _mm(pw[s][g], Us[g]) for g in pairs]
            yield
        Ys = [AH[g][2 * L:] + _mm(Mr[g], jnp.concatenate([Us[g], Vs[g]], axis=0)) for g in pairs]
        upd = [_mm(jnp.concatenate([sl(Bh, g), sl(Kh, g)], axis=0),
                   jnp.concatenate([Us[g][:L] + Us[g][L:], sl(v, g)], axis=0), _TN) for g in pairs]
        yield
        rw = rows(q)
        for g in pairs:
            y_ref[rw, g * PAIR:(g + 1) * PAIR] = (Ys[g][:L] + Ys[g][L:]).astype(y_ref.dtype)
        state[0] = [H[g] * plc[g] + jnp.where(bd, upd[g], 0.0) for g in pairs]
        yield

    def run(primary, others):
        while next(primary, _DONE) is not _DONE:
            for o in others:
                next(o, _DONE)

    fronts = [front(q) for q in range(nsub)]
    for _ in range(_FRONT_LEAD):
        next(fronts[0])
    for q in range(nsub):
        nxt = fronts[q + 1:q + 2]
        run(fronts[q], nxt)
        run(back(q), nxt)
    for g in pairs:
        h_scr[g] = state[0][g]

    @pl.when(c == nc - 1)
    def _():
        hT_ref[...] = h_scr[...]


def _wkv_scan(s1, kk, w2d, a2d, w0, a0, k_a, h0, B, T, D):
    G = min(SCAN_PAIRS, D // PAIR)
    W = G * PAIR
    nsub = SCAN_CHUNKS
    L = nsub * CHUNK
    nc = T // L
    npg = D // W
    N1 = s1.shape[1]
    s1 = s1.reshape(B, T, N1)
    kk = kk.reshape(B, T, D)
    chunk = lambda d, c: c + d * (nc - 1 - 2 * c)
    tok = lambda off: pl.BlockSpec((None, L, W), lambda d, b, p, c: (b, chunk(d, c), off + p))
    dirw = lambda rows: pl.BlockSpec((None, rows, W), lambda d, b, p, c: (d, 0, p))
    state = pl.BlockSpec((None, None, G, PAIR, PAIR), lambda d, b, p, c: (d, b, p, 0, 0))
    out_tok = pl.BlockSpec((None, None, L, W), lambda d, b, p, c: (d, b, chunk(d, c), p))
    y, a, hT = pl.pallas_call(
        functools.partial(_scan_kernel, G=G, nsub=nsub),
        grid=(2, B, npg, nc),
        in_specs=[
            tok(0), tok(npg), tok(2 * npg),
            pl.BlockSpec((None, L, W), lambda d, b, p, c: (b, chunk(d, c), p)),
            pl.BlockSpec((None, L, 2 * LORA_BLK), lambda d, b, p, c: (b, chunk(d, c), 3 * D // (2 * LORA_BLK))),
            dirw(LORA_BLK), dirw(LORA_BLK), dirw(1), dirw(1),
            pl.BlockSpec((1, W), lambda d, b, p, c: (0, p)),
            state,
        ],
        out_specs=[out_tok, out_tok, state],
        out_shape=[jax.ShapeDtypeStruct((2, B, T, D), BF16),
                   jax.ShapeDtypeStruct((2, B, T, D), BF16),
                   jax.ShapeDtypeStruct((2, B, D // PAIR, PAIR, PAIR), F32)],
        scratch_shapes=[pltpu.VMEM((G, PAIR, PAIR), F32)],
        compiler_params=_cparams(("arbitrary", "arbitrary", "arbitrary", "arbitrary")),
        name="wkv_scan",
    )(s1, s1, s1, kk, s1, w2d, a2d, w0, a0, k_a, h0)
    return y.reshape(2, B * T, D), a.reshape(2, B * T, D), hT


def _rwkv_out_kernel(y0_ref, y1_ref, r_ref, k_ref, v_ref, a0_ref, a1_ref, sg_ref, g2_ref,
                     lnw_ref, lnb_ref, rk_ref, ka_ref, ones_ref, w_ref, x_ref, gate_ref,
                     o_ref, *, D):
    ones = ones_ref[...]
    inv = 1.0 / RWKV_HEAD
    sg = sg_ref[:, :LORA_BLK]
    parts = []

    def slab(s):
        cs = slice(s * MXU_DIM, (s + 1) * MXU_DIM)
        f = lambda ref: ref[:, cs].astype(F32)
        wkv = f(y0_ref) + f(y1_ref)
        mean = _dot(wkv.astype(BF16), ones) * inv
        ks = f(k_ref) * (2.0 + (f(a0_ref) + f(a1_ref) - 2.0) * ka_ref[:, cs])
        bsum = _dot((f(r_ref) * ks * rk_ref[:, cs]).astype(BF16), ones)
        gate = _dot(sg, g2_ref[:, cs])
        yield
        cen = wkv - mean
        var = _dot((cen * cen).astype(BF16), ones) * inv
        yield
        yn = cen * lax.rsqrt(var + GN_EPS) * lnw_ref[:, cs] + lnb_ref[:, cs]
        parts.append(_dot(((yn + bsum * f(v_ref)) * gate).astype(BF16), w_ref[cs, :]))
        yield

    gens = [slab(s) for s in range(D // MXU_DIM)]
    done = [False] * len(gens)
    tick = 0
    while not all(done):
        for s, gen in enumerate(gens):
            if not done[s] and tick >= s:
                done[s] = next(gen, _DONE) is _DONE
        tick += 1
    acc = parts[0]
    for p in parts[1:]:
        acc = acc + p
    o_ref[...] = x_ref[...] + gate_ref[...] * acc


def _rwkv_out(y, a, s1, g2, x, mod, row_fn, ln_w, ln_b, r_k, k_a, w_o, tm):
    M, D = x.shape
    big = lambda col: pl.BlockSpec((tm, D), lambda i: (i, col))
    dirblk = lambda d: pl.BlockSpec((None, tm, D), lambda i: (d, i, 0))
    row = pl.BlockSpec((1, D), lambda i: (0, 0))
    return pl.pallas_call(
        functools.partial(_rwkv_out_kernel, D=D),
        grid=(M // tm,),
        in_specs=[
            dirblk(0), dirblk(1),
            big(0), big(1), big(2),
            dirblk(0), dirblk(1),
            pl.BlockSpec((tm, 2 * LORA_BLK), lambda i: (i, 3 * D // (2 * LORA_BLK) + 1)),
            pl.BlockSpec((LORA_BLK, D), lambda i: (0, 0)),
            row, row, row, row,
            pl.BlockSpec((MXU_DIM, MXU_DIM), lambda i: (0, 0)),
            pl.BlockSpec((D, D), lambda i: (0, 0)),
            pl.BlockSpec((tm, D), lambda i: (i, 0)),
            pl.BlockSpec((None, 1, D), lambda i: (row_fn(i), 0, 2)),
        ],
        out_specs=pl.BlockSpec((tm, D), lambda i: (i, 0)),
        out_shape=jax.ShapeDtypeStruct((M, D), F32),
        compiler_params=_cparams(("arbitrary",)),
        name="rwkv_out",
    )(y, y, s1, s1, s1, a, a, s1, g2, ln_w, ln_b, r_k, k_a, _group_ones(MXU_DIM, RWKV_HEAD), w_o, x, mod)


def _ffn_kernel(x_ref, g_ref, sh_ref, sc_ref, gate_ref, w1_ref, w3_ref, w2_ref, o_ref, a_scr, *, tf):
    f = pl.program_id(1)
    nf = pl.num_programs(1)

    def hidden_block(a):
        acts = []
        for s in range(tf // MXU_DIM):
            cs = slice(s * MXU_DIM, (s + 1) * MXU_DIM)
            h1 = _dot(a, w1_ref[:, cs])
            h3 = _dot(a, w3_ref[:, cs])
            acts.append((_silu(h1) * h3).astype(BF16))
        return _dot(jnp.concatenate(acts, axis=1), w2_ref[...])

    @pl.when(f == 0)
    def _():
        a = _modulate(x_ref[...], g_ref[...], sh_ref[...], sc_ref[...]).astype(BF16)
        a_scr[...] = a
        o_ref[...] = hidden_block(a)

    @pl.when((f > 0) & (f < nf - 1))
    def _():
        o_ref[...] += hidden_block(a_scr[...])

    @pl.when(f == nf - 1)
    def _():
        o_ref[...] = x_ref[...] + gate_ref[...] * (o_ref[...] + hidden_block(a_scr[...]))


def _ffn(x, mod, row_fn, gain, w1b, w3b, w2, tm, tf):
    M, D = x.shape
    nf = w1b.shape[1] // tf
    modspec = lambda ch: pl.BlockSpec((None, 1, D), lambda i, f: (row_fn(i), 0, ch))
    return pl.pallas_call(
        functools.partial(_ffn_kernel, tf=tf),
        grid=(M // tm, nf),
        in_specs=[
            pl.BlockSpec((tm, D), lambda i, f: (i, 0), pipeline_mode=pl.Buffered(1)),
            pl.BlockSpec((1, D), lambda i, f: (0, 0)),
            modspec(3), modspec(4), modspec(5),
            pl.BlockSpec((D, tf), lambda i, f: (0, f)),
            pl.BlockSpec((D, tf), lambda i, f: (0, f)),
            pl.BlockSpec((tf, D), lambda i, f: (f, 0)),
        ],
        out_specs=pl.BlockSpec((tm, D), lambda i, f: (i, 0)),
        out_shape=jax.ShapeDtypeStruct((M, D), F32),
        scratch_shapes=[pltpu.VMEM((tm, D), BF16)],
        compiler_params=_cparams(("arbitrary", "arbitrary")),
        name="ffn",
    )(x, gain, mod, mod, mod, w1b, w3b, w2)


def _qkv_kernel(x_ref, g_ref, sh_ref, sc_ref, w_ref, hg_ref, cos_ref, sin_ref, ones_ref, o_ref, a_scr,
                *, nrope, nnorm, tn):
    n = pl.program_id(1)

    def normed_block(a):
        acc = _dot(a, w_ref[...])
        hg = hg_ref[...]
        ones = ones_ref[...]
        for h in range(tn // MXU_DIM):
            blk = acc[:, h * MXU_DIM:(h + 1) * MXU_DIM]
            ms = _dot((blk * blk).astype(BF16), ones) * (1.0 / NA_HEAD)
            blk = blk * lax.rsqrt(ms + RMS_EPS)
            for s in range(MXU_DIM // NA_HEAD):
                t = blk[:, s * NA_HEAD:(s + 1) * NA_HEAD] * hg
                if nrope:
                    t = t * cos_ref[...] + pltpu.roll(t, NA_HEAD // 2, axis=1) * sin_ref[...]
                c0 = h * MXU_DIM + s * NA_HEAD
                o_ref[:, c0:c0 + NA_HEAD] = t.astype(o_ref.dtype)

    @pl.when(n == 0)
    def _():
        a = _modulate(x_ref[...], g_ref[...], sh_ref[...], sc_ref[...]).astype(BF16)
        a_scr[...] = a
        normed_block(a)

    @pl.when((n > 0) & (n < nnorm))
    def _():
        normed_block(a_scr[...])

    @pl.when(n >= nnorm)
    def _():
        o_ref[...] = _dot(a_scr[...], w_ref[...]).astype(o_ref.dtype)


def _na_qkv(x, mod, row_fn, gain, w, head_gain, cos, sin, *, nrope, nnorm, tm, tn):
    assert nrope in (0, nnorm)
    M, D = x.shape
    N = w.shape[1]
    tpb = cos.shape[0] // tm
    hrow = lambda n: jnp.minimum(n, head_gain.shape[0] - 1)
    return pl.pallas_call(
        functools.partial(_qkv_kernel, nrope=nrope, nnorm=nnorm, tn=tn),
        grid=(M // tm, N // tn),
        in_specs=[
            pl.BlockSpec((tm, D), lambda i, n: (i, 0)),
            pl.BlockSpec((1, D), lambda i, n: (0, 0)),
            pl.BlockSpec((None, 1, D), lambda i, n: (row_fn(i), 0, 0)),
            pl.BlockSpec((None, 1, D), lambda i, n: (row_fn(i), 0, 1)),
            pl.BlockSpec((D, tn), lambda i, n: (0, n)),
            pl.BlockSpec((None, 1, NA_HEAD), lambda i, n: (hrow(n), 0, 0)),
            pl.BlockSpec((tm, NA_HEAD), lambda i, n: (i % tpb, 0)),
            pl.BlockSpec((tm, NA_HEAD), lambda i, n: (i % tpb, 0)),
            pl.BlockSpec((MXU_DIM, MXU_DIM), lambda i, n: (0, 0)),
        ],
        out_specs=pl.BlockSpec((tm, tn), lambda i, n: (i, n)),
        out_shape=jax.ShapeDtypeStruct((M, N), BF16),
        scratch_shapes=[pltpu.VMEM((tm, D), BF16)],
        compiler_params=_cparams(("arbitrary", "arbitrary")),
        name="na_qkv",
    )(x, gain, mod, mod, w, head_gain, cos, sin, _group_ones(MXU_DIM, NA_HEAD))


def _na_kernel(q_ref, k_ref, v_ref, kc_ref, vc_ref, bias_ref, o_ref, *, rows, rb):
    j0 = pl.program_id(2) * rb
    win_h = min(WIN_H, rows)
    nk = win_h * GRID_W
    qrows = range(rb)
    rs = lambda t, j: t[j * GRID_W:(j + 1) * GRID_W]

    qa = q_ref[...]
    s_ctx = lax.dot_general(qa, kc_ref[...], _NT, preferred_element_type=F32)

    def row(j):
        r0 = jnp.clip(j0 + j - WIN_H // 2, 0, rows - win_h)
        off = pl.multiple_of(r0 * GRID_W, GRID_W)
        s_lat = (lax.dot_general(rs(qa, j), k_ref[pl.ds(off, nk), :], _NT, preferred_element_type=F32)
                 + bias_ref[r0 - (j0 + j) + WIN_H - 1])
        yield
        sc = rs(s_ctx, j)
        m = jnp.maximum(jnp.max(s_lat, axis=-1, keepdims=True), jnp.max(sc, axis=-1, keepdims=True))
        yield
        p_lat = jnp.exp(s_lat - m)
        p_ctx = jnp.exp(sc - m)
        yield
        den = jnp.sum(p_lat, axis=-1, keepdims=True) + jnp.sum(p_ctx, axis=-1, keepdims=True)
        o = _dot(p_lat.astype(BF16), v_ref[pl.ds(off, nk), :]) + _dot(p_ctx.astype(BF16), vc_ref[...])
        yield
        o_ref[j * GRID_W:(j + 1) * GRID_W, :] = (o / den).astype(o_ref.dtype)
        yield

    gens = [row(j) for j in qrows]
    done = [False] * rb
    tick = 0
    while not all(done):
        for j in qrows:
            if not done[j] and tick >= j * _ROW_STAGGER:
                done[j] = next(gens[j], _DONE) is _DONE
        tick += 1


def _na_attention(qkv, kvc, bias, B, T, D, rb):
    H = D // NA_HEAD
    rows = T // GRID_W
    C = kvc.shape[1]
    nbias = bias.shape[1]
    return pl.pallas_call(
        functools.partial(_na_kernel, rows=rows, rb=rb),
        grid=(B, H, rows // rb),
        in_specs=[
            pl.BlockSpec((None, rb * GRID_W, NA_HEAD), lambda b, h, j: (b, j, h)),
            pl.BlockSpec((None, T, NA_HEAD), lambda b, h, j: (b, 0, H + h)),
            pl.BlockSpec((None, T, NA_HEAD), lambda b, h, j: (b, 0, 2 * H + h)),
            pl.BlockSpec((None, C, NA_HEAD), lambda b, h, j: (b, 0, h)),
            pl.BlockSpec((None, C, NA_HEAD), lambda b, h, j: (b, 0, H + h)),
            pl.BlockSpec((None, nbias, GRID_W, bias.shape[3]), lambda b, h, j: (h, 0, 0, 0)),
        ],
        out_specs=pl.BlockSpec((None, rb * GRID_W, NA_HEAD), lambda b, h, j: (b, j, h)),
        out_shape=jax.ShapeDtypeStruct((B, T, D), BF16),
        compiler_params=_cparams(("arbitrary", "arbitrary", "arbitrary")),
        name="na_attention",
    )(qkv, qkv, qkv, kvc, kvc, bias)


def _proj_res_kernel(a_ref, w_ref, x_ref, gate_ref, o_ref):
    o_ref[...] = x_ref[...] + gate_ref[...] * _dot(a_ref[...], w_ref[...])


def _proj_residual(a, w, x, mod, row_fn, chunk, tm):
    M, K = a.shape
    N = w.shape[1]
    return pl.pallas_call(
        _proj_res_kernel,
        grid=(M // tm,),
        in_specs=[
            pl.BlockSpec((tm, K), lambda i: (i, 0)),
            pl.BlockSpec((K, N), lambda i: (0, 0)),
            pl.BlockSpec((tm, N), lambda i: (i, 0)),
            pl.BlockSpec((None, 1, N), lambda i: (row_fn(i), 0, chunk)),
        ],
        out_specs=pl.BlockSpec((tm, N), lambda i: (i, 0)),
        out_shape=jax.ShapeDtypeStruct((M, N), F32),
        compiler_params=_cparams(("arbitrary",)),
        name="proj_residual",
    )(a, w, x, mod)


_QUART = NA_HEAD // 4
_ROPE_PERM = np.concatenate([np.arange(0, _QUART), np.arange(2 * _QUART, 3 * _QUART),
                             np.arange(_QUART, 2 * _QUART), np.arange(3 * _QUART, NA_HEAD)])


def _rope_layout(w_qk):
    K, N = w_qk.shape
    return w_qk.reshape(K, N // NA_HEAD, NA_HEAD)[:, :, _ROPE_PERM].reshape(K, N)


def _rope_tables(T):
    t = np.arange(T)
    half = NA_HEAD // 2
    freqs = ROPE_THETA ** (-np.arange(0, half, 2, dtype=np.float64) / half)
    ang_r = (t // GRID_W).astype(np.float64)[:, None] * freqs
    ang_c = (t % GRID_W).astype(np.float64)[:, None] * freqs
    cos = np.concatenate([np.cos(ang_r), np.cos(ang_c), np.cos(ang_r), np.cos(ang_c)], axis=1)
    sin = np.concatenate([-np.sin(ang_r), -np.sin(ang_c), np.sin(ang_r), np.sin(ang_c)], axis=1)
    return jnp.asarray(cos, F32), jnp.asarray(sin, F32)


def _na_bias(rpb, rows):
    win_h = min(WIN_H, rows)
    qc = np.arange(GRID_W)[:, None]
    kc = np.arange(GRID_W)[None, :]
    wstart = np.clip(qc - WIN_W // 2, 0, GRID_W - WIN_W)
    valid = (kc >= wstart) & (kc < wstart + WIN_W)
    rel = np.clip(kc - qc, -(WIN_W - 1), WIN_W - 1) + WIN_W - 1
    g = rpb[:, :, rel]
    g = jnp.where(valid[None, None], g, NEG_INF)
    tabs = []
    for i0 in range(2 * WIN_H - win_h):
        blk = g[:, i0:i0 + win_h]
        tabs.append(jnp.transpose(blk, (0, 2, 1, 3)).reshape(g.shape[0], GRID_W, win_h * GRID_W))
    return jnp.stack(tabs, axis=1)


def _pad_cols(w, n):
    return jnp.pad(w, ((0, 0), (0, n - w.shape[1])))


def _pad_rows(w, n):
    return jnp.pad(w, ((0, n - w.shape[0]), (0, 0)))


def kernel(x, c, ctx, c_ctx, ada_w, ada_b, norm1, norm2, rwkv_mu, rwkv_w_r, rwkv_w_k, rwkv_w_v, rwkv_w_o, rwkv_w0, rwkv_w1, rwkv_w2, rwkv_a0, rwkv_a1, rwkv_a2, rwkv_g1, rwkv_g2, rwkv_k_k, rwkv_k_a, rwkv_r_k, rwkv_ln_w, rwkv_ln_b, na_w_qkv, na_w_o, na_q_gain, na_k_gain, na_rpb, ffn_w1, ffn_w3, ffn_w2):
    B, T, D = x.shape
    C = ctx.shape[1]
    rows = T // GRID_W
    depth = ada_w.shape[0]
    n_mix = 2
    TM = 512
    TMO = 256
    TF = 512 if ffn_w1.shape[-1] % 512 == 0 else 256
    TMF = min(1024, T)

    cvec = jnp.concatenate([c, c_ctx[None, :], jnp.zeros((8 - B - 1, D), F32)], axis=0)
    mods = _ada(cvec, ada_w, ada_b)
    lat_row = lambda tm: (lambda i: i // (T // tm))
    ctx_row = lambda i: B

    xl = x.reshape(B * T, D)
    xc = ctx.reshape(B * C, D)

    for i in range(depth):
        last = i == depth - 1
        j = i // n_mix
        mod = mods[i].reshape(8, 1, 6 * D)
        g1 = norm1[i].reshape(1, D)
        g2 = norm2[i].reshape(1, D)
        w1 = ffn_w1[i].astype(BF16)
        w3 = ffn_w3[i].astype(BF16)
        w2 = ffn_w2[i].astype(BF16)
        if i % n_mix == 0:
            lw = rwkv_w1.shape[-1]
            la = rwkv_a1.shape[-1]
            assert max(2 * lw, 2 * la, rwkv_g1.shape[-1]) <= LORA_BLK and D % TN1 == 0
            wcat = jnp.concatenate([
                rwkv_w_r[j], rwkv_w_k[j], rwkv_w_v[j],
                _pad_cols(jnp.concatenate([rwkv_w1[j, 0], rwkv_w1[j, 1]], axis=1), LORA_BLK),
                _pad_cols(jnp.concatenate([rwkv_a1[j, 0], rwkv_a1[j, 1]], axis=1), LORA_BLK),
                _pad_cols(rwkv_g1[j], 2 * LORA_BLK)], axis=1).astype(BF16)
            z = lambda n: jnp.zeros((n, D), F32)
            w2d = jnp.stack([_pad_rows(rwkv_w2[j, 0], LORA_BLK),
                             _pad_rows(jnp.concatenate([z(lw), rwkv_w2[j, 1]], axis=0), LORA_BLK)]).astype(BF16)
            a2d = jnp.stack([_pad_rows(rwkv_a2[j, 0], LORA_BLK),
                             _pad_rows(jnp.concatenate([z(la), rwkv_a2[j, 1]], axis=0), LORA_BLK)]).astype(BF16)
            wg2 = _pad_rows(rwkv_g2[j], LORA_BLK).astype(BF16)
            w0 = rwkv_w0[j].reshape(2, 1, D)
            a0 = rwkv_a0[j].reshape(2, 1, D)
            k_k = rwkv_k_k[j].reshape(1, D)
            k_a = rwkv_k_a[j].reshape(1, D)
            r_k = rwkv_r_k[j].reshape(1, D)
            ln_w = rwkv_ln_w[j].reshape(1, D)
            ln_b = rwkv_ln_b[j].reshape(1, D)
            w_o = rwkv_w_o[j].astype(BF16)

            s1c, kkc = _rwkv_stage1(xc, mod, ctx_row, g1, rwkv_mu[j], k_k, wcat, latent=False, rows=rows, tm=C)
            h0 = jnp.zeros((2, B, D // PAIR, PAIR, PAIR), F32)
            yc, ac, hc = _wkv_scan(s1c, kkc, w2d, a2d, w0, a0, k_a, h0, B, C, D)
            s1l, kkl = _rwkv_stage1(xl, mod, lat_row(TM), g1, rwkv_mu[j], k_k, wcat, latent=True, rows=rows, tm=TM)
            yl, al, _ = _wkv_scan(s1l, kkl, w2d, a2d, w0, a0, k_a, hc, B, T, D)
            xl = _rwkv_out(yl, al, s1l, wg2, xl, mod, lat_row(TMO), ln_w, ln_b, r_k, k_a, w_o, TMO)
            if not last:
                xc = _rwkv_out(yc, ac, s1c, wg2, xc, mod, ctx_row, ln_w, ln_b, r_k, k_a, w_o, min(TMO, C))
        else:
            H = D // NA_HEAD
            cos, sin = _rope_tables(T)
            scale = NA_HEAD ** -0.5
            hg = jnp.stack([(na_q_gain[j] * scale)[_ROPE_PERM], na_k_gain[j][_ROPE_PERM]]).reshape(2, 1, NA_HEAD)
            tn = min(1024, D)
            nq = D // tn
            wqkv = jnp.concatenate([_rope_layout(na_w_qkv[j][:, :2 * D]), na_w_qkv[j][:, 2 * D:]],
                                   axis=1).astype(BF16)
            hg_blocks = jnp.concatenate([jnp.broadcast_to(hg[0], (nq, 1, NA_HEAD)),
                                         jnp.broadcast_to(hg[1], (nq, 1, NA_HEAD))], axis=0)
            qkv = _na_qkv(xl, mod, lat_row(TMF), g1, wqkv, hg_blocks, cos, sin, nrope=2 * nq, nnorm=2 * nq,
                          tm=TMF, tn=tn)
            kg_blocks = jnp.broadcast_to(hg[1], (nq, 1, NA_HEAD))
            kvc = _na_qkv(xc, mod, ctx_row, g1, wqkv[:, D:], kg_blocks, cos[:C], sin[:C], nrope=0, nnorm=nq,
                          tm=C, tn=tn)
            bias = _na_bias(na_rpb[j], rows)
            o = _na_attention(qkv.reshape(B, T, 3 * D), kvc.reshape(B, C, 2 * D), bias, B, T, D, min(32, rows))
            xl = _proj_residual(o.reshape(B * T, D), na_w_o[j].astype(BF16), xl, mod,
                                lat_row(TM), 2, TM)
            if not last:
                raise NotImplementedError("attention layer with context output")
        xl = _ffn(xl, mod, lat_row(TMF), g2, w1, w3, w2, TMF, TF)
        if not last:
            xc = _ffn(xc, mod, ctx_row, g2, w1, w3, w2, C, TF)
    return xl.reshape(B, T, D)
```

```python
import functools
import math

import numpy as np
import jax
import jax.numpy as jnp
from jax import lax
from jax.experimental import pallas as pl
from jax.experimental.pallas import tpu as pltpu

F32 = jnp.float32
BF16 = jnp.bfloat16

GRID_W = 64
RMS_EPS = 1e-6
RWKV_HEAD = 64
GN_EPS = 64e-5
NA_HEAD = 128
WIN_H = 8
WIN_W = 16
ROPE_THETA = 10000.0
NEG_INF = -1e30

MXU_DIM = 256
VMEM_LIMIT = 56 * 1024 * 1024

CHUNK = 64
PAIR = 2 * RWKV_HEAD
SCAN_PAIRS = 16
SCAN_CHUNKS = 4
LORA_BLK = 256
TN1 = 4 * LORA_BLK


def _cparams(sem):
    return pltpu.CompilerParams(dimension_semantics=sem, vmem_limit_bytes=VMEM_LIMIT)


def _dot(a, b):
    return jnp.dot(a, b, preferred_element_type=F32)


def _split2(x):
    hi = x.astype(BF16)
    lo = (x - hi.astype(F32)).astype(BF16)
    return hi, lo


def _rms_scale(x):
    return lax.rsqrt(jnp.mean(x * x, axis=-1, keepdims=True) + RMS_EPS)


def _modulate(x, gain, shift, scale):
    return (x * _rms_scale(x) * gain) * (1.0 + scale) + shift


def _sigmoid(x):
    return 1.0 / (1.0 + jnp.exp(-x))


def _silu(x):
    return x * _sigmoid(x)


def _group_ones(n, group):
    idx = np.arange(n) // group
    return jnp.asarray((idx[:, None] == idx[None, :]).astype(np.float32), dtype=BF16)


def _ada_kernel(c_ref, w_ref, b_ref, o_ref):
    s = _silu(c_ref[...]).astype(BF16)
    o_ref[...] = _dot(s, w_ref[...].astype(BF16)) + b_ref[...]


def _ada(cvec, ada_w, ada_b):
    depth, D, N = ada_w.shape
    R = cvec.shape[0]
    tn = 1024
    return pl.pallas_call(
        _ada_kernel,
        grid=(depth, N // tn),
        in_specs=[
            pl.BlockSpec((R, D), lambda l, n: (0, 0)),
            pl.BlockSpec((None, D, tn), lambda l, n: (l, 0, n)),
            pl.BlockSpec((None, 1, tn), lambda l, n: (l, 0, n)),
        ],
        out_specs=pl.BlockSpec((None, R, tn), lambda l, n: (l, 0, n)),
        out_shape=jax.ShapeDtypeStruct((depth, R, N), F32),
        compiler_params=_cparams(("arbitrary", "arbitrary")),
        name="ada",
    )(cvec, ada_w, ada_b.reshape(depth, 1, N))


def _s1_kernel(xp_ref, x_ref, xn_ref, g_ref, sh_ref, sc_ref, mu_ref, kkw_ref, ones_ref, w_ref,
               o_ref, kk_ref, a_scr, h_scr, xx_scr, *, latent, rows, tm, D, nb):
    i = pl.program_id(0)
    n = pl.program_id(1)
    R = GRID_W
    pad = h_scr.shape[0] - tm

    @pl.when(n == 0)
    def _():
        g = g_ref[...]
        sh = sh_ref[...]
        sc = sc_ref[...]
        lo = pad // 2
        h_scr[lo:lo + tm, :] = _modulate(x_ref[...], g, sh, sc)
        if latent:
            h_scr[0:lo, :] = _modulate(xp_ref[...], g, sh, sc)
            h_scr[lo + tm:, :] = _modulate(xn_ref[...], g, sh, sc)
        else:
            h_scr[0:lo, :] = jnp.zeros((lo, D), F32)
            h_scr[lo + tm:, :] = jnp.zeros((pad - lo, D), F32)
        for rr in range(tm // R):
            base = lo + rr * R
            hc = h_scr[base:base + R, :]
            if latent:
                q = D // 4
                col = lax.broadcasted_iota(jnp.int32, (R, q), 0)
                grow = (i * (tm // R) + rr) & (rows - 1)
                hs = jnp.concatenate([
                    jnp.where(col > 0, h_scr[base - 1:base - 1 + R, 0:q], 0.0),
                    jnp.where(col < R - 1, h_scr[base + 1:base + 1 + R, q:2 * q], 0.0),
                    jnp.where(grow > 0, h_scr[base - R:base, 2 * q:3 * q], 0.0),
                    jnp.where(grow < rows - 1, h_scr[base + R:base + 2 * R, 3 * q:], 0.0)], axis=1)
            else:
                half = D // 2
                hs = jnp.concatenate([h_scr[base - 1:base - 1 + R, :half],
                                      h_scr[base + 1:base + 1 + R, half:]], axis=1)
            xx = hs - hc
            xx_scr[rr * R:(rr + 1) * R, :] = xx
            a_scr[0, rr * R:(rr + 1) * R, :] = (hc + xx * mu_ref[0:1, :]).astype(BF16)

    def mix(j):
        lo = pad // 2
        for rr in range(tm // R):
            rs_ = slice(rr * R, (rr + 1) * R)
            a_scr[j, rs_, :] = (h_scr[lo + rr * R:lo + (rr + 1) * R, :] + xx_scr[rs_, :] * mu_ref[j:j + 1, :]).astype(BF16)

    def r_step():
        o_ref[...] = _dot(a_scr[0], w_ref[...]).astype(o_ref.dtype)

    def k_step():
        acc = _dot(a_scr[2], w_ref[...])
        o_ref[...] = acc.astype(o_ref.dtype)
        kkf = acc * kkw_ref[...]
        sq = kkf * kkf
        ones = ones_ref[...]
        parts = []
        for s in range(TN1 // MXU_DIM):
            parts.append(_dot(sq[:, s * MXU_DIM:(s + 1) * MXU_DIM].astype(BF16), ones))
        ss = jnp.concatenate(parts, axis=1)
        kk_ref[...] = (kkf / jnp.maximum(jnp.sqrt(ss), 1e-12)).astype(kk_ref.dtype)

    def v_step():
        o_ref[...] = _dot(a_scr[3], w_ref[...]).astype(o_ref.dtype)

    def lora_step():
        c1, c2, c3 = LORA_BLK, 2 * LORA_BLK, 3 * LORA_BLK
        o_ref[:, :c1] = jnp.tanh(_dot(a_scr[1], w_ref[:, :c1])).astype(o_ref.dtype)
        o_ref[:, c1:c2] = _dot(a_scr[4], w_ref[:, c1:c2]).astype(o_ref.dtype)
        o_ref[:, c2:c3] = _sigmoid(_dot(a_scr[5], w_ref[:, c2:c3])).astype(o_ref.dtype)
        o_ref[:, c3:] = jnp.zeros((o_ref.shape[0], TN1 - c3), o_ref.dtype)

    steps = [r_step] * nb + [k_step] * nb + [v_step] * nb + [lora_step]
    first_use = {2: nb, 3: 2 * nb, 1: 3 * nb, 4: 3 * nb, 5: 3 * nb}
    pending = sorted(first_use, key=lambda j: first_use[j])
    for t, step in enumerate(steps):
        side = [j for j in pending if first_use[j] == t + 1] or pending[:1]
        pending = [j for j in pending if j not in side]

        @pl.when(n == t)
        def _(step=step, side=side):
            step()
            for j in side:
                mix(j)


def _rwkv_stage1(x, mod, row_fn, gain, mu, k_k, wcat, *, latent, rows, tm):
    M, D = x.shape
    nb = D // TN1
    N1 = wcat.shape[1]
    nsteps = N1 // TN1
    hb = tm // GRID_W if latent else 1
    hrows = GRID_W if latent else 8
    nhb = M // hrows
    kcol = lambda n: jnp.clip(n - nb, 0, nb - 1)
    kern = functools.partial(_s1_kernel, latent=latent, rows=rows, tm=tm, D=D, nb=nb)
    return pl.pallas_call(
        kern,
        grid=(M // tm, nsteps),
        in_specs=[
            pl.BlockSpec((hrows, D), lambda i, n: (jnp.maximum(i * hb - 1, 0), 0)),
            pl.BlockSpec((tm, D), lambda i, n: (i, 0)),
            pl.BlockSpec((hrows, D), lambda i, n: (jnp.minimum(i * hb + hb, nhb - 1), 0)),
            pl.BlockSpec((1, D), lambda i, n: (0, 0)),
            pl.BlockSpec((None, 1, D), lambda i, n: (row_fn(i), 0, 0)),
            pl.BlockSpec((None, 1, D), lambda i, n: (row_fn(i), 0, 1)),
            pl.BlockSpec((6, D), lambda i, n: (0, 0)),
            pl.BlockSpec((1, TN1), lambda i, n: (0, kcol(n))),
            pl.BlockSpec((MXU_DIM, MXU_DIM), lambda i, n: (0, 0)),
            pl.BlockSpec((D, TN1), lambda i, n: (0, n)),
        ],
        out_specs=[
            pl.BlockSpec((tm, TN1), lambda i, n: (i, n)),
            pl.BlockSpec((tm, TN1), lambda i, n: (i, kcol(n))),
        ],
        out_shape=[jax.ShapeDtypeStruct((M, N1), BF16), jax.ShapeDtypeStruct((M, D), BF16)],
        scratch_shapes=[pltpu.VMEM((6, tm, D), BF16),
                        pltpu.VMEM((tm + 2 * hrows, D), F32),
                        pltpu.VMEM((tm, D), F32)],
        compiler_params=_cparams(("arbitrary", "arbitrary")),
        name="rwkv_stage1",
    )(x, x, x, gain, mod, mod, mu, k_k, _group_ones(MXU_DIM, RWKV_HEAD), wcat)


_NT = (((1,), (1,)), ((), ()))
_TN = (((0,), (0,)), ((), ()))


def _mm(a, b, dims=None):
    a = a.astype(BF16)
    b = b.astype(BF16)
    if dims is None:
        return jnp.dot(a, b, preferred_element_type=F32)
    return lax.dot_general(a, b, dims, preferred_element_type=F32)


_DONE = object()
_FRONT_LEAD = 1
_ROW_STAGGER = 1


def _scan_kernel(r_ref, k_ref, v_ref, kk_ref, lo_ref, w2_ref, a2_ref, w0_ref, a0_ref, ka_ref, h0_ref,
                 y_ref, a_ref, hT_ref, h_scr, *, G, nsub):
    d = pl.program_id(0)
    c = pl.program_id(3)
    nc = pl.num_programs(3)
    L = CHUNK
    pairs = range(G)
    nsteps = int(math.log2(L))

    @pl.when(c == 0)
    def _():
        h_scr[...] = h0_ref[...]

    sgn = 1 - 2 * d
    lane = lax.broadcasted_iota(jnp.int32, (L, PAIR), 1)
    m0 = lane < RWKV_HEAD
    ti = lax.broadcasted_iota(jnp.int32, (L, L), 0)
    si = lax.broadcasted_iota(jnp.int32, (L, L), 1)
    tri = jnp.where((si - ti) * sgn <= 0, 1.0, 0.0).astype(BF16)
    i2 = lax.broadcasted_iota(jnp.int32, (2 * L, 2 * L), 0)
    j2 = lax.broadcasted_iota(jnp.int32, (2 * L, 2 * L), 1)
    dif = jnp.where((i2 // L) == (j2 // L), ((j2 % L) - (i2 % L)) * sgn, 1)
    strict = dif < 0
    incl = dif <= 0
    r2 = lax.broadcasted_iota(jnp.int32, (PAIR, PAIR), 0)
    c2 = lax.broadcasted_iota(jnp.int32, (PAIR, PAIR), 1)
    bd = (r2 // RWKV_HEAD) == (c2 // RWKV_HEAD)

    def sm(x):
        return jnp.concatenate([jnp.where(m0, x, 0.0), jnp.where(m0, 0.0, x)], axis=0)

    def sl(x, g):
        return x[:, g * PAIR:(g + 1) * PAIR]

    def mm2(lhs, rhs):
        out = []
        for g in range(0, len(lhs), 2):
            ra, rb = rhs[g].astype(BF16), rhs[g + 1].astype(BF16)
            z = jnp.zeros_like(ra)
            bd2 = jnp.concatenate([jnp.concatenate([ra, z], axis=1), jnp.concatenate([z, rb], axis=1)], axis=0)
            both = _mm(jnp.concatenate([lhs[g].astype(BF16), lhs[g + 1].astype(BF16)], axis=1), bd2)
            out += [both[:, :PAIR], both[:, PAIR:]]
        return out

    def rows(q):
        return pl.ds(pl.multiple_of((q + d * (nsub - 1 - 2 * q)) * L, L), L)

    res = [None] * nsub

    def front(q):
        rw = rows(q)
        z = _dot(lo_ref[rw, :LORA_BLK], w2_ref[...]) + w0_ref[...]
        za = _dot(lo_ref[rw, LORA_BLK:], a2_ref[...]) + a0_ref[...]
        yield
        ld = -math.exp(-0.5) * _sigmoid(z)
        a = _sigmoid(za)
        a_ref[rw, :] = a.astype(a_ref.dtype)
        p0, p1 = _split2(ld)
        yield
        cum = _dot(tri, p0) + _dot(tri, p1)
        PL = jnp.exp(jnp.sum(ld, axis=0, keepdims=True))
        yield
        invP = jnp.exp(-cum)
        kk = kk_ref[rw, :].astype(F32)
        v = v_ref[rw, :].astype(F32)
        At = -kk * jnp.exp(cum - ld)
        Bt = (kk * a) * invP
        Kt = (k_ref[rw, :].astype(F32) * (1.0 + (a - 1.0) * ka_ref[...])) * invP
        Rt = r_ref[rw, :].astype(F32) * jnp.exp(cum)
        Bh = Bt * PL
        Kh = Kt * PL
        yield
        As = [sm(sl(At, g)) for g in pairs]
        Rs = [sm(sl(Rt, g)) for g in pairs]
        Vs = [sm(sl(v, g)) for g in pairs]
        Gc = [_mm(jnp.concatenate([As[g], Rs[g]], axis=0),
                  jnp.concatenate([sl(Bt, g), sl(Kt, g)], axis=0), _NT) for g in pairs]
        yield
        Gr = [pltpu.roll(Gc[g], L, axis=1) for g in pairs]

        def place(g, r0, first):
            a, b = (Gc[g], Gr[g]) if first else (Gr[g], Gc[g])
            return jnp.concatenate([a[r0:r0 + L], b[r0 + L:r0 + 2 * L]], axis=0)

        Np = [jnp.where(strict, place(g, 0, True), 0.0).astype(BF16) for g in pairs]
        Aak = [jnp.where(strict, place(g, 0, False), 0.0) for g in pairs]
        Mr = [jnp.concatenate([jnp.where(incl, place(g, 2 * L, True), 0.0),
                               jnp.where(incl, place(g, 2 * L, False), 0.0)], axis=1).astype(BF16) for g in pairs]
        AV = mm2(Aak, Vs)
        yield
        pw = [Np]
        for s in range(nsteps - 1):
            pw.append([t.astype(BF16) for t in mm2(pw[-1], pw[-1])])
            yield
        plc = [jnp.transpose(jnp.broadcast_to(sl(PL, g), (PAIR, PAIR))) for g in pairs]
        res[q] = (As, Rs, Vs, Mr, AV, pw, Bh, Kh, v, plc)
        yield

    state = [[h_scr[g] for g in pairs]]

    def back(q):
        As, Rs, Vs, Mr, AV, pw, Bh, Kh, v, plc = res[q]
        H = state[0]
        AH = mm2([jnp.concatenate([As[g], Rs[g]], axis=0) for g in pairs], H)
        yield
        Us = [AH[g][:2 * L] + AV[g] for g in pairs]
        for s in range(nsteps):
            Us = [Us[g] + _mm(pw[s][g], Us[g]) for g in pairs]
            yield
        Ys = [AH[g][2 * L:] + _mm(Mr[g], jnp.concatenate([Us[g], Vs[g]], axis=0)) for g in pairs]
        upd = [_mm(jnp.concatenate([sl(Bh, g), sl(Kh, g)], axis=0),
                   jnp.concatenate([Us[g][:L] + Us[g][L:], sl(v, g)], axis=0), _TN) for g in pairs]
        yield
        rw = rows(q)
        for g in pairs:
            y_ref[rw, g * PAIR:(g + 1) * PAIR] = (Ys[g][:L] + Ys[g][L:]).astype(y_ref.dtype)
        state[0] = [H[g] * plc[g] + jnp.where(bd, upd[g], 0.0) for g in pairs]
        yield

    def run(primary, others):
        while next(primary, _DONE) is not _DONE:
            for o in others:
                next(o, _DONE)

    fronts = [front(q) for q in range(nsub)]
    for _ in range(_FRONT_LEAD):
        next(fronts[0])
    for q in range(nsub):
        nxt = fronts[q + 1:q + 2]
        run(fronts[q], nxt)
        run(back(q), nxt)
    for g in pairs:
        h_scr[g] = state[0][g]

    @pl.when(c == nc - 1)
    def _():
        hT_ref[...] = h_scr[...]


def _wkv_scan(s1, kk, w2d, a2d, w0, a0, k_a, h0, B, T, D):
    G = min(SCAN_PAIRS, D // PAIR)
    W = G * PAIR
    nsub = SCAN_CHUNKS
    L = nsub * CHUNK
    nc = T // L
    npg = D // W
    N1 = s1.shape[1]
    s1 = s1.reshape(B, T, N1)
    kk = kk.reshape(B, T, D)
    chunk = lambda d, c: c + d * (nc - 1 - 2 * c)
    tok = lambda off: pl.BlockSpec((None, L, W), lambda d, b, p, c: (b, chunk(d, c), off + p))
    dirw = lambda rows: pl.BlockSpec((None, rows, W), lambda d, b, p, c: (d, 0, p))
    state = pl.BlockSpec((None, None, G, PAIR, PAIR), lambda d, b, p, c: (d, b, p, 0, 0))
    out_tok = pl.BlockSpec((None, None, L, W), lambda d, b, p, c: (d, b, chunk(d, c), p))
    y, a, hT = pl.pallas_call(
        functools.partial(_scan_kernel, G=G, nsub=nsub),
        grid=(2, B, npg, nc),
        in_specs=[
            tok(0), tok(npg), tok(2 * npg),
            pl.BlockSpec((None, L, W), lambda d, b, p, c: (b, chunk(d, c), p)),
            pl.BlockSpec((None, L, 2 * LORA_BLK), lambda d, b, p, c: (b, chunk(d, c), 3 * D // (2 * LORA_BLK))),
            dirw(LORA_BLK), dirw(LORA_BLK), dirw(1), dirw(1),
            pl.BlockSpec((1, W), lambda d, b, p, c: (0, p)),
            state,
        ],
        out_specs=[out_tok, out_tok, state],
        out_shape=[jax.ShapeDtypeStruct((2, B, T, D), BF16),
                   jax.ShapeDtypeStruct((2, B, T, D), BF16),
                   jax.ShapeDtypeStruct((2, B, D // PAIR, PAIR, PAIR), F32)],
        scratch_shapes=[pltpu.VMEM((G, PAIR, PAIR), F32)],
        compiler_params=_cparams(("arbitrary", "arbitrary", "arbitrary", "arbitrary")),
        name="wkv_scan",
    )(s1, s1, s1, kk, s1, w2d, a2d, w0, a0, k_a, h0)
    return y.reshape(2, B * T, D), a.reshape(2, B * T, D), hT


def _rwkv_out_kernel(y0_ref, y1_ref, r_ref, k_ref, v_ref, a0_ref, a1_ref, sg_ref, g2_ref,
                     lnw_ref, lnb_ref, rk_ref, ka_ref, ones_ref, w_ref, x_ref, gate_ref,
                     o_ref, *, D):
    ones = ones_ref[...]
    inv = 1.0 / RWKV_HEAD
    sg = sg_ref[:, :LORA_BLK]
    parts = []

    def slab(s):
        cs = slice(s * MXU_DIM, (s + 1) * MXU_DIM)
        f = lambda ref: ref[:, cs].astype(F32)
        wkv = f(y0_ref) + f(y1_ref)
        mean = _dot(wkv.astype(BF16), ones) * inv
        ks = f(k_ref) * (2.0 + (f(a0_ref) + f(a1_ref) - 2.0) * ka_ref[:, cs])
        bsum = _dot((f(r_ref) * ks * rk_ref[:, cs]).astype(BF16), ones)
        gate = _dot(sg, g2_ref[:, cs])
        yield
        cen = wkv - mean
        var = _dot((cen * cen).astype(BF16), ones) * inv
        yield
        yn = cen * lax.rsqrt(var + GN_EPS) * lnw_ref[:, cs] + lnb_ref[:, cs]
        parts.append(_dot(((yn + bsum * f(v_ref)) * gate).astype(BF16), w_ref[cs, :]))
        yield

    gens = [slab(s) for s in range(D // MXU_DIM)]
    done = [False] * len(gens)
    tick = 0
    while not all(done):
        for s, gen in enumerate(gens):
            if not done[s] and tick >= s:
                done[s] = next(gen, _DONE) is _DONE
        tick += 1
    acc = parts[0]
    for p in parts[1:]:
        acc = acc + p
    o_ref[...] = x_ref[...] + gate_ref[...] * acc


def _rwkv_out(y, a, s1, g2, x, mod, row_fn, ln_w, ln_b, r_k, k_a, w_o, tm):
    M, D = x.shape
    big = lambda col: pl.BlockSpec((tm, D), lambda i: (i, col))
    dirblk = lambda d: pl.BlockSpec((None, tm, D), lambda i: (d, i, 0))
    row = pl.BlockSpec((1, D), lambda i: (0, 0))
    return pl.pallas_call(
        functools.partial(_rwkv_out_kernel, D=D),
        grid=(M // tm,),
        in_specs=[
            dirblk(0), dirblk(1),
            big(0), big(1), big(2),
            dirblk(0), dirblk(1),
            pl.BlockSpec((tm, 2 * LORA_BLK), lambda i: (i, 3 * D // (2 * LORA_BLK) + 1)),
            pl.BlockSpec((LORA_BLK, D), lambda i: (0, 0)),
            row, row, row, row,
            pl.BlockSpec((MXU_DIM, MXU_DIM), lambda i: (0, 0)),
            pl.BlockSpec((D, D), lambda i: (0, 0)),
            pl.BlockSpec((tm, D), lambda i: (i, 0)),
            pl.BlockSpec((None, 1, D), lambda i: (row_fn(i), 0, 2)),
        ],
        out_specs=pl.BlockSpec((tm, D), lambda i: (i, 0)),
        out_shape=jax.ShapeDtypeStruct((M, D), F32),
        compiler_params=_cparams(("arbitrary",)),
        name="rwkv_out",
    )(y, y, s1, s1, s1, a, a, s1, g2, ln_w, ln_b, r_k, k_a, _group_ones(MXU_DIM, RWKV_HEAD), w_o, x, mod)


def _ffn_kernel(x_ref, g_ref, sh_ref, sc_ref, gate_ref, w1_ref, w3_ref, w2_ref, o_ref, a_scr, *, tf):
    f = pl.program_id(1)
    nf = pl.num_programs(1)

    def hidden_block(a):
        acts = []
        for s in range(tf // MXU_DIM):
            cs = slice(s * MXU_DIM, (s + 1) * MXU_DIM)
            h1 = _dot(a, w1_ref[:, cs])
            h3 = _dot(a, w3_ref[:, cs])
            acts.append((_silu(h1) * h3).astype(BF16))
        return _dot(jnp.concatenate(acts, axis=1), w2_ref[...])

    @pl.when(f == 0)
    def _():
        a = _modulate(x_ref[...], g_ref[...], sh_ref[...], sc_ref[...]).astype(BF16)
        a_scr[...] = a
        o_ref[...] = hidden_block(a)

    @pl.when((f > 0) & (f < nf - 1))
    def _():
        o_ref[...] += hidden_block(a_scr[...])

    @pl.when(f == nf - 1)
    def _():
        o_ref[...] = x_ref[...] + gate_ref[...] * (o_ref[...] + hidden_block(a_scr[...]))


def _ffn(x, mod, row_fn, gain, w1b, w3b, w2, tm, tf):
    M, D = x.shape
    nf = w1b.shape[1] // tf
    modspec = lambda ch: pl.BlockSpec((None, 1, D), lambda i, f: (row_fn(i), 0, ch))
    return pl.pallas_call(
        functools.partial(_ffn_kernel, tf=tf),
        grid=(M // tm, nf),
        in_specs=[
            pl.BlockSpec((tm, D), lambda i, f: (i, 0), pipeline_mode=pl.Buffered(1)),
            pl.BlockSpec((1, D), lambda i, f: (0, 0)),
            modspec(3), modspec(4), modspec(5),
            pl.BlockSpec((D, tf), lambda i, f: (0, f)),
            pl.BlockSpec((D, tf), lambda i, f: (0, f)),
            pl.BlockSpec((tf, D), lambda i, f: (f, 0)),
        ],
        out_specs=pl.BlockSpec((tm, D), lambda i, f: (i, 0)),
        out_shape=jax.ShapeDtypeStruct((M, D), F32),
        scratch_shapes=[pltpu.VMEM((tm, D), BF16)],
        compiler_params=_cparams(("arbitrary", "arbitrary")),
        name="ffn",
    )(x, gain, mod, mod, mod, w1b, w3b, w2)


def _qkv_kernel(x_ref, g_ref, sh_ref, sc_ref, w_ref, hg_ref, cos_ref, sin_ref, ones_ref, o_ref, a_scr,
                *, nrope, nnorm, tn):
    n = pl.program_id(1)

    def normed_block(a):
        acc = _dot(a, w_ref[...])
        hg = hg_ref[...]
        ones = ones_ref[...]
        for h in range(tn // MXU_DIM):
            blk = acc[:, h * MXU_DIM:(h + 1) * MXU_DIM]
            ms = _dot((blk * blk).astype(BF16), ones) * (1.0 / NA_HEAD)
            blk = blk * lax.rsqrt(ms + RMS_EPS)
            for s in range(MXU_DIM // NA_HEAD):
                t = blk[:, s * NA_HEAD:(s + 1) * NA_HEAD] * hg
                if nrope:
                    t = t * cos_ref[...] + pltpu.roll(t, NA_HEAD // 2, axis=1) * sin_ref[...]
                c0 = h * MXU_DIM + s * NA_HEAD
                o_ref[:, c0:c0 + NA_HEAD] = t.astype(o_ref.dtype)

    @pl.when(n == 0)
    def _():
        a = _modulate(x_ref[...], g_ref[...], sh_ref[...], sc_ref[...]).astype(BF16)
        a_scr[...] = a
        normed_block(a)

    @pl.when((n > 0) & (n < nnorm))
    def _():
        normed_block(a_scr[...])

    @pl.when(n >= nnorm)
    def _():
        o_ref[...] = _dot(a_scr[...], w_ref[...]).astype(o_ref.dtype)


def _na_qkv(x, mod, row_fn, gain, w, head_gain, cos, sin, *, nrope, nnorm, tm, tn):
    assert nrope in (0, nnorm)
    M, D = x.shape
    N = w.shape[1]
    tpb = cos.shape[0] // tm
    hrow = lambda n: jnp.minimum(n, head_gain.shape[0] - 1)
    return pl.pallas_call(
        functools.partial(_qkv_kernel, nrope=nrope, nnorm=nnorm, tn=tn),
        grid=(M // tm, N // tn),
        in_specs=[
            pl.BlockSpec((tm, D), lambda i, n: (i, 0)),
            pl.BlockSpec((1, D), lambda i, n: (0, 0)),
            pl.BlockSpec((None, 1, D), lambda i, n: (row_fn(i), 0, 0)),
            pl.BlockSpec((None, 1, D), lambda i, n: (row_fn(i), 0, 1)),
            pl.BlockSpec((D, tn), lambda i, n: (0, n)),
            pl.BlockSpec((None, 1, NA_HEAD), lambda i, n: (hrow(n), 0, 0)),
            pl.BlockSpec((tm, NA_HEAD), lambda i, n: (i % tpb, 0)),
            pl.BlockSpec((tm, NA_HEAD), lambda i, n: (i % tpb, 0)),
            pl.BlockSpec((MXU_DIM, MXU_DIM), lambda i, n: (0, 0)),
        ],
        out_specs=pl.BlockSpec((tm, tn), lambda i, n: (i, n)),
        out_shape=jax.ShapeDtypeStruct((M, N), BF16),
        scratch_shapes=[pltpu.VMEM((tm, D), BF16)],
        compiler_params=_cparams(("arbitrary", "arbitrary")),
        name="na_qkv",
    )(x, gain, mod, mod, w, head_gain, cos, sin, _group_ones(MXU_DIM, NA_HEAD))


def _na_kernel(q_ref, k_ref, v_ref, kc_ref, vc_ref, bias_ref, o_ref, *, rows, rb):
    j0 = pl.program_id(2) * rb
    win_h = min(WIN_H, rows)
    nk = win_h * GRID_W
    qrows = range(rb)
    rs = lambda t, j: t[j * GRID_W:(j + 1) * GRID_W]

    qa = q_ref[...]
    s_ctx = lax.dot_general(qa, kc_ref[...], _NT, preferred_element_type=F32)

    def row(j):
        r0 = jnp.clip(j0 + j - WIN_H // 2, 0, rows - win_h)
        off = pl.multiple_of(r0 * GRID_W, GRID_W)
        s_lat = (lax.dot_general(rs(qa, j), k_ref[pl.ds(off, nk), :], _NT, preferred_element_type=F32)
                 + bias_ref[r0 - (j0 + j) + WIN_H - 1])
        yield
        sc = rs(s_ctx, j)
        m = jnp.maximum(jnp.max(s_lat, axis=-1, keepdims=True), jnp.max(sc, axis=-1, keepdims=True))
        yield
        p_lat = jnp.exp(s_lat - m)
        p_ctx = jnp.exp(sc - m)
        yield
        den = jnp.sum(p_lat, axis=-1, keepdims=True) + jnp.sum(p_ctx, axis=-1, keepdims=True)
        o = _dot(p_lat.astype(BF16), v_ref[pl.ds(off, nk), :])
        yield
        held[j] = (o, den, p_ctx.astype(BF16))
        yield

    held = [None] * rb

    gens = [row(j) for j in qrows]
    done = [False] * rb
    tick = 0
    while not all(done):
        for j in qrows:
            if not done[j] and tick >= j * _ROW_STAGGER:
                done[j] = next(gens[j], _DONE) is _DONE
        tick += 1
    o_ctx = _dot(jnp.concatenate([held[j][2] for j in qrows], axis=0), vc_ref[...])
    for j in qrows:
        o, den, _ = held[j]
        o_ref[j * GRID_W:(j + 1) * GRID_W, :] = ((o + rs(o_ctx, j)) / den).astype(o_ref.dtype)


def _na_attention(qkv, kvc, bias, B, T, D, rb):
    H = D // NA_HEAD
    rows = T // GRID_W
    C = kvc.shape[1]
    nbias = bias.shape[1]
    return pl.pallas_call(
        functools.partial(_na_kernel, rows=rows, rb=rb),
        grid=(B, H, rows // rb),
        in_specs=[
            pl.BlockSpec((None, rb * GRID_W, NA_HEAD), lambda b, h, j: (b, j, h)),
            pl.BlockSpec((None, T, NA_HEAD), lambda b, h, j: (b, 0, H + h)),
            pl.BlockSpec((None, T, NA_HEAD), lambda b, h, j: (b, 0, 2 * H + h)),
            pl.BlockSpec((None, C, NA_HEAD), lambda b, h, j: (b, 0, h)),
            pl.BlockSpec((None, C, NA_HEAD), lambda b, h, j: (b, 0, H + h)),
            pl.BlockSpec((None, nbias, GRID_W, bias.shape[3]), lambda b, h, j: (h, 0, 0, 0)),
        ],
        out_specs=pl.BlockSpec((None, rb * GRID_W, NA_HEAD), lambda b, h, j: (b, j, h)),
        out_shape=jax.ShapeDtypeStruct((B, T, D), BF16),
        compiler_params=_cparams(("arbitrary", "arbitrary", "arbitrary")),
        name="na_attention",
    )(qkv, qkv, qkv, kvc, kvc, bias)


def _proj_res_kernel(a_ref, w_ref, x_ref, gate_ref, o_ref):
    o_ref[...] = x_ref[...] + gate_ref[...] * _dot(a_ref[...], w_ref[...])


def _proj_residual(a, w, x, mod, row_fn, chunk, tm):
    M, K = a.shape
    N = w.shape[1]
    return pl.pallas_call(
        _proj_res_kernel,
        grid=(M // tm,),
        in_specs=[
            pl.BlockSpec((tm, K), lambda i: (i, 0)),
            pl.BlockSpec((K, N), lambda i: (0, 0)),
            pl.BlockSpec((tm, N), lambda i: (i, 0)),
            pl.BlockSpec((None, 1, N), lambda i: (row_fn(i), 0, chunk)),
        ],
        out_specs=pl.BlockSpec((tm, N), lambda i: (i, 0)),
        out_shape=jax.ShapeDtypeStruct((M, N), F32),
        compiler_params=_cparams(("arbitrary",)),
        name="proj_residual",
    )(a, w, x, mod)


_QUART = NA_HEAD // 4
_ROPE_PERM = np.concatenate([np.arange(0, _QUART), np.arange(2 * _QUART, 3 * _QUART),
                             np.arange(_QUART, 2 * _QUART), np.arange(3 * _QUART, NA_HEAD)])


def _rope_layout(w_qk):
    K, N = w_qk.shape
    return w_qk.reshape(K, N // NA_HEAD, NA_HEAD)[:, :, _ROPE_PERM].reshape(K, N)


def _rope_tables(T):
    t = np.arange(T)
    half = NA_HEAD // 2
    freqs = ROPE_THETA ** (-np.arange(0, half, 2, dtype=np.float64) / half)
    ang_r = (t // GRID_W).astype(np.float64)[:, None] * freqs
    ang_c = (t % GRID_W).astype(np.float64)[:, None] * freqs
    cos = np.concatenate([np.cos(ang_r), np.cos(ang_c), np.cos(ang_r), np.cos(ang_c)], axis=1)
    sin = np.concatenate([-np.sin(ang_r), -np.sin(ang_c), np.sin(ang_r), np.sin(ang_c)], axis=1)
    return jnp.asarray(cos, F32), jnp.asarray(sin, F32)


def _na_bias(rpb, rows):
    win_h = min(WIN_H, rows)
    qc = np.arange(GRID_W)[:, None]
    kc = np.arange(GRID_W)[None, :]
    wstart = np.clip(qc - WIN_W // 2, 0, GRID_W - WIN_W)
    valid = (kc >= wstart) & (kc < wstart + WIN_W)
    rel = np.clip(kc - qc, -(WIN_W - 1), WIN_W - 1) + WIN_W - 1
    g = rpb[:, :, rel]
    g = jnp.where(valid[None, None], g, NEG_INF)
    tabs = []
    for i0 in range(2 * WIN_H - win_h):
        blk = g[:, i0:i0 + win_h]
        tabs.append(jnp.transpose(blk, (0, 2, 1, 3)).reshape(g.shape[0], GRID_W, win_h * GRID_W))
    return jnp.stack(tabs, axis=1)


def _pad_cols(w, n):
    return jnp.pad(w, ((0, 0), (0, n - w.shape[1])))


def _pad_rows(w, n):
    return jnp.pad(w, ((0, n - w.shape[0]), (0, 0)))


def kernel(x, c, ctx, c_ctx, ada_w, ada_b, norm1, norm2, rwkv_mu, rwkv_w_r, rwkv_w_k, rwkv_w_v, rwkv_w_o, rwkv_w0, rwkv_w1, rwkv_w2, rwkv_a0, rwkv_a1, rwkv_a2, rwkv_g1, rwkv_g2, rwkv_k_k, rwkv_k_a, rwkv_r_k, rwkv_ln_w, rwkv_ln_b, na_w_qkv, na_w_o, na_q_gain, na_k_gain, na_rpb, ffn_w1, ffn_w3, ffn_w2):
    B, T, D = x.shape
    C = ctx.shape[1]
    rows = T // GRID_W
    depth = ada_w.shape[0]
    n_mix = 2
    TM = 512
    TMO = 256
    TF = 512 if ffn_w1.shape[-1] % 512 == 0 else 256
    TMF = min(1024, T)

    cvec = jnp.concatenate([c, c_ctx[None, :], jnp.zeros((8 - B - 1, D), F32)], axis=0)
    mods = _ada(cvec, ada_w, ada_b)
    lat_row = lambda tm: (lambda i: i // (T // tm))
    ctx_row = lambda i: B

    xl = x.reshape(B * T, D)
    xc = ctx.reshape(B * C, D)

    for i in range(depth):
        last = i == depth - 1
        j = i // n_mix
        mod = mods[i].reshape(8, 1, 6 * D)
        g1 = norm1[i].reshape(1, D)
        g2 = norm2[i].reshape(1, D)
        w1 = ffn_w1[i].astype(BF16)
        w3 = ffn_w3[i].astype(BF16)
        w2 = ffn_w2[i].astype(BF16)
        if i % n_mix == 0:
            lw = rwkv_w1.shape[-1]
            la = rwkv_a1.shape[-1]
            assert max(2 * lw, 2 * la, rwkv_g1.shape[-1]) <= LORA_BLK and D % TN1 == 0
            wcat = jnp.concatenate([
                rwkv_w_r[j], rwkv_w_k[j], rwkv_w_v[j],
                _pad_cols(jnp.concatenate([rwkv_w1[j, 0], rwkv_w1[j, 1]], axis=1), LORA_BLK),
                _pad_cols(jnp.concatenate([rwkv_a1[j, 0], rwkv_a1[j, 1]], axis=1), LORA_BLK),
                _pad_cols(rwkv_g1[j], 2 * LORA_BLK)], axis=1).astype(BF16)
            z = lambda n: jnp.zeros((n, D), F32)
            w2d = jnp.stack([_pad_rows(rwkv_w2[j, 0], LORA_BLK),
                             _pad_rows(jnp.concatenate([z(lw), rwkv_w2[j, 1]], axis=0), LORA_BLK)]).astype(BF16)
            a2d = jnp.stack([_pad_rows(rwkv_a2[j, 0], LORA_BLK),
                             _pad_rows(jnp.concatenate([z(la), rwkv_a2[j, 1]], axis=0), LORA_BLK)]).astype(BF16)
            wg2 = _pad_rows(rwkv_g2[j], LORA_BLK).astype(BF16)
            w0 = rwkv_w0[j].reshape(2, 1, D)
            a0 = rwkv_a0[j].reshape(2, 1, D)
            k_k = rwkv_k_k[j].reshape(1, D)
            k_a = rwkv_k_a[j].reshape(1, D)
            r_k = rwkv_r_k[j].reshape(1, D)
            ln_w = rwkv_ln_w[j].reshape(1, D)
            ln_b = rwkv_ln_b[j].reshape(1, D)
            w_o = rwkv_w_o[j].astype(BF16)

            s1c, kkc = _rwkv_stage1(xc, mod, ctx_row, g1, rwkv_mu[j], k_k, wcat, latent=False, rows=rows, tm=C)
            h0 = jnp.zeros((2, B, D // PAIR, PAIR, PAIR), F32)
            yc, ac, hc = _wkv_scan(s1c, kkc, w2d, a2d, w0, a0, k_a, h0, B, C, D)
            s1l, kkl = _rwkv_stage1(xl, mod, lat_row(TM), g1, rwkv_mu[j], k_k, wcat, latent=True, rows=rows, tm=TM)
            yl, al, _ = _wkv_scan(s1l, kkl, w2d, a2d, w0, a0, k_a, hc, B, T, D)
            xl = _rwkv_out(yl, al, s1l, wg2, xl, mod, lat_row(TMO), ln_w, ln_b, r_k, k_a, w_o, TMO)
            if not last:
                xc = _rwkv_out(yc, ac, s1c, wg2, xc, mod, ctx_row, ln_w, ln_b, r_k, k_a, w_o, min(TMO, C))
        else:
            H = D // NA_HEAD
            cos, sin = _rope_tables(T)
            scale = NA_HEAD ** -0.5
            hg = jnp.stack([(na_q_gain[j] * scale)[_ROPE_PERM], na_k_gain[j][_ROPE_PERM]]).reshape(2, 1, NA_HEAD)
            tn = min(1024, D)
            nq = D // tn
            wqkv = jnp.concatenate([_rope_layout(na_w_qkv[j][:, :2 * D]), na_w_qkv[j][:, 2 * D:]],
                                   axis=1).astype(BF16)
            hg_blocks = jnp.concatenate([jnp.broadcast_to(hg[0], (nq, 1, NA_HEAD)),
                                         jnp.broadcast_to(hg[1], (nq, 1, NA_HEAD))], axis=0)
            qkv = _na_qkv(xl, mod, lat_row(TMF), g1, wqkv, hg_blocks, cos, sin, nrope=2 * nq, nnorm=2 * nq,
                          tm=TMF, tn=tn)
            kg_blocks = jnp.broadcast_to(hg[1], (nq, 1, NA_HEAD))
            kvc = _na_qkv(xc, mod, ctx_row, g1, wqkv[:, D:], kg_blocks, cos[:C], sin[:C], nrope=0, nnorm=nq,
                          tm=C, tn=tn)
            bias = _na_bias(na_rpb[j], rows)
            o = _na_attention(qkv.reshape(B, T, 3 * D), kvc.reshape(B, C, 2 * D), bias, B, T, D, min(32, rows))
            xl = _proj_residual(o.reshape(B * T, D), na_w_o[j].astype(BF16), xl, mod,
                                lat_row(TM), 2, TM)
            if not last:
                raise NotImplementedError("attention layer with context output")
        xl = _ffn(xl, mod, lat_row(TMF), g2, w1, w3, w2, TMF, TF)
        if not last:
            xc = _ffn(xc, mod, ctx_row, g2, w1, w3, w2, C, TF)
    return xl.reshape(B, T, D)
```
